```python
import math
import jax, jax.numpy as jnp
from jax import lax
import numpy as np

D_MODEL = 1024
BATCH = 32
SEQ = 256
DEPTH = 2
DEC_BATCH = 4
DEC_SEQ = 2048
PAST_LEN = 512

GRID_W = 64
N_EVEN = (DEPTH + 1) // 2
N_ODD = DEPTH // 2
HALF = D_MODEL // 2
S5_P = 16
S5_G = HALF // S5_P
S5_N = 64
S5_DT_MIN = 0.001
S5_DT_MAX = 0.1
DA_DK = 64
DA_DV = 2 * DA_DK
DA_HEADS = HALF // DA_DV
ROPE_THETA = 10000.0
ROPE_F = DA_DK // 4
Q_BLOCK = 128
SC_WIDTH = 3
LRU_WIDTH = HALF
LRU_BLOCKS = 8
LRU_BS = LRU_WIDTH // LRU_BLOCKS
LRU_CONV = 4
LRU_C = 8.0
D_FF = 2816
FFN_CONV = 3
AB_IN = 4 * HALF
CD_IN = 3 * HALF + 2 * LRU_WIDTH
EPS = 1e-6

kernel_name = "hybrid_s5_diffattn_shortconv_rglru_dit_step"


def rmsnorm(x, g):
    x32 = x.astype(jnp.float32)
    y = x32 * lax.rsqrt(jnp.mean(x32 * x32, axis=-1, keepdims=True) + EPS)
    return (y * g.astype(jnp.float32)).astype(x.dtype)


def dwconv(x, w, left):
    k = w.shape[0]
    L = x.shape[1]
    xp = jnp.pad(x, ((0, 0), (left, k - 1 - left), (0, 0)))
    out = xp[:, 0:L] * w[0]
    for j in range(1, k):
        out = out + xp[:, j:j + L] * w[j]
    return out


def rope_tables(L):
    rows = L // GRID_W
    t_row = jnp.repeat(jnp.arange(rows), GRID_W).astype(jnp.float32)
    t_col = jnp.tile(jnp.arange(GRID_W), rows).astype(jnp.float32)
    inv = ROPE_THETA ** (-jnp.arange(ROPE_F, dtype=jnp.float32) / ROPE_F)
    ang = jnp.stack([t_row[:, None] * inv, t_col[:, None] * inv], axis=1)
    return jnp.cos(ang), jnp.sin(ang)


def rope2d(x, cos, sin):
    xs = x.reshape(x.shape[:-1] + (2, 2, ROPE_F))
    x1, x2 = xs[..., 0, :], xs[..., 1, :]
    c = cos[:, None, None].astype(x.dtype)
    s = sin[:, None, None].astype(x.dtype)
    o = jnp.stack([x1 * c - x2 * s, x2 * c + x1 * s], axis=-2)
    return o.reshape(x.shape)


def modulation(cv, w_mod, b_mod):
    return (jax.nn.silu(cv) @ w_mod + b_mod).reshape(cv.shape[0], 6, D_MODEL)


def _pre(x, m, g, j):
    return rmsnorm(x, g) * (1 + m[:, 3 * j + 1, None]) + m[:, 3 * j, None]


def _post(x, y, m, g, j):
    return x + m[:, 3 * j + 2, None] * rmsnorm(y, g)


def _cplx_comb(e1, e2):
    a1r, a1i, b1r, b1i = e1
    a2r, a2i, b2r, b2i = e2
    return (a2r * a1r - a2i * a1i, a2r * a1i + a2i * a1r,
            a2r * b1r - a2i * b1i + b2r, a2r * b1i + a2i * b1r + b2i)


def _real_comb(e1, e2):
    a1, b1 = e1
    a2, b2 = e2
    return (a1 * a2, a2 * b1 + b2)


def s5_direction(u, lam_re, lam_im, log_dt, b_w, c_w, h0, reverse):
    lam_re = lam_re.astype(jnp.float32)
    lam_im = lam_im.astype(jnp.float32)
    dt = jnp.exp(log_dt.astype(jnp.float32))[:, None]
    mag = jnp.exp(lam_re * dt)
    ang = lam_im * dt
    ab_r, ab_i = mag * jnp.cos(ang), mag * jnp.sin(ang)
    den = lam_re * lam_re + lam_im * lam_im
    nr, ni = ab_r - 1.0, ab_i
    f_r = (nr * lam_re + ni * lam_im) / den
    f_i = (ni * lam_re - nr * lam_im) / den
    b_w = b_w.astype(jnp.float32)
    c_w = c_w.astype(jnp.float32)
    bu_r = jnp.einsum('blgp,gnp->blgn', u, b_w[0])
    bu_i = jnp.einsum('blgp,gnp->blgn', u, b_w[1])
    br = f_r * bu_r - f_i * bu_i
    bi = f_r * bu_i + f_i * bu_r
    if reverse:
        br, bi = jnp.flip(br, axis=1), jnp.flip(bi, axis=1)
    if h0 is not None:
        h0r, h0i = h0[0], h0[1]
        br = br.at[:, 0].add(ab_r * h0r - ab_i * h0i)
        bi = bi.at[:, 0].add(ab_r * h0i + ab_i * h0r)
    ar = jnp.broadcast_to(ab_r, br.shape)
    ai = jnp.broadcast_to(ab_i, br.shape)
    _, _, hr, hi = lax.associative_scan(_cplx_comb, (ar, ai, br, bi), axis=1)
    final = jnp.stack([hr[:, -1], hi[:, -1]], axis=1)
    if reverse:
        hr, hi = jnp.flip(hr, axis=1), jnp.flip(hi, axis=1)
    y = jnp.einsum('blgn,gpn->blgp', hr, c_w[0]) - jnp.einsum('blgn,gpn->blgp', hi, c_w[1])
    return y, final


def diff_attention(q, keys, vals, lam):
    b, lq = q.shape[0], q.shape[1]
    nb = lq // Q_BLOCK
    qb = q.reshape((b, nb, Q_BLOCK) + q.shape[2:]).swapaxes(0, 1)
    scale = 1.0 / math.sqrt(DA_DK)

    def one_block(qblk):
        s = jnp.einsum('bqhmd,bkhmd->bhmqk', qblk, keys, preferred_element_type=jnp.float32) * scale
        p = jax.nn.softmax(s, axis=-1)
        w = p[:, :, 0] - lam * p[:, :, 1]
        return jnp.einsum('bhqk,bkhe->bqhe', w.astype(vals.dtype), vals)

    o = lax.map(one_block, qb)
    return o.swapaxes(0, 1).reshape(b, lq, DA_HEADS, DA_DV)


def mixer_ab(h, w_in, w_out, lam_re, lam_im, log_dt, s5_b, s5_c, s5_d, w_glu, b_glu,
             da_lam, da_g, lam_init, rope, k_ctx, v_ctx, s5_h0):
    b, L, _ = h.shape
    proj = h @ w_in
    u, q, k, v = jnp.split(proj, [HALF, 2 * HALF, 3 * HALF], axis=-1)
    u4 = u.reshape(b, L, S5_G, S5_P).astype(jnp.float32)
    ys = u4 * s5_d.reshape(S5_G, S5_P).astype(jnp.float32)
    finals = []
    for d in range(2):
        h0 = None if s5_h0 is None else jnp.moveaxis(s5_h0[:, d].astype(jnp.float32), 1, 0)
        y_d, fin = s5_direction(u4, lam_re[d], lam_im[d], log_dt[d], s5_b[d], s5_c[d], h0, d == 1)
        ys = ys + y_d
        finals.append(fin)
    s5_state = jnp.stack(finals, axis=1).astype(h.dtype)
    ys = jax.nn.gelu(ys.reshape(b, L, HALF)).astype(h.dtype)
    y_a = ys * jax.nn.sigmoid(ys @ w_glu + b_glu)
    q = q.reshape(b, L, DA_HEADS, 2, DA_DK)
    k = k.reshape(b, L, DA_HEADS, 2, DA_DK)
    v = v.reshape(b, L, DA_HEADS, DA_DV)
    dl = da_lam.astype(jnp.float32)
    lam = jnp.exp(jnp.sum(dl[0] * dl[1])) - jnp.exp(jnp.sum(dl[2] * dl[3])) + lam_init
    if k_ctx is None:
        keys, vals, qr = k, v, q
    else:
        cos, sin = rope
        qr = rope2d(q, cos, sin)
        keys = jnp.concatenate([rope2d(k, cos, sin), k_ctx.astype(h.dtype)], axis=1)
        vals = jnp.concatenate([v, v_ctx.astype(h.dtype)], axis=1)
    o = diff_attention(qr, keys, vals, lam)
    o = rmsnorm(o, da_g) * (1.0 - lam_init)
    y_b = o.reshape(b, L, HALF)
    out = jnp.concatenate([y_a, y_b], axis=-1) @ w_out
    return out, k, v, s5_state


def mixer_cd(h, w_in, w_out, sc_w, conv_w, conv_b, w_a, b_a, w_x, b_x, lru_lam, h0):
    b, L, _ = h.shape
    proj = h @ w_in
    xin, bg, cg, xr, gb = jnp.split(proj, [HALF, 2 * HALF, 3 * HALF, 3 * HALF + LRU_WIDTH], axis=-1)
    y_c = bg * dwconv(cg * xin, sc_w, 1)
    xc = dwconv(xr, conv_w, 2) + conv_b
    xc32 = xc.astype(jnp.float32)
    xb = xc32.reshape(b, L, LRU_BLOCKS, LRU_BS)
    hsum = None
    finals = []
    for d in range(2):
        r = jax.nn.sigmoid(jnp.einsum('blkc,kcd->blkd', xb, w_a[d].astype(jnp.float32)).reshape(b, L, LRU_WIDTH) + b_a[d])
        i = jax.nn.sigmoid(jnp.einsum('blkc,kcd->blkd', xb, w_x[d].astype(jnp.float32)).reshape(b, L, LRU_WIDTH) + b_x[d])
        log_a = -LRU_C * r * jax.nn.softplus(-lru_lam[d].astype(jnp.float32))
        a = jnp.exp(log_a)
        bval = jnp.sqrt(-jnp.expm1(2.0 * log_a)) * (i * xc32)
        if d == 1:
            a, bval = jnp.flip(a, axis=1), jnp.flip(bval, axis=1)
        if h0 is not None:
            bval = bval.at[:, 0].add(a[:, 0] * h0[:, d].astype(jnp.float32))
        _, hs = lax.associative_scan(_real_comb, (a, bval), axis=1)
        finals.append(hs[:, -1])
        if d == 1:
            hs = jnp.flip(hs, axis=1)
        hsum = hs if hsum is None else hsum + hs
    y_d = hsum.astype(h.dtype) * jax.nn.gelu(gb)
    out = jnp.concatenate([y_c, y_d], axis=-1) @ w_out
    return out, jnp.stack(finals, axis=1).astype(h.dtype)


def conv_ffn(h, w_up, cw, cb, w_down):
    u = dwconv(h @ w_up, cw, 1) + cb
    g, val = jnp.split(u, 2, axis=-1)
    return (jax.nn.gelu(g) * val) @ w_down


def setup_inputs(seed: int = 0) -> dict:
    key = jax.random.key(seed)
    ks = iter(jax.random.split(key, 48))
    f32 = jnp.float32

    def nrm(shape, scale):
        return jax.random.normal(next(ks), shape, f32) * scale

    lam_im = math.pi * jnp.arange(S5_N, dtype=f32)
    s5_lam_im = jnp.broadcast_to(lam_im, (N_EVEN, 2, S5_G, S5_N)) + nrm((N_EVEN, 2, S5_G, S5_N), 0.01)
    a_init = jax.random.uniform(next(ks), (N_ODD, 2, LRU_WIDTH), f32, 0.9, 0.999) ** (1.0 / LRU_C)
    return {
        "x_prompt": nrm((BATCH, SEQ, D_MODEL), 1.0),
        "x_sample": nrm((DEC_BATCH, DEC_SEQ, D_MODEL), 1.0),
        "cache_attn_k": nrm((DEC_BATCH, N_EVEN, PAST_LEN, DA_HEADS, 2, DA_DK), 1.0),
        "cache_attn_v": nrm((DEC_BATCH, N_EVEN, PAST_LEN, DA_HEADS, DA_DV), 1.0),
        "state_s5": nrm((DEC_BATCH, N_EVEN, 2, 2, S5_G, S5_N), 0.5),
        "state_rglru": nrm((DEC_BATCH, N_ODD, 2, LRU_WIDTH), 0.5),
        "c": nrm((DEC_BATCH, D_MODEL), 1.0),
        "c_ctx": nrm((D_MODEL,), 1.0),
        "w_mod": nrm((DEPTH, D_MODEL, 6 * D_MODEL), 0.5 * D_MODEL ** -0.5),
        "b_mod": nrm((DEPTH, 6 * D_MODEL), 0.02),
        "norm_g": 1.0 + nrm((DEPTH, 4, D_MODEL), 0.02),
        "w_in_ab": nrm((N_EVEN, D_MODEL, AB_IN), D_MODEL ** -0.5),
        "w_out_ab": nrm((N_EVEN, 2 * HALF, D_MODEL), (2 * HALF) ** -0.5),
        "s5_lam_re": -0.5 + nrm((N_EVEN, 2, S5_G, S5_N), 0.01),
        "s5_lam_im": s5_lam_im,
        "s5_log_dt": jax.random.uniform(next(ks), (N_EVEN, 2, S5_G), f32, math.log(S5_DT_MIN), math.log(S5_DT_MAX)),
        "s5_b": nrm((N_EVEN, 2, 2, S5_G, S5_N, S5_P), S5_P ** -0.5),
        "s5_c": nrm((N_EVEN, 2, 2, S5_G, S5_P, S5_N), (2 * S5_N) ** -0.5),
        "s5_d": nrm((N_EVEN, HALF), 1.0),
        "s5_w_glu": nrm((N_EVEN, HALF, HALF), HALF ** -0.5),
        "s5_b_glu": nrm((N_EVEN, HALF), 0.02),
        "da_lam": nrm((N_EVEN, 4, DA_DK), 0.1),
        "da_g": 1.0 + nrm((N_EVEN, DA_DV), 0.02),
        "w_in_cd": nrm((N_ODD, D_MODEL, CD_IN), D_MODEL ** -0.5),
        "w_out_cd": nrm((N_ODD, 2 * HALF, D_MODEL), (2 * HALF) ** -0.5),
        "sc_conv_w": nrm((N_ODD, SC_WIDTH, HALF), SC_WIDTH ** -0.5),
        "lru_conv_w": nrm((N_ODD, LRU_CONV, LRU_WIDTH), LRU_CONV ** -0.5),
        "lru_conv_b": nrm((N_ODD, LRU_WIDTH), 0.02),
        "lru_w_a": nrm((N_ODD, 2, LRU_BLOCKS, LRU_BS, LRU_BS), LRU_BS ** -0.5),
        "lru_b_a": nrm((N_ODD, 2, LRU_WIDTH), 0.1),
        "lru_w_x": nrm((N_ODD, 2, LRU_BLOCKS, LRU_BS, LRU_BS), LRU_BS ** -0.5),
        "lru_b_x": nrm((N_ODD, 2, LRU_WIDTH), 0.1),
        "lru_lam": jnp.log(a_init / (1.0 - a_init)),
        "ffn_w_up": nrm((DEPTH, D_MODEL, 2 * D_FF), D_MODEL ** -0.5),
        "ffn_conv_w": nrm((DEPTH, FFN_CONV, 2 * D_FF), FFN_CONV ** -0.5),
        "ffn_conv_b": nrm((DEPTH, 2 * D_FF), 0.02),
        "ffn_w_down": nrm((DEPTH, D_FF, D_MODEL), D_FF ** -0.5),
    }


def reference(x_prompt, x_sample, cache_attn_k, cache_attn_v, state_s5, state_rglru, c, c_ctx,
              w_mod, b_mod, norm_g, w_in_ab, w_out_ab, s5_lam_re, s5_lam_im, s5_log_dt, s5_b, s5_c,
              s5_d, s5_w_glu, s5_b_glu, da_lam, da_g, w_in_cd, w_out_cd, sc_conv_w, lru_conv_w,
              lru_conv_b, lru_w_a, lru_b_a, lru_w_x, lru_b_x, lru_lam, ffn_w_up, ffn_conv_w,
              ffn_conv_b, ffn_w_down):
    rope_lat = rope_tables(x_sample.shape[1])
    xp, xs = x_prompt, x_sample
    new_k, new_v, new_s5, new_lru = [], [], [], []
    for l in range(DEPTH):
        m_p = modulation(c_ctx[None], w_mod[l], b_mod[l])
        m_s = modulation(c, w_mod[l], b_mod[l])
        hp = _pre(xp, m_p, norm_g[l, 0], 0)
        hs = _pre(xs, m_s, norm_g[l, 0], 0)
        e = l // 2
        if l % 2 == 0:
            lam_init = 0.8 - 0.6 * math.exp(-0.3 * l)
            wts = (w_in_ab[e], w_out_ab[e], s5_lam_re[e], s5_lam_im[e], s5_log_dt[e], s5_b[e], s5_c[e],
                   s5_d[e], s5_w_glu[e], s5_b_glu[e], da_lam[e], da_g[e], lam_init)
            yp, kc, vc, s5c = mixer_ab(hp, *wts, None, None, None, None)
            ys, _, _, _ = mixer_ab(hs, *wts, rope_lat, cache_attn_k[:, e], cache_attn_v[:, e], state_s5[:, e])
            new_k.append(kc)
            new_v.append(vc)
            new_s5.append(s5c)
        else:
            wts = (w_in_cd[e], w_out_cd[e], sc_conv_w[e], lru_conv_w[e], lru_conv_b[e], lru_w_a[e],
                   lru_b_a[e], lru_w_x[e], lru_b_x[e], lru_lam[e])
            yp, lc = mixer_cd(hp, *wts, None)
            ys, _ = mixer_cd(hs, *wts, state_rglru[:, e])
            new_lru.append(lc)
        xp = _post(xp, yp, m_p, norm_g[l, 1], 0)
        xs = _post(xs, ys, m_s, norm_g[l, 1], 0)
        ffw = (ffn_w_up[l], ffn_conv_w[l], ffn_conv_b[l], ffn_w_down[l])
        xp = _post(xp, conv_ffn(_pre(xp, m_p, norm_g[l, 2], 1), *ffw), m_p, norm_g[l, 3], 1)
        xs = _post(xs, conv_ffn(_pre(xs, m_s, norm_g[l, 2], 1), *ffw), m_s, norm_g[l, 3], 1)
    y_prompt, y_sample = xp, xs
    new_attn_k = jnp.stack(new_k, axis=1)
    new_attn_v = jnp.stack(new_v, axis=1)
    new_s5_state = jnp.stack(new_s5, axis=1)
    new_rglru_state = jnp.stack(new_lru, axis=1)
    return (y_prompt, y_sample, new_attn_k, new_attn_v, new_s5_state, new_rglru_state)
```

```python
import functools
import math

import jax
import jax.numpy as jnp
from jax import lax
from jax.experimental import pallas as pl
from jax.experimental.pallas import tpu as pltpu

D_MODEL = 1024
BATCH = 32
SEQ = 256
DEPTH = 2
DEC_BATCH = 4
DEC_SEQ = 2048
PAST_LEN = 512
GRID_W = 64
HALF = D_MODEL // 2
S5_P = 16
S5_G = HALF // S5_P
S5_N = 64
DA_DK = 64
DA_DV = 2 * DA_DK
DA_HEADS = HALF // DA_DV
ROPE_THETA = 10000.0
ROPE_F = DA_DK // 4
LRU_WIDTH = HALF
LRU_BLOCKS = 8
LRU_BS = LRU_WIDTH // LRU_BLOCKS
LRU_C = 8.0
D_FF = 2816
EPS = 1e-6

F32 = jnp.float32
BF16 = jnp.bfloat16
HIGHEST = lax.Precision.HIGHEST

TM = 256
NT_P = BATCH * SEQ // TM
TPS = DEC_SEQ // TM
NT_S = DEC_BATCH * TPS
NT = NT_P + NT_S
N_TOK = NT * TM
HALO = 16
S5_T = 16
S5_K = S5_T * S5_P
NC_P = SEQ // S5_T
NC_S = DEC_SEQ // S5_T
SB = 8
FF_CHUNK = 256
VMEM_LIMIT = 56 * 1024 * 1024


def _mod_row(i):
    return jnp.where(i < NT_P, 0, 1 + (i - NT_P) // TPS)


def _seq_pos(i):
    return jnp.where(i < NT_P, 0, (i - NT_P) % TPS), jnp.where(i < NT_P, 1, TPS)


def _params(sem=("arbitrary",)):
    return pltpu.CompilerParams(dimension_semantics=sem, vmem_limit_bytes=VMEM_LIMIT)


def _rms(x, g):
    ms = jnp.mean(x * x, axis=-1, keepdims=True)
    return x * lax.rsqrt(ms + EPS) * g


def _pre(x, g, shift, scale):
    return _rms(x, g) * (1.0 + scale) + shift


def _mod_kernel(cv_ref, w_ref, b_ref, o_ref):
    cv = cv_ref[...]
    s = cv * jax.nn.sigmoid(cv)
    o_ref[...] = jnp.dot(s, w_ref[...], precision=HIGHEST, preferred_element_type=F32) + b_ref[...]


def _modulation(cv8, w_mod, b_mod):
    nb = 1536
    out = pl.pallas_call(
        _mod_kernel,
        grid=(DEPTH, 6 * D_MODEL // nb),
        in_specs=[
            pl.BlockSpec((8, D_MODEL), lambda l, j: (0, 0)),
            pl.BlockSpec((None, D_MODEL, nb), lambda l, j: (l, 0, j)),
            pl.BlockSpec((None, 1, nb), lambda l, j: (l, 0, j)),
        ],
        out_specs=pl.BlockSpec((None, 8, nb), lambda l, j: (l, 0, j)),
        out_shape=jax.ShapeDtypeStruct((DEPTH, 8, 6 * D_MODEL), F32),
        compiler_params=_params(("arbitrary", "arbitrary")),
        name="modulation",
    )(cv8, w_mod, b_mod.reshape(DEPTH, 1, 6 * D_MODEL))
    return out.reshape(DEPTH, 8, 6, D_MODEL)


def _rope_partner(x):
    lane = lax.broadcasted_iota(jnp.int32, (1, HALF), 1)
    first = (lane % (2 * ROPE_F)) < ROPE_F
    return jnp.where(first, pltpu.roll(x, HALF - ROPE_F, axis=1), pltpu.roll(x, ROPE_F, axis=1))


def _ab_in_kernel(x_ref, mod_ref, g_ref, w_ref, cos_ref, sin_ref, u_ref, qkv_ref, kv_ref):
    i = pl.program_id(0)
    h = _pre(x_ref[...], g_ref[...], mod_ref[0:1, :], mod_ref[1:2, :])
    proj = jnp.dot(h.astype(BF16), w_ref[...], preferred_element_type=F32)
    u_ref[...] = proj[:, 0:HALF]
    q = proj[:, HALF:2 * HALF]
    k = proj[:, 2 * HALF:3 * HALF]
    v = proj[:, 3 * HALF:4 * HALF]
    cos = cos_ref[...]
    sin = sin_ref[...]
    qr = q * cos + _rope_partner(q) * sin
    kr = k * cos + _rope_partner(k) * sin
    qkv_ref[:, 0:HALF] = (qr * (1.0 / math.sqrt(DA_DK))).astype(BF16)
    qkv_ref[:, HALF:2 * HALF] = kr.astype(BF16)
    qkv_ref[:, 2 * HALF:3 * HALF] = v.astype(BF16)

    @pl.when(i < NT_P)
    def _():
        kv_ref[:, 0:HALF] = k
        kv_ref[:, HALF:2 * HALF] = v


def _ab_in(x, mod, g, w, cos_t, sin_t):
    def rope_tile(i):
        return jnp.where(i < NT_P, 0, 1 + (i - NT_P) % TPS)

    return pl.pallas_call(
        _ab_in_kernel,
        grid=(NT,),
        in_specs=[
            pl.BlockSpec((TM, D_MODEL), lambda i: (i, 0)),
            pl.BlockSpec((None, 6, D_MODEL), lambda i: (_mod_row(i), 0, 0)),
            pl.BlockSpec((1, D_MODEL), lambda i: (0, 0)),
            pl.BlockSpec((D_MODEL, 4 * HALF), lambda i: (0, 0)),
            pl.BlockSpec((TM, HALF), lambda i: (rope_tile(i), 0)),
            pl.BlockSpec((TM, HALF), lambda i: (rope_tile(i), 0)),
        ],
        out_specs=[
            pl.BlockSpec((TM, HALF), lambda i: (i, 0)),
            pl.BlockSpec((TM, 3 * HALF), lambda i: (i, 0)),
            pl.BlockSpec((TM, 2 * HALF), lambda i: (jnp.minimum(i, NT_P - 1), 0)),
        ],
        out_shape=[
            jax.ShapeDtypeStruct((N_TOK, HALF), F32),
            jax.ShapeDtypeStruct((N_TOK, 3 * HALF), BF16),
            jax.ShapeDtypeStruct((NT_P * TM, 2 * HALF), F32),
        ],
        compiler_params=_params(),
        name="ab_in",
    )(x, mod, g, w, cos_t, sin_t)


def _rope_tables():
    pos = jnp.arange(DEC_SEQ)
    t_row = (pos // GRID_W).astype(F32)
    t_col = (pos % GRID_W).astype(F32)
    inv = ROPE_THETA ** (-jnp.arange(ROPE_F, dtype=F32) / ROPE_F)
    ang = jnp.stack([t_row[:, None] * inv, t_col[:, None] * inv], axis=1)
    cos, sin = jnp.cos(ang), jnp.sin(ang)
    cos64 = jnp.stack([cos, cos], axis=2).reshape(DEC_SEQ, DA_DK)
    sin64 = jnp.stack([-sin, sin], axis=2).reshape(DEC_SEQ, DA_DK)
    reps = HALF // DA_DK
    cos_t = jnp.concatenate([jnp.ones((TM, HALF), F32), jnp.tile(cos64, (1, reps))], axis=0)
    sin_t = jnp.concatenate([jnp.zeros((TM, HALF), F32), jnp.tile(sin64, (1, reps))], axis=0)
    return cos_t, sin_t


def _s5_param_kernel(lre_ref, lim_ref, ldt_ref, bt_ref, c_ref, f_ref, e_ref, k_ref, coef_ref):
    for d in range(2):
        lre = lre_ref[d:d + 1, :]
        lim = lim_ref[d:d + 1, :]
        dt = jnp.exp(ldt_ref[d:d + 1, :])
        mag = jnp.exp(lre * dt)
        ang = lim * dt
        ar = mag * jnp.cos(ang)
        ai = mag * jnp.sin(ang)
        den = lre * lre + lim * lim
        nr = ar - 1.0
        fr = (nr * lre + ai * lim) / den
        fi = (ai * lre - nr * lim) / den
        btr = bt_ref[d, 0]
        bti = bt_ref[d, 1]
        bfr = fr * btr - fi * bti
        bfi = fr * bti + fi * btr
        cr = c_ref[d, 0]
        ci = c_ref[d, 1]
        pr = jnp.ones_like(ar)
        pi = jnp.zeros_like(ar)
        for tau in range(S5_T + 1):
            f_ref[d, tau] = jnp.concatenate([cr * pr - ci * pi, -(cr * pi + ci * pr)], axis=1)
            if tau < S5_T:
                e_ref[d, tau] = jnp.concatenate([pr * bfr - pi * bfi, pr * bfi + pi * bfr], axis=1)
            pr, pi = pr * ar - pi * ai, pr * ai + pi * ar
        e0 = e_ref[d, 0]
        for tau in range(S5_T):
            k_ref[d, tau] = lax.dot_general(f_ref[d, tau], e0, (((1,), (1,)), ((), ())),
                                            precision=HIGHEST, preferred_element_type=F32)
        qr = jnp.ones_like(ar)
        qi = jnp.zeros_like(ar)
        for _ in range(S5_T):
            qr, qi = qr * ar - qi * ai, qr * ai + qi * ar
        coef_ref[d] = jnp.concatenate([qr, qi], axis=0)


def _s5_params(lam_re, lam_im, log_dt, b_w, c_w):
    lre = lam_re.transpose(1, 0, 2)
    lim = lam_im.transpose(1, 0, 2)
    ldt = jnp.broadcast_to(log_dt[:, :, None], (2, S5_G, S5_N)).transpose(1, 0, 2)
    bt = b_w.transpose(2, 0, 1, 4, 3)
    cc = c_w.transpose(2, 0, 1, 3, 4)
    vec = pl.BlockSpec((None, 2, S5_N), lambda g: (g, 0, 0))
    mat = pl.BlockSpec((None, 2, 2, S5_P, S5_N), lambda g: (g, 0, 0, 0, 0))
    f, e, k, coef = pl.pallas_call(
        _s5_param_kernel,
        grid=(S5_G,),
        in_specs=[vec, vec, vec, mat, mat],
        out_specs=[
            pl.BlockSpec((None, 2, S5_T + 1, S5_P, 2 * S5_N), lambda g: (g, 0, 0, 0, 0)),
            pl.BlockSpec((None, 2, S5_T, S5_P, 2 * S5_N), lambda g: (g, 0, 0, 0, 0)),
            pl.BlockSpec((None, 2, S5_T, S5_P, S5_P), lambda g: (g, 0, 0, 0, 0)),
            pl.BlockSpec((None, 2, 2, S5_N), lambda g: (g, 0, 0, 0)),
        ],
        out_shape=[
            jax.ShapeDtypeStruct((S5_G, 2, S5_T + 1, S5_P, 2 * S5_N), F32),
            jax.ShapeDtypeStruct((S5_G, 2, S5_T, S5_P, 2 * S5_N), F32),
            jax.ShapeDtypeStruct((S5_G, 2, S5_T, S5_P, S5_P), F32),
            jax.ShapeDtypeStruct((S5_G, 2, 2, S5_N), F32),
        ],
        compiler_params=_params(),
        name="s5_params",
    )(lre, lim, ldt, bt, cc)

    s_idx = jnp.arange(S5_T)[:, None]
    t_idx = jnp.arange(S5_T)[None, :]
    diff = t_idx - s_idx
    kf = jnp.where((diff >= 0)[None, :, :, None, None], k[:, 0][:, jnp.clip(diff, 0, S5_T - 1)], 0.0)
    kr = jnp.where((diff <= 0)[None, :, :, None, None], k[:, 1][:, jnp.clip(-diff, 0, S5_T - 1)], 0.0)
    m1 = (kf + kr).transpose(0, 1, 4, 2, 3).reshape(S5_G, S5_K, S5_K)
    ef = e[:, 0, ::-1]
    er = e[:, 1]
    m2 = jnp.concatenate([ef[..., :S5_N], er[..., :S5_N], ef[..., S5_N:], er[..., S5_N:]], axis=-1)
    m2 = m2.reshape(S5_G, S5_K, 4 * S5_N)
    ff = f[:, 0, 1:S5_T + 1]
    fr_ = f[:, 1, S5_T:0:-1]
    rows = jnp.concatenate([ff[..., :S5_N], fr_[..., :S5_N], ff[..., S5_N:], fr_[..., S5_N:]], axis=-1)
    m3 = rows.transpose(0, 3, 1, 2).reshape(S5_G, 4 * S5_N, S5_K)
    coef2 = jnp.stack([jnp.concatenate([coef[:, 0, 0], coef[:, 1, 0]], axis=-1),
                       jnp.concatenate([coef[:, 0, 1], coef[:, 1, 1]], axis=-1)], axis=1)
    return m1.astype(BF16), m2.astype(BF16), m3.astype(BF16), coef2


def _s5_kernel(up_ref, us_ref, m1_ref, m2_ref, m3_ref, coef_ref, h0_ref, d_ref,
               yp_ref, ys_ref, fin_ref, s_scr, hp_scr):
    a_re = coef_ref[0:1, :]
    a_im = coef_ref[1:2, :]
    m1 = m1_ref[...]
    m2 = m2_ref[...]
    m3 = m3_ref[...]
    dvec = d_ref[...]
    ln = 2 * S5_N

    def run(u_ref, y_ref, nb, nc, p0, q0, dynamic):
        rows = nb * nc
        u = u_ref[...]
        ub = u.astype(BF16)
        y_ref[...] = jnp.dot(ub, m1, preferred_element_type=F32) + u * dvec
        s_scr[0:rows, :] = jnp.dot(ub, m2, preferred_element_type=F32)
        lane = lax.broadcasted_iota(jnp.int32, (nb, ln), 1)
        fwd = lane < S5_N
        are = jnp.broadcast_to(a_re, (nb, ln))
        aim = jnp.broadcast_to(a_im, (nb, ln))

        def step(kf, kr, p, q):
            hp_scr[pl.ds(kf, nb), 0:S5_N] = p[:, 0:S5_N]
            hp_scr[pl.ds(kr, nb), S5_N:ln] = p[:, S5_N:ln]
            hp_scr[pl.ds(kf, nb), ln:ln + S5_N] = q[:, 0:S5_N]
            hp_scr[pl.ds(kr, nb), ln + S5_N:2 * ln] = q[:, S5_N:ln]
            s_re = jnp.where(fwd, s_scr[pl.ds(kf, nb), 0:ln], s_scr[pl.ds(kr, nb), 0:ln])
            s_im = jnp.where(fwd, s_scr[pl.ds(kf, nb), ln:2 * ln], s_scr[pl.ds(kr, nb), ln:2 * ln])
            return are * p - aim * q + s_re, are * q + aim * p + s_im

        if dynamic:
            def body(c, carry):
                kf = pl.multiple_of(c * nb, nb)
                kr = pl.multiple_of((nc - 1 - c) * nb, nb)
                return step(kf, kr, *carry)
            p, q = lax.fori_loop(0, nc, body, (p0, q0))
        else:
            p, q = p0, q0
            for c in range(nc):
                p, q = step(c * nb, (nc - 1 - c) * nb, p, q)
        y_ref[...] += jnp.dot(hp_scr[0:rows, :].astype(BF16), m3, preferred_element_type=F32)
        return p, q

    zero = jnp.zeros((BATCH, ln), F32)
    p, q = run(up_ref, yp_ref, BATCH, NC_P, zero, zero, False)
    fin_ref[0] = p
    fin_ref[1] = q
    run(us_ref, ys_ref, SB, NC_S, h0_ref[0], h0_ref[1], True)


def _s5(u, m1, m2, m3, coef2, h0, s5_d):
    up = u[:NT_P * TM].reshape(BATCH, NC_P, S5_T, S5_G, S5_P).transpose(3, 1, 0, 2, 4)
    up = up.reshape(S5_G, NC_P * BATCH, S5_K)
    us = u[NT_P * TM:].reshape(DEC_BATCH, NC_S, S5_T, S5_G, S5_P).transpose(3, 1, 0, 2, 4)
    us = jnp.pad(us, ((0, 0), (0, 0), (0, SB - DEC_BATCH), (0, 0), (0, 0))).reshape(S5_G, NC_S * SB, S5_K)
    dvec = jnp.tile(s5_d.reshape(S5_G, 1, S5_P), (1, S5_T, 1)).reshape(S5_G, 1, S5_K)
    h0g = h0.transpose(3, 2, 0, 1, 4).reshape(S5_G, 2, DEC_BATCH, 2 * S5_N)
    h0g = jnp.pad(h0g, ((0, 0), (0, 0), (0, SB - DEC_BATCH), (0, 0)))
    mat = pl.BlockSpec((None, S5_K, S5_K), lambda g: (g, 0, 0))
    yp, ys, fin = pl.pallas_call(
        _s5_kernel,
        grid=(S5_G,),
        in_specs=[
            pl.BlockSpec((None, NC_P * BATCH, S5_K), lambda g: (g, 0, 0)),
            pl.BlockSpec((None, NC_S * SB, S5_K), lambda g: (g, 0, 0)),
            mat, mat, mat,
            pl.BlockSpec((None, 2, 2 * S5_N), lambda g: (g, 0, 0)),
            pl.BlockSpec((None, 2, SB, 2 * S5_N), lambda g: (g, 0, 0, 0)),
            pl.BlockSpec((None, 1, S5_K), lambda g: (g, 0, 0)),
        ],
        out_specs=[
            pl.BlockSpec((None, NC_P * BATCH, S5_K), lambda g: (g, 0, 0)),
            pl.BlockSpec((None, NC_S * SB, S5_K), lambda g: (g, 0, 0)),
            pl.BlockSpec((None, 2, BATCH, 2 * S5_N), lambda g: (g, 0, 0, 0)),
        ],
        out_shape=[
            jax.ShapeDtypeStruct((S5_G, NC_P * BATCH, S5_K), F32),
            jax.ShapeDtypeStruct((S5_G, NC_S * SB, S5_K), F32),
            jax.ShapeDtypeStruct((S5_G, 2, BATCH, 2 * S5_N), F32),
        ],
        scratch_shapes=[
            pltpu.VMEM((NC_S * SB, 4 * S5_N), F32),
            pltpu.VMEM((NC_S * SB, 4 * S5_N), F32),
        ],
        compiler_params=_params(),
        name="s5",
    )(up, us, m1, m2, m3, coef2, h0g, dvec)
    yp = yp.reshape(S5_G, NC_P, BATCH, S5_T, S5_P).transpose(2, 1, 3, 0, 4).reshape(NT_P * TM, HALF)
    ys = ys.reshape(S5_G, NC_S, SB, S5_T, S5_P)[:, :, :DEC_BATCH].transpose(2, 1, 3, 0, 4)
    y = jnp.concatenate([yp, ys.reshape(NT_S * TM, HALF)], axis=0)
    new_s5 = fin.reshape(S5_G, 2, BATCH, 2, S5_N).transpose(2, 3, 1, 0, 4)
    return y, new_s5


def _attn_kernel(q_ref, k_ref, v_ref, dl_ref, g_ref, o_ref, *, lam_init):
    dl = dl_ref[...]
    lam = (jnp.exp(jnp.sum(dl[0:1] * dl[1:2], keepdims=True))
           - jnp.exp(jnp.sum(dl[2:3] * dl[3:4], keepdims=True)) + lam_init)
    lane = lax.broadcasted_iota(jnp.int32, (1, DA_DV), 1)
    first = lane < DA_DK
    g = g_ref[...]
    for h in range(DA_HEADS):
        cols = slice(h * DA_DV, (h + 1) * DA_DV)
        qh = q_ref[:, cols]
        kh = k_ref[:, cols]
        vh = v_ref[:, cols]
        zero = jnp.zeros_like(qh)
        es, rs = [], []
        for m in range(2):
            qm = jnp.where(first if m == 0 else jnp.logical_not(first), qh, zero)
            s = lax.dot_general(qm, kh, (((1,), (1,)), ((), ())), preferred_element_type=F32)
            e = jnp.exp(s - jnp.max(s, axis=-1, keepdims=True))
            es.append(e)
            rs.append(1.0 / jnp.sum(e, axis=-1, keepdims=True))
        w = es[0] * rs[0] - es[1] * (lam * rs[1])
        o = jnp.dot(w.astype(BF16), vh, preferred_element_type=F32)
        o_ref[:, cols] = (_rms(o, g) * (1.0 - lam_init)).astype(BF16)


def _attention(q_arr, q_tile0, k_arr, v_arr, kv_col, nb, nq, lk, da_lam, da_g, lam_init):
    if k_arr.ndim == 2:
        kspec = pl.BlockSpec((lk, HALF), lambda b, j: (b, kv_col))
        vspec = pl.BlockSpec((lk, HALF), lambda b, j: (b, kv_col + 1))
    else:
        kspec = pl.BlockSpec((None, lk, HALF), lambda b, j: (b, 0, 0))
        vspec = pl.BlockSpec((None, lk, HALF), lambda b, j: (b, 0, 0))
    return pl.pallas_call(
        functools.partial(_attn_kernel, lam_init=lam_init),
        grid=(nb, nq),
        in_specs=[
            pl.BlockSpec((TM, HALF), lambda b, j: (q_tile0 + b * nq + j, 0)),
            kspec, vspec,
            pl.BlockSpec((4, DA_DK), lambda b, j: (0, 0)),
            pl.BlockSpec((1, DA_DV), lambda b, j: (0, 0)),
        ],
        out_specs=pl.BlockSpec((TM, HALF), lambda b, j: (b * nq + j, 0)),
        out_shape=jax.ShapeDtypeStruct((nb * nq * TM, HALF), BF16),
        compiler_params=_params(("arbitrary", "arbitrary")),
        name="diff_attn_%d" % lk,
    )(q_arr, k_arr, v_arr, da_lam, da_g.reshape(1, DA_DV))


def _post(x, y, g, gate):
    return x + gate * _rms(y, g)


def _ab_out_kernel(ys_ref, yb_ref, x_ref, mod_ref, g_ref, wg_ref, bg_ref, wo_ref, o_ref):
    ys = jax.nn.gelu(ys_ref[...])
    glu = jnp.dot(ys.astype(BF16), wg_ref[...], preferred_element_type=F32) + bg_ref[...]
    ya = ys * jax.nn.sigmoid(glu)
    out = (jnp.dot(ya.astype(BF16), wo_ref[0:HALF, :], preferred_element_type=F32)
           + jnp.dot(yb_ref[...], wo_ref[HALF:2 * HALF, :], preferred_element_type=F32))
    o_ref[...] = _post(x_ref[...], out, g_ref[...], mod_ref[2:3, :])


def _ab_out(ys5, yb, x, mod, g, w_glu, b_glu, w_out):
    return pl.pallas_call(
        _ab_out_kernel,
        grid=(NT,),
        in_specs=[
            pl.BlockSpec((TM, HALF), lambda i: (i, 0)),
            pl.BlockSpec((TM, HALF), lambda i: (i, 0)),
            pl.BlockSpec((TM, D_MODEL), lambda i: (i, 0)),
            pl.BlockSpec((None, 6, D_MODEL), lambda i: (_mod_row(i), 0, 0)),
            pl.BlockSpec((1, D_MODEL), lambda i: (0, 0)),
            pl.BlockSpec((HALF, HALF), lambda i: (0, 0)),
            pl.BlockSpec((1, HALF), lambda i: (0, 0)),
            pl.BlockSpec((D_MODEL, D_MODEL), lambda i: (0, 0)),
        ],
        out_specs=pl.BlockSpec((TM, D_MODEL), lambda i: (i, 0)),
        out_shape=jax.ShapeDtypeStruct((N_TOK, D_MODEL), F32),
        compiler_params=_params(),
        name="ab_out",
    )(ys5, yb, x, mod, g, w_glu, b_glu, w_out)


def _halo_specs(width):
    blocks = TM // HALO
    last = N_TOK // HALO - 1
    return [
        pl.BlockSpec((TM, width), lambda i: (i, 0)),
        pl.BlockSpec((HALO, width), lambda i: (jnp.maximum(i * blocks - 1, 0), 0)),
        pl.BlockSpec((HALO, width), lambda i: (jnp.minimum((i + 1) * blocks, last), 0)),
    ]


def _fill_hbuf(hbuf, x_ref, xp_ref, xn_ref, g, shift, scale):
    i = pl.program_id(0)
    pos, n = _seq_pos(i)
    hp = _pre(xp_ref[...], g, shift, scale)
    hn = _pre(xn_ref[...], g, shift, scale)
    hbuf[0:HALO, :] = jnp.where(pos > 0, hp, 0.0).astype(BF16)
    hbuf[HALO:HALO + TM, :] = _pre(x_ref[...], g, shift, scale).astype(BF16)
    hbuf[HALO + TM:2 * HALO + TM, :] = jnp.where(pos < n - 1, hn, 0.0).astype(BF16)


def _shift_rows(x, s):
    rows = x.shape[0]
    if s == 0:
        return x[HALO:HALO + TM]
    return pltpu.roll(x, (-s) % rows, axis=0)[HALO:HALO + TM]


def _ffn_kernel(x_ref, xp_ref, xn_ref, mod_ref, g2_ref, g3_ref, wu_ref, cw_ref, cb_ref, wd_ref, o_ref,
                hbuf, acc):
    _fill_hbuf(hbuf, x_ref, xp_ref, xn_ref, g2_ref[...], mod_ref[3:4, :], mod_ref[4:5, :])
    h = hbuf[...]
    for j in range(D_FF // FF_CHUNK):
        parts = []
        for half in range(2):
            c0 = half * D_FF + j * FF_CHUNK
            u = jnp.dot(h, wu_ref[:, c0:c0 + FF_CHUNK], preferred_element_type=F32)
            cw = cw_ref[:, c0:c0 + FF_CHUNK]
            parts.append(_shift_rows(u, -1) * cw[0:1] + _shift_rows(u, 0) * cw[1:2]
                         + _shift_rows(u, 1) * cw[2:3] + cb_ref[:, c0:c0 + FF_CHUNK])
        act = (jax.nn.gelu(parts[0]) * parts[1]).astype(BF16)
        contrib = jnp.dot(act, wd_ref[j * FF_CHUNK:(j + 1) * FF_CHUNK, :], preferred_element_type=F32)
        if j == 0:
            acc[...] = contrib
        else:
            acc[...] += contrib
    o_ref[...] = _post(x_ref[...], acc[...], g3_ref[...], mod_ref[5:6, :])


def _ffn(x, mod, g2, g3, w_up, cw, cb, w_down):
    const = lambda i: (0, 0)
    return pl.pallas_call(
        _ffn_kernel,
        grid=(NT,),
        in_specs=_halo_specs(D_MODEL) + [
            pl.BlockSpec((None, 6, D_MODEL), lambda i: (_mod_row(i), 0, 0)),
            pl.BlockSpec((1, D_MODEL), const),
            pl.BlockSpec((1, D_MODEL), const),
            pl.BlockSpec((D_MODEL, 2 * D_FF), const, pipeline_mode=pl.Buffered(1)),
            pl.BlockSpec((3, 2 * D_FF), const),
            pl.BlockSpec((1, 2 * D_FF), const),
            pl.BlockSpec((D_FF, D_MODEL), const, pipeline_mode=pl.Buffered(1)),
        ],
        out_specs=pl.BlockSpec((TM, D_MODEL), lambda i: (i, 0)),
        out_shape=jax.ShapeDtypeStruct((N_TOK, D_MODEL), F32),
        scratch_shapes=[
            pltpu.VMEM((TM + 2 * HALO, D_MODEL), BF16),
            pltpu.VMEM((TM, D_MODEL), F32),
        ],
        compiler_params=_params(),
        name="conv_ffn",
    )(x, x, x, mod, g2, g3, w_up, cw, cb, w_down)


def _softplus(z):
    return jnp.maximum(z, 0.0) + jnp.log(1.0 + jnp.exp(-jnp.abs(z)))


def _lru_scan(a_scr, b_scr, hs_scr, carry, reverse):
    nblk = TM // 8

    def body(blk, h):
        blk = (nblk - 1 - blk) if reverse else blk
        base = pl.multiple_of(blk * 8, 8)
        for j in (range(7, -1, -1) if reverse else range(8)):
            r = pl.ds(base + j, 1)
            h = a_scr[r, :] * h + b_scr[r, :]
            hs_scr[r, :] = h
        return h

    carry[...] = lax.fori_loop(0, nblk, body, carry[...])


def _cd_in_kernel(x_ref, xp_ref, xn_ref, mod_ref, g_ref, w_ref, scw_ref, cw_ref, cb_ref, wg_ref, bg_ref,
                  lam_ref, h0_ref, yc_ref, gate_ref, hsf_ref, ar_ref, br_ref,
                  hbuf, a_scr, b_scr, hs_scr, carry):
    i = pl.program_id(0)
    pos, _ = _seq_pos(i)
    _fill_hbuf(hbuf, x_ref, xp_ref, xn_ref, g_ref[...], mod_ref[0:1, :], mod_ref[1:2, :])
    proj = jnp.dot(hbuf[...], w_ref[...], preferred_element_type=F32)
    xin = proj[:, 0:HALF]
    bg = proj[HALO:HALO + TM, HALF:2 * HALF]
    cg = proj[:, 2 * HALF:3 * HALF]
    xr = proj[:, 3 * HALF:4 * HALF]
    gb = proj[HALO:HALO + TM, 4 * HALF:5 * HALF]
    prod = cg * xin
    scw = scw_ref[...]
    yc = bg * (_shift_rows(prod, -1) * scw[0:1] + _shift_rows(prod, 0) * scw[1:2]
               + _shift_rows(prod, 1) * scw[2:3])
    yc_ref[...] = yc.astype(BF16)
    gate_ref[...] = jax.nn.gelu(gb)
    cw = cw_ref[...]
    xc = (_shift_rows(xr, -2) * cw[0:1] + _shift_rows(xr, -1) * cw[1:2] + _shift_rows(xr, 0) * cw[2:3]
          + _shift_rows(xr, 1) * cw[3:4] + cb_ref[...])
    gates = jax.nn.sigmoid(jnp.dot(xc.astype(BF16), wg_ref[...], preferred_element_type=F32) + bg_ref[...])
    for d in range(2):
        r = gates[:, 2 * d * HALF:(2 * d + 1) * HALF]
        ig = gates[:, (2 * d + 1) * HALF:(2 * d + 2) * HALF]
        log_a = (-LRU_C) * r * _softplus(-lam_ref[d:d + 1, :])
        a = jnp.exp(log_a)
        bval = jnp.sqrt(1.0 - jnp.exp(2.0 * log_a)) * (ig * xc)
        if d == 0:
            a_scr[...] = a
            b_scr[...] = bval
        else:
            ar_ref[...] = a
            br_ref[...] = bval

    @pl.when(pos == 0)
    def _():
        carry[...] = h0_ref[0:1, :]

    _lru_scan(a_scr, b_scr, hs_scr, carry, False)
    hsf_ref[...] = hs_scr[...]


def _cd_in(x, mod, g, w_in, sc_w, conv_w, conv_b, w_gates, b_gates, lru_lam, h0t):
    const = lambda i: (0, 0)
    tok = lambda dt: jax.ShapeDtypeStruct((N_TOK, HALF), dt)
    row = pl.BlockSpec((TM, HALF), lambda i: (i, 0))
    return pl.pallas_call(
        _cd_in_kernel,
        grid=(NT,),
        in_specs=_halo_specs(D_MODEL) + [
            pl.BlockSpec((None, 6, D_MODEL), lambda i: (_mod_row(i), 0, 0)),
            pl.BlockSpec((1, D_MODEL), const),
            pl.BlockSpec((D_MODEL, 5 * HALF), const),
            pl.BlockSpec((3, HALF), const),
            pl.BlockSpec((4, HALF), const),
            pl.BlockSpec((1, HALF), const),
            pl.BlockSpec((HALF, 4 * HALF), const),
            pl.BlockSpec((1, 4 * HALF), const),
            pl.BlockSpec((2, HALF), const),
            pl.BlockSpec((None, 2, HALF), lambda i: (_mod_row(i), 0, 0)),
        ],
        out_specs=[row, row, row, row, row],
        out_shape=[tok(BF16), tok(F32), tok(F32), tok(F32), tok(F32)],
        scratch_shapes=[
            pltpu.VMEM((TM + 2 * HALO, D_MODEL), BF16),
            pltpu.VMEM((TM, HALF), F32),
            pltpu.VMEM((TM, HALF), F32),
            pltpu.VMEM((TM, HALF), F32),
            pltpu.VMEM((1, HALF), F32),
        ],
        compiler_params=_params(),
        name="cd_in",
    )(x, x, x, mod, g, w_in, sc_w, conv_w, conv_b, w_gates, b_gates, lru_lam, h0t)


def _cd_out_kernel(ar_ref, br_ref, hsf_ref, gate_ref, yc_ref, x_ref, mod_ref, g_ref, wo_ref, h0_ref,
                   o_ref, fin_ref, hs_scr, carry):
    ti = NT - 1 - pl.program_id(0)
    pos, n = _seq_pos(ti)

    @pl.when(pos == n - 1)
    def _():
        carry[...] = h0_ref[1:2, :]

    _lru_scan(ar_ref, br_ref, hs_scr, carry, True)
    fin_ref[...] = jnp.broadcast_to(carry[...], (8, HALF))
    yd = (hsf_ref[...] + hs_scr[...]) * gate_ref[...]
    out = (jnp.dot(yc_ref[...], wo_ref[0:HALF, :], preferred_element_type=F32)
           + jnp.dot(yd.astype(BF16), wo_ref[HALF:2 * HALF, :], preferred_element_type=F32))
    o_ref[...] = _post(x_ref[...], out, g_ref[...], mod_ref[2:3, :])


def _cd_out(a_r, b_r, hs_f, gate, yc, x, mod, g, w_out, h0t):
    const = lambda i: (0, 0)
    rev = lambda i: (NT - 1 - i, 0)
    row = pl.BlockSpec((TM, HALF), rev)
    return pl.pallas_call(
        _cd_out_kernel,
        grid=(NT,),
        in_specs=[
            row, row, row, row, row,
            pl.BlockSpec((TM, D_MODEL), rev),
            pl.BlockSpec((None, 6, D_MODEL), lambda i: (_mod_row(NT - 1 - i), 0, 0)),
            pl.BlockSpec((1, D_MODEL), const),
            pl.BlockSpec((D_MODEL, D_MODEL), const),
            pl.BlockSpec((None, 2, HALF), lambda i: (_mod_row(NT - 1 - i), 0, 0)),
        ],
        out_specs=[
            pl.BlockSpec((TM, D_MODEL), rev),
            pl.BlockSpec((8, HALF), rev),
        ],
        out_shape=[
            jax.ShapeDtypeStruct((N_TOK, D_MODEL), F32),
            jax.ShapeDtypeStruct((NT * 8, HALF), F32),
        ],
        scratch_shapes=[
            pltpu.VMEM((TM, HALF), F32),
            pltpu.VMEM((1, HALF), F32),
        ],
        compiler_params=_params(),
        name="cd_out",
    )(a_r, b_r, hs_f, gate, yc, x, mod, g, w_out, h0t)


def kernel(x_prompt, x_sample, cache_attn_k, cache_attn_v, state_s5, state_rglru, c, c_ctx, w_mod, b_mod, norm_g, w_in_ab, w_out_ab, s5_lam_re, s5_lam_im, s5_log_dt, s5_b, s5_c, s5_d, s5_w_glu, s5_b_glu, da_lam, da_g, w_in_cd, w_out_cd, sc_conv_w, lru_conv_w, lru_conv_b, lru_w_a, lru_b_a, lru_w_x, lru_b_x, lru_lam, ffn_w_up, ffn_conv_w, ffn_conv_b, ffn_w_down):
    x = jnp.concatenate([x_prompt.reshape(NT_P * TM, D_MODEL), x_sample.reshape(NT_S * TM, D_MODEL)], axis=0)
    cv8 = jnp.zeros((8, D_MODEL), F32).at[0].set(c_ctx).at[1:1 + DEC_BATCH].set(c)
    mod = _modulation(cv8, w_mod, b_mod)
    cos_t, sin_t = _rope_tables()
    new_k = new_v = new_s5 = new_lru = None
    for l in range(DEPTH):
        e = l // 2
        g = norm_g[l].reshape(4, 1, D_MODEL)
        if l % 2 == 0:
            lam_init = 0.8 - 0.6 * math.exp(-0.3 * l)
            u, qkv, kv = _ab_in(x, mod[l], g[0], w_in_ab[e].astype(BF16), cos_t, sin_t)
            m1, m2, m3, coef2 = _s5_params(s5_lam_re[e], s5_lam_im[e], s5_log_dt[e], s5_b[e], s5_c[e])
            ys5, new_s5 = _s5(u, m1, m2, m3, coef2, state_s5[:, e], s5_d[e])
            yb_p = _attention(qkv, 0, qkv, qkv, 1, BATCH, 1, SEQ, da_lam[e], da_g[e], lam_init)
            lat = qkv[NT_P * TM:].reshape(DEC_BATCH, DEC_SEQ, 3 * HALF)
            keys = jnp.concatenate(
                [lat[:, :, HALF:2 * HALF], cache_attn_k[:, e].reshape(DEC_BATCH, PAST_LEN, HALF).astype(BF16)], axis=1)
            vals = jnp.concatenate(
                [lat[:, :, 2 * HALF:], cache_attn_v[:, e].reshape(DEC_BATCH, PAST_LEN, HALF).astype(BF16)], axis=1)
            yb_s = _attention(qkv, NT_P, keys, vals, 0, DEC_BATCH, TPS, DEC_SEQ + PAST_LEN,
                              da_lam[e], da_g[e], lam_init)
            yb = jnp.concatenate([yb_p, yb_s], axis=0)
            x = _ab_out(ys5, yb, x, mod[l], g[1], s5_w_glu[e].astype(BF16), s5_b_glu[e].reshape(1, HALF),
                        w_out_ab[e].astype(BF16))
            new_k = kv[:, :HALF].reshape(BATCH, 1, SEQ, DA_HEADS, 2, DA_DK)
            new_v = kv[:, HALF:].reshape(BATCH, 1, SEQ, DA_HEADS, DA_DV)
            new_s5 = new_s5[:, None]
        else:
            eye = jnp.eye(LRU_BLOCKS, dtype=F32)
            dense = lambda w: jnp.einsum('kcd,kl->kcld', w, eye).reshape(LRU_WIDTH, LRU_WIDTH)
            w_gates = jnp.concatenate([dense(lru_w_a[e, 0]), dense(lru_w_x[e, 0]),
                                       dense(lru_w_a[e, 1]), dense(lru_w_x[e, 1])], axis=1).astype(BF16)
            b_gates = jnp.concatenate([lru_b_a[e, 0], lru_b_x[e, 0], lru_b_a[e, 1], lru_b_x[e, 1]]).reshape(1, 4 * HALF)
            h0t = jnp.zeros((8, 2, HALF), F32).at[1:1 + DEC_BATCH].set(state_rglru[:, e])
            yc, gate, hs_f, a_r, b_r = _cd_in(x, mod[l], g[0], w_in_cd[e].astype(BF16), sc_conv_w[e], lru_conv_w[e],
                                              lru_conv_b[e].reshape(1, HALF), w_gates, b_gates, lru_lam[e], h0t)
            x, fin_r = _cd_out(a_r, b_r, hs_f, gate, yc, x, mod[l], g[1], w_out_cd[e].astype(BF16), h0t)
            fin_f = hs_f[:NT_P * TM].reshape(BATCH, SEQ, HALF)[:, SEQ - 1]
            fin_r = fin_r.reshape(NT, 8, HALF)[:NT_P, 0]
            new_lru = jnp.stack([fin_f, fin_r], axis=1)[:, None]
        x = _ffn(x, mod[l], g[2], g[3], ffn_w_up[l].astype(BF16), ffn_conv_w[l], ffn_conv_b[l].reshape(1, 2 * D_FF),
                 ffn_w_down[l].astype(BF16))
    y_prompt = x[:NT_P * TM].reshape(BATCH, SEQ, D_MODEL)
    y_sample = x[NT_P * TM:].reshape(DEC_BATCH, DEC_SEQ, D_MODEL)
    return (y_prompt, y_sample, new_k, new_v, new_s5, new_lru)
```

```python
import functools
import math

import jax
import jax.numpy as jnp
from jax import lax
from jax.experimental import pallas as pl
from jax.experimental.pallas import tpu as pltpu

D_MODEL = 1024
BATCH = 32
SEQ = 256
DEPTH = 2
DEC_BATCH = 4
DEC_SEQ = 2048
PAST_LEN = 512
GRID_W = 64
HALF = D_MODEL // 2
S5_P = 16
S5_G = HALF // S5_P
S5_N = 64
DA_DK = 64
DA_DV = 2 * DA_DK
DA_HEADS = HALF // DA_DV
ROPE_THETA = 10000.0
ROPE_F = DA_DK // 4
LRU_WIDTH = HALF
LRU_BLOCKS = 8
LRU_BS = LRU_WIDTH // LRU_BLOCKS
LRU_C = 8.0
D_FF = 2816
EPS = 1e-6

F32 = jnp.float32
BF16 = jnp.bfloat16
HIGHEST = lax.Precision.HIGHEST

TM = 256
NT_P = BATCH * SEQ // TM
TPS = DEC_SEQ // TM
NT_S = DEC_BATCH * TPS
NT = NT_P + NT_S
N_TOK = NT * TM
HALO = 16
S5_T = 16
S5_K = S5_T * S5_P
NC_P = SEQ // S5_T
NC_S = DEC_SEQ // S5_T
SB = 8
FF_CHUNK = 256
VMEM_LIMIT = 56 * 1024 * 1024


def _mod_row(i):
    return jnp.where(i < NT_P, 0, 1 + (i - NT_P) // TPS)


def _seq_pos(i):
    return jnp.where(i < NT_P, 0, (i - NT_P) % TPS), jnp.where(i < NT_P, 1, TPS)


def _split_specs(width):
    return [
        pl.BlockSpec((TM, width), lambda i: (jnp.minimum(i, NT_P - 1), 0)),
        pl.BlockSpec((TM, width), lambda i: (jnp.maximum(i - NT_P, 0), 0)),
    ]


def _pick(p_ref, s_ref):
    return jnp.where(pl.program_id(0) < NT_P, p_ref[...], s_ref[...])


def _params(sem=("arbitrary",)):
    return pltpu.CompilerParams(dimension_semantics=sem, vmem_limit_bytes=VMEM_LIMIT)


def _rms(x, g):
    ms = jnp.mean(x * x, axis=-1, keepdims=True)
    return x * lax.rsqrt(ms + EPS) * g


def _pre(x, g, shift, scale):
    return _rms(x, g) * (1.0 + scale) + shift


def _mod_kernel(cv_ref, w_ref, b_ref, o_ref):
    cv = cv_ref[...]
    s = cv * jax.nn.sigmoid(cv)
    o_ref[...] = jnp.dot(s, w_ref[...], precision=HIGHEST, preferred_element_type=F32) + b_ref[...]


def _modulation(cv8, w_mod, b_mod):
    nb = 1536
    out = pl.pallas_call(
        _mod_kernel,
        grid=(DEPTH, 6 * D_MODEL // nb),
        in_specs=[
            pl.BlockSpec((8, D_MODEL), lambda l, j: (0, 0)),
            pl.BlockSpec((None, D_MODEL, nb), lambda l, j: (l, 0, j)),
            pl.BlockSpec((None, 1, nb), lambda l, j: (l, 0, j)),
        ],
        out_specs=pl.BlockSpec((None, 8, nb), lambda l, j: (l, 0, j)),
        out_shape=jax.ShapeDtypeStruct((DEPTH, 8, 6 * D_MODEL), F32),
        compiler_params=_params(("arbitrary", "arbitrary")),
        name="modulation",
    )(cv8, w_mod, b_mod.reshape(DEPTH, 1, 6 * D_MODEL))
    return out.reshape(DEPTH, 8, 6, D_MODEL)


def _rope_partner(x):
    lane = lax.broadcasted_iota(jnp.int32, (1, HALF), 1)
    first = (lane % (2 * ROPE_F)) < ROPE_F
    return jnp.where(first, pltpu.roll(x, HALF - ROPE_F, axis=1), pltpu.roll(x, ROPE_F, axis=1))


def _ab_in_kernel(xp_ref, xs_ref, mod_ref, g_ref, w_ref, cos_ref, sin_ref, u_ref, qkv_ref, k_ref, v_ref):
    i = pl.program_id(0)
    h = _pre(_pick(xp_ref, xs_ref), g_ref[...], mod_ref[0:1, :], mod_ref[1:2, :])
    proj = jnp.dot(h.astype(BF16), w_ref[...], preferred_element_type=F32)
    u_ref[...] = proj[:, 0:HALF]
    q = proj[:, HALF:2 * HALF]
    k = proj[:, 2 * HALF:3 * HALF]
    v = proj[:, 3 * HALF:4 * HALF]
    cos = cos_ref[...]
    sin = sin_ref[...]
    qr = q * cos + _rope_partner(q) * sin
    kr = k * cos + _rope_partner(k) * sin
    qkv_ref[:, 0:HALF] = (qr * (1.0 / math.sqrt(DA_DK))).astype(BF16)
    qkv_ref[:, HALF:2 * HALF] = kr.astype(BF16)
    qkv_ref[:, 2 * HALF:3 * HALF] = v.astype(BF16)

    @pl.when(i < NT_P)
    def _():
        k_ref[...] = k
        v_ref[...] = v


def _ab_in(xp, xs, mod, g, w, cos_t, sin_t):
    def rope_tile(i):
        return jnp.where(i < NT_P, 0, 1 + (i - NT_P) % TPS)

    return pl.pallas_call(
        _ab_in_kernel,
        grid=(NT,),
        in_specs=_split_specs(D_MODEL) + [
            pl.BlockSpec((None, 6, D_MODEL), lambda i: (_mod_row(i), 0, 0)),
            pl.BlockSpec((1, D_MODEL), lambda i: (0, 0)),
            pl.BlockSpec((D_MODEL, 4 * HALF), lambda i: (0, 0)),
            pl.BlockSpec((TM, HALF), lambda i: (rope_tile(i), 0)),
            pl.BlockSpec((TM, HALF), lambda i: (rope_tile(i), 0)),
        ],
        out_specs=[
            pl.BlockSpec((TM, HALF), lambda i: (i, 0)),
            pl.BlockSpec((TM, 3 * HALF), lambda i: (i, 0)),
            pl.BlockSpec((TM, HALF), lambda i: (jnp.minimum(i, NT_P - 1), 0)),
            pl.BlockSpec((TM, HALF), lambda i: (jnp.minimum(i, NT_P - 1), 0)),
        ],
        out_shape=[
            jax.ShapeDtypeStruct((N_TOK, HALF), F32),
            jax.ShapeDtypeStruct((N_TOK, 3 * HALF), BF16),
            jax.ShapeDtypeStruct((NT_P * TM, HALF), F32),
            jax.ShapeDtypeStruct((NT_P * TM, HALF), F32),
        ],
        compiler_params=_params(),
        name="ab_in",
    )(xp, xs, mod, g, w, cos_t, sin_t)


def _rope_tables():
    pos = jnp.arange(DEC_SEQ)
    t_row = (pos // GRID_W).astype(F32)
    t_col = (pos % GRID_W).astype(F32)
    inv = ROPE_THETA ** (-jnp.arange(ROPE_F, dtype=F32) / ROPE_F)
    ang = jnp.stack([t_row[:, None] * inv, t_col[:, None] * inv], axis=1)
    cos, sin = jnp.cos(ang), jnp.sin(ang)
    cos64 = jnp.stack([cos, cos], axis=2).reshape(DEC_SEQ, DA_DK)
    sin64 = jnp.stack([-sin, sin], axis=2).reshape(DEC_SEQ, DA_DK)
    reps = HALF // DA_DK
    cos_t = jnp.concatenate([jnp.ones((TM, HALF), F32), jnp.tile(cos64, (1, reps))], axis=0)
    sin_t = jnp.concatenate([jnp.zeros((TM, HALF), F32), jnp.tile(sin64, (1, reps))], axis=0)
    return cos_t, sin_t


def _s5_param_kernel(lre_ref, lim_ref, ldt_ref, bt_ref, c_ref, f_ref, e_ref, k_ref, coef_ref):
    for d in range(2):
        lre = lre_ref[d:d + 1, :]
        lim = lim_ref[d:d + 1, :]
        dt = jnp.exp(ldt_ref[d:d + 1, :])
        mag = jnp.exp(lre * dt)
        ang = lim * dt
        ar = mag * jnp.cos(ang)
        ai = mag * jnp.sin(ang)
        den = lre * lre + lim * lim
        nr = ar - 1.0
        fr = (nr * lre + ai * lim) / den
        fi = (ai * lre - nr * lim) / den
        btr = bt_ref[d, 0]
        bti = bt_ref[d, 1]
        bfr = fr * btr - fi * bti
        bfi = fr * bti + fi * btr
        cr = c_ref[d, 0]
        ci = c_ref[d, 1]
        pr = jnp.ones_like(ar)
        pi = jnp.zeros_like(ar)
        for tau in range(S5_T + 1):
            f_ref[d, tau] = jnp.concatenate([cr * pr - ci * pi, -(cr * pi + ci * pr)], axis=1)
            if tau < S5_T:
                e_ref[d, tau] = jnp.concatenate([pr * bfr - pi * bfi, pr * bfi + pi * bfr], axis=1)
            pr, pi = pr * ar - pi * ai, pr * ai + pi * ar
        e0 = e_ref[d, 0]
        for tau in range(S5_T):
            k_ref[d, tau] = lax.dot_general(f_ref[d, tau], e0, (((1,), (1,)), ((), ())),
                                            precision=HIGHEST, preferred_element_type=F32)
        qr = jnp.ones_like(ar)
        qi = jnp.zeros_like(ar)
        for _ in range(S5_T):
            qr, qi = qr * ar - qi * ai, qr * ai + qi * ar
        coef_ref[d] = jnp.concatenate([qr, qi], axis=0)


def _s5_params(lam_re, lam_im, log_dt, b_w, c_w):
    lre = lam_re.transpose(1, 0, 2)
    lim = lam_im.transpose(1, 0, 2)
    ldt = jnp.broadcast_to(log_dt[:, :, None], (2, S5_G, S5_N)).transpose(1, 0, 2)
    bt = b_w.transpose(2, 0, 1, 4, 3)
    cc = c_w.transpose(2, 0, 1, 3, 4)
    vec = pl.BlockSpec((None, 2, S5_N), lambda g: (g, 0, 0))
    mat = pl.BlockSpec((None, 2, 2, S5_P, S5_N), lambda g: (g, 0, 0, 0, 0))
    f, e, k, coef = pl.pallas_call(
        _s5_param_kernel,
        grid=(S5_G,),
        in_specs=[vec, vec, vec, mat, mat],
        out_specs=[
            pl.BlockSpec((None, 2, S5_T + 1, S5_P, 2 * S5_N), lambda g: (g, 0, 0, 0, 0)),
            pl.BlockSpec((None, 2, S5_T, S5_P, 2 * S5_N), lambda g: (g, 0, 0, 0, 0)),
            pl.BlockSpec((None, 2, S5_T, S5_P, S5_P), lambda g: (g, 0, 0, 0, 0)),
            pl.BlockSpec((None, 2, 2, S5_N), lambda g: (g, 0, 0, 0)),
        ],
        out_shape=[
            jax.ShapeDtypeStruct((S5_G, 2, S5_T + 1, S5_P, 2 * S5_N), F32),
            jax.ShapeDtypeStruct((S5_G, 2, S5_T, S5_P, 2 * S5_N), F32),
            jax.ShapeDtypeStruct((S5_G, 2, S5_T, S5_P, S5_P), F32),
            jax.ShapeDtypeStruct((S5_G, 2, 2, S5_N), F32),
        ],
        compiler_params=_params(),
        name="s5_params",
    )(lre, lim, ldt, bt, cc)

    s_idx = jnp.arange(S5_T)[:, None]
    t_idx = jnp.arange(S5_T)[None, :]
    diff = t_idx - s_idx
    kf = jnp.where((diff >= 0)[None, :, :, None, None], k[:, 0][:, jnp.clip(diff, 0, S5_T - 1)], 0.0)
    kr = jnp.where((diff <= 0)[None, :, :, None, None], k[:, 1][:, jnp.clip(-diff, 0, S5_T - 1)], 0.0)
    m1 = (kf + kr).transpose(0, 1, 4, 2, 3).reshape(S5_G, S5_K, S5_K)
    ef = e[:, 0, ::-1]
    er = e[:, 1]
    m2 = jnp.concatenate([ef[..., :S5_N], er[..., :S5_N], ef[..., S5_N:], er[..., S5_N:]], axis=-1)
    m2 = m2.reshape(S5_G, S5_K, 4 * S5_N)
    ff = f[:, 0, 1:S5_T + 1]
    fr_ = f[:, 1, S5_T:0:-1]
    rows = jnp.concatenate([ff[..., :S5_N], fr_[..., :S5_N], ff[..., S5_N:], fr_[..., S5_N:]], axis=-1)
    m3 = rows.transpose(0, 3, 1, 2).reshape(S5_G, 4 * S5_N, S5_K)
    coef2 = jnp.stack([jnp.concatenate([coef[:, 0, 0], coef[:, 1, 0]], axis=-1),
                       jnp.concatenate([coef[:, 0, 1], coef[:, 1, 1]], axis=-1)], axis=1)
    return m1.astype(BF16), m2.astype(BF16), m3.astype(BF16), coef2


def _s5_kernel(up_ref, us_ref, m1_ref, m2_ref, m3_ref, coef_ref, h0_ref, d_ref,
               yp_ref, ys_ref, fin_ref, s_scr, hp_scr):
    a_re = coef_ref[0:1, :]
    a_im = coef_ref[1:2, :]
    m1 = m1_ref[...]
    m2 = m2_ref[...]
    m3 = m3_ref[...]
    dvec = d_ref[...]
    ln = 2 * S5_N

    def run(u_ref, y_ref, nb, nc, p0, q0, dynamic):
        rows = nb * nc
        u = u_ref[...]
        ub = u.astype(BF16)
        y_ref[...] = jnp.dot(ub, m1, preferred_element_type=F32) + u * dvec
        s_scr[0:rows, :] = jnp.dot(ub, m2, preferred_element_type=F32)
        lane = lax.broadcasted_iota(jnp.int32, (nb, ln), 1)
        fwd = lane < S5_N
        are = jnp.broadcast_to(a_re, (nb, ln))
        aim = jnp.broadcast_to(a_im, (nb, ln))

        def step(kf, kr, p, q):
            hp_scr[pl.ds(kf, nb), 0:S5_N] = p[:, 0:S5_N]
            hp_scr[pl.ds(kr, nb), S5_N:ln] = p[:, S5_N:ln]
            hp_scr[pl.ds(kf, nb), ln:ln + S5_N] = q[:, 0:S5_N]
            hp_scr[pl.ds(kr, nb), ln + S5_N:2 * ln] = q[:, S5_N:ln]
            s_re = jnp.where(fwd, s_scr[pl.ds(kf, nb), 0:ln], s_scr[pl.ds(kr, nb), 0:ln])
            s_im = jnp.where(fwd, s_scr[pl.ds(kf, nb), ln:2 * ln], s_scr[pl.ds(kr, nb), ln:2 * ln])
            return are * p - aim * q + s_re, are * q + aim * p + s_im

        if dynamic:
            def body(c, carry):
                kf = pl.multiple_of(c * nb, nb)
                kr = pl.multiple_of((nc - 1 - c) * nb, nb)
                return step(kf, kr, *carry)
            p, q = lax.fori_loop(0, nc, body, (p0, q0))
        else:
            p, q = p0, q0
            for c in range(nc):
                p, q = step(c * nb, (nc - 1 - c) * nb, p, q)
        y_ref[...] += jnp.dot(hp_scr[0:rows, :].astype(BF16), m3, preferred_element_type=F32)
        return p, q

    zero = jnp.zeros((BATCH, ln), F32)
    p, q = run(up_ref, yp_ref, BATCH, NC_P, zero, zero, False)
    fin_ref[0] = p
    fin_ref[1] = q
    run(us_ref, ys_ref, SB, NC_S, h0_ref[0], h0_ref[1], True)


def _s5(u, m1, m2, m3, coef2, h0, s5_d):
    up = u[:NT_P * TM].reshape(BATCH, NC_P, S5_T, S5_G, S5_P).transpose(3, 1, 0, 2, 4)
    up = up.reshape(S5_G, NC_P * BATCH, S5_K)
    us = u[NT_P * TM:].reshape(DEC_BATCH, NC_S, S5_T, S5_G, S5_P).transpose(3, 1, 0, 2, 4)
    us = jnp.pad(us, ((0, 0), (0, 0), (0, SB - DEC_BATCH), (0, 0), (0, 0))).reshape(S5_G, NC_S * SB, S5_K)
    dvec = jnp.tile(s5_d.reshape(S5_G, 1, S5_P), (1, S5_T, 1)).reshape(S5_G, 1, S5_K)
    h0g = h0.transpose(3, 2, 0, 1, 4).reshape(S5_G, 2, DEC_BATCH, 2 * S5_N)
    h0g = jnp.pad(h0g, ((0, 0), (0, 0), (0, SB - DEC_BATCH), (0, 0)))
    mat = pl.BlockSpec((None, S5_K, S5_K), lambda g: (g, 0, 0))
    yp, ys, fin = pl.pallas_call(
        _s5_kernel,
        grid=(S5_G,),
        in_specs=[
            pl.BlockSpec((None, NC_P * BATCH, S5_K), lambda g: (g, 0, 0)),
            pl.BlockSpec((None, NC_S * SB, S5_K), lambda g: (g, 0, 0)),
            mat, mat, mat,
            pl.BlockSpec((None, 2, 2 * S5_N), lambda g: (g, 0, 0)),
            pl.BlockSpec((None, 2, SB, 2 * S5_N), lambda g: (g, 0, 0, 0)),
            pl.BlockSpec((None, 1, S5_K), lambda g: (g, 0, 0)),
        ],
        out_specs=[
            pl.BlockSpec((None, NC_P * BATCH, S5_K), lambda g: (g, 0, 0)),
            pl.BlockSpec((None, NC_S * SB, S5_K), lambda g: (g, 0, 0)),
            pl.BlockSpec((None, 2, BATCH, 2 * S5_N), lambda g: (g, 0, 0, 0)),
        ],
        out_shape=[
            jax.ShapeDtypeStruct((S5_G, NC_P * BATCH, S5_K), F32),
            jax.ShapeDtypeStruct((S5_G, NC_S * SB, S5_K), F32),
            jax.ShapeDtypeStruct((S5_G, 2, BATCH, 2 * S5_N), F32),
        ],
        scratch_shapes=[
            pltpu.VMEM((NC_S * SB, 4 * S5_N), F32),
            pltpu.VMEM((NC_S * SB, 4 * S5_N), F32),
        ],
        compiler_params=_params(),
        name="s5",
    )(up, us, m1, m2, m3, coef2, h0g, dvec)
    yp = yp.reshape(S5_G, NC_P, BATCH, S5_T, S5_P).transpose(2, 1, 3, 0, 4).reshape(NT_P * TM, HALF)
    ys = ys.reshape(S5_G, NC_S, SB, S5_T, S5_P)[:, :, :DEC_BATCH].transpose(2, 1, 3, 0, 4)
    y = jnp.concatenate([yp, ys.reshape(NT_S * TM, HALF)], axis=0)
    new_s5 = fin.reshape(S5_G, 2, BATCH, 2, S5_N).transpose(2, 3, 1, 0, 4)
    return y, new_s5


def _attn_kernel(*refs, lam_init, has_ctx):
    if has_ctx:
        q_ref, k_ref, v_ref, kc_ref, vc_ref, dl_ref, g_ref, o_ref = refs
    else:
        q_ref, k_ref, v_ref, dl_ref, g_ref, o_ref = refs
    dl = dl_ref[...]
    lam = (jnp.exp(jnp.sum(dl[0:1] * dl[1:2], keepdims=True))
           - jnp.exp(jnp.sum(dl[2:3] * dl[3:4], keepdims=True)) + lam_init)
    lane = lax.broadcasted_iota(jnp.int32, (1, DA_DV), 1)
    first = lane < DA_DK
    g = g_ref[...]
    nt = (((1,), (1,)), ((), ()))
    for h in range(DA_HEADS):
        cols = slice(h * DA_DV, (h + 1) * DA_DV)
        qh = q_ref[:, cols]
        zero = jnp.zeros_like(qh)
        es, rs = [], []
        for m in range(2):
            qm = jnp.where(first if m == 0 else jnp.logical_not(first), qh, zero)
            s = [lax.dot_general(qm, k_ref[:, cols], nt, preferred_element_type=F32)]
            if has_ctx:
                s.append(lax.dot_general(qm, kc_ref[:, cols], nt, preferred_element_type=F32))
            mx = functools.reduce(jnp.maximum, [jnp.max(x, axis=-1, keepdims=True) for x in s])
            e = [jnp.exp(x - mx) for x in s]
            es.append(e)
            rs.append(1.0 / sum(jnp.sum(x, axis=-1, keepdims=True) for x in e))
        r1 = lam * rs[1]
        w = [(e0 * rs[0] - e1 * r1).astype(BF16) for e0, e1 in zip(es[0], es[1])]
        o = jnp.dot(w[0], v_ref[:, cols], preferred_element_type=F32)
        if has_ctx:
            o = o + jnp.dot(w[1], vc_ref[:, cols], preferred_element_type=F32)
        o_ref[:, cols] = (_rms(o, g) * (1.0 - lam_init)).astype(BF16)


def _attention(qkv, q_tile0, nb, nq, lk, ctx, da_lam, da_g, lam_init):
    kb0 = q_tile0 * TM // lk
    in_specs = [
        pl.BlockSpec((TM, HALF), lambda b, j: (q_tile0 + b * nq + j, 0)),
        pl.BlockSpec((lk, HALF), lambda b, j: (kb0 + b, 1)),
        pl.BlockSpec((lk, HALF), lambda b, j: (kb0 + b, 2)),
    ]
    args = [qkv, qkv, qkv]
    if ctx is not None:
        in_specs += [pl.BlockSpec((None, PAST_LEN, HALF), lambda b, j: (b, 0, 0))] * 2
        args += list(ctx)
    in_specs += [
        pl.BlockSpec((4, DA_DK), lambda b, j: (0, 0)),
        pl.BlockSpec((1, DA_DV), lambda b, j: (0, 0)),
    ]
    return pl.pallas_call(
        functools.partial(_attn_kernel, lam_init=lam_init, has_ctx=ctx is not None),
        grid=(nb, nq),
        in_specs=in_specs,
        out_specs=pl.BlockSpec((TM, HALF), lambda b, j: (b * nq + j, 0)),
        out_shape=jax.ShapeDtypeStruct((nb * nq * TM, HALF), BF16),
        compiler_params=_params(("arbitrary", "arbitrary")),
        name="diff_attn_%d" % lk,
    )(*args, da_lam, da_g.reshape(1, DA_DV))


def _post(x, y, g, gate):
    return x + gate * _rms(y, g)


def _ab_out_kernel(ys_ref, ybp_ref, ybs_ref, xp_ref, xs_ref, mod_ref, g_ref, wg_ref, bg_ref, wo_ref, o_ref):
    ys = jax.nn.gelu(ys_ref[...])
    glu = jnp.dot(ys.astype(BF16), wg_ref[...], preferred_element_type=F32) + bg_ref[...]
    ya = ys * jax.nn.sigmoid(glu)
    out = (jnp.dot(ya.astype(BF16), wo_ref[0:HALF, :], preferred_element_type=F32)
           + jnp.dot(_pick(ybp_ref, ybs_ref), wo_ref[HALF:2 * HALF, :], preferred_element_type=F32))
    o_ref[...] = _post(_pick(xp_ref, xs_ref), out, g_ref[...], mod_ref[2:3, :])


def _ab_out(ys5, yb_p, yb_s, xp, xs, mod, g, w_glu, b_glu, w_out):
    return pl.pallas_call(
        _ab_out_kernel,
        grid=(NT,),
        in_specs=[pl.BlockSpec((TM, HALF), lambda i: (i, 0))] + _split_specs(HALF) + _split_specs(D_MODEL) + [
            pl.BlockSpec((None, 6, D_MODEL), lambda i: (_mod_row(i), 0, 0)),
            pl.BlockSpec((1, D_MODEL), lambda i: (0, 0)),
            pl.BlockSpec((HALF, HALF), lambda i: (0, 0)),
            pl.BlockSpec((1, HALF), lambda i: (0, 0)),
            pl.BlockSpec((D_MODEL, D_MODEL), lambda i: (0, 0)),
        ],
        out_specs=pl.BlockSpec((TM, D_MODEL), lambda i: (i, 0)),
        out_shape=jax.ShapeDtypeStruct((N_TOK, D_MODEL), F32),
        compiler_params=_params(),
        name="ab_out",
    )(ys5, yb_p, yb_s, xp, xs, mod, g, w_glu, b_glu, w_out)


def _halo_specs(width):
    blocks = TM // HALO
    last = N_TOK // HALO - 1
    return [
        pl.BlockSpec((TM, width), lambda i: (i, 0)),
        pl.BlockSpec((HALO, width), lambda i: (jnp.maximum(i * blocks - 1, 0), 0)),
        pl.BlockSpec((HALO, width), lambda i: (jnp.minimum((i + 1) * blocks, last), 0)),
    ]


def _fill_hbuf(hbuf, x_ref, xp_ref, xn_ref, g, shift, scale):
    i = pl.program_id(0)
    pos, n = _seq_pos(i)
    hp = _pre(xp_ref[...], g, shift, scale)
    hn = _pre(xn_ref[...], g, shift, scale)
    hbuf[0:HALO, :] = jnp.where(pos > 0, hp, 0.0).astype(BF16)
    hbuf[HALO:HALO + TM, :] = _pre(x_ref[...], g, shift, scale).astype(BF16)
    hbuf[HALO + TM:2 * HALO + TM, :] = jnp.where(pos < n - 1, hn, 0.0).astype(BF16)


def _shift_rows(x, s):
    rows = x.shape[0]
    if s == 0:
        return x[HALO:HALO + TM]
    return pltpu.roll(x, (-s) % rows, axis=0)[HALO:HALO + TM]


def _ffn_kernel(x_ref, xp_ref, xn_ref, mod_ref, g2_ref, g3_ref, wu_ref, cw_ref, cb_ref, wd_ref, *rest, split):
    hbuf, acc = rest[-2:]
    _fill_hbuf(hbuf, x_ref, xp_ref, xn_ref, g2_ref[...], mod_ref[3:4, :], mod_ref[4:5, :])
    h = hbuf[...]
    for j in range(D_FF // FF_CHUNK):
        parts = []
        for half in range(2):
            c0 = half * D_FF + j * FF_CHUNK
            u = jnp.dot(h, wu_ref[:, c0:c0 + FF_CHUNK], preferred_element_type=F32)
            cw = cw_ref[:, c0:c0 + FF_CHUNK]
            parts.append(_shift_rows(u, -1) * cw[0:1] + _shift_rows(u, 0) * cw[1:2]
                         + _shift_rows(u, 1) * cw[2:3] + cb_ref[:, c0:c0 + FF_CHUNK])
        act = (jax.nn.gelu(parts[0]) * parts[1]).astype(BF16)
        contrib = jnp.dot(act, wd_ref[j * FF_CHUNK:(j + 1) * FF_CHUNK, :], preferred_element_type=F32)
        if j == 0:
            acc[...] = contrib
        else:
            acc[...] += contrib
    res = _post(x_ref[...], acc[...], g3_ref[...], mod_ref[5:6, :])
    if split:
        i = pl.program_id(0)

        @pl.when(i < NT_P)
        def _():
            rest[0][...] = res

        @pl.when(i >= NT_P)
        def _():
            rest[1][...] = res
    else:
        rest[0][...] = res


def _ffn(x, mod, g2, g3, w_up, cw, cb, w_down, split):
    const = lambda i: (0, 0)
    if split:
        out_specs = _split_specs(D_MODEL)
        out_shape = [jax.ShapeDtypeStruct((NT_P * TM, D_MODEL), F32), jax.ShapeDtypeStruct((NT_S * TM, D_MODEL), F32)]
    else:
        out_specs = pl.BlockSpec((TM, D_MODEL), lambda i: (i, 0))
        out_shape = jax.ShapeDtypeStruct((N_TOK, D_MODEL), F32)
    return pl.pallas_call(
        functools.partial(_ffn_kernel, split=split),
        grid=(NT,),
        in_specs=_halo_specs(D_MODEL) + [
            pl.BlockSpec((None, 6, D_MODEL), lambda i: (_mod_row(i), 0, 0)),
            pl.BlockSpec((1, D_MODEL), const),
            pl.BlockSpec((1, D_MODEL), const),
            pl.BlockSpec((D_MODEL, 2 * D_FF), const, pipeline_mode=pl.Buffered(1)),
            pl.BlockSpec((3, 2 * D_FF), const),
            pl.BlockSpec((1, 2 * D_FF), const),
            pl.BlockSpec((D_FF, D_MODEL), const, pipeline_mode=pl.Buffered(1)),
        ],
        out_specs=out_specs,
        out_shape=out_shape,
        scratch_shapes=[
            pltpu.VMEM((TM + 2 * HALO, D_MODEL), BF16),
            pltpu.VMEM((TM, D_MODEL), F32),
        ],
        compiler_params=_params(),
        name="conv_ffn",
    )(x, x, x, mod, g2, g3, w_up, cw, cb, w_down)


def _softplus(z):
    return jnp.maximum(z, 0.0) + jnp.log(1.0 + jnp.exp(-jnp.abs(z)))


def _lru_scan(a_scr, b_scr, hs_scr, carry, reverse):
    nblk = TM // 8

    def body(blk, h):
        blk = (nblk - 1 - blk) if reverse else blk
        base = pl.multiple_of(blk * 8, 8)
        for j in (range(7, -1, -1) if reverse else range(8)):
            r = pl.ds(base + j, 1)
            h = a_scr[r, :] * h + b_scr[r, :]
            hs_scr[r, :] = h
        return h

    carry[...] = lax.fori_loop(0, nblk, body, carry[...])


def _cd_in_kernel(x_ref, xp_ref, xn_ref, mod_ref, g_ref, w_ref, scw_ref, cw_ref, cb_ref, wg_ref, bg_ref,
                  lam_ref, h0_ref, yc_ref, gate_ref, hsf_ref, ar_ref, br_ref,
                  hbuf, a_scr, b_scr, hs_scr, carry):
    i = pl.program_id(0)
    pos, _ = _seq_pos(i)
    _fill_hbuf(hbuf, x_ref, xp_ref, xn_ref, g_ref[...], mod_ref[0:1, :], mod_ref[1:2, :])
    proj = jnp.dot(hbuf[...], w_ref[...], preferred_element_type=F32)
    xin = proj[:, 0:HALF]
    bg = proj[HALO:HALO + TM, HALF:2 * HALF]
    cg = proj[:, 2 * HALF:3 * HALF]
    xr = proj[:, 3 * HALF:4 * HALF]
    gb = proj[HALO:HALO + TM, 4 * HALF:5 * HALF]
    prod = cg * xin
    scw = scw_ref[...]
    yc = bg * (_shift_rows(prod, -1) * scw[0:1] + _shift_rows(prod, 0) * scw[1:2]
               + _shift_rows(prod, 1) * scw[2:3])
    yc_ref[...] = yc.astype(BF16)
    gate_ref[...] = jax.nn.gelu(gb)
    cw = cw_ref[...]
    xc = (_shift_rows(xr, -2) * cw[0:1] + _shift_rows(xr, -1) * cw[1:2] + _shift_rows(xr, 0) * cw[2:3]
          + _shift_rows(xr, 1) * cw[3:4] + cb_ref[...])
    gates = jax.nn.sigmoid(jnp.dot(xc.astype(BF16), wg_ref[...], preferred_element_type=F32) + bg_ref[...])
    for d in range(2):
        r = gates[:, 2 * d * HALF:(2 * d + 1) * HALF]
        ig = gates[:, (2 * d + 1) * HALF:(2 * d + 2) * HALF]
        log_a = (-LRU_C) * r * _softplus(-lam_ref[d:d + 1, :])
        a = jnp.exp(log_a)
        bval = jnp.sqrt(1.0 - jnp.exp(2.0 * log_a)) * (ig * xc)
        if d == 0:
            a_scr[...] = a
            b_scr[...] = bval
        else:
            ar_ref[...] = a
            br_ref[...] = bval

    @pl.when(pos == 0)
    def _():
        carry[...] = h0_ref[0:1, :]

    _lru_scan(a_scr, b_scr, hs_scr, carry, False)
    hsf_ref[...] = hs_scr[...]


def _cd_in(x, mod, g, w_in, sc_w, conv_w, conv_b, w_gates, b_gates, lru_lam, h0t):
    const = lambda i: (0, 0)
    tok = lambda dt: jax.ShapeDtypeStruct((N_TOK, HALF), dt)
    row = pl.BlockSpec((TM, HALF), lambda i: (i, 0))
    return pl.pallas_call(
        _cd_in_kernel,
        grid=(NT,),
        in_specs=_halo_specs(D_MODEL) + [
            pl.BlockSpec((None, 6, D_MODEL), lambda i: (_mod_row(i), 0, 0)),
            pl.BlockSpec((1, D_MODEL), const),
            pl.BlockSpec((D_MODEL, 5 * HALF), const),
            pl.BlockSpec((3, HALF), const),
            pl.BlockSpec((4, HALF), const),
            pl.BlockSpec((1, HALF), const),
            pl.BlockSpec((HALF, 4 * HALF), const),
            pl.BlockSpec((1, 4 * HALF), const),
            pl.BlockSpec((2, HALF), const),
            pl.BlockSpec((None, 2, HALF), lambda i: (_mod_row(i), 0, 0)),
        ],
        out_specs=[row, row, row, row, row],
        out_shape=[tok(BF16), tok(F32), tok(F32), tok(F32), tok(F32)],
        scratch_shapes=[
            pltpu.VMEM((TM + 2 * HALO, D_MODEL), BF16),
            pltpu.VMEM((TM, HALF), F32),
            pltpu.VMEM((TM, HALF), F32),
            pltpu.VMEM((TM, HALF), F32),
            pltpu.VMEM((1, HALF), F32),
        ],
        compiler_params=_params(),
        name="cd_in",
    )(x, x, x, mod, g, w_in, sc_w, conv_w, conv_b, w_gates, b_gates, lru_lam, h0t)


def _cd_out_kernel(ar_ref, br_ref, hsf_ref, gate_ref, yc_ref, x_ref, mod_ref, g_ref, wo_ref, h0_ref,
                   o_ref, fin_ref, hs_scr, carry):
    ti = NT - 1 - pl.program_id(0)
    pos, n = _seq_pos(ti)

    @pl.when(pos == n - 1)
    def _():
        carry[...] = h0_ref[1:2, :]

    _lru_scan(ar_ref, br_ref, hs_scr, carry, True)
    fin_ref[...] = jnp.broadcast_to(carry[...], (8, HALF))
    yd = (hsf_ref[...] + hs_scr[...]) * gate_ref[...]
    out = (jnp.dot(yc_ref[...], wo_ref[0:HALF, :], preferred_element_type=F32)
           + jnp.dot(yd.astype(BF16), wo_ref[HALF:2 * HALF, :], preferred_element_type=F32))
    o_ref[...] = _post(x_ref[...], out, g_ref[...], mod_ref[2:3, :])


def _cd_out(a_r, b_r, hs_f, gate, yc, x, mod, g, w_out, h0t):
    const = lambda i: (0, 0)
    rev = lambda i: (NT - 1 - i, 0)
    row = pl.BlockSpec((TM, HALF), rev)
    return pl.pallas_call(
        _cd_out_kernel,
        grid=(NT,),
        in_specs=[
            row, row, row, row, row,
            pl.BlockSpec((TM, D_MODEL), rev),
            pl.BlockSpec((None, 6, D_MODEL), lambda i: (_mod_row(NT - 1 - i), 0, 0)),
            pl.BlockSpec((1, D_MODEL), const),
            pl.BlockSpec((D_MODEL, D_MODEL), const),
            pl.BlockSpec((None, 2, HALF), lambda i: (_mod_row(NT - 1 - i), 0, 0)),
        ],
        out_specs=[
            pl.BlockSpec((TM, D_MODEL), rev),
            pl.BlockSpec((8, HALF), rev),
        ],
        out_shape=[
            jax.ShapeDtypeStruct((N_TOK, D_MODEL), F32),
            jax.ShapeDtypeStruct((NT * 8, HALF), F32),
        ],
        scratch_shapes=[
            pltpu.VMEM((TM, HALF), F32),
            pltpu.VMEM((1, HALF), F32),
        ],
        compiler_params=_params(),
        name="cd_out",
    )(a_r, b_r, hs_f, gate, yc, x, mod, g, w_out, h0t)


def kernel(x_prompt, x_sample, cache_attn_k, cache_attn_v, state_s5, state_rglru, c, c_ctx, w_mod, b_mod, norm_g, w_in_ab, w_out_ab, s5_lam_re, s5_lam_im, s5_log_dt, s5_b, s5_c, s5_d, s5_w_glu, s5_b_glu, da_lam, da_g, w_in_cd, w_out_cd, sc_conv_w, lru_conv_w, lru_conv_b, lru_w_a, lru_b_a, lru_w_x, lru_b_x, lru_lam, ffn_w_up, ffn_conv_w, ffn_conv_b, ffn_w_down):
    assert DEPTH == 2
    xp = x_prompt.reshape(NT_P * TM, D_MODEL)
    xs = x_sample.reshape(NT_S * TM, D_MODEL)
    cv8 = jnp.zeros((8, D_MODEL), F32).at[0].set(c_ctx).at[1:1 + DEC_BATCH].set(c)
    mod = _modulation(cv8, w_mod, b_mod)
    cos_t, sin_t = _rope_tables()
    g = norm_g.reshape(DEPTH, 4, 1, D_MODEL)

    lam_init = 0.8 - 0.6 * math.exp(-0.3 * 0)
    u, qkv, k32, v32 = _ab_in(xp, xs, mod[0], g[0, 0], w_in_ab[0].astype(BF16), cos_t, sin_t)
    m1, m2, m3, coef2 = _s5_params(s5_lam_re[0], s5_lam_im[0], s5_log_dt[0], s5_b[0], s5_c[0])
    ys5, new_s5 = _s5(u, m1, m2, m3, coef2, state_s5[:, 0], s5_d[0])
    yb_p = _attention(qkv, 0, BATCH, 1, SEQ, None, da_lam[0], da_g[0], lam_init)
    ctx = (cache_attn_k[:, 0].reshape(DEC_BATCH, PAST_LEN, HALF).astype(BF16),
           cache_attn_v[:, 0].reshape(DEC_BATCH, PAST_LEN, HALF).astype(BF16))
    yb_s = _attention(qkv, NT_P, DEC_BATCH, TPS, DEC_SEQ, ctx, da_lam[0], da_g[0], lam_init)
    x = _ab_out(ys5, yb_p, yb_s, xp, xs, mod[0], g[0, 1], s5_w_glu[0].astype(BF16), s5_b_glu[0].reshape(1, HALF),
                w_out_ab[0].astype(BF16))
    new_k = k32.reshape(BATCH, 1, SEQ, DA_HEADS, 2, DA_DK)
    new_v = v32.reshape(BATCH, 1, SEQ, DA_HEADS, DA_DV)
    x = _ffn(x, mod[0], g[0, 2], g[0, 3], ffn_w_up[0].astype(BF16), ffn_conv_w[0], ffn_conv_b[0].reshape(1, 2 * D_FF),
             ffn_w_down[0].astype(BF16), False)

    eye = jnp.eye(LRU_BLOCKS, dtype=F32)
    dense = lambda w: jnp.einsum('kcd,kl->kcld', w, eye).reshape(LRU_WIDTH, LRU_WIDTH)
    w_gates = jnp.concatenate([dense(lru_w_a[0, 0]), dense(lru_w_x[0, 0]),
                               dense(lru_w_a[0, 1]), dense(lru_w_x[0, 1])], axis=1).astype(BF16)
    b_gates = jnp.concatenate([lru_b_a[0, 0], lru_b_x[0, 0], lru_b_a[0, 1], lru_b_x[0, 1]]).reshape(1, 4 * HALF)
    h0t = jnp.zeros((8, 2, HALF), F32).at[1:1 + DEC_BATCH].set(state_rglru[:, 0])
    yc, gate, hs_f, a_r, b_r = _cd_in(x, mod[1], g[1, 0], w_in_cd[0].astype(BF16), sc_conv_w[0], lru_conv_w[0],
                                      lru_conv_b[0].reshape(1, HALF), w_gates, b_gates, lru_lam[0], h0t)
    x, fin_r = _cd_out(a_r, b_r, hs_f, gate, yc, x, mod[1], g[1, 1], w_out_cd[0].astype(BF16), h0t)
    fin_f = hs_f[:NT_P * TM].reshape(BATCH, SEQ, HALF)[:, SEQ - 1]
    fin_r = fin_r.reshape(NT, 8, HALF)[:NT_P, 0]
    new_lru = jnp.stack([fin_f, fin_r], axis=1)[:, None]
    yp, ys = _ffn(x, mod[1], g[1, 2], g[1, 3], ffn_w_up[1].astype(BF16), ffn_conv_w[1],
                  ffn_conv_b[1].reshape(1, 2 * D_FF), ffn_w_down[1].astype(BF16), True)
    return (yp.reshape(BATCH, SEQ, D_MODEL), ys.reshape(DEC_BATCH, DEC_SEQ, D_MODEL),
            new_k, new_v, new_s5[:, None], new_lru)
```

```python
import functools
import math

import jax
import jax.numpy as jnp
from jax import lax
from jax.experimental import pallas as pl
from jax.experimental.pallas import tpu as pltpu

D_MODEL = 1024
BATCH = 32
SEQ = 256
DEPTH = 2
DEC_BATCH = 4
DEC_SEQ = 2048
PAST_LEN = 512
GRID_W = 64
HALF = D_MODEL // 2
S5_P = 16
S5_G = HALF // S5_P
S5_N = 64
DA_DK = 64
DA_DV = 2 * DA_DK
DA_HEADS = HALF // DA_DV
ROPE_THETA = 10000.0
ROPE_F = DA_DK // 4
LRU_WIDTH = HALF
LRU_BLOCKS = 8
LRU_BS = LRU_WIDTH // LRU_BLOCKS
LRU_C = 8.0
D_FF = 2816
EPS = 1e-6

F32 = jnp.float32
BF16 = jnp.bfloat16
HIGHEST = lax.Precision.HIGHEST

TM = 256
NT_P = BATCH * SEQ // TM
TPS = DEC_SEQ // TM
NT_S = DEC_BATCH * TPS
NT = NT_P + NT_S
N_TOK = NT * TM
HALO = 16
S5_T = 8
S5_GB = 128 // S5_P
S5_NB = S5_G // S5_GB
S5_K = S5_T * 128
S5_ST = 4 * S5_N
S5_GS = 4
FF_CHUNK = 256
VMEM_LIMIT = 56 * 1024 * 1024


def _mod_row(i):
    return jnp.where(i < NT_P, 0, 1 + (i - NT_P) // TPS)


def _seq_pos(i):
    return jnp.where(i < NT_P, 0, (i - NT_P) % TPS), jnp.where(i < NT_P, 1, TPS)


def _split_specs(width):
    return [
        pl.BlockSpec((TM, width), lambda i: (jnp.minimum(i, NT_P - 1), 0)),
        pl.BlockSpec((TM, width), lambda i: (jnp.maximum(i - NT_P, 0), 0)),
    ]


def _pick(p_ref, s_ref):
    return jnp.where(pl.program_id(0) < NT_P, p_ref[...], s_ref[...])


def _params(sem=("arbitrary",)):
    return pltpu.CompilerParams(dimension_semantics=sem, vmem_limit_bytes=VMEM_LIMIT)


def _rms(x, g):
    ms = jnp.mean(x * x, axis=-1, keepdims=True)
    return x * lax.rsqrt(ms + EPS) * g


def _pre(x, g, shift, scale):
    return _rms(x, g) * (1.0 + scale) + shift


def _mod_kernel(cv_ref, w_ref, b_ref, o_ref):
    cv = cv_ref[...]
    s = cv * jax.nn.sigmoid(cv)
    o_ref[...] = jnp.dot(s, w_ref[...], precision=HIGHEST, preferred_element_type=F32) + b_ref[...]


def _modulation(cv8, w_mod, b_mod):
    nb = 1536
    out = pl.pallas_call(
        _mod_kernel,
        grid=(DEPTH, 6 * D_MODEL // nb),
        in_specs=[
            pl.BlockSpec((8, D_MODEL), lambda l, j: (0, 0)),
            pl.BlockSpec((None, D_MODEL, nb), lambda l, j: (l, 0, j)),
            pl.BlockSpec((None, 1, nb), lambda l, j: (l, 0, j)),
        ],
        out_specs=pl.BlockSpec((None, 8, nb), lambda l, j: (l, 0, j)),
        out_shape=jax.ShapeDtypeStruct((DEPTH, 8, 6 * D_MODEL), F32),
        compiler_params=_params(("arbitrary", "arbitrary")),
        name="modulation",
    )(cv8, w_mod, b_mod.reshape(DEPTH, 1, 6 * D_MODEL))
    return out.reshape(DEPTH, 8, 6, D_MODEL)


def _rope_partner(x):
    lane = lax.broadcasted_iota(jnp.int32, (1, HALF), 1)
    first = (lane % (2 * ROPE_F)) < ROPE_F
    return jnp.where(first, pltpu.roll(x, HALF - ROPE_F, axis=1), pltpu.roll(x, ROPE_F, axis=1))


def _ab_in_kernel(xp_ref, xs_ref, mod_ref, g_ref, w_ref, cos_ref, sin_ref, u_ref, qkv_ref, k_ref, v_ref):
    i = pl.program_id(0)
    h = _pre(_pick(xp_ref, xs_ref), g_ref[...], mod_ref[0:1, :], mod_ref[1:2, :])
    proj = jnp.dot(h.astype(BF16), w_ref[...], preferred_element_type=F32)
    u_ref[...] = proj[:, 0:HALF]
    q = proj[:, HALF:2 * HALF]
    k = proj[:, 2 * HALF:3 * HALF]
    v = proj[:, 3 * HALF:4 * HALF]
    cos = cos_ref[...]
    sin = sin_ref[...]
    qr = q * cos + _rope_partner(q) * sin
    kr = k * cos + _rope_partner(k) * sin
    qkv_ref[:, 0:HALF] = (qr * (1.0 / math.sqrt(DA_DK))).astype(BF16)
    qkv_ref[:, HALF:2 * HALF] = kr.astype(BF16)
    qkv_ref[:, 2 * HALF:3 * HALF] = v.astype(BF16)

    @pl.when(i < NT_P)
    def _():
        k_ref[...] = k
        v_ref[...] = v


def _ab_in(xp, xs, mod, g, w, cos_t, sin_t):
    def rope_tile(i):
        return jnp.where(i < NT_P, 0, 1 + (i - NT_P) % TPS)

    return pl.pallas_call(
        _ab_in_kernel,
        grid=(NT,),
        in_specs=_split_specs(D_MODEL) + [
            pl.BlockSpec((None, 6, D_MODEL), lambda i: (_mod_row(i), 0, 0)),
            pl.BlockSpec((1, D_MODEL), lambda i: (0, 0)),
            pl.BlockSpec((D_MODEL, 4 * HALF), lambda i: (0, 0)),
            pl.BlockSpec((TM, HALF), lambda i: (rope_tile(i), 0)),
            pl.BlockSpec((TM, HALF), lambda i: (rope_tile(i), 0)),
        ],
        out_specs=[
            pl.BlockSpec((TM, HALF), lambda i: (i, 0)),
            pl.BlockSpec((TM, 3 * HALF), lambda i: (i, 0)),
            pl.BlockSpec((TM, HALF), lambda i: (jnp.minimum(i, NT_P - 1), 0)),
            pl.BlockSpec((TM, HALF), lambda i: (jnp.minimum(i, NT_P - 1), 0)),
        ],
        out_shape=[
            jax.ShapeDtypeStruct((N_TOK, HALF), F32),
            jax.ShapeDtypeStruct((N_TOK, 3 * HALF), BF16),
            jax.ShapeDtypeStruct((NT_P * TM, HALF), F32),
            jax.ShapeDtypeStruct((NT_P * TM, HALF), F32),
        ],
        compiler_params=_params(),
        name="ab_in",
    )(xp, xs, mod, g, w, cos_t, sin_t)


def _rope_tables():
    pos = jnp.arange(DEC_SEQ)
    t_row = (pos // GRID_W).astype(F32)
    t_col = (pos % GRID_W).astype(F32)
    inv = ROPE_THETA ** (-jnp.arange(ROPE_F, dtype=F32) / ROPE_F)
    ang = jnp.stack([t_row[:, None] * inv, t_col[:, None] * inv], axis=1)
    cos, sin = jnp.cos(ang), jnp.sin(ang)
    cos64 = jnp.stack([cos, cos], axis=2).reshape(DEC_SEQ, DA_DK)
    sin64 = jnp.stack([-sin, sin], axis=2).reshape(DEC_SEQ, DA_DK)
    reps = HALF // DA_DK
    cos_t = jnp.concatenate([jnp.ones((TM, HALF), F32), jnp.tile(cos64, (1, reps))], axis=0)
    sin_t = jnp.concatenate([jnp.zeros((TM, HALF), F32), jnp.tile(sin64, (1, reps))], axis=0)
    return cos_t, sin_t


def _s5_param_kernel(lre_ref, lim_ref, ldt_ref, bt_ref, c_ref, m1_ref, m2_ref, m3_ref, at_ref):
    nt = (((1,), (1,)), ((), ()))
    e_tab, f_tab, k_tab = [], [], []
    r_i = lax.broadcasted_iota(jnp.int32, (128, 128), 0) // S5_P
    c_i = lax.broadcasted_iota(jnp.int32, (128, 128), 1) // S5_P
    same_group = r_i == c_i
    for d in range(2):
        lre = lre_ref[d]
        lim = lim_ref[d]
        dt = jnp.exp(ldt_ref[d])
        mag = jnp.exp(lre * dt)
        ang = lim * dt
        ar = mag * jnp.cos(ang)
        ai = mag * jnp.sin(ang)
        den = lre * lre + lim * lim
        nr = ar - 1.0
        fr = (nr * lre + ai * lim) / den
        fi = (ai * lre - nr * lim) / den
        btr = bt_ref[d, 0]
        bti = bt_ref[d, 1]
        bfr = fr * btr - fi * bti
        bfi = fr * bti + fi * btr
        cr = c_ref[d, 0]
        ci = c_ref[d, 1]
        cst = jnp.concatenate([cr, -ci], axis=1)
        pr = jnp.ones_like(ar)
        pi = jnp.zeros_like(ar)
        e_d, f_d, k_d = [], [], []
        for tau in range(S5_T + 1):
            f_d.append((cr * pr - ci * pi, cr * pi + ci * pr))
            if tau < S5_T:
                er, ei = pr * bfr - pi * bfi, pr * bfi + pi * bfr
                e_d.append((er, ei))
                full = lax.dot_general(jnp.concatenate([er, ei], axis=1), cst, nt,
                                       precision=HIGHEST, preferred_element_type=F32)
                k_d.append(jnp.where(same_group, full, 0.0))
            else:
                at_ref[d, 0] = pr
                at_ref[d, 1] = pi
            pr, pi = pr * ar - pi * ai, pr * ai + pi * ar
        e_tab.append(e_d)
        f_tab.append(f_d)
        k_tab.append(k_d)

    for s in range(S5_T):
        for t in range(S5_T):
            blk = k_tab[0][t - s] if t >= s else k_tab[1][s - t]
            if t == s:
                blk = blk + k_tab[1][0]
            m1_ref[s * 128:(s + 1) * 128, t * 128:(t + 1) * 128] = blk.astype(BF16)
    row_group = lax.broadcasted_iota(jnp.int32, (128, S5_ST), 0) // S5_P
    for s in range(S5_T):
        ef, er_ = e_tab[0][S5_T - 1 - s], e_tab[1][s]
        v = jnp.concatenate([ef[0], er_[0], ef[1], er_[1]], axis=1)
        ff, fr_ = f_tab[0][s + 1], f_tab[1][S5_T - s]
        w = jnp.concatenate([ff[0], fr_[0], -ff[1], -fr_[1]], axis=1)
        for gi in range(S5_GB):
            sel = row_group == gi
            m2_ref[s * 128:(s + 1) * 128, gi * S5_ST:(gi + 1) * S5_ST] = jnp.where(sel, v, 0.0).astype(BF16)
            m3_ref[s * 128:(s + 1) * 128, gi * S5_ST:(gi + 1) * S5_ST] = jnp.where(sel, w, 0.0).astype(BF16)


def _s5_params(lam_re, lam_im, log_dt, b_w, c_w):
    rows = S5_G * S5_P
    per_row = lambda a: jnp.repeat(a, S5_P, axis=1)
    lre = per_row(lam_re)
    lim = per_row(lam_im)
    ldt = per_row(jnp.broadcast_to(log_dt[:, :, None], (2, S5_G, S5_N)))
    bt = b_w.transpose(0, 1, 2, 4, 3).reshape(2, 2, rows, S5_N)
    cc = c_w.reshape(2, 2, rows, S5_N)
    vec = pl.BlockSpec((2, 128, S5_N), lambda j: (0, j, 0))
    mat = pl.BlockSpec((2, 2, 128, S5_N), lambda j: (0, 0, j, 0))
    wide = lambda n: pl.BlockSpec((None, S5_K, n), lambda j: (j, 0, 0))
    m1, m2, m3, at = pl.pallas_call(
        _s5_param_kernel,
        grid=(S5_NB,),
        in_specs=[vec, vec, vec, mat, mat],
        out_specs=[wide(S5_K), wide(S5_GB * S5_ST), wide(S5_GB * S5_ST),
                   pl.BlockSpec((None, 2, 2, 128, S5_N), lambda j: (j, 0, 0, 0, 0))],
        out_shape=[
            jax.ShapeDtypeStruct((S5_NB, S5_K, S5_K), BF16),
            jax.ShapeDtypeStruct((S5_NB, S5_K, S5_GB * S5_ST), BF16),
            jax.ShapeDtypeStruct((S5_NB, S5_K, S5_GB * S5_ST), BF16),
            jax.ShapeDtypeStruct((S5_NB, 2, 2, 128, S5_N), F32),
        ],
        compiler_params=_params(),
        name="s5_params",
    )(lre, lim, ldt, bt, cc)
    coef = at[:, :, :, ::S5_P, :].transpose(0, 3, 2, 1, 4).reshape(S5_NB, S5_GB, 2, 2 * S5_N)
    return m1, m2, m3, coef


def _s5_kernel(u_ref, m1_ref, m2_ref, m3_ref, coef_ref, d_ref, *rest, stride, nc, two_pass):
    if two_pass:
        h0_ref, y_ref, lhs, yacc, s_scr, hp_scr = rest
    else:
        y_ref, fin_ref, lhs, yacc, s_scr, hp_scr = rest
    ln = 2 * S5_N
    nt = (((1,), (1,)), ((), ()))

    def gather(c2, carry):
        r0 = pl.multiple_of(c2 * 16, 16)
        for t in range(S5_T):
            a = u_ref[pl.ds(c2 * 16 + t, 8, stride=stride), :]
            b = u_ref[pl.ds(c2 * 16 + S5_T + t, 8, stride=stride), :]
            lhs[pl.ds(r0, 16), t * 128:(t + 1) * 128] = jnp.concatenate([a, b], axis=0).astype(BF16)
        return carry

    lax.fori_loop(0, nc // 2, gather, 0)
    lb = lhs[...]
    yacc[...] = jnp.dot(lb, m1_ref[...], preferred_element_type=F32)

    lane = lax.broadcasted_iota(jnp.int32, (8, ln), 1)
    row = lax.broadcasted_iota(jnp.int32, (8, ln), 0)
    fwd = lane < S5_N
    odd = (row % 2) == 1
    wide = S5_GS * S5_ST
    for gq in range(S5_GB // S5_GS):
        groups = range(gq * S5_GS, (gq + 1) * S5_GS)
        cols = slice(gq * wide, (gq + 1) * wide)
        s_scr[...] = jnp.dot(lb, m2_ref[:, cols], preferred_element_type=F32)
        are = [jnp.broadcast_to(coef_ref[g, 0:1, :], (8, ln)) for g in groups]
        aim = [jnp.broadcast_to(coef_ref[g, 1:2, :], (8, ln)) for g in groups]

        def scan(init, store):
            def body(c, carry):
                kf = pl.multiple_of(c * 8, 8)
                kr = pl.multiple_of((nc - 1 - c) * 8, 8)
                out = []
                for j in range(S5_GS):
                    p, q = carry[2 * j], carry[2 * j + 1]
                    o = j * S5_ST
                    if store:
                        hp_scr[pl.ds(kf, 8), o:o + S5_N] = p[:, 0:S5_N]
                        hp_scr[pl.ds(kr, 8), o + S5_N:o + ln] = p[:, S5_N:ln]
                        hp_scr[pl.ds(kf, 8), o + ln:o + ln + S5_N] = q[:, 0:S5_N]
                        hp_scr[pl.ds(kr, 8), o + ln + S5_N:o + 2 * ln] = q[:, S5_N:ln]
                    s_re = jnp.where(fwd, s_scr[pl.ds(kf, 8), o:o + ln], s_scr[pl.ds(kr, 8), o:o + ln])
                    s_im = jnp.where(fwd, s_scr[pl.ds(kf, 8), o + ln:o + 2 * ln],
                                     s_scr[pl.ds(kr, 8), o + ln:o + 2 * ln])
                    out += [are[j] * p - aim[j] * q + s_re, are[j] * q + aim[j] * p + s_im]
                return tuple(out)
            return lax.fori_loop(0, nc, body, tuple(init))

        if two_pass:
            h0 = [h0_ref[g, r] for g in groups for r in range(2)]
            fin = scan(h0, False)
            seed = [h + jnp.where(fwd & odd, pltpu.roll(f, 1, axis=0), 0.0)
                    + jnp.where(jnp.logical_not(fwd | odd), pltpu.roll(f, 7, axis=0), 0.0)
                    for h, f in zip(h0, fin)]
            scan(seed, True)
        else:
            fin = scan([jnp.zeros((8, ln), F32)] * (2 * S5_GS), True)
            for j, g in enumerate(groups):
                fin_ref[g, 0] = fin[2 * j]
                fin_ref[g, 1] = fin[2 * j + 1]
        yacc[...] += lax.dot_general(hp_scr[...].astype(BF16), m3_ref[:, cols], nt, preferred_element_type=F32)

    dvec = d_ref[...]

    def scatter(c, carry):
        r0 = pl.multiple_of(c * 8, 8)
        for t in range(S5_T):
            idx = pl.ds(c * S5_T + t, 8, stride=stride)
            y_ref[idx, :] = yacc[pl.ds(r0, 8), t * 128:(t + 1) * 128] + u_ref[idx, :] * dvec
        return carry

    lax.fori_loop(0, nc, scatter, 0)


def _s5_call(u, row_block, mats, coef, dvec, h0, stride, nc, parts):
    m1, m2, m3 = mats
    rows = 8 * stride
    two_pass = h0 is not None
    single = dict(pipeline_mode=pl.Buffered(1)) if parts == 1 else {}
    mat = lambda n: pl.BlockSpec((None, S5_K, n), lambda j, i: (j, 0, 0), **single)
    tok = pl.BlockSpec((rows, 128), lambda j, i: (row_block * parts + i, j))
    in_specs = [
        tok, mat(S5_K), mat(S5_GB * S5_ST), mat(S5_GB * S5_ST),
        pl.BlockSpec((None, S5_GB, 2, 2 * S5_N), lambda j, i: (j, 0, 0, 0)),
        pl.BlockSpec((None, 1, 128), lambda j, i: (j, 0, 0)),
    ]
    args = [u, m1, m2, m3, coef, dvec]
    out_specs = [pl.BlockSpec((rows, 128), lambda j, i: (i, j))]
    out_shape = [jax.ShapeDtypeStruct((parts * rows, HALF), F32)]
    if two_pass:
        in_specs.append(pl.BlockSpec((None, S5_GB, 2, 8, 2 * S5_N), lambda j, i: (j, 0, 0, 0, 0)))
        args.append(h0)
    else:
        out_specs.append(pl.BlockSpec((None, S5_GB, 2, 8, 2 * S5_N), lambda j, i: (j, 0, 0, i, 0)))
        out_shape.append(jax.ShapeDtypeStruct((S5_NB, S5_GB, 2, parts * 8, 2 * S5_N), F32))
    return pl.pallas_call(
        functools.partial(_s5_kernel, stride=stride, nc=nc, two_pass=two_pass),
        grid=(S5_NB, parts),
        in_specs=in_specs,
        out_specs=out_specs,
        out_shape=out_shape,
        scratch_shapes=[
            pltpu.VMEM((nc * 8, S5_K), BF16),
            pltpu.VMEM((nc * 8, S5_K), F32),
            pltpu.VMEM((nc * 8, S5_GS * S5_ST), F32),
            pltpu.VMEM((nc * 8, S5_GS * S5_ST), F32),
        ],
        compiler_params=_params(("arbitrary", "arbitrary")),
        name="s5_%d" % stride,
    )(*args)


def _s5(u, mats, coef, h0, s5_d):
    dvec = s5_d.reshape(S5_NB, 1, 128)
    yp, fin = _s5_call(u, 0, mats, coef, dvec, None, SEQ, SEQ // S5_T, BATCH // 8)
    hh = h0.transpose(3, 2, 0, 1, 4)
    z = jnp.zeros_like(hh[:, :, :, 0])
    first = jnp.concatenate([hh[:, :, :, 0], z], axis=-1)
    second = jnp.concatenate([z, hh[:, :, :, 1]], axis=-1)
    h0g = jnp.stack([first, second], axis=3).reshape(S5_NB, S5_GB, 2, 2 * DEC_BATCH, 2 * S5_N)
    half = DEC_SEQ // 2
    (ys,) = _s5_call(u, 1, mats, coef, dvec, h0g, half, half // S5_T, 1)
    new_s5 = fin.reshape(S5_G, 2, BATCH, 2, S5_N).transpose(2, 3, 1, 0, 4)
    return yp, ys, new_s5


def _attn_kernel(*refs, lam_init, has_ctx):
    if has_ctx:
        q_ref, k_ref, v_ref, kc_ref, vc_ref, dl_ref, g_ref, o_ref = refs
    else:
        q_ref, k_ref, v_ref, dl_ref, g_ref, o_ref = refs
    dl = dl_ref[...]
    lam = (jnp.exp(jnp.sum(dl[0:1] * dl[1:2], keepdims=True))
           - jnp.exp(jnp.sum(dl[2:3] * dl[3:4], keepdims=True)) + lam_init)
    lane = lax.broadcasted_iota(jnp.int32, (1, DA_DV), 1)
    first = lane < DA_DK
    g = g_ref[...]
    nt = (((1,), (1,)), ((), ()))
    for h in range(DA_HEADS):
        cols = slice(h * DA_DV, (h + 1) * DA_DV)
        qh = q_ref[:, cols]
        zero = jnp.zeros_like(qh)
        es, rs = [], []
        for m in range(2):
            qm = jnp.where(first if m == 0 else jnp.logical_not(first), qh, zero)
            s = [lax.dot_general(qm, k_ref[:, cols], nt, preferred_element_type=F32)]
            if has_ctx:
                s.append(lax.dot_general(qm, kc_ref[:, cols], nt, preferred_element_type=F32))
            mx = functools.reduce(jnp.maximum, [jnp.max(x, axis=-1, keepdims=True) for x in s])
            e = [jnp.exp(x - mx) for x in s]
            es.append(e)
            rs.append(1.0 / sum(jnp.sum(x, axis=-1, keepdims=True) for x in e))
        r1 = lam * rs[1]
        w = [(e0 * rs[0] - e1 * r1).astype(BF16) for e0, e1 in zip(es[0], es[1])]
        o = jnp.dot(w[0], v_ref[:, cols], preferred_element_type=F32)
        if has_ctx:
            o = o + jnp.dot(w[1], vc_ref[:, cols], preferred_element_type=F32)
        o_ref[:, cols] = (_rms(o, g) * (1.0 - lam_init)).astype(BF16)


def _attention(qkv, q_tile0, nb, nq, lk, ctx, da_lam, da_g, lam_init):
    kb0 = q_tile0 * TM // lk
    in_specs = [
        pl.BlockSpec((TM, HALF), lambda b, j: (q_tile0 + b * nq + j, 0)),
        pl.BlockSpec((lk, HALF), lambda b, j: (kb0 + b, 1)),
        pl.BlockSpec((lk, HALF), lambda b, j: (kb0 + b, 2)),
    ]
    args = [qkv, qkv, qkv]
    if ctx is not None:
        in_specs += [pl.BlockSpec((None, PAST_LEN, HALF), lambda b, j: (b, 0, 0))] * 2
        args += list(ctx)
    in_specs += [
        pl.BlockSpec((4, DA_DK), lambda b, j: (0, 0)),
        pl.BlockSpec((1, DA_DV), lambda b, j: (0, 0)),
    ]
    return pl.pallas_call(
        functools.partial(_attn_kernel, lam_init=lam_init, has_ctx=ctx is not None),
        grid=(nb, nq),
        in_specs=in_specs,
        out_specs=pl.BlockSpec((TM, HALF), lambda b, j: (b * nq + j, 0)),
        out_shape=jax.ShapeDtypeStruct((nb * nq * TM, HALF), BF16),
        compiler_params=_params(("arbitrary", "arbitrary")),
        name="diff_attn_%d" % lk,
    )(*args, da_lam, da_g.reshape(1, DA_DV))


def _post(x, y, g, gate):
    return x + gate * _rms(y, g)


def _ab_out_kernel(ysp_ref, yss_ref, ybp_ref, ybs_ref, xp_ref, xs_ref, mod_ref, g_ref, wg_ref, bg_ref, wo_ref,
                   o_ref):
    ys = jax.nn.gelu(_pick(ysp_ref, yss_ref))
    glu = jnp.dot(ys.astype(BF16), wg_ref[...], preferred_element_type=F32) + bg_ref[...]
    ya = ys * jax.nn.sigmoid(glu)
    out = (jnp.dot(ya.astype(BF16), wo_ref[0:HALF, :], preferred_element_type=F32)
           + jnp.dot(_pick(ybp_ref, ybs_ref), wo_ref[HALF:2 * HALF, :], preferred_element_type=F32))
    o_ref[...] = _post(_pick(xp_ref, xs_ref), out, g_ref[...], mod_ref[2:3, :])


def _ab_out(ys5_p, ys5_s, yb_p, yb_s, xp, xs, mod, g, w_glu, b_glu, w_out):
    return pl.pallas_call(
        _ab_out_kernel,
        grid=(NT,),
        in_specs=_split_specs(HALF) + _split_specs(HALF) + _split_specs(D_MODEL) + [
            pl.BlockSpec((None, 6, D_MODEL), lambda i: (_mod_row(i), 0, 0)),
            pl.BlockSpec((1, D_MODEL), lambda i: (0, 0)),
            pl.BlockSpec((HALF, HALF), lambda i: (0, 0)),
            pl.BlockSpec((1, HALF), lambda i: (0, 0)),
            pl.BlockSpec((D_MODEL, D_MODEL), lambda i: (0, 0)),
        ],
        out_specs=pl.BlockSpec((TM, D_MODEL), lambda i: (i, 0)),
        out_shape=jax.ShapeDtypeStruct((N_TOK, D_MODEL), F32),
        compiler_params=_params(),
        name="ab_out",
    )(ys5_p, ys5_s, yb_p, yb_s, xp, xs, mod, g, w_glu, b_glu, w_out)


def _halo_specs(width):
    blocks = TM // HALO
    last = N_TOK // HALO - 1
    return [
        pl.BlockSpec((TM, width), lambda i: (i, 0)),
        pl.BlockSpec((HALO, width), lambda i: (jnp.maximum(i * blocks - 1, 0), 0)),
        pl.BlockSpec((HALO, width), lambda i: (jnp.minimum((i + 1) * blocks, last), 0)),
    ]


def _fill_hbuf(hbuf, x_ref, xp_ref, xn_ref, g, shift, scale):
    i = pl.program_id(0)
    pos, n = _seq_pos(i)
    hp = _pre(xp_ref[...], g, shift, scale)
    hn = _pre(xn_ref[...], g, shift, scale)
    hbuf[0:HALO, :] = jnp.where(pos > 0, hp, 0.0).astype(BF16)
    hbuf[HALO:HALO + TM, :] = _pre(x_ref[...], g, shift, scale).astype(BF16)
    hbuf[HALO + TM:2 * HALO + TM, :] = jnp.where(pos < n - 1, hn, 0.0).astype(BF16)


def _shift_rows(x, s):
    rows = x.shape[0]
    if s == 0:
        return x[HALO:HALO + TM]
    return pltpu.roll(x, (-s) % rows, axis=0)[HALO:HALO + TM]


def _ffn_kernel(x_ref, xp_ref, xn_ref, mod_ref, g2_ref, g3_ref, wu_ref, cw_ref, cb_ref, wd_ref, *rest, split):
    hbuf, acc = rest[-2:]
    _fill_hbuf(hbuf, x_ref, xp_ref, xn_ref, g2_ref[...], mod_ref[3:4, :], mod_ref[4:5, :])
    h = hbuf[...]
    for j in range(D_FF // FF_CHUNK):
        parts = []
        for half in range(2):
            c0 = half * D_FF + j * FF_CHUNK
            u = jnp.dot(h, wu_ref[:, c0:c0 + FF_CHUNK], preferred_element_type=F32)
            cw = cw_ref[:, c0:c0 + FF_CHUNK]
            parts.append(_shift_rows(u, -1) * cw[0:1] + _shift_rows(u, 0) * cw[1:2]
                         + _shift_rows(u, 1) * cw[2:3] + cb_ref[:, c0:c0 + FF_CHUNK])
        act = (jax.nn.gelu(parts[0]) * parts[1]).astype(BF16)
        contrib = jnp.dot(act, wd_ref[j * FF_CHUNK:(j + 1) * FF_CHUNK, :], preferred_element_type=F32)
        if j == 0:
            acc[...] = contrib
        else:
            acc[...] += contrib
    res = _post(x_ref[...], acc[...], g3_ref[...], mod_ref[5:6, :])
    if split:
        i = pl.program_id(0)

        @pl.when(i < NT_P)
        def _():
            rest[0][...] = res

        @pl.when(i >= NT_P)
        def _():
            rest[1][...] = res
    else:
        rest[0][...] = res


def _ffn(x, mod, g2, g3, w_up, cw, cb, w_down, split):
    const = lambda i: (0, 0)
    if split:
        out_specs = _split_specs(D_MODEL)
        out_shape = [jax.ShapeDtypeStruct((NT_P * TM, D_MODEL), F32), jax.ShapeDtypeStruct((NT_S * TM, D_MODEL), F32)]
    else:
        out_specs = pl.BlockSpec((TM, D_MODEL), lambda i: (i, 0))
        out_shape = jax.ShapeDtypeStruct((N_TOK, D_MODEL), F32)
    return pl.pallas_call(
        functools.partial(_ffn_kernel, split=split),
        grid=(NT,),
        in_specs=_halo_specs(D_MODEL) + [
            pl.BlockSpec((None, 6, D_MODEL), lambda i: (_mod_row(i), 0, 0)),
            pl.BlockSpec((1, D_MODEL), const),
            pl.BlockSpec((1, D_MODEL), const),
            pl.BlockSpec((D_MODEL, 2 * D_FF), const, pipeline_mode=pl.Buffered(1)),
            pl.BlockSpec((3, 2 * D_FF), const),
            pl.BlockSpec((1, 2 * D_FF), const),
            pl.BlockSpec((D_FF, D_MODEL), const, pipeline_mode=pl.Buffered(1)),
        ],
        out_specs=out_specs,
        out_shape=out_shape,
        scratch_shapes=[
            pltpu.VMEM((TM + 2 * HALO, D_MODEL), BF16),
            pltpu.VMEM((TM, D_MODEL), F32),
        ],
        compiler_params=_params(),
        name="conv_ffn",
    )(x, x, x, mod, g2, g3, w_up, cw, cb, w_down)


def _softplus(z):
    return jnp.maximum(z, 0.0) + jnp.log(1.0 + jnp.exp(-jnp.abs(z)))


def _lru_scan(a_scr, b_scr, hs_scr, carry, reverse):
    nblk = TM // 8

    def body(blk, h):
        blk = (nblk - 1 - blk) if reverse else blk
        base = pl.multiple_of(blk * 8, 8)
        for j in (range(7, -1, -1) if reverse else range(8)):
            r = pl.ds(base + j, 1)
            h = a_scr[r, :] * h + b_scr[r, :]
            hs_scr[r, :] = h
        return h

    carry[...] = lax.fori_loop(0, nblk, body, carry[...])


def _cd_in_kernel(x_ref, xp_ref, xn_ref, mod_ref, g_ref, w_ref, scw_ref, cw_ref, cb_ref, wg_ref, bg_ref,
                  lam_ref, h0_ref, yc_ref, gate_ref, hsf_ref, ar_ref, br_ref,
                  hbuf, a_scr, b_scr, hs_scr, carry):
    i = pl.program_id(0)
    pos, _ = _seq_pos(i)
    _fill_hbuf(hbuf, x_ref, xp_ref, xn_ref, g_ref[...], mod_ref[0:1, :], mod_ref[1:2, :])
    proj = jnp.dot(hbuf[...], w_ref[...], preferred_element_type=F32)
    xin = proj[:, 0:HALF]
    bg = proj[HALO:HALO + TM, HALF:2 * HALF]
    cg = proj[:, 2 * HALF:3 * HALF]
    xr = proj[:, 3 * HALF:4 * HALF]
    gb = proj[HALO:HALO + TM, 4 * HALF:5 * HALF]
    prod = cg * xin
    scw = scw_ref[...]
    yc = bg * (_shift_rows(prod, -1) * scw[0:1] + _shift_rows(prod, 0) * scw[1:2]
               + _shift_rows(prod, 1) * scw[2:3])
    yc_ref[...] = yc.astype(BF16)
    gate_ref[...] = jax.nn.gelu(gb)
    cw = cw_ref[...]
    xc = (_shift_rows(xr, -2) * cw[0:1] + _shift_rows(xr, -1) * cw[1:2] + _shift_rows(xr, 0) * cw[2:3]
          + _shift_rows(xr, 1) * cw[3:4] + cb_ref[...])
    gates = jax.nn.sigmoid(jnp.dot(xc.astype(BF16), wg_ref[...], preferred_element_type=F32) + bg_ref[...])
    for d in range(2):
        r = gates[:, 2 * d * HALF:(2 * d + 1) * HALF]
        ig = gates[:, (2 * d + 1) * HALF:(2 * d + 2) * HALF]
        log_a = (-LRU_C) * r * _softplus(-lam_ref[d:d + 1, :])
        a = jnp.exp(log_a)
        bval = jnp.sqrt(1.0 - jnp.exp(2.0 * log_a)) * (ig * xc)
        if d == 0:
            a_scr[...] = a
            b_scr[...] = bval
        else:
            ar_ref[...] = a
            br_ref[...] = bval

    @pl.when(pos == 0)
    def _():
        carry[...] = h0_ref[0:1, :]

    _lru_scan(a_scr, b_scr, hs_scr, carry, False)
    hsf_ref[...] = hs_scr[...]


def _cd_in(x, mod, g, w_in, sc_w, conv_w, conv_b, w_gates, b_gates, lru_lam, h0t):
    const = lambda i: (0, 0)
    tok = lambda dt: jax.ShapeDtypeStruct((N_TOK, HALF), dt)
    row = pl.BlockSpec((TM, HALF), lambda i: (i, 0))
    return pl.pallas_call(
        _cd_in_kernel,
        grid=(NT,),
        in_specs=_halo_specs(D_MODEL) + [
            pl.BlockSpec((None, 6, D_MODEL), lambda i: (_mod_row(i), 0, 0)),
            pl.BlockSpec((1, D_MODEL), const),
            pl.BlockSpec((D_MODEL, 5 * HALF), const),
            pl.BlockSpec((3, HALF), const),
            pl.BlockSpec((4, HALF), const),
            pl.BlockSpec((1, HALF), const),
            pl.BlockSpec((HALF, 4 * HALF), const),
            pl.BlockSpec((1, 4 * HALF), const),
            pl.BlockSpec((2, HALF), const),
            pl.BlockSpec((None, 2, HALF), lambda i: (_mod_row(i), 0, 0)),
        ],
        out_specs=[row, row, row, row, row],
        out_shape=[tok(BF16), tok(F32), tok(F32), tok(F32), tok(F32)],
        scratch_shapes=[
            pltpu.VMEM((TM + 2 * HALO, D_MODEL), BF16),
            pltpu.VMEM((TM, HALF), F32),
            pltpu.VMEM((TM, HALF), F32),
            pltpu.VMEM((TM, HALF), F32),
            pltpu.VMEM((1, HALF), F32),
        ],
        compiler_params=_params(),
        name="cd_in",
    )(x, x, x, mod, g, w_in, sc_w, conv_w, conv_b, w_gates, b_gates, lru_lam, h0t)


def _cd_out_kernel(ar_ref, br_ref, hsf_ref, gate_ref, yc_ref, x_ref, mod_ref, g_ref, wo_ref, h0_ref,
                   o_ref, fin_ref, hs_scr, carry):
    ti = NT - 1 - pl.program_id(0)
    pos, n = _seq_pos(ti)

    @pl.when(pos == n - 1)
    def _():
        carry[...] = h0_ref[1:2, :]

    _lru_scan(ar_ref, br_ref, hs_scr, carry, True)
    fin_ref[...] = jnp.broadcast_to(carry[...], (8, HALF))
    yd = (hsf_ref[...] + hs_scr[...]) * gate_ref[...]
    out = (jnp.dot(yc_ref[...], wo_ref[0:HALF, :], preferred_element_type=F32)
           + jnp.dot(yd.astype(BF16), wo_ref[HALF:2 * HALF, :], preferred_element_type=F32))
    o_ref[...] = _post(x_ref[...], out, g_ref[...], mod_ref[2:3, :])


def _cd_out(a_r, b_r, hs_f, gate, yc, x, mod, g, w_out, h0t):
    const = lambda i: (0, 0)
    rev = lambda i: (NT - 1 - i, 0)
    row = pl.BlockSpec((TM, HALF), rev)
    return pl.pallas_call(
        _cd_out_kernel,
        grid=(NT,),
        in_specs=[
            row, row, row, row, row,
            pl.BlockSpec((TM, D_MODEL), rev),
            pl.BlockSpec((None, 6, D_MODEL), lambda i: (_mod_row(NT - 1 - i), 0, 0)),
            pl.BlockSpec((1, D_MODEL), const),
            pl.BlockSpec((D_MODEL, D_MODEL), const),
            pl.BlockSpec((None, 2, HALF), lambda i: (_mod_row(NT - 1 - i), 0, 0)),
        ],
        out_specs=[
            pl.BlockSpec((TM, D_MODEL), rev),
            pl.BlockSpec((8, HALF), rev),
        ],
        out_shape=[
            jax.ShapeDtypeStruct((N_TOK, D_MODEL), F32),
            jax.ShapeDtypeStruct((NT * 8, HALF), F32),
        ],
        scratch_shapes=[
            pltpu.VMEM((TM, HALF), F32),
            pltpu.VMEM((1, HALF), F32),
        ],
        compiler_params=_params(),
        name="cd_out",
    )(a_r, b_r, hs_f, gate, yc, x, mod, g, w_out, h0t)


def kernel(x_prompt, x_sample, cache_attn_k, cache_attn_v, state_s5, state_rglru, c, c_ctx, w_mod, b_mod, norm_g, w_in_ab, w_out_ab, s5_lam_re, s5_lam_im, s5_log_dt, s5_b, s5_c, s5_d, s5_w_glu, s5_b_glu, da_lam, da_g, w_in_cd, w_out_cd, sc_conv_w, lru_conv_w, lru_conv_b, lru_w_a, lru_b_a, lru_w_x, lru_b_x, lru_lam, ffn_w_up, ffn_conv_w, ffn_conv_b, ffn_w_down):
    assert DEPTH == 2
    xp = x_prompt.reshape(NT_P * TM, D_MODEL)
    xs = x_sample.reshape(NT_S * TM, D_MODEL)
    cv8 = jnp.zeros((8, D_MODEL), F32).at[0].set(c_ctx).at[1:1 + DEC_BATCH].set(c)
    mod = _modulation(cv8, w_mod, b_mod)
    cos_t, sin_t = _rope_tables()
    g = norm_g.reshape(DEPTH, 4, 1, D_MODEL)

    lam_init = 0.8 - 0.6 * math.exp(-0.3 * 0)
    u, qkv, k32, v32 = _ab_in(xp, xs, mod[0], g[0, 0], w_in_ab[0].astype(BF16), cos_t, sin_t)
    m1, m2, m3, coef = _s5_params(s5_lam_re[0], s5_lam_im[0], s5_log_dt[0], s5_b[0], s5_c[0])
    ys5_p, ys5_s, new_s5 = _s5(u, (m1, m2, m3), coef, state_s5[:, 0], s5_d[0])
    yb_p = _attention(qkv, 0, BATCH, 1, SEQ, None, da_lam[0], da_g[0], lam_init)
    ctx = (cache_attn_k[:, 0].reshape(DEC_BATCH, PAST_LEN, HALF).astype(BF16),
           cache_attn_v[:, 0].reshape(DEC_BATCH, PAST_LEN, HALF).astype(BF16))
    yb_s = _attention(qkv, NT_P, DEC_BATCH, TPS, DEC_SEQ, ctx, da_lam[0], da_g[0], lam_init)
    x = _ab_out(ys5_p, ys5_s, yb_p, yb_s, xp, xs, mod[0], g[0, 1], s5_w_glu[0].astype(BF16), s5_b_glu[0].reshape(1, HALF),
                w_out_ab[0].astype(BF16))
    new_k = k32.reshape(BATCH, 1, SEQ, DA_HEADS, 2, DA_DK)
    new_v = v32.reshape(BATCH, 1, SEQ, DA_HEADS, DA_DV)
    x = _ffn(x, mod[0], g[0, 2], g[0, 3], ffn_w_up[0].astype(BF16), ffn_conv_w[0], ffn_conv_b[0].reshape(1, 2 * D_FF),
             ffn_w_down[0].astype(BF16), False)

    eye = jnp.eye(LRU_BLOCKS, dtype=F32)
    dense = lambda w: jnp.einsum('kcd,kl->kcld', w, eye).reshape(LRU_WIDTH, LRU_WIDTH)
    w_gates = jnp.concatenate([dense(lru_w_a[0, 0]), dense(lru_w_x[0, 0]),
                               dense(lru_w_a[0, 1]), dense(lru_w_x[0, 1])], axis=1).astype(BF16)
    b_gates = jnp.concatenate([lru_b_a[0, 0], lru_b_x[0, 0], lru_b_a[0, 1], lru_b_x[0, 1]]).reshape(1, 4 * HALF)
    h0t = jnp.zeros((8, 2, HALF), F32).at[1:1 + DEC_BATCH].set(state_rglru[:, 0])
    yc, gate, hs_f, a_r, b_r = _cd_in(x, mod[1], g[1, 0], w_in_cd[0].astype(BF16), sc_conv_w[0], lru_conv_w[0],
                                      lru_conv_b[0].reshape(1, HALF), w_gates, b_gates, lru_lam[0], h0t)
    x, fin_r = _cd_out(a_r, b_r, hs_f, gate, yc, x, mod[1], g[1, 1], w_out_cd[0].astype(BF16), h0t)
    fin_f = hs_f[:NT_P * TM].reshape(BATCH, SEQ, HALF)[:, SEQ - 1]
    fin_r = fin_r.reshape(NT, 8, HALF)[:NT_P, 0]
    new_lru = jnp.stack([fin_f, fin_r], axis=1)[:, None]
    yp, ys = _ffn(x, mod[1], g[1, 2], g[1, 3], ffn_w_up[1].astype(BF16), ffn_conv_w[1],
                  ffn_conv_b[1].reshape(1, 2 * D_FF), ffn_w_down[1].astype(BF16), True)
    return (yp.reshape(BATCH, SEQ, D_MODEL), ys.reshape(DEC_BATCH, DEC_SEQ, D_MODEL),
            new_k, new_v, new_s5[:, None], new_lru)
```

```python
import functools
import math

import jax
import jax.numpy as jnp
from jax import lax
from jax.experimental import pallas as pl
from jax.experimental.pallas import tpu as pltpu

D_MODEL = 1024
BATCH = 32
SEQ = 256
DEPTH = 2
DEC_BATCH = 4
DEC_SEQ = 2048
PAST_LEN = 512
GRID_W = 64
HALF = D_MODEL // 2
S5_P = 16
S5_G = HALF // S5_P
S5_N = 64
DA_DK = 64
DA_DV = 2 * DA_DK
DA_HEADS = HALF // DA_DV
ROPE_THETA = 10000.0
ROPE_F = DA_DK // 4
LRU_WIDTH = HALF
LRU_BLOCKS = 8
LRU_BS = LRU_WIDTH // LRU_BLOCKS
LRU_C = 8.0
D_FF = 2816
EPS = 1e-6

F32 = jnp.float32
BF16 = jnp.bfloat16
HIGHEST = lax.Precision.HIGHEST

TM = 256
NT_P = BATCH * SEQ // TM
TPS = DEC_SEQ // TM
NT_S = DEC_BATCH * TPS
NT = NT_P + NT_S
N_TOK = NT * TM
HALO = 8
ROWS = TM + 2 * HALO
S5_T = 8
S5_GB = 128 // S5_P
S5_NB = S5_G // S5_GB
S5_K = S5_T * 128
S5_ST = 4 * S5_N
S5_GS = 4
FF_CHUNK = 256
GELU_K = math.sqrt(2.0 / math.pi)
GELU_C = 0.044715
VMEM_LIMIT = 56 * 1024 * 1024


def _mod_row(i):
    return jnp.where(i < NT_P, 0, 1 + (i - NT_P) // TPS)


def _seq_pos(i):
    return jnp.where(i < NT_P, 0, (i - NT_P) % TPS), jnp.where(i < NT_P, 1, TPS)


def _split_specs(width):
    return [
        pl.BlockSpec((TM, width), lambda i: (jnp.minimum(i, NT_P - 1), 0)),
        pl.BlockSpec((TM, width), lambda i: (jnp.maximum(i - NT_P, 0), 0)),
    ]


def _pick(p_ref, s_ref):
    return jnp.where(pl.program_id(0) < NT_P, p_ref[...], s_ref[...])


def _params(sem=("arbitrary",)):
    return pltpu.CompilerParams(dimension_semantics=sem, vmem_limit_bytes=VMEM_LIMIT)


def _rms(x, g):
    ms = jnp.mean(x * x, axis=-1, keepdims=True)
    return x * lax.rsqrt(ms + EPS) * g


def _pre(x, g, shift, scale):
    return _rms(x, g) * (1.0 + scale) + shift


def _mod_kernel(cv_ref, w_ref, b_ref, o_ref):
    cv = cv_ref[...]
    s = cv * jax.nn.sigmoid(cv)
    o_ref[...] = jnp.dot(s, w_ref[...], precision=HIGHEST, preferred_element_type=F32) + b_ref[...]


def _modulation(cv8, w_mod, b_mod):
    nb = 1536
    out = pl.pallas_call(
        _mod_kernel,
        grid=(DEPTH, 6 * D_MODEL // nb),
        in_specs=[
            pl.BlockSpec((8, D_MODEL), lambda l, j: (0, 0)),
            pl.BlockSpec((None, D_MODEL, nb), lambda l, j: (l, 0, j)),
            pl.BlockSpec((None, 1, nb), lambda l, j: (l, 0, j)),
        ],
        out_specs=pl.BlockSpec((None, 8, nb), lambda l, j: (l, 0, j)),
        out_shape=jax.ShapeDtypeStruct((DEPTH, 8, 6 * D_MODEL), F32),
        compiler_params=_params(("arbitrary", "arbitrary")),
        name="modulation",
    )(cv8, w_mod, b_mod.reshape(DEPTH, 1, 6 * D_MODEL))
    return out.reshape(DEPTH, 8, 6, D_MODEL)


def _rope_partner(x):
    lane = lax.broadcasted_iota(jnp.int32, (1, HALF), 1)
    first = (lane % (2 * ROPE_F)) < ROPE_F
    return jnp.where(first, pltpu.roll(x, HALF - ROPE_F, axis=1), pltpu.roll(x, ROPE_F, axis=1))


def _ab_in_kernel(xp_ref, xs_ref, mod_ref, g_ref, w_ref, cos_ref, sin_ref, u_ref, qkv_ref, k_ref, v_ref):
    i = pl.program_id(0)
    h = _pre(_pick(xp_ref, xs_ref), g_ref[...], mod_ref[0:1, :], mod_ref[1:2, :])
    proj = jnp.dot(h.astype(BF16), w_ref[...], preferred_element_type=F32)
    u_ref[...] = proj[:, 0:HALF]
    q = proj[:, HALF:2 * HALF]
    k = proj[:, 2 * HALF:3 * HALF]
    v = proj[:, 3 * HALF:4 * HALF]
    cos = cos_ref[...]
    sin = sin_ref[...]
    qr = q * cos + _rope_partner(q) * sin
    kr = k * cos + _rope_partner(k) * sin
    qkv_ref[:, 0:HALF] = (qr * (1.0 / math.sqrt(DA_DK))).astype(BF16)
    qkv_ref[:, HALF:2 * HALF] = kr.astype(BF16)
    qkv_ref[:, 2 * HALF:3 * HALF] = v.astype(BF16)

    @pl.when(i < NT_P)
    def _():
        k_ref[...] = k
        v_ref[...] = v


def _ab_in(xp, xs, mod, g, w, cos_t, sin_t):
    def rope_tile(i):
        return jnp.where(i < NT_P, 0, 1 + (i - NT_P) % TPS)

    return pl.pallas_call(
        _ab_in_kernel,
        grid=(NT,),
        in_specs=_split_specs(D_MODEL) + [
            pl.BlockSpec((None, 6, D_MODEL), lambda i: (_mod_row(i), 0, 0)),
            pl.BlockSpec((1, D_MODEL), lambda i: (0, 0)),
            pl.BlockSpec((D_MODEL, 4 * HALF), lambda i: (0, 0)),
            pl.BlockSpec((TM, HALF), lambda i: (rope_tile(i), 0)),
            pl.BlockSpec((TM, HALF), lambda i: (rope_tile(i), 0)),
        ],
        out_specs=[
            pl.BlockSpec((TM, HALF), lambda i: (i, 0)),
            pl.BlockSpec((TM, 3 * HALF), lambda i: (i, 0)),
            pl.BlockSpec((TM, HALF), lambda i: (jnp.minimum(i, NT_P - 1), 0)),
            pl.BlockSpec((TM, HALF), lambda i: (jnp.minimum(i, NT_P - 1), 0)),
        ],
        out_shape=[
            jax.ShapeDtypeStruct((N_TOK, HALF), F32),
            jax.ShapeDtypeStruct((N_TOK, 3 * HALF), BF16),
            jax.ShapeDtypeStruct((NT_P * TM, HALF), F32),
            jax.ShapeDtypeStruct((NT_P * TM, HALF), F32),
        ],
        compiler_params=_params(),
        name="ab_in",
    )(xp, xs, mod, g, w, cos_t, sin_t)


def _rope_tables():
    pos = jnp.arange(DEC_SEQ)
    t_row = (pos // GRID_W).astype(F32)
    t_col = (pos % GRID_W).astype(F32)
    inv = ROPE_THETA ** (-jnp.arange(ROPE_F, dtype=F32) / ROPE_F)
    ang = jnp.stack([t_row[:, None] * inv, t_col[:, None] * inv], axis=1)
    cos, sin = jnp.cos(ang), jnp.sin(ang)
    cos64 = jnp.stack([cos, cos], axis=2).reshape(DEC_SEQ, DA_DK)
    sin64 = jnp.stack([-sin, sin], axis=2).reshape(DEC_SEQ, DA_DK)
    reps = HALF // DA_DK
    cos_t = jnp.concatenate([jnp.ones((TM, HALF), F32), jnp.tile(cos64, (1, reps))], axis=0)
    sin_t = jnp.concatenate([jnp.zeros((TM, HALF), F32), jnp.tile(sin64, (1, reps))], axis=0)
    return cos_t, sin_t


def _s5_param_kernel(lre_ref, lim_ref, ldt_ref, bt_ref, c_ref, m1_ref, m2_ref, m3_ref, at_ref):
    nt = (((1,), (1,)), ((), ()))
    e_tab, f_tab, k_tab = [], [], []
    r_i = lax.broadcasted_iota(jnp.int32, (128, 128), 0) // S5_P
    c_i = lax.broadcasted_iota(jnp.int32, (128, 128), 1) // S5_P
    same_group = r_i == c_i
    for d in range(2):
        lre = lre_ref[d]
        lim = lim_ref[d]
        dt = jnp.exp(ldt_ref[d])
        mag = jnp.exp(lre * dt)
        ang = lim * dt
        ar = mag * jnp.cos(ang)
        ai = mag * jnp.sin(ang)
        den = lre * lre + lim * lim
        nr = ar - 1.0
        fr = (nr * lre + ai * lim) / den
        fi = (ai * lre - nr * lim) / den
        btr = bt_ref[d, 0]
        bti = bt_ref[d, 1]
        bfr = fr * btr - fi * bti
        bfi = fr * bti + fi * btr
        cr = c_ref[d, 0]
        ci = c_ref[d, 1]
        cst = jnp.concatenate([cr, -ci], axis=1)
        pr = jnp.ones_like(ar)
        pi = jnp.zeros_like(ar)
        e_d, f_d, k_d = [], [], []
        for tau in range(S5_T + 1):
            f_d.append((cr * pr - ci * pi, cr * pi + ci * pr))
            if tau < S5_T:
                er, ei = pr * bfr - pi * bfi, pr * bfi + pi * bfr
                e_d.append((er, ei))
                full = lax.dot_general(jnp.concatenate([er, ei], axis=1), cst, nt,
                                       precision=HIGHEST, preferred_element_type=F32)
                k_d.append(jnp.where(same_group, full, 0.0))
            else:
                at_ref[d, 0] = pr
                at_ref[d, 1] = pi
            pr, pi = pr * ar - pi * ai, pr * ai + pi * ar
        e_tab.append(e_d)
        f_tab.append(f_d)
        k_tab.append(k_d)

    for s in range(S5_T):
        for t in range(S5_T):
            blk = k_tab[0][t - s] if t >= s else k_tab[1][s - t]
            if t == s:
                blk = blk + k_tab[1][0]
            m1_ref[s * 128:(s + 1) * 128, t * 128:(t + 1) * 128] = blk.astype(BF16)
    row_group = lax.broadcasted_iota(jnp.int32, (128, S5_ST), 0) // S5_P
    for s in range(S5_T):
        ef, er_ = e_tab[0][S5_T - 1 - s], e_tab[1][s]
        v = jnp.concatenate([ef[0], er_[0], ef[1], er_[1]], axis=1)
        ff, fr_ = f_tab[0][s + 1], f_tab[1][S5_T - s]
        w = jnp.concatenate([ff[0], fr_[0], -ff[1], -fr_[1]], axis=1)
        for gi in range(S5_GB):
            sel = row_group == gi
            m2_ref[s * 128:(s + 1) * 128, gi * S5_ST:(gi + 1) * S5_ST] = jnp.where(sel, v, 0.0).astype(BF16)
            m3_ref[s * 128:(s + 1) * 128, gi * S5_ST:(gi + 1) * S5_ST] = jnp.where(sel, w, 0.0).astype(BF16)


def _s5_params(lam_re, lam_im, log_dt, b_w, c_w):
    rows = S5_G * S5_P
    per_row = lambda a: jnp.repeat(a, S5_P, axis=1)
    lre = per_row(lam_re)
    lim = per_row(lam_im)
    ldt = per_row(jnp.broadcast_to(log_dt[:, :, None], (2, S5_G, S5_N)))
    bt = b_w.transpose(0, 1, 2, 4, 3).reshape(2, 2, rows, S5_N)
    cc = c_w.reshape(2, 2, rows, S5_N)
    vec = pl.BlockSpec((2, 128, S5_N), lambda j: (0, j, 0))
    mat = pl.BlockSpec((2, 2, 128, S5_N), lambda j: (0, 0, j, 0))
    wide = lambda n: pl.BlockSpec((None, S5_K, n), lambda j: (j, 0, 0))
    m1, m2, m3, at = pl.pallas_call(
        _s5_param_kernel,
        grid=(S5_NB,),
        in_specs=[vec, vec, vec, mat, mat],
        out_specs=[wide(S5_K), wide(S5_GB * S5_ST), wide(S5_GB * S5_ST),
                   pl.BlockSpec((None, 2, 2, 128, S5_N), lambda j: (j, 0, 0, 0, 0))],
        out_shape=[
            jax.ShapeDtypeStruct((S5_NB, S5_K, S5_K), BF16),
            jax.ShapeDtypeStruct((S5_NB, S5_K, S5_GB * S5_ST), BF16),
            jax.ShapeDtypeStruct((S5_NB, S5_K, S5_GB * S5_ST), BF16),
            jax.ShapeDtypeStruct((S5_NB, 2, 2, 128, S5_N), F32),
        ],
        compiler_params=_params(),
        name="s5_params",
    )(lre, lim, ldt, bt, cc)
    coef = at[:, :, :, ::S5_P, :].transpose(0, 3, 2, 1, 4).reshape(S5_NB, S5_GB, 2, 2 * S5_N)
    return m1, m2, m3, coef


def _s5_kernel(u_ref, m1_ref, m2_ref, m3_ref, coef_ref, d_ref, *rest, stride, nc, two_pass):
    if two_pass:
        h0_ref, y_ref, lhs, yacc, s_scr, hp_scr = rest
    else:
        y_ref, fin_ref, lhs, yacc, s_scr, hp_scr = rest
    ln = 2 * S5_N
    nt = (((1,), (1,)), ((), ()))

    def gather(c2, carry):
        r0 = pl.multiple_of(c2 * 16, 16)
        for t in range(S5_T):
            a = u_ref[pl.ds(c2 * 16 + t, 8, stride=stride), :]
            b = u_ref[pl.ds(c2 * 16 + S5_T + t, 8, stride=stride), :]
            lhs[pl.ds(r0, 16), t * 128:(t + 1) * 128] = jnp.concatenate([a, b], axis=0).astype(BF16)
        return carry

    lax.fori_loop(0, nc // 2, gather, 0)
    lb = lhs[...]
    yacc[...] = jnp.dot(lb, m1_ref[...], preferred_element_type=F32)

    lane = lax.broadcasted_iota(jnp.int32, (8, ln), 1)
    row = lax.broadcasted_iota(jnp.int32, (8, ln), 0)
    fwd = lane < S5_N
    odd = (row % 2) == 1
    wide = S5_GS * S5_ST
    for gq in range(S5_GB // S5_GS):
        groups = range(gq * S5_GS, (gq + 1) * S5_GS)
        cols = slice(gq * wide, (gq + 1) * wide)
        s_scr[...] = jnp.dot(lb, m2_ref[:, cols], preferred_element_type=F32)
        are = [jnp.broadcast_to(coef_ref[g, 0:1, :], (8, ln)) for g in groups]
        aim = [jnp.broadcast_to(coef_ref[g, 1:2, :], (8, ln)) for g in groups]

        def scan(init, store):
            def body(c, carry):
                kf = pl.multiple_of(c * 8, 8)
                kr = pl.multiple_of((nc - 1 - c) * 8, 8)
                out = []
                for j in range(S5_GS):
                    p, q = carry[2 * j], carry[2 * j + 1]
                    o = j * S5_ST
                    if store:
                        hp_scr[pl.ds(kf, 8), o:o + S5_N] = p[:, 0:S5_N]
                        hp_scr[pl.ds(kr, 8), o + S5_N:o + ln] = p[:, S5_N:ln]
                        hp_scr[pl.ds(kf, 8), o + ln:o + ln + S5_N] = q[:, 0:S5_N]
                        hp_scr[pl.ds(kr, 8), o + ln + S5_N:o + 2 * ln] = q[:, S5_N:ln]
                    s_re = jnp.where(fwd, s_scr[pl.ds(kf, 8), o:o + ln], s_scr[pl.ds(kr, 8), o:o + ln])
                    s_im = jnp.where(fwd, s_scr[pl.ds(kf, 8), o + ln:o + 2 * ln],
                                     s_scr[pl.ds(kr, 8), o + ln:o + 2 * ln])
                    out += [are[j] * p - aim[j] * q + s_re, are[j] * q + aim[j] * p + s_im]
                return tuple(out)
            return lax.fori_loop(0, nc, body, tuple(init))

        if two_pass:
            h0 = [h0_ref[g, r] for g in groups for r in range(2)]
            fin = scan(h0, False)
            seed = [h + jnp.where(fwd & odd, pltpu.roll(f, 1, axis=0), 0.0)
                    + jnp.where(jnp.logical_not(fwd | odd), pltpu.roll(f, 7, axis=0), 0.0)
                    for h, f in zip(h0, fin)]
            scan(seed, True)
        else:
            fin = scan([jnp.zeros((8, ln), F32)] * (2 * S5_GS), True)
            for j, g in enumerate(groups):
                fin_ref[g, 0] = fin[2 * j]
                fin_ref[g, 1] = fin[2 * j + 1]
        yacc[...] += lax.dot_general(hp_scr[...].astype(BF16), m3_ref[:, cols], nt, preferred_element_type=F32)

    dvec = d_ref[...]

    def scatter(c, carry):
        r0 = pl.multiple_of(c * 8, 8)
        for t in range(S5_T):
            idx = pl.ds(c * S5_T + t, 8, stride=stride)
            y_ref[idx, :] = yacc[pl.ds(r0, 8), t * 128:(t + 1) * 128] + u_ref[idx, :] * dvec
        return carry

    lax.fori_loop(0, nc, scatter, 0)


def _s5_call(u, row_block, mats, coef, dvec, h0, stride, nc, parts):
    m1, m2, m3 = mats
    rows = 8 * stride
    two_pass = h0 is not None
    single = dict(pipeline_mode=pl.Buffered(1)) if parts == 1 else {}
    mat = lambda n: pl.BlockSpec((None, S5_K, n), lambda j, i: (j, 0, 0), **single)
    tok = pl.BlockSpec((rows, 128), lambda j, i: (row_block * parts + i, j))
    in_specs = [
        tok, mat(S5_K), mat(S5_GB * S5_ST), mat(S5_GB * S5_ST),
        pl.BlockSpec((None, S5_GB, 2, 2 * S5_N), lambda j, i: (j, 0, 0, 0)),
        pl.BlockSpec((None, 1, 128), lambda j, i: (j, 0, 0)),
    ]
    args = [u, m1, m2, m3, coef, dvec]
    out_specs = [pl.BlockSpec((rows, 128), lambda j, i: (i, j))]
    out_shape = [jax.ShapeDtypeStruct((parts * rows, HALF), F32)]
    if two_pass:
        in_specs.append(pl.BlockSpec((None, S5_GB, 2, 8, 2 * S5_N), lambda j, i: (j, 0, 0, 0, 0)))
        args.append(h0)
    else:
        out_specs.append(pl.BlockSpec((None, S5_GB, 2, 8, 2 * S5_N), lambda j, i: (j, 0, 0, i, 0)))
        out_shape.append(jax.ShapeDtypeStruct((S5_NB, S5_GB, 2, parts * 8, 2 * S5_N), F32))
    return pl.pallas_call(
        functools.partial(_s5_kernel, stride=stride, nc=nc, two_pass=two_pass),
        grid=(S5_NB, parts),
        in_specs=in_specs,
        out_specs=out_specs,
        out_shape=out_shape,
        scratch_shapes=[
            pltpu.VMEM((nc * 8, S5_K), BF16),
            pltpu.VMEM((nc * 8, S5_K), F32),
            pltpu.VMEM((nc * 8, S5_GS * S5_ST), F32),
            pltpu.VMEM((nc * 8, S5_GS * S5_ST), F32),
        ],
        compiler_params=_params(("arbitrary", "arbitrary")),
        name="s5_%d" % stride,
    )(*args)


def _s5(u, mats, coef, h0, s5_d):
    dvec = s5_d.reshape(S5_NB, 1, 128)
    yp, fin = _s5_call(u, 0, mats, coef, dvec, None, SEQ, SEQ // S5_T, BATCH // 8)
    hh = h0.transpose(3, 2, 0, 1, 4)
    z = jnp.zeros_like(hh[:, :, :, 0])
    first = jnp.concatenate([hh[:, :, :, 0], z], axis=-1)
    second = jnp.concatenate([z, hh[:, :, :, 1]], axis=-1)
    h0g = jnp.stack([first, second], axis=3).reshape(S5_NB, S5_GB, 2, 2 * DEC_BATCH, 2 * S5_N)
    half = DEC_SEQ // 2
    (ys,) = _s5_call(u, 1, mats, coef, dvec, h0g, half, half // S5_T, 1)
    new_s5 = fin.reshape(S5_G, 2, BATCH, 2, S5_N).transpose(2, 3, 1, 0, 4)
    return yp, ys, new_s5


def _attn_kernel(*refs, lam_init, has_ctx):
    if has_ctx:
        q_ref, k_ref, v_ref, kc_ref, vc_ref, dl_ref, g_ref, o_ref = refs
    else:
        q_ref, k_ref, v_ref, dl_ref, g_ref, o_ref = refs
    dl = dl_ref[...]
    lam = (jnp.exp(jnp.sum(dl[0:1] * dl[1:2], keepdims=True))
           - jnp.exp(jnp.sum(dl[2:3] * dl[3:4], keepdims=True)) + lam_init)
    lane = lax.broadcasted_iota(jnp.int32, (1, DA_DV), 1)
    first = lane < DA_DK
    g = g_ref[...]
    nt = (((1,), (1,)), ((), ()))
    for h in range(DA_HEADS):
        cols = slice(h * DA_DV, (h + 1) * DA_DV)
        qh = q_ref[:, cols]
        zero = jnp.zeros_like(qh)
        es, rs = [], []
        for m in range(2):
            qm = jnp.where(first if m == 0 else jnp.logical_not(first), qh, zero)
            s = [lax.dot_general(qm, k_ref[:, cols], nt, preferred_element_type=F32)]
            if has_ctx:
                s.append(lax.dot_general(qm, kc_ref[:, cols], nt, preferred_element_type=F32))
            mx = functools.reduce(jnp.maximum, [jnp.max(x, axis=-1, keepdims=True) for x in s])
            e = [jnp.exp(x - mx) for x in s]
            es.append(e)
            rs.append(1.0 / sum(jnp.sum(x, axis=-1, keepdims=True) for x in e))
        r1 = lam * rs[1]
        w = [(e0 * rs[0] - e1 * r1).astype(BF16) for e0, e1 in zip(es[0], es[1])]
        o = jnp.dot(w[0], v_ref[:, cols], preferred_element_type=F32)
        if has_ctx:
            o = o + jnp.dot(w[1], vc_ref[:, cols], preferred_element_type=F32)
        o_ref[:, cols] = (_rms(o, g) * (1.0 - lam_init)).astype(BF16)


def _attention(qkv, q_tile0, nb, nq, lk, ctx, da_lam, da_g, lam_init):
    kb0 = q_tile0 * TM // lk
    in_specs = [
        pl.BlockSpec((TM, HALF), lambda b, j: (q_tile0 + b * nq + j, 0)),
        pl.BlockSpec((lk, HALF), lambda b, j: (kb0 + b, 1)),
        pl.BlockSpec((lk, HALF), lambda b, j: (kb0 + b, 2)),
    ]
    args = [qkv, qkv, qkv]
    if ctx is not None:
        in_specs += [pl.BlockSpec((None, PAST_LEN, HALF), lambda b, j: (b, 0, 0))] * 2
        args += list(ctx)
    in_specs += [
        pl.BlockSpec((4, DA_DK), lambda b, j: (0, 0)),
        pl.BlockSpec((1, DA_DV), lambda b, j: (0, 0)),
    ]
    return pl.pallas_call(
        functools.partial(_attn_kernel, lam_init=lam_init, has_ctx=ctx is not None),
        grid=(nb, nq),
        in_specs=in_specs,
        out_specs=pl.BlockSpec((TM, HALF), lambda b, j: (b * nq + j, 0)),
        out_shape=jax.ShapeDtypeStruct((nb * nq * TM, HALF), BF16),
        compiler_params=_params(("arbitrary", "arbitrary")),
        name="diff_attn_%d" % lk,
    )(*args, da_lam, da_g.reshape(1, DA_DV))


def _post(x, y, g, gate):
    return x + gate * _rms(y, g)


def _ab_out_kernel(ysp_ref, yss_ref, ybp_ref, ybs_ref, xp_ref, xs_ref, mod_ref, g_ref, wg_ref, bg_ref, wo_ref,
                   o_ref):
    ys = jax.nn.gelu(_pick(ysp_ref, yss_ref))
    glu = jnp.dot(ys.astype(BF16), wg_ref[...], preferred_element_type=F32) + bg_ref[...]
    ya = ys * jax.nn.sigmoid(glu)
    out = (jnp.dot(ya.astype(BF16), wo_ref[0:HALF, :], preferred_element_type=F32)
           + jnp.dot(_pick(ybp_ref, ybs_ref), wo_ref[HALF:2 * HALF, :], preferred_element_type=F32))
    o_ref[...] = _post(_pick(xp_ref, xs_ref), out, g_ref[...], mod_ref[2:3, :])


def _ab_out(ys5_p, ys5_s, yb_p, yb_s, xp, xs, mod, g, w_glu, b_glu, w_out):
    return pl.pallas_call(
        _ab_out_kernel,
        grid=(NT,),
        in_specs=_split_specs(HALF) + _split_specs(HALF) + _split_specs(D_MODEL) + [
            pl.BlockSpec((None, 6, D_MODEL), lambda i: (_mod_row(i), 0, 0)),
            pl.BlockSpec((1, D_MODEL), lambda i: (0, 0)),
            pl.BlockSpec((HALF, HALF), lambda i: (0, 0)),
            pl.BlockSpec((1, HALF), lambda i: (0, 0)),
            pl.BlockSpec((D_MODEL, D_MODEL), lambda i: (0, 0)),
        ],
        out_specs=pl.BlockSpec((TM, D_MODEL), lambda i: (i, 0)),
        out_shape=jax.ShapeDtypeStruct((N_TOK, D_MODEL), F32),
        compiler_params=_params(),
        name="ab_out",
    )(ys5_p, ys5_s, yb_p, yb_s, xp, xs, mod, g, w_glu, b_glu, w_out)


def _halo_specs(width):
    blocks = TM // HALO
    last = N_TOK // HALO - 1
    return [
        pl.BlockSpec((TM, width), lambda i: (i, 0)),
        pl.BlockSpec((HALO, width), lambda i: (jnp.maximum(i * blocks - 1, 0), 0)),
        pl.BlockSpec((HALO, width), lambda i: (jnp.minimum((i + 1) * blocks, last), 0)),
    ]


def _fill_hbuf(hbuf, x_ref, xp_ref, xn_ref, g, shift, scale):
    i = pl.program_id(0)
    pos, n = _seq_pos(i)
    hp = jnp.where(pos > 0, _pre(xp_ref[...], g, shift, scale), 0.0)
    hn = jnp.where(pos < n - 1, _pre(xn_ref[...], g, shift, scale), 0.0)
    hbuf[0:TM, :] = _pre(x_ref[...], g, shift, scale).astype(BF16)
    hbuf[TM:ROWS, :] = jnp.concatenate([hn, hp], axis=0).astype(BF16)


def _shift_rows(x, s):
    if s == 0:
        return x[0:TM]
    return pltpu.roll(x, (-s) % ROWS, axis=0)[0:TM]


def _ffn_kernel(x_ref, xp_ref, xn_ref, mod_ref, g2_ref, g3_ref, wu_ref, cw_ref, cb_ref, wd_ref, *rest, split):
    hbuf, acc, u_a, u_b = rest[-4:]
    _fill_hbuf(hbuf, x_ref, xp_ref, xn_ref, g2_ref[...], mod_ref[3:4, :], mod_ref[4:5, :])

    def up(j, dst):
        for half in range(2):
            c0 = half * D_FF + j * FF_CHUNK
            dst[half] = jnp.dot(hbuf[...], wu_ref[:, c0:c0 + FF_CHUNK], preferred_element_type=F32)

    def activation(j, src):
        parts = []
        for half in range(2):
            c0 = half * D_FF + j * FF_CHUNK
            sc = 1.0 if half == 0 else 0.5
            cw = cw_ref[:, c0:c0 + FF_CHUNK] * sc
            u = src[half]
            parts.append(_shift_rows(u, -1) * cw[0:1] + _shift_rows(u, 0) * cw[1:2]
                         + _shift_rows(u, 1) * cw[2:3] + cb_ref[:, c0:c0 + FF_CHUNK] * sc)
        gt = parts[0]
        z = gt * (gt * gt * (GELU_C * GELU_K) + GELU_K)
        return ((gt + gt * jnp.tanh(z)) * parts[1]).astype(BF16)

    n_chunks = D_FF // FF_CHUNK
    up(0, u_a)
    for j in range(n_chunks):
        cur, nxt = (u_a, u_b) if j % 2 == 0 else (u_b, u_a)
        if j + 1 < n_chunks:
            up(j + 1, nxt)
        contrib = jnp.dot(activation(j, cur), wd_ref[j * FF_CHUNK:(j + 1) * FF_CHUNK, :],
                          preferred_element_type=F32)
        if j == 0:
            acc[...] = contrib
        else:
            acc[...] += contrib
    res = _post(x_ref[...], acc[...], g3_ref[...], mod_ref[5:6, :])
    if split:
        i = pl.program_id(0)

        @pl.when(i < NT_P)
        def _():
            rest[0][...] = res

        @pl.when(i >= NT_P)
        def _():
            rest[1][...] = res
    else:
        rest[0][...] = res


def _ffn(x, mod, g2, g3, w_up, cw, cb, w_down, layer, split):
    const = lambda i: (0, 0)
    wsel = lambda i: (layer, 0, 0)
    if split:
        out_specs = _split_specs(D_MODEL)
        out_shape = [jax.ShapeDtypeStruct((NT_P * TM, D_MODEL), F32), jax.ShapeDtypeStruct((NT_S * TM, D_MODEL), F32)]
    else:
        out_specs = pl.BlockSpec((TM, D_MODEL), lambda i: (i, 0))
        out_shape = jax.ShapeDtypeStruct((N_TOK, D_MODEL), F32)
    return pl.pallas_call(
        functools.partial(_ffn_kernel, split=split),
        grid=(NT,),
        in_specs=_halo_specs(D_MODEL) + [
            pl.BlockSpec((None, 6, D_MODEL), lambda i: (_mod_row(i), 0, 0)),
            pl.BlockSpec((1, D_MODEL), const),
            pl.BlockSpec((1, D_MODEL), const),
            pl.BlockSpec((None, D_MODEL, 2 * D_FF), wsel, pipeline_mode=pl.Buffered(1)),
            pl.BlockSpec((3, 2 * D_FF), const),
            pl.BlockSpec((1, 2 * D_FF), const),
            pl.BlockSpec((None, D_FF, D_MODEL), wsel, pipeline_mode=pl.Buffered(1)),
        ],
        out_specs=out_specs,
        out_shape=out_shape,
        scratch_shapes=[
            pltpu.VMEM((ROWS, D_MODEL), BF16),
            pltpu.VMEM((TM, D_MODEL), F32),
            pltpu.VMEM((2, ROWS, FF_CHUNK), F32),
            pltpu.VMEM((2, ROWS, FF_CHUNK), F32),
        ],
        compiler_params=_params(),
        name="conv_ffn",
    )(x, x, x, mod, g2, g3, w_up, cw, cb, w_down)


def _softplus(z):
    return jnp.maximum(z, 0.0) + jnp.log(1.0 + jnp.exp(-jnp.abs(z)))


def _lru_scan(a_scr, b_scr, hs_scr, carry, reverse):
    nblk = TM // 8

    def body(blk, h):
        blk = (nblk - 1 - blk) if reverse else blk
        base = pl.multiple_of(blk * 8, 8)
        for j in (range(7, -1, -1) if reverse else range(8)):
            r = pl.ds(base + j, 1)
            h = a_scr[r, :] * h + b_scr[r, :]
            hs_scr[r, :] = h
        return h

    carry[...] = lax.fori_loop(0, nblk, body, carry[...])


def _cd_in_kernel(x_ref, xp_ref, xn_ref, mod_ref, g_ref, w_ref, scw_ref, cw_ref, cb_ref, wg_ref, bg_ref,
                  lam_ref, h0_ref, yc_ref, gate_ref, hsf_ref, ar_ref, br_ref, fin_ref,
                  hbuf, a_scr, b_scr, hs_scr, carry):
    i = pl.program_id(0)
    pos, _ = _seq_pos(i)
    _fill_hbuf(hbuf, x_ref, xp_ref, xn_ref, g_ref[...], mod_ref[0:1, :], mod_ref[1:2, :])

    def col(k, rows):
        return jnp.dot(hbuf[0:rows, :], w_ref[:, k * HALF:(k + 1) * HALF], preferred_element_type=F32)

    xr = col(3, ROWS)
    xin = col(0, ROWS)
    cg = col(2, ROWS)
    cw = cw_ref[...]
    xc = (_shift_rows(xr, -2) * cw[0:1] + _shift_rows(xr, -1) * cw[1:2] + _shift_rows(xr, 0) * cw[2:3]
          + _shift_rows(xr, 1) * cw[3:4] + cb_ref[...])
    pre = jnp.dot(xc.astype(BF16), wg_ref[...], preferred_element_type=F32) + bg_ref[...]
    bg = col(1, TM)
    gb = col(4, TM)
    prod = cg * xin
    scw = scw_ref[...]
    yc = bg * (_shift_rows(prod, -1) * scw[0:1] + _shift_rows(prod, 0) * scw[1:2]
               + _shift_rows(prod, 1) * scw[2:3])
    yc_ref[...] = yc.astype(BF16)
    gate_ref[...] = jax.nn.gelu(gb)
    gates = jax.nn.sigmoid(pre)
    for d in range(2):
        r = gates[:, 2 * d * HALF:(2 * d + 1) * HALF]
        ig = gates[:, (2 * d + 1) * HALF:(2 * d + 2) * HALF]
        log_a = (-LRU_C) * r * _softplus(-lam_ref[d:d + 1, :])
        a = jnp.exp(log_a)
        bval = jnp.sqrt(1.0 - a * a) * (ig * xc)
        if d == 0:
            a_scr[...] = a
            b_scr[...] = bval
        else:
            ar_ref[...] = a
            br_ref[...] = bval

    @pl.when(pos == 0)
    def _():
        carry[...] = h0_ref[0:1, :]

    _lru_scan(a_scr, b_scr, hs_scr, carry, False)
    hsf_ref[...] = hs_scr[...]
    fin_ref[...] = jnp.broadcast_to(carry[...], (8, HALF))


def _cd_in(x, mod, g, w_in, sc_w, conv_w, conv_b, w_gates, b_gates, lru_lam, h0t):
    const = lambda i: (0, 0)
    tok = lambda dt: jax.ShapeDtypeStruct((N_TOK, HALF), dt)
    row = pl.BlockSpec((TM, HALF), lambda i: (i, 0))
    return pl.pallas_call(
        _cd_in_kernel,
        grid=(NT,),
        in_specs=_halo_specs(D_MODEL) + [
            pl.BlockSpec((None, 6, D_MODEL), lambda i: (_mod_row(i), 0, 0)),
            pl.BlockSpec((1, D_MODEL), const),
            pl.BlockSpec((D_MODEL, 5 * HALF), const),
            pl.BlockSpec((3, HALF), const),
            pl.BlockSpec((4, HALF), const),
            pl.BlockSpec((1, HALF), const),
            pl.BlockSpec((HALF, 4 * HALF), const),
            pl.BlockSpec((1, 4 * HALF), const),
            pl.BlockSpec((2, HALF), const),
            pl.BlockSpec((None, 2, HALF), lambda i: (_mod_row(i), 0, 0)),
        ],
        out_specs=[row, row, row, row, row, pl.BlockSpec((8, HALF), lambda i: (i, 0))],
        out_shape=[tok(BF16), tok(F32), tok(F32), tok(F32), tok(F32), jax.ShapeDtypeStruct((NT * 8, HALF), F32)],
        scratch_shapes=[
            pltpu.VMEM((ROWS, D_MODEL), BF16),
            pltpu.VMEM((TM, HALF), F32),
            pltpu.VMEM((TM, HALF), F32),
            pltpu.VMEM((TM, HALF), F32),
            pltpu.VMEM((1, HALF), F32),
        ],
        compiler_params=_params(),
        name="cd_in",
    )(x, x, x, mod, g, w_in, sc_w, conv_w, conv_b, w_gates, b_gates, lru_lam, h0t)


def _cd_out_kernel(ar_ref, br_ref, hsf_ref, gate_ref, yc_ref, x_ref, mod_ref, g_ref, wo_ref, h0_ref,
                   o_ref, fin_ref, hs_scr, carry):
    ti = NT - 1 - pl.program_id(0)
    pos, n = _seq_pos(ti)

    @pl.when(pos == n - 1)
    def _():
        carry[...] = h0_ref[1:2, :]

    _lru_scan(ar_ref, br_ref, hs_scr, carry, True)
    fin_ref[...] = jnp.broadcast_to(carry[...], (8, HALF))
    yd = (hsf_ref[...] + hs_scr[...]) * gate_ref[...]
    out = (jnp.dot(yc_ref[...], wo_ref[0:HALF, :], preferred_element_type=F32)
           + jnp.dot(yd.astype(BF16), wo_ref[HALF:2 * HALF, :], preferred_element_type=F32))
    o_ref[...] = _post(x_ref[...], out, g_ref[...], mod_ref[2:3, :])


def _cd_out(a_r, b_r, hs_f, gate, yc, x, mod, g, w_out, h0t):
    const = lambda i: (0, 0)
    rev = lambda i: (NT - 1 - i, 0)
    row = pl.BlockSpec((TM, HALF), rev)
    return pl.pallas_call(
        _cd_out_kernel,
        grid=(NT,),
        in_specs=[
            row, row, row, row, row,
            pl.BlockSpec((TM, D_MODEL), rev),
            pl.BlockSpec((None, 6, D_MODEL), lambda i: (_mod_row(NT - 1 - i), 0, 0)),
            pl.BlockSpec((1, D_MODEL), const),
            pl.BlockSpec((D_MODEL, D_MODEL), const),
            pl.BlockSpec((None, 2, HALF), lambda i: (_mod_row(NT - 1 - i), 0, 0)),
        ],
        out_specs=[
            pl.BlockSpec((TM, D_MODEL), rev),
            pl.BlockSpec((8, HALF), rev),
        ],
        out_shape=[
            jax.ShapeDtypeStruct((N_TOK, D_MODEL), F32),
            jax.ShapeDtypeStruct((NT * 8, HALF), F32),
        ],
        scratch_shapes=[
            pltpu.VMEM((TM, HALF), F32),
            pltpu.VMEM((1, HALF), F32),
        ],
        compiler_params=_params(),
        name="cd_out",
    )(a_r, b_r, hs_f, gate, yc, x, mod, g, w_out, h0t)


def kernel(x_prompt, x_sample, cache_attn_k, cache_attn_v, state_s5, state_rglru, c, c_ctx, w_mod, b_mod, norm_g, w_in_ab, w_out_ab, s5_lam_re, s5_lam_im, s5_log_dt, s5_b, s5_c, s5_d, s5_w_glu, s5_b_glu, da_lam, da_g, w_in_cd, w_out_cd, sc_conv_w, lru_conv_w, lru_conv_b, lru_w_a, lru_b_a, lru_w_x, lru_b_x, lru_lam, ffn_w_up, ffn_conv_w, ffn_conv_b, ffn_w_down):
    assert DEPTH == 2
    xp = x_prompt.reshape(NT_P * TM, D_MODEL)
    xs = x_sample.reshape(NT_S * TM, D_MODEL)
    cv8 = jnp.zeros((8, D_MODEL), F32).at[0].set(c_ctx).at[1:1 + DEC_BATCH].set(c)
    mod = _modulation(cv8, w_mod, b_mod)
    cos_t, sin_t = _rope_tables()
    g = norm_g.reshape(DEPTH, 4, 1, D_MODEL)

    lam_init = 0.8 - 0.6 * math.exp(-0.3 * 0)
    u, qkv, k32, v32 = _ab_in(xp, xs, mod[0], g[0, 0], w_in_ab[0].astype(BF16), cos_t, sin_t)
    m1, m2, m3, coef = _s5_params(s5_lam_re[0], s5_lam_im[0], s5_log_dt[0], s5_b[0], s5_c[0])
    ys5_p, ys5_s, new_s5 = _s5(u, (m1, m2, m3), coef, state_s5[:, 0], s5_d[0])
    yb_p = _attention(qkv, 0, BATCH, 1, SEQ, None, da_lam[0], da_g[0], lam_init)
    ctx = (cache_attn_k[:, 0].reshape(DEC_BATCH, PAST_LEN, HALF).astype(BF16),
           cache_attn_v[:, 0].reshape(DEC_BATCH, PAST_LEN, HALF).astype(BF16))
    yb_s = _attention(qkv, NT_P, DEC_BATCH, TPS, DEC_SEQ, ctx, da_lam[0], da_g[0], lam_init)
    x = _ab_out(ys5_p, ys5_s, yb_p, yb_s, xp, xs, mod[0], g[0, 1], s5_w_glu[0].astype(BF16), s5_b_glu[0].reshape(1, HALF),
                w_out_ab[0].astype(BF16))
    new_k = k32.reshape(BATCH, 1, SEQ, DA_HEADS, 2, DA_DK)
    new_v = v32.reshape(BATCH, 1, SEQ, DA_HEADS, DA_DV)
    w_up = ffn_w_up.astype(BF16)
    w_down = ffn_w_down.astype(BF16)
    x = _ffn(x, mod[0], g[0, 2], g[0, 3], w_up, ffn_conv_w[0], ffn_conv_b[0].reshape(1, 2 * D_FF), w_down, 0, False)

    eye = jnp.eye(LRU_BLOCKS, dtype=F32)
    dense = lambda w: jnp.einsum('kcd,kl->kcld', w, eye).reshape(LRU_WIDTH, LRU_WIDTH)
    w_gates = jnp.concatenate([dense(lru_w_a[0, 0]), dense(lru_w_x[0, 0]),
                               dense(lru_w_a[0, 1]), dense(lru_w_x[0, 1])], axis=1).astype(BF16)
    b_gates = jnp.concatenate([lru_b_a[0, 0], lru_b_x[0, 0], lru_b_a[0, 1], lru_b_x[0, 1]]).reshape(1, 4 * HALF)
    h0t = jnp.zeros((8, 2, HALF), F32).at[1:1 + DEC_BATCH].set(state_rglru[:, 0])
    yc, gate, hs_f, a_r, b_r, fin_f = _cd_in(x, mod[1], g[1, 0], w_in_cd[0].astype(BF16), sc_conv_w[0], lru_conv_w[0],
                                      lru_conv_b[0].reshape(1, HALF), w_gates, b_gates, lru_lam[0], h0t)
    x, fin_r = _cd_out(a_r, b_r, hs_f, gate, yc, x, mod[1], g[1, 1], w_out_cd[0].astype(BF16), h0t)
    tile_row0 = lambda f: f.reshape(NT, 8, HALF)[:NT_P, 0]
    new_lru = jnp.stack([tile_row0(fin_f), tile_row0(fin_r)], axis=1)[:, None]
    yp, ys = _ffn(x, mod[1], g[1, 2], g[1, 3], w_up, ffn_conv_w[1], ffn_conv_b[1].reshape(1, 2 * D_FF), w_down, 1, True)
    return (yp.reshape(BATCH, SEQ, D_MODEL), ys.reshape(DEC_BATCH, DEC_SEQ, D_MODEL),
            new_k, new_v, new_s5[:, None], new_lru)
```

```python
import functools
import math

import jax
import jax.numpy as jnp
from jax import lax
from jax.experimental import pallas as pl
from jax.experimental.pallas import tpu as pltpu

D_MODEL = 1024
BATCH = 32
SEQ = 256
DEPTH = 2
DEC_BATCH = 4
DEC_SEQ = 2048
PAST_LEN = 512
GRID_W = 64
HALF = D_MODEL // 2
S5_P = 16
S5_G = HALF // S5_P
S5_N = 64
DA_DK = 64
DA_DV = 2 * DA_DK
DA_HEADS = HALF // DA_DV
ROPE_THETA = 10000.0
ROPE_F = DA_DK // 4
LRU_WIDTH = HALF
LRU_BLOCKS = 8
LRU_BS = LRU_WIDTH // LRU_BLOCKS
LRU_C = 8.0
D_FF = 2816
EPS = 1e-6

F32 = jnp.float32
BF16 = jnp.bfloat16
HIGHEST = lax.Precision.HIGHEST

TM = 256
NT_P = BATCH * SEQ // TM
TPS = DEC_SEQ // TM
NT_S = DEC_BATCH * TPS
NT = NT_P + NT_S
N_TOK = NT * TM
HALO = 8
ROWS = TM + 2 * HALO
S5_T = 8
S5_GB = 128 // S5_P
S5_NB = S5_G // S5_GB
S5_K = S5_T * 128
S5_ST = 4 * S5_N
S5_GS = 4
FF_CHUNK = 256
Q_SCALE = math.log2(math.e) / math.sqrt(DA_DK)
GELU_K = math.sqrt(2.0 / math.pi)
GELU_C = 0.044715
VMEM_LIMIT = 56 * 1024 * 1024


def _mod_row(i):
    return jnp.where(i < NT_P, 0, 1 + (i - NT_P) // TPS)


def _seq_pos(i):
    return jnp.where(i < NT_P, 0, (i - NT_P) % TPS), jnp.where(i < NT_P, 1, TPS)


def _split_specs(width):
    return [
        pl.BlockSpec((TM, width), lambda i: (jnp.minimum(i, NT_P - 1), 0)),
        pl.BlockSpec((TM, width), lambda i: (jnp.maximum(i - NT_P, 0), 0)),
    ]


def _pick(p_ref, s_ref):
    return jnp.where(pl.program_id(0) < NT_P, p_ref[...], s_ref[...])


def _params(sem=("arbitrary",)):
    return pltpu.CompilerParams(dimension_semantics=sem, vmem_limit_bytes=VMEM_LIMIT)


def _rms(x, g):
    ms = jnp.mean(x * x, axis=-1, keepdims=True)
    return x * lax.rsqrt(ms + EPS) * g


def _pre(x, g, shift, scale):
    return _rms(x, g) * (1.0 + scale) + shift


def _mod_kernel(cv_ref, w_ref, b_ref, o_ref):
    cv = cv_ref[...]
    s = cv * jax.nn.sigmoid(cv)
    o_ref[...] = jnp.dot(s, w_ref[...], precision=HIGHEST, preferred_element_type=F32) + b_ref[...]


def _modulation(cv8, w_mod, b_mod):
    nb = 1536
    out = pl.pallas_call(
        _mod_kernel,
        grid=(DEPTH, 6 * D_MODEL // nb),
        in_specs=[
            pl.BlockSpec((8, D_MODEL), lambda l, j: (0, 0)),
            pl.BlockSpec((None, D_MODEL, nb), lambda l, j: (l, 0, j)),
            pl.BlockSpec((None, 1, nb), lambda l, j: (l, 0, j)),
        ],
        out_specs=pl.BlockSpec((None, 8, nb), lambda l, j: (l, 0, j)),
        out_shape=jax.ShapeDtypeStruct((DEPTH, 8, 6 * D_MODEL), F32),
        compiler_params=_params(("arbitrary", "arbitrary")),
        name="modulation",
    )(cv8, w_mod, b_mod.reshape(DEPTH, 1, 6 * D_MODEL))
    return out.reshape(DEPTH, 8, 6, D_MODEL)


def _rope_partner(x):
    lane = lax.broadcasted_iota(jnp.int32, (1, HALF), 1)
    first = (lane % (2 * ROPE_F)) < ROPE_F
    return jnp.where(first, pltpu.roll(x, HALF - ROPE_F, axis=1), pltpu.roll(x, ROPE_F, axis=1))


def _ab_in_kernel(xp_ref, xs_ref, mod_ref, g_ref, w_ref, cos_ref, sin_ref, u_ref, qkv_ref, k_ref, v_ref):
    i = pl.program_id(0)
    h = _pre(_pick(xp_ref, xs_ref), g_ref[...], mod_ref[0:1, :], mod_ref[1:2, :])
    proj = jnp.dot(h.astype(BF16), w_ref[...], preferred_element_type=F32)
    u_ref[...] = proj[:, 0:HALF]
    q = proj[:, HALF:2 * HALF]
    k = proj[:, 2 * HALF:3 * HALF]
    v = proj[:, 3 * HALF:4 * HALF]
    cos = cos_ref[...]
    sin = sin_ref[...]
    qr = q * cos + _rope_partner(q) * sin
    kr = k * cos + _rope_partner(k) * sin
    qkv_ref[:, 0:HALF] = (qr * Q_SCALE).astype(BF16)
    qkv_ref[:, HALF:2 * HALF] = kr.astype(BF16)
    qkv_ref[:, 2 * HALF:3 * HALF] = v.astype(BF16)

    @pl.when(i < NT_P)
    def _():
        k_ref[...] = k
        v_ref[...] = v


def _ab_in(xp, xs, mod, g, w, cos_t, sin_t):
    def rope_tile(i):
        return jnp.where(i < NT_P, 0, 1 + (i - NT_P) % TPS)

    return pl.pallas_call(
        _ab_in_kernel,
        grid=(NT,),
        in_specs=_split_specs(D_MODEL) + [
            pl.BlockSpec((None, 6, D_MODEL), lambda i: (_mod_row(i), 0, 0)),
            pl.BlockSpec((1, D_MODEL), lambda i: (0, 0)),
            pl.BlockSpec((D_MODEL, 4 * HALF), lambda i: (0, 0)),
            pl.BlockSpec((TM, HALF), lambda i: (rope_tile(i), 0)),
            pl.BlockSpec((TM, HALF), lambda i: (rope_tile(i), 0)),
        ],
        out_specs=[
            pl.BlockSpec((TM, HALF), lambda i: (i, 0)),
            pl.BlockSpec((TM, 3 * HALF), lambda i: (i, 0)),
            pl.BlockSpec((TM, HALF), lambda i: (jnp.minimum(i, NT_P - 1), 0)),
            pl.BlockSpec((TM, HALF), lambda i: (jnp.minimum(i, NT_P - 1), 0)),
        ],
        out_shape=[
            jax.ShapeDtypeStruct((N_TOK, HALF), F32),
            jax.ShapeDtypeStruct((N_TOK, 3 * HALF), BF16),
            jax.ShapeDtypeStruct((NT_P * TM, HALF), F32),
            jax.ShapeDtypeStruct((NT_P * TM, HALF), F32),
        ],
        compiler_params=_params(),
        name="ab_in",
    )(xp, xs, mod, g, w, cos_t, sin_t)


def _rope_tables():
    pos = jnp.arange(DEC_SEQ)
    t_row = (pos // GRID_W).astype(F32)
    t_col = (pos % GRID_W).astype(F32)
    inv = ROPE_THETA ** (-jnp.arange(ROPE_F, dtype=F32) / ROPE_F)
    ang = jnp.stack([t_row[:, None] * inv, t_col[:, None] * inv], axis=1)
    cos, sin = jnp.cos(ang), jnp.sin(ang)
    cos64 = jnp.stack([cos, cos], axis=2).reshape(DEC_SEQ, DA_DK)
    sin64 = jnp.stack([-sin, sin], axis=2).reshape(DEC_SEQ, DA_DK)
    reps = HALF // DA_DK
    cos_t = jnp.concatenate([jnp.ones((TM, HALF), F32), jnp.tile(cos64, (1, reps))], axis=0)
    sin_t = jnp.concatenate([jnp.zeros((TM, HALF), F32), jnp.tile(sin64, (1, reps))], axis=0)
    return cos_t, sin_t


def _s5_param_kernel(lre_ref, lim_ref, ldt_ref, bt_ref, c_ref, m1_ref, m2_ref, m3_ref, at_ref):
    nt = (((1,), (1,)), ((), ()))
    e_tab, f_tab, k_tab = [], [], []
    r_i = lax.broadcasted_iota(jnp.int32, (128, 128), 0) // S5_P
    c_i = lax.broadcasted_iota(jnp.int32, (128, 128), 1) // S5_P
    same_group = r_i == c_i
    for d in range(2):
        lre = lre_ref[d]
        lim = lim_ref[d]
        dt = jnp.exp(ldt_ref[d])
        mag = jnp.exp(lre * dt)
        ang = lim * dt
        ar = mag * jnp.cos(ang)
        ai = mag * jnp.sin(ang)
        den = lre * lre + lim * lim
        nr = ar - 1.0
        fr = (nr * lre + ai * lim) / den
        fi = (ai * lre - nr * lim) / den
        btr = bt_ref[d, 0]
        bti = bt_ref[d, 1]
        bfr = fr * btr - fi * bti
        bfi = fr * bti + fi * btr
        cr = c_ref[d, 0]
        ci = c_ref[d, 1]
        cst = jnp.concatenate([cr, -ci], axis=1)
        pr = jnp.ones_like(ar)
        pi = jnp.zeros_like(ar)
        e_d, f_d, k_d = [], [], []
        for tau in range(S5_T + 1):
            f_d.append((cr * pr - ci * pi, cr * pi + ci * pr))
            if tau < S5_T:
                er, ei = pr * bfr - pi * bfi, pr * bfi + pi * bfr
                e_d.append((er, ei))
                full = lax.dot_general(jnp.concatenate([er, ei], axis=1), cst, nt,
                                       precision=HIGHEST, preferred_element_type=F32)
                k_d.append(jnp.where(same_group, full, 0.0))
            else:
                at_ref[d, 0] = pr
                at_ref[d, 1] = pi
            pr, pi = pr * ar - pi * ai, pr * ai + pi * ar
        e_tab.append(e_d)
        f_tab.append(f_d)
        k_tab.append(k_d)

    for s in range(S5_T):
        for t in range(S5_T):
            blk = k_tab[0][t - s] if t >= s else k_tab[1][s - t]
            if t == s:
                blk = blk + k_tab[1][0]
            m1_ref[s * 128:(s + 1) * 128, t * 128:(t + 1) * 128] = blk.astype(BF16)
    row_group = lax.broadcasted_iota(jnp.int32, (128, S5_ST), 0) // S5_P
    for s in range(S5_T):
        ef, er_ = e_tab[0][S5_T - 1 - s], e_tab[1][s]
        v = jnp.concatenate([ef[0], er_[0], ef[1], er_[1]], axis=1)
        ff, fr_ = f_tab[0][s + 1], f_tab[1][S5_T - s]
        w = jnp.concatenate([ff[0], fr_[0], -ff[1], -fr_[1]], axis=1)
        for gi in range(S5_GB):
            sel = row_group == gi
            m2_ref[s * 128:(s + 1) * 128, gi * S5_ST:(gi + 1) * S5_ST] = jnp.where(sel, v, 0.0).astype(BF16)
            m3_ref[s * 128:(s + 1) * 128, gi * S5_ST:(gi + 1) * S5_ST] = jnp.where(sel, w, 0.0).astype(BF16)


def _s5_params(lam_re, lam_im, log_dt, b_w, c_w):
    rows = S5_G * S5_P
    per_row = lambda a: jnp.repeat(a, S5_P, axis=1)
    lre = per_row(lam_re)
    lim = per_row(lam_im)
    ldt = per_row(jnp.broadcast_to(log_dt[:, :, None], (2, S5_G, S5_N)))
    bt = b_w.transpose(0, 1, 2, 4, 3).reshape(2, 2, rows, S5_N)
    cc = c_w.reshape(2, 2, rows, S5_N)
    vec = pl.BlockSpec((2, 128, S5_N), lambda j: (0, j, 0))
    mat = pl.BlockSpec((2, 2, 128, S5_N), lambda j: (0, 0, j, 0))
    wide = lambda n: pl.BlockSpec((None, S5_K, n), lambda j: (j, 0, 0))
    m1, m2, m3, at = pl.pallas_call(
        _s5_param_kernel,
        grid=(S5_NB,),
        in_specs=[vec, vec, vec, mat, mat],
        out_specs=[wide(S5_K), wide(S5_GB * S5_ST), wide(S5_GB * S5_ST),
                   pl.BlockSpec((None, 2, 2, 128, S5_N), lambda j: (j, 0, 0, 0, 0))],
        out_shape=[
            jax.ShapeDtypeStruct((S5_NB, S5_K, S5_K), BF16),
            jax.ShapeDtypeStruct((S5_NB, S5_K, S5_GB * S5_ST), BF16),
            jax.ShapeDtypeStruct((S5_NB, S5_K, S5_GB * S5_ST), BF16),
            jax.ShapeDtypeStruct((S5_NB, 2, 2, 128, S5_N), F32),
        ],
        compiler_params=_params(),
        name="s5_params",
    )(lre, lim, ldt, bt, cc)
    coef = at[:, :, :, ::S5_P, :].transpose(0, 3, 2, 1, 4).reshape(S5_NB, S5_GB, 2, 2 * S5_N)
    return m1, m2, m3, coef


def _s5_kernel(u_ref, m1_ref, m2_ref, m3_ref, coef_ref, d_ref, *rest, stride, nc, two_pass):
    if two_pass:
        h0_ref, y_ref, lhs, yacc, s_scr, hp_scr = rest
    else:
        y_ref, fin_ref, lhs, yacc, s_scr, hp_scr = rest
    ln = 2 * S5_N
    nt = (((1,), (1,)), ((), ()))

    def seq_rows(k, c):
        return pl.ds(pl.multiple_of(k * stride + c * S5_T, S5_T), S5_T)

    def gather(c2, carry):
        halves = []
        for c in (2 * c2, 2 * c2 + 1):
            tiles = [u_ref[seq_rows(k, c), :] for k in range(8)]
            halves.append(jnp.stack(tiles, axis=0).reshape(8, S5_K))
        lhs[pl.ds(pl.multiple_of(c2 * 16, 16), 16), :] = jnp.concatenate(halves, axis=0).astype(BF16)
        return carry

    lax.fori_loop(0, nc // 2, gather, 0)
    lb = lhs[...]
    yacc[...] = jnp.dot(lb, m1_ref[...], preferred_element_type=F32)

    lane = lax.broadcasted_iota(jnp.int32, (8, ln), 1)
    row = lax.broadcasted_iota(jnp.int32, (8, ln), 0)
    fwd = lane < S5_N
    odd = (row % 2) == 1
    wide = S5_GS * S5_ST
    for gq in range(S5_GB // S5_GS):
        groups = range(gq * S5_GS, (gq + 1) * S5_GS)
        cols = slice(gq * wide, (gq + 1) * wide)
        s_scr[...] = jnp.dot(lb, m2_ref[:, cols], preferred_element_type=F32)
        are = [jnp.broadcast_to(coef_ref[g, 0:1, :], (8, ln)) for g in groups]
        aim = [jnp.broadcast_to(coef_ref[g, 1:2, :], (8, ln)) for g in groups]

        def scan(init, store):
            def body(c, carry):
                kf = pl.multiple_of(c * 8, 8)
                kr = pl.multiple_of((nc - 1 - c) * 8, 8)
                out = []
                for j in range(S5_GS):
                    p, q = carry[2 * j], carry[2 * j + 1]
                    o = j * S5_ST
                    if store:
                        hp_scr[pl.ds(kf, 8), o:o + S5_N] = p[:, 0:S5_N]
                        hp_scr[pl.ds(kr, 8), o + S5_N:o + ln] = p[:, S5_N:ln]
                        hp_scr[pl.ds(kf, 8), o + ln:o + ln + S5_N] = q[:, 0:S5_N]
                        hp_scr[pl.ds(kr, 8), o + ln + S5_N:o + 2 * ln] = q[:, S5_N:ln]
                    s_re = jnp.where(fwd, s_scr[pl.ds(kf, 8), o:o + ln], s_scr[pl.ds(kr, 8), o:o + ln])
                    s_im = jnp.where(fwd, s_scr[pl.ds(kf, 8), o + ln:o + 2 * ln],
                                     s_scr[pl.ds(kr, 8), o + ln:o + 2 * ln])
                    out += [are[j] * p - aim[j] * q + s_re, are[j] * q + aim[j] * p + s_im]
                return tuple(out)
            return lax.fori_loop(0, nc, body, tuple(init))

        if two_pass:
            h0 = [h0_ref[g, r] for g in groups for r in range(2)]
            fin = scan(h0, False)
            seed = [h + jnp.where(fwd & odd, pltpu.roll(f, 1, axis=0), 0.0)
                    + jnp.where(jnp.logical_not(fwd | odd), pltpu.roll(f, 7, axis=0), 0.0)
                    for h, f in zip(h0, fin)]
            scan(seed, True)
        else:
            fin = scan([jnp.zeros((8, ln), F32)] * (2 * S5_GS), True)
            for j, g in enumerate(groups):
                fin_ref[g, 0] = fin[2 * j]
                fin_ref[g, 1] = fin[2 * j + 1]
        yacc[...] += lax.dot_general(hp_scr[...].astype(BF16), m3_ref[:, cols], nt, preferred_element_type=F32)

    dvec = d_ref[...]

    def scatter(c, carry):
        blk = yacc[pl.ds(pl.multiple_of(c * 8, 8), 8), :].reshape(8, S5_T, 128)
        for k in range(8):
            rows = seq_rows(k, c)
            y_ref[rows, :] = blk[k] + u_ref[rows, :] * dvec
        return carry

    lax.fori_loop(0, nc, scatter, 0)


def _s5_call(u, row_block, mats, coef, dvec, h0, stride, nc, parts):
    m1, m2, m3 = mats
    rows = 8 * stride
    two_pass = h0 is not None
    single = dict(pipeline_mode=pl.Buffered(1)) if parts == 1 else {}
    mat = lambda n: pl.BlockSpec((None, S5_K, n), lambda j, i: (j, 0, 0), **single)
    tok = pl.BlockSpec((rows, 128), lambda j, i: (row_block * parts + i, j))
    in_specs = [
        tok, mat(S5_K), mat(S5_GB * S5_ST), mat(S5_GB * S5_ST),
        pl.BlockSpec((None, S5_GB, 2, 2 * S5_N), lambda j, i: (j, 0, 0, 0)),
        pl.BlockSpec((None, 1, 128), lambda j, i: (j, 0, 0)),
    ]
    args = [u, m1, m2, m3, coef, dvec]
    out_specs = [pl.BlockSpec((rows, 128), lambda j, i: (i, j))]
    out_shape = [jax.ShapeDtypeStruct((parts * rows, HALF), F32)]
    if two_pass:
        in_specs.append(pl.BlockSpec((None, S5_GB, 2, 8, 2 * S5_N), lambda j, i: (j, 0, 0, 0, 0)))
        args.append(h0)
    else:
        out_specs.append(pl.BlockSpec((None, S5_GB, 2, 8, 2 * S5_N), lambda j, i: (j, 0, 0, i, 0)))
        out_shape.append(jax.ShapeDtypeStruct((S5_NB, S5_GB, 2, parts * 8, 2 * S5_N), F32))
    return pl.pallas_call(
        functools.partial(_s5_kernel, stride=stride, nc=nc, two_pass=two_pass),
        grid=(S5_NB, parts),
        in_specs=in_specs,
        out_specs=out_specs,
        out_shape=out_shape,
        scratch_shapes=[
            pltpu.VMEM((nc * 8, S5_K), BF16),
            pltpu.VMEM((nc * 8, S5_K), F32),
            pltpu.VMEM((nc * 8, S5_GS * S5_ST), F32),
            pltpu.VMEM((nc * 8, S5_GS * S5_ST), F32),
        ],
        compiler_params=_params(("arbitrary", "arbitrary")),
        name="s5_%d" % stride,
    )(*args)


def _s5(u, mats, coef, h0, s5_d):
    dvec = s5_d.reshape(S5_NB, 1, 128)
    yp, fin = _s5_call(u, 0, mats, coef, dvec, None, SEQ, SEQ // S5_T, BATCH // 8)
    hh = h0.transpose(3, 2, 0, 1, 4)
    z = jnp.zeros_like(hh[:, :, :, 0])
    first = jnp.concatenate([hh[:, :, :, 0], z], axis=-1)
    second = jnp.concatenate([z, hh[:, :, :, 1]], axis=-1)
    h0g = jnp.stack([first, second], axis=3).reshape(S5_NB, S5_GB, 2, 2 * DEC_BATCH, 2 * S5_N)
    half = DEC_SEQ // 2
    (ys,) = _s5_call(u, 1, mats, coef, dvec, h0g, half, half // S5_T, 1)
    new_s5 = fin.reshape(S5_G, 2, BATCH, 2, S5_N).transpose(2, 3, 1, 0, 4)
    return yp, ys, new_s5


def _attn_kernel(*refs, lam_init, has_ctx, lk):
    if has_ctx:
        q_ref, k_ref, v_ref, kc_ref, vc_ref, dl_ref, g_ref, o_ref, s_a, s_b, vx, vcx = refs
    else:
        q_ref, k_ref, v_ref, dl_ref, g_ref, o_ref, s_a, s_b, vx = refs
    lt = lk + (PAST_LEN if has_ctx else 0)
    dl = dl_ref[...]
    lam = (jnp.exp(jnp.sum(dl[0:1] * dl[1:2], keepdims=True))
           - jnp.exp(jnp.sum(dl[2:3] * dl[3:4], keepdims=True)) + lam_init)
    lane = lax.broadcasted_iota(jnp.int32, (1, DA_DV), 1)
    first = lane < DA_DK
    g = g_ref[...]
    nt = (((1,), (1,)), ((), ()))

    @pl.when(pl.program_id(1) == 0)
    def _():
        onehot = (lane == 0).astype(BF16)
        for h in range(DA_HEADS):
            cols = slice(h * DA_DV, (h + 1) * DA_DV)
            vx[:, 2 * h * DA_DV:(2 * h + 1) * DA_DV] = v_ref[:, cols]
            vx[:, (2 * h + 1) * DA_DV:(2 * h + 2) * DA_DV] = jnp.broadcast_to(onehot, (lk, DA_DV))
            if has_ctx:
                vcx[:, 2 * h * DA_DV:(2 * h + 1) * DA_DV] = vc_ref[:, cols]
                vcx[:, (2 * h + 1) * DA_DV:(2 * h + 2) * DA_DV] = jnp.broadcast_to(onehot, (PAST_LEN, DA_DV))

    def scores(h, dst):
        cols = slice(h * DA_DV, (h + 1) * DA_DV)
        qh = q_ref[:, cols]
        zero = jnp.zeros_like(qh)
        for m in range(2):
            qm = jnp.where(first if m == 0 else jnp.logical_not(first), qh, zero)
            dst[m, :, 0:lk] = lax.dot_general(qm, k_ref[:, cols], nt, preferred_element_type=F32)
            if has_ctx:
                dst[m, :, lk:lt] = lax.dot_general(qm, kc_ref[:, cols], nt, preferred_element_type=F32)

    def head(h, src):
        xcols = slice(2 * h * DA_DV, (2 * h + 2) * DA_DV)
        outs = []
        for m in range(2):
            s = src[m]
            e = jnp.exp2(s - jnp.max(s, axis=-1, keepdims=True)).astype(BF16)
            ox = jnp.dot(e[:, 0:lk], vx[:, xcols], preferred_element_type=F32)
            if has_ctx:
                ox = ox + jnp.dot(e[:, lk:lt], vcx[:, xcols], preferred_element_type=F32)
            outs.append(ox[:, 0:DA_DV] * (1.0 / ox[:, DA_DV:DA_DV + 1]))
        o = outs[0] - lam * outs[1]
        o_ref[:, h * DA_DV:(h + 1) * DA_DV] = (_rms(o, g) * (1.0 - lam_init)).astype(BF16)

    scores(0, s_a)
    for h in range(DA_HEADS):
        cur, nxt = (s_a, s_b) if h % 2 == 0 else (s_b, s_a)
        if h + 1 < DA_HEADS:
            scores(h + 1, nxt)
        head(h, cur)


def _attention(qkv, q_tile0, nb, nq, lk, ctx, da_lam, da_g, lam_init):
    kb0 = q_tile0 * TM // lk
    in_specs = [
        pl.BlockSpec((TM, HALF), lambda b, j: (q_tile0 + b * nq + j, 0)),
        pl.BlockSpec((lk, HALF), lambda b, j: (kb0 + b, 1)),
        pl.BlockSpec((lk, HALF), lambda b, j: (kb0 + b, 2)),
    ]
    args = [qkv, qkv, qkv]
    if ctx is not None:
        in_specs += [pl.BlockSpec((None, PAST_LEN, HALF), lambda b, j: (b, 0, 0))] * 2
        args += list(ctx)
    in_specs += [
        pl.BlockSpec((4, DA_DK), lambda b, j: (0, 0)),
        pl.BlockSpec((1, DA_DV), lambda b, j: (0, 0)),
    ]
    lt = lk + (PAST_LEN if ctx is not None else 0)
    scratch = [pltpu.VMEM((2, TM, lt), F32), pltpu.VMEM((2, TM, lt), F32), pltpu.VMEM((lk, 2 * HALF), BF16)]
    if ctx is not None:
        scratch.append(pltpu.VMEM((PAST_LEN, 2 * HALF), BF16))
    return pl.pallas_call(
        functools.partial(_attn_kernel, lam_init=lam_init, has_ctx=ctx is not None, lk=lk),
        grid=(nb, nq),
        in_specs=in_specs,
        out_specs=pl.BlockSpec((TM, HALF), lambda b, j: (b * nq + j, 0)),
        out_shape=jax.ShapeDtypeStruct((nb * nq * TM, HALF), BF16),
        scratch_shapes=scratch,
        compiler_params=_params(("arbitrary", "arbitrary")),
        name="diff_attn_%d" % lk,
    )(*args, da_lam, da_g.reshape(1, DA_DV))


def _post(x, y, g, gate):
    return x + gate * _rms(y, g)


def _ab_out_kernel(ysp_ref, yss_ref, ybp_ref, ybs_ref, xp_ref, xs_ref, mod_ref, g_ref, wg_ref, bg_ref, wo_ref,
                   o_ref):
    ys = jax.nn.gelu(_pick(ysp_ref, yss_ref))
    glu = jnp.dot(ys.astype(BF16), wg_ref[...], preferred_element_type=F32) + bg_ref[...]
    ya = ys * jax.nn.sigmoid(glu)
    out = (jnp.dot(ya.astype(BF16), wo_ref[0:HALF, :], preferred_element_type=F32)
           + jnp.dot(_pick(ybp_ref, ybs_ref), wo_ref[HALF:2 * HALF, :], preferred_element_type=F32))
    o_ref[...] = _post(_pick(xp_ref, xs_ref), out, g_ref[...], mod_ref[2:3, :])


def _ab_out(ys5_p, ys5_s, yb_p, yb_s, xp, xs, mod, g, w_glu, b_glu, w_out):
    return pl.pallas_call(
        _ab_out_kernel,
        grid=(NT,),
        in_specs=_split_specs(HALF) + _split_specs(HALF) + _split_specs(D_MODEL) + [
            pl.BlockSpec((None, 6, D_MODEL), lambda i: (_mod_row(i), 0, 0)),
            pl.BlockSpec((1, D_MODEL), lambda i: (0, 0)),
            pl.BlockSpec((HALF, HALF), lambda i: (0, 0)),
            pl.BlockSpec((1, HALF), lambda i: (0, 0)),
            pl.BlockSpec((D_MODEL, D_MODEL), lambda i: (0, 0)),
        ],
        out_specs=pl.BlockSpec((TM, D_MODEL), lambda i: (i, 0)),
        out_shape=jax.ShapeDtypeStruct((N_TOK, D_MODEL), F32),
        compiler_params=_params(),
        name="ab_out",
    )(ys5_p, ys5_s, yb_p, yb_s, xp, xs, mod, g, w_glu, b_glu, w_out)


def _halo_specs(width):
    blocks = TM // HALO
    last = N_TOK // HALO - 1
    return [
        pl.BlockSpec((TM, width), lambda i: (i, 0)),
        pl.BlockSpec((HALO, width), lambda i: (jnp.maximum(i * blocks - 1, 0), 0)),
        pl.BlockSpec((HALO, width), lambda i: (jnp.minimum((i + 1) * blocks, last), 0)),
    ]


def _fill_hbuf(hbuf, x_ref, xp_ref, xn_ref, g, shift, scale):
    i = pl.program_id(0)
    pos, n = _seq_pos(i)
    hp = jnp.where(pos > 0, _pre(xp_ref[...], g, shift, scale), 0.0)
    hn = jnp.where(pos < n - 1, _pre(xn_ref[...], g, shift, scale), 0.0)
    hbuf[0:TM, :] = _pre(x_ref[...], g, shift, scale).astype(BF16)
    hbuf[TM:ROWS, :] = jnp.concatenate([hn, hp], axis=0).astype(BF16)


def _shift_rows(x, s):
    if s == 0:
        return x[0:TM]
    return pltpu.roll(x, (-s) % ROWS, axis=0)[0:TM]


def _ffn_kernel(x_ref, xp_ref, xn_ref, mod_ref, g2_ref, g3_ref, wu_ref, cw_ref, cb_ref, wd_ref, *rest, split):
    hbuf, acc, u_a, u_b = rest[-4:]
    _fill_hbuf(hbuf, x_ref, xp_ref, xn_ref, g2_ref[...], mod_ref[3:4, :], mod_ref[4:5, :])

    def up(j, dst):
        for half in range(2):
            c0 = half * D_FF + j * FF_CHUNK
            dst[half] = jnp.dot(hbuf[...], wu_ref[:, c0:c0 + FF_CHUNK], preferred_element_type=F32)

    def activation(j, src):
        parts = []
        for half in range(2):
            c0 = half * D_FF + j * FF_CHUNK
            sc = 1.0 if half == 0 else 0.5
            cw = cw_ref[:, c0:c0 + FF_CHUNK] * sc
            u = src[half]
            parts.append(_shift_rows(u, -1) * cw[0:1] + _shift_rows(u, 0) * cw[1:2]
                         + _shift_rows(u, 1) * cw[2:3] + cb_ref[:, c0:c0 + FF_CHUNK] * sc)
        gt = parts[0]
        z = gt * (gt * gt * (GELU_C * GELU_K) + GELU_K)
        return ((gt + gt * jnp.tanh(z)) * parts[1]).astype(BF16)

    n_chunks = D_FF // FF_CHUNK
    up(0, u_a)
    for j in range(n_chunks):
        cur, nxt = (u_a, u_b) if j % 2 == 0 else (u_b, u_a)
        if j + 1 < n_chunks:
            up(j + 1, nxt)
        contrib = jnp.dot(activation(j, cur), wd_ref[j * FF_CHUNK:(j + 1) * FF_CHUNK, :],
                          preferred_element_type=F32)
        if j == 0:
            acc[...] = contrib
        else:
            acc[...] += contrib
    res = _post(x_ref[...], acc[...], g3_ref[...], mod_ref[5:6, :])
    if split:
        i = pl.program_id(0)

        @pl.when(i < NT_P)
        def _():
            rest[0][...] = res

        @pl.when(i >= NT_P)
        def _():
            rest[1][...] = res
    else:
        rest[0][...] = res


def _ffn(x, mod, g2, g3, w_up, cw, cb, w_down, layer, split):
    const = lambda i: (0, 0)
    wsel = lambda i: (layer, 0, 0)
    if split:
        out_specs = _split_specs(D_MODEL)
        out_shape = [jax.ShapeDtypeStruct((NT_P * TM, D_MODEL), F32), jax.ShapeDtypeStruct((NT_S * TM, D_MODEL), F32)]
    else:
        out_specs = pl.BlockSpec((TM, D_MODEL), lambda i: (i, 0))
        out_shape = jax.ShapeDtypeStruct((N_TOK, D_MODEL), F32)
    return pl.pallas_call(
        functools.partial(_ffn_kernel, split=split),
        grid=(NT,),
        in_specs=_halo_specs(D_MODEL) + [
            pl.BlockSpec((None, 6, D_MODEL), lambda i: (_mod_row(i), 0, 0)),
            pl.BlockSpec((1, D_MODEL), const),
            pl.BlockSpec((1, D_MODEL), const),
            pl.BlockSpec((None, D_MODEL, 2 * D_FF), wsel, pipeline_mode=pl.Buffered(1)),
            pl.BlockSpec((3, 2 * D_FF), const),
            pl.BlockSpec((1, 2 * D_FF), const),
            pl.BlockSpec((None, D_FF, D_MODEL), wsel, pipeline_mode=pl.Buffered(1)),
        ],
        out_specs=out_specs,
        out_shape=out_shape,
        scratch_shapes=[
            pltpu.VMEM((ROWS, D_MODEL), BF16),
            pltpu.VMEM((TM, D_MODEL), F32),
            pltpu.VMEM((2, ROWS, FF_CHUNK), F32),
            pltpu.VMEM((2, ROWS, FF_CHUNK), F32),
        ],
        compiler_params=_params(),
        name="conv_ffn",
    )(x, x, x, mod, g2, g3, w_up, cw, cb, w_down)


def _softplus(z):
    return jnp.maximum(z, 0.0) + jnp.log(1.0 + jnp.exp(-jnp.abs(z)))


def _lru_scan(a_scr, b_scr, hs_scr, carry, reverse):
    nblk = TM // 8

    def body(blk, h):
        blk = (nblk - 1 - blk) if reverse else blk
        base = pl.multiple_of(blk * 8, 8)
        for j in (range(7, -1, -1) if reverse else range(8)):
            r = pl.ds(base + j, 1)
            h = a_scr[r, :] * h + b_scr[r, :]
            hs_scr[r, :] = h
        return h

    carry[...] = lax.fori_loop(0, nblk, body, carry[...])


def _cd_in_kernel(x_ref, xp_ref, xn_ref, mod_ref, g_ref, w_ref, scw_ref, cw_ref, cb_ref, wg_ref, bg_ref,
                  lam_ref, h0_ref, yc_ref, gate_ref, hsf_ref, ar_ref, br_ref, fin_ref,
                  hbuf, a_scr, b_scr, hs_scr, carry):
    i = pl.program_id(0)
    pos, _ = _seq_pos(i)
    _fill_hbuf(hbuf, x_ref, xp_ref, xn_ref, g_ref[...], mod_ref[0:1, :], mod_ref[1:2, :])

    def col(k, rows):
        return jnp.dot(hbuf[0:rows, :], w_ref[:, k * HALF:(k + 1) * HALF], preferred_element_type=F32)

    xr = col(3, ROWS)
    xin = col(0, ROWS)
    cg = col(2, ROWS)
    cw = cw_ref[...]
    xc = (_shift_rows(xr, -2) * cw[0:1] + _shift_rows(xr, -1) * cw[1:2] + _shift_rows(xr, 0) * cw[2:3]
          + _shift_rows(xr, 1) * cw[3:4] + cb_ref[...])
    pre = jnp.dot(xc.astype(BF16), wg_ref[...], preferred_element_type=F32) + bg_ref[...]
    bg = col(1, TM)
    gb = col(4, TM)
    prod = cg * xin
    scw = scw_ref[...]
    yc = bg * (_shift_rows(prod, -1) * scw[0:1] + _shift_rows(prod, 0) * scw[1:2]
               + _shift_rows(prod, 1) * scw[2:3])
    yc_ref[...] = yc.astype(BF16)
    gate_ref[...] = jax.nn.gelu(gb)
    gates = jax.nn.sigmoid(pre)
    for d in range(2):
        r = gates[:, 2 * d * HALF:(2 * d + 1) * HALF]
        ig = gates[:, (2 * d + 1) * HALF:(2 * d + 2) * HALF]
        log_a = (-LRU_C) * r * _softplus(-lam_ref[d:d + 1, :])
        a = jnp.exp(log_a)
        bval = jnp.sqrt(1.0 - a * a) * (ig * xc)
        if d == 0:
            a_scr[...] = a
            b_scr[...] = bval
        else:
            ar_ref[...] = a
            br_ref[...] = bval

    @pl.when(pos == 0)
    def _():
        carry[...] = h0_ref[0:1, :]

    _lru_scan(a_scr, b_scr, hs_scr, carry, False)
    hsf_ref[...] = hs_scr[...]
    fin_ref[...] = jnp.broadcast_to(carry[...], (8, HALF))


def _cd_in(x, mod, g, w_in, sc_w, conv_w, conv_b, w_gates, b_gates, lru_lam, h0t):
    const = lambda i: (0, 0)
    tok = lambda dt: jax.ShapeDtypeStruct((N_TOK, HALF), dt)
    row = pl.BlockSpec((TM, HALF), lambda i: (i, 0))
    return pl.pallas_call(
        _cd_in_kernel,
        grid=(NT,),
        in_specs=_halo_specs(D_MODEL) + [
            pl.BlockSpec((None, 6, D_MODEL), lambda i: (_mod_row(i), 0, 0)),
            pl.BlockSpec((1, D_MODEL), const),
            pl.BlockSpec((D_MODEL, 5 * HALF), const),
            pl.BlockSpec((3, HALF), const),
            pl.BlockSpec((4, HALF), const),
            pl.BlockSpec((1, HALF), const),
            pl.BlockSpec((HALF, 4 * HALF), const),
            pl.BlockSpec((1, 4 * HALF), const),
            pl.BlockSpec((2, HALF), const),
            pl.BlockSpec((None, 2, HALF), lambda i: (_mod_row(i), 0, 0)),
        ],
        out_specs=[row, row, row, row, row, pl.BlockSpec((8, HALF), lambda i: (i, 0))],
        out_shape=[tok(BF16), tok(F32), tok(F32), tok(F32), tok(F32), jax.ShapeDtypeStruct((NT * 8, HALF), F32)],
        scratch_shapes=[
            pltpu.VMEM((ROWS, D_MODEL), BF16),
            pltpu.VMEM((TM, HALF), F32),
            pltpu.VMEM((TM, HALF), F32),
            pltpu.VMEM((TM, HALF), F32),
            pltpu.VMEM((1, HALF), F32),
        ],
        compiler_params=_params(),
        name="cd_in",
    )(x, x, x, mod, g, w_in, sc_w, conv_w, conv_b, w_gates, b_gates, lru_lam, h0t)


def _cd_out_kernel(ar_ref, br_ref, hsf_ref, gate_ref, yc_ref, x_ref, mod_ref, g_ref, wo_ref, h0_ref,
                   o_ref, fin_ref, hs_scr, carry):
    ti = NT - 1 - pl.program_id(0)
    pos, n = _seq_pos(ti)

    @pl.when(pos == n - 1)
    def _():
        carry[...] = h0_ref[1:2, :]

    _lru_scan(ar_ref, br_ref, hs_scr, carry, True)
    fin_ref[...] = jnp.broadcast_to(carry[...], (8, HALF))
    yd = (hsf_ref[...] + hs_scr[...]) * gate_ref[...]
    out = (jnp.dot(yc_ref[...], wo_ref[0:HALF, :], preferred_element_type=F32)
           + jnp.dot(yd.astype(BF16), wo_ref[HALF:2 * HALF, :], preferred_element_type=F32))
    o_ref[...] = _post(x_ref[...], out, g_ref[...], mod_ref[2:3, :])


def _cd_out(a_r, b_r, hs_f, gate, yc, x, mod, g, w_out, h0t):
    const = lambda i: (0, 0)
    rev = lambda i: (NT - 1 - i, 0)
    row = pl.BlockSpec((TM, HALF), rev)
    return pl.pallas_call(
        _cd_out_kernel,
        grid=(NT,),
        in_specs=[
            row, row, row, row, row,
            pl.BlockSpec((TM, D_MODEL), rev),
            pl.BlockSpec((None, 6, D_MODEL), lambda i: (_mod_row(NT - 1 - i), 0, 0)),
            pl.BlockSpec((1, D_MODEL), const),
            pl.BlockSpec((D_MODEL, D_MODEL), const),
            pl.BlockSpec((None, 2, HALF), lambda i: (_mod_row(NT - 1 - i), 0, 0)),
        ],
        out_specs=[
            pl.BlockSpec((TM, D_MODEL), rev),
            pl.BlockSpec((8, HALF), rev),
        ],
        out_shape=[
            jax.ShapeDtypeStruct((N_TOK, D_MODEL), F32),
            jax.ShapeDtypeStruct((NT * 8, HALF), F32),
        ],
        scratch_shapes=[
            pltpu.VMEM((TM, HALF), F32),
            pltpu.VMEM((1, HALF), F32),
        ],
        compiler_params=_params(),
        name="cd_out",
    )(a_r, b_r, hs_f, gate, yc, x, mod, g, w_out, h0t)


def kernel(x_prompt, x_sample, cache_attn_k, cache_attn_v, state_s5, state_rglru, c, c_ctx, w_mod, b_mod, norm_g, w_in_ab, w_out_ab, s5_lam_re, s5_lam_im, s5_log_dt, s5_b, s5_c, s5_d, s5_w_glu, s5_b_glu, da_lam, da_g, w_in_cd, w_out_cd, sc_conv_w, lru_conv_w, lru_conv_b, lru_w_a, lru_b_a, lru_w_x, lru_b_x, lru_lam, ffn_w_up, ffn_conv_w, ffn_conv_b, ffn_w_down):
    assert DEPTH == 2
    xp = x_prompt.reshape(NT_P * TM, D_MODEL)
    xs = x_sample.reshape(NT_S * TM, D_MODEL)
    cv8 = jnp.zeros((8, D_MODEL), F32).at[0].set(c_ctx).at[1:1 + DEC_BATCH].set(c)
    mod = _modulation(cv8, w_mod, b_mod)
    cos_t, sin_t = _rope_tables()
    g = norm_g.reshape(DEPTH, 4, 1, D_MODEL)

    lam_init = 0.8 - 0.6 * math.exp(-0.3 * 0)
    u, qkv, k32, v32 = _ab_in(xp, xs, mod[0], g[0, 0], w_in_ab[0].astype(BF16), cos_t, sin_t)
    m1, m2, m3, coef = _s5_params(s5_lam_re[0], s5_lam_im[0], s5_log_dt[0], s5_b[0], s5_c[0])
    ys5_p, ys5_s, new_s5 = _s5(u, (m1, m2, m3), coef, state_s5[:, 0], s5_d[0])
    yb_p = _attention(qkv, 0, BATCH, 1, SEQ, None, da_lam[0], da_g[0], lam_init)
    ctx = (cache_attn_k[:, 0].reshape(DEC_BATCH, PAST_LEN, HALF).astype(BF16),
           cache_attn_v[:, 0].reshape(DEC_BATCH, PAST_LEN, HALF).astype(BF16))
    yb_s = _attention(qkv, NT_P, DEC_BATCH, TPS, DEC_SEQ, ctx, da_lam[0], da_g[0], lam_init)
    x = _ab_out(ys5_p, ys5_s, yb_p, yb_s, xp, xs, mod[0], g[0, 1], s5_w_glu[0].astype(BF16), s5_b_glu[0].reshape(1, HALF),
                w_out_ab[0].astype(BF16))
    new_k = k32.reshape(BATCH, 1, SEQ, DA_HEADS, 2, DA_DK)
    new_v = v32.reshape(BATCH, 1, SEQ, DA_HEADS, DA_DV)
    w_up = ffn_w_up.astype(BF16)
    w_down = ffn_w_down.astype(BF16)
    x = _ffn(x, mod[0], g[0, 2], g[0, 3], w_up, ffn_conv_w[0], ffn_conv_b[0].reshape(1, 2 * D_FF), w_down, 0, False)

    eye = jnp.eye(LRU_BLOCKS, dtype=F32)
    dense = lambda w: jnp.einsum('kcd,kl->kcld', w, eye).reshape(LRU_WIDTH, LRU_WIDTH)
    w_gates = jnp.concatenate([dense(lru_w_a[0, 0]), dense(lru_w_x[0, 0]),
                               dense(lru_w_a[0, 1]), dense(lru_w_x[0, 1])], axis=1).astype(BF16)
    b_gates = jnp.concatenate([lru_b_a[0, 0], lru_b_x[0, 0], lru_b_a[0, 1], lru_b_x[0, 1]]).reshape(1, 4 * HALF)
    h0t = jnp.zeros((8, 2, HALF), F32).at[1:1 + DEC_BATCH].set(state_rglru[:, 0])
    yc, gate, hs_f, a_r, b_r, fin_f = _cd_in(x, mod[1], g[1, 0], w_in_cd[0].astype(BF16), sc_conv_w[0], lru_conv_w[0],
                                      lru_conv_b[0].reshape(1, HALF), w_gates, b_gates, lru_lam[0], h0t)
    x, fin_r = _cd_out(a_r, b_r, hs_f, gate, yc, x, mod[1], g[1, 1], w_out_cd[0].astype(BF16), h0t)
    tile_row0 = lambda f: f.reshape(NT, 8, HALF)[:NT_P, 0]
    new_lru = jnp.stack([tile_row0(fin_f), tile_row0(fin_r)], axis=1)[:, None]
    yp, ys = _ffn(x, mod[1], g[1, 2], g[1, 3], w_up, ffn_conv_w[1], ffn_conv_b[1].reshape(1, 2 * D_FF), w_down, 1, True)
    return (yp.reshape(BATCH, SEQ, D_MODEL), ys.reshape(DEC_BATCH, DEC_SEQ, D_MODEL),
            new_k, new_v, new_s5[:, None], new_lru)
```

```python
import functools
import math

import jax
import jax.numpy as jnp
from jax import lax
from jax.experimental import pallas as pl
from jax.experimental.pallas import tpu as pltpu

D_MODEL = 1024
BATCH = 32
SEQ = 256
DEPTH = 2
DEC_BATCH = 4
DEC_SEQ = 2048
PAST_LEN = 512
GRID_W = 64
HALF = D_MODEL // 2
S5_P = 16
S5_G = HALF // S5_P
S5_N = 64
DA_DK = 64
DA_DV = 2 * DA_DK
DA_HEADS = HALF // DA_DV
ROPE_THETA = 10000.0
ROPE_F = DA_DK // 4
LRU_WIDTH = HALF
LRU_BLOCKS = 8
LRU_BS = LRU_WIDTH // LRU_BLOCKS
LRU_C = 8.0
D_FF = 2816
EPS = 1e-6

F32 = jnp.float32
BF16 = jnp.bfloat16
HIGHEST = lax.Precision.HIGHEST

TM = 256
NT_P = BATCH * SEQ // TM
TPS = DEC_SEQ // TM
NT_S = DEC_BATCH * TPS
NT = NT_P + NT_S
N_TOK = NT * TM
HALO = 8
ROWS = TM + 2 * HALO
S5_T = 8
S5_GB = 128 // S5_P
S5_NB = S5_G // S5_GB
S5_K = S5_T * 128
S5_ST = 4 * S5_N
S5_GS = 4
FF_CHUNK = 256
FF_NCH = D_FF // FF_CHUNK
Q_SCALE = math.log2(math.e) / math.sqrt(DA_DK)
GELU_K = math.sqrt(2.0 / math.pi)
GELU_C = 0.044715
VMEM_LIMIT = 56 * 1024 * 1024


def _mod_row(i):
    return jnp.where(i < NT_P, 0, 1 + (i - NT_P) // TPS)


def _seq_pos(i):
    return jnp.where(i < NT_P, 0, (i - NT_P) % TPS), jnp.where(i < NT_P, 1, TPS)


def _split_specs(width):
    return [
        pl.BlockSpec((TM, width), lambda i: (jnp.minimum(i, NT_P - 1), 0)),
        pl.BlockSpec((TM, width), lambda i: (jnp.maximum(i - NT_P, 0), 0)),
    ]


def _pick(p_ref, s_ref):
    return jnp.where(pl.program_id(0) < NT_P, p_ref[...], s_ref[...])


def _params(sem=("arbitrary",)):
    return pltpu.CompilerParams(dimension_semantics=sem, vmem_limit_bytes=VMEM_LIMIT)


def _rms(x, g):
    ms = jnp.mean(x * x, axis=-1, keepdims=True)
    return x * lax.rsqrt(ms + EPS) * g


def _pre(x, g, shift, scale):
    return _rms(x, g) * (1.0 + scale) + shift


def _mod_kernel(cv_ref, w_ref, b_ref, o_ref):
    cv = cv_ref[...]
    s = cv * jax.nn.sigmoid(cv)
    o_ref[...] = jnp.dot(s, w_ref[...], precision=HIGHEST, preferred_element_type=F32) + b_ref[...]


def _modulation(cv8, w_mod, b_mod):
    nb = 1536
    out = pl.pallas_call(
        _mod_kernel,
        grid=(DEPTH, 6 * D_MODEL // nb),
        in_specs=[
            pl.BlockSpec((8, D_MODEL), lambda l, j: (0, 0)),
            pl.BlockSpec((None, D_MODEL, nb), lambda l, j: (l, 0, j)),
            pl.BlockSpec((None, 1, nb), lambda l, j: (l, 0, j)),
        ],
        out_specs=pl.BlockSpec((None, 8, nb), lambda l, j: (l, 0, j)),
        out_shape=jax.ShapeDtypeStruct((DEPTH, 8, 6 * D_MODEL), F32),
        compiler_params=_params(("arbitrary", "arbitrary")),
        name="modulation",
    )(cv8, w_mod, b_mod.reshape(DEPTH, 1, 6 * D_MODEL))
    return out.reshape(DEPTH, 8, 6, D_MODEL)


def _rope_partner(x):
    lane = lax.broadcasted_iota(jnp.int32, (1, HALF), 1)
    first = (lane % (2 * ROPE_F)) < ROPE_F
    return jnp.where(first, pltpu.roll(x, HALF - ROPE_F, axis=1), pltpu.roll(x, ROPE_F, axis=1))


def _ab_in_kernel(xp_ref, xs_ref, mod_ref, g_ref, w_ref, cos_ref, sin_ref, u_ref, qkv_ref, k_ref, v_ref):
    i = pl.program_id(0)
    h = _pre(_pick(xp_ref, xs_ref), g_ref[...], mod_ref[0:1, :], mod_ref[1:2, :])
    proj = jnp.dot(h.astype(BF16), w_ref[...], preferred_element_type=F32)
    u_ref[...] = proj[:, 0:HALF]
    q = proj[:, HALF:2 * HALF]
    k = proj[:, 2 * HALF:3 * HALF]
    v = proj[:, 3 * HALF:4 * HALF]
    cos = jnp.concatenate([cos_ref[...]] * DA_HEADS, axis=1)
    sin = jnp.concatenate([sin_ref[...]] * DA_HEADS, axis=1)
    qr = q * cos + _rope_partner(q) * sin
    kr = k * cos + _rope_partner(k) * sin
    qkv_ref[:, 0:HALF] = (qr * Q_SCALE).astype(BF16)
    qkv_ref[:, HALF:2 * HALF] = kr.astype(BF16)
    qkv_ref[:, 2 * HALF:3 * HALF] = v.astype(BF16)

    @pl.when(i < NT_P)
    def _():
        k_ref[...] = k.T
        v_ref[...] = v.reshape(TM, DA_HEADS, DA_DV)


def _ab_in(xp, xs, mod, g, w, cos_t, sin_t):
    def rope_tile(i):
        return jnp.where(i < NT_P, 0, 1 + (i - NT_P) % TPS)

    return pl.pallas_call(
        _ab_in_kernel,
        grid=(NT,),
        in_specs=_split_specs(D_MODEL) + [
            pl.BlockSpec((None, 6, D_MODEL), lambda i: (_mod_row(i), 0, 0)),
            pl.BlockSpec((1, D_MODEL), lambda i: (0, 0)),
            pl.BlockSpec((D_MODEL, 4 * HALF), lambda i: (0, 0)),
            pl.BlockSpec((TM, DA_DV), lambda i: (rope_tile(i), 0)),
            pl.BlockSpec((TM, DA_DV), lambda i: (rope_tile(i), 0)),
        ],
        out_specs=[
            pl.BlockSpec((TM, HALF), lambda i: (i, 0)),
            pl.BlockSpec((TM, 3 * HALF), lambda i: (i, 0)),
            pl.BlockSpec((None, HALF, TM), lambda i: (jnp.minimum(i, NT_P - 1), 0, 0)),
            pl.BlockSpec((None, None, TM, DA_HEADS, DA_DV), lambda i: (jnp.minimum(i, NT_P - 1), 0, 0, 0, 0)),
        ],
        out_shape=[
            jax.ShapeDtypeStruct((N_TOK, HALF), F32),
            jax.ShapeDtypeStruct((N_TOK, 3 * HALF), BF16),
            jax.ShapeDtypeStruct((NT_P, HALF, TM), F32),
            jax.ShapeDtypeStruct((NT_P, 1, TM, DA_HEADS, DA_DV), F32),
        ],
        compiler_params=_params(),
        name="ab_in",
    )(xp, xs, mod, g, w, cos_t, sin_t)


def _rope_tables():
    pos = jnp.arange(DEC_SEQ)
    t_row = (pos // GRID_W).astype(F32)
    t_col = (pos % GRID_W).astype(F32)
    inv = ROPE_THETA ** (-jnp.arange(ROPE_F, dtype=F32) / ROPE_F)
    ang = jnp.stack([t_row[:, None] * inv, t_col[:, None] * inv], axis=1)
    cos, sin = jnp.cos(ang), jnp.sin(ang)
    cos64 = jnp.stack([cos, cos], axis=2).reshape(DEC_SEQ, DA_DK)
    sin64 = jnp.stack([-sin, sin], axis=2).reshape(DEC_SEQ, DA_DK)
    cos_t = jnp.concatenate([jnp.ones((TM, DA_DV), F32), jnp.tile(cos64, (1, 2))], axis=0)
    sin_t = jnp.concatenate([jnp.zeros((TM, DA_DV), F32), jnp.tile(sin64, (1, 2))], axis=0)
    return cos_t, sin_t


def _s5_param_kernel(lre_ref, lim_ref, ldt_ref, bt_ref, c_ref, m1_ref, m2_ref, m3_ref, at_ref):
    nt = (((1,), (1,)), ((), ()))
    e_tab, f_tab, k_tab = [], [], []
    r_i = lax.broadcasted_iota(jnp.int32, (128, 128), 0) // S5_P
    c_i = lax.broadcasted_iota(jnp.int32, (128, 128), 1) // S5_P
    same_group = r_i == c_i
    for d in range(2):
        lre = lre_ref[d]
        lim = lim_ref[d]
        dt = jnp.exp(ldt_ref[d])
        mag = jnp.exp(lre * dt)
        ang = lim * dt
        ar = mag * jnp.cos(ang)
        ai = mag * jnp.sin(ang)
        den = lre * lre + lim * lim
        nr = ar - 1.0
        fr = (nr * lre + ai * lim) / den
        fi = (ai * lre - nr * lim) / den
        btr = bt_ref[d, 0]
        bti = bt_ref[d, 1]
        bfr = fr * btr - fi * bti
        bfi = fr * bti + fi * btr
        cr = c_ref[d, 0]
        ci = c_ref[d, 1]
        cst = jnp.concatenate([cr, -ci], axis=1)
        pr = jnp.ones_like(ar)
        pi = jnp.zeros_like(ar)
        e_d, f_d, k_d = [], [], []
        for tau in range(S5_T + 1):
            f_d.append((cr * pr - ci * pi, cr * pi + ci * pr))
            if tau < S5_T:
                er, ei = pr * bfr - pi * bfi, pr * bfi + pi * bfr
                e_d.append((er, ei))
                full = lax.dot_general(jnp.concatenate([er, ei], axis=1), cst, nt,
                                       precision=HIGHEST, preferred_element_type=F32)
                k_d.append(jnp.where(same_group, full, 0.0))
            else:
                at_ref[d, 0] = pr
                at_ref[d, 1] = pi
            pr, pi = pr * ar - pi * ai, pr * ai + pi * ar
        e_tab.append(e_d)
        f_tab.append(f_d)
        k_tab.append(k_d)

    for s in range(S5_T):
        for t in range(S5_T):
            blk = k_tab[0][t - s] if t >= s else k_tab[1][s - t]
            if t == s:
                blk = blk + k_tab[1][0]
            m1_ref[s * 128:(s + 1) * 128, t * 128:(t + 1) * 128] = blk.astype(BF16)
    row_group = lax.broadcasted_iota(jnp.int32, (128, S5_ST), 0) // S5_P
    for s in range(S5_T):
        ef, er_ = e_tab[0][S5_T - 1 - s], e_tab[1][s]
        v = jnp.concatenate([ef[0], er_[0], ef[1], er_[1]], axis=1)
        ff, fr_ = f_tab[0][s + 1], f_tab[1][S5_T - s]
        w = jnp.concatenate([ff[0], fr_[0], -ff[1], -fr_[1]], axis=1)
        for gi in range(S5_GB):
            sel = row_group == gi
            m2_ref[s * 128:(s + 1) * 128, gi * S5_ST:(gi + 1) * S5_ST] = jnp.where(sel, v, 0.0).astype(BF16)
            m3_ref[s * 128:(s + 1) * 128, gi * S5_ST:(gi + 1) * S5_ST] = jnp.where(sel, w, 0.0).astype(BF16)


def _s5_params(lam_re, lam_im, log_dt, b_w, c_w):
    rows = S5_G * S5_P
    per_row = lambda a: jnp.repeat(a, S5_P, axis=1)
    lre = per_row(lam_re)
    lim = per_row(lam_im)
    ldt = per_row(jnp.broadcast_to(log_dt[:, :, None], (2, S5_G, S5_N)))
    bt = b_w.transpose(0, 1, 2, 4, 3).reshape(2, 2, rows, S5_N)
    cc = c_w.reshape(2, 2, rows, S5_N)
    vec = pl.BlockSpec((2, 128, S5_N), lambda j: (0, j, 0))
    mat = pl.BlockSpec((2, 2, 128, S5_N), lambda j: (0, 0, j, 0))
    wide = lambda n: pl.BlockSpec((None, S5_K, n), lambda j: (j, 0, 0))
    m1, m2, m3, at = pl.pallas_call(
        _s5_param_kernel,
        grid=(S5_NB,),
        in_specs=[vec, vec, vec, mat, mat],
        out_specs=[wide(S5_K), wide(S5_GB * S5_ST), wide(S5_GB * S5_ST),
                   pl.BlockSpec((None, 2, 2, 128, S5_N), lambda j: (j, 0, 0, 0, 0))],
        out_shape=[
            jax.ShapeDtypeStruct((S5_NB, S5_K, S5_K), BF16),
            jax.ShapeDtypeStruct((S5_NB, S5_K, S5_GB * S5_ST), BF16),
            jax.ShapeDtypeStruct((S5_NB, S5_K, S5_GB * S5_ST), BF16),
            jax.ShapeDtypeStruct((S5_NB, 2, 2, 128, S5_N), F32),
        ],
        compiler_params=_params(),
        name="s5_params",
    )(lre, lim, ldt, bt, cc)
    coef = at[:, :, :, ::S5_P, :].transpose(0, 3, 2, 1, 4).reshape(S5_NB, S5_GB, 2, 2 * S5_N)
    return m1, m2, m3, coef


def _s5_kernel(u_ref, m1_ref, m2_ref, m3_ref, coef_ref, d_ref, *rest, stride, nc, two_pass):
    if two_pass:
        h0_ref, y_ref, lhs, yacc, s_scr, hp_scr = rest
    else:
        y_ref, fin_ref, lhs, yacc, s_scr, hp_scr = rest
    ln = 2 * S5_N
    nt = (((1,), (1,)), ((), ()))

    def seq_rows(k, c):
        return pl.ds(pl.multiple_of(k * stride + c * S5_T, S5_T), S5_T)

    def gather(c2, carry):
        halves = []
        for c in (2 * c2, 2 * c2 + 1):
            tiles = [u_ref[seq_rows(k, c), :] for k in range(8)]
            halves.append(jnp.stack(tiles, axis=0).reshape(8, S5_K))
        lhs[pl.ds(pl.multiple_of(c2 * 16, 16), 16), :] = jnp.concatenate(halves, axis=0).astype(BF16)
        return carry

    lax.fori_loop(0, nc // 2, gather, 0)
    lb = lhs[...]
    yacc[...] = jnp.dot(lb, m1_ref[...], preferred_element_type=F32)

    lane = lax.broadcasted_iota(jnp.int32, (8, ln), 1)
    row = lax.broadcasted_iota(jnp.int32, (8, ln), 0)
    fwd = lane < S5_N
    odd = (row % 2) == 1
    wide = S5_GS * S5_ST
    for gq in range(S5_GB // S5_GS):
        groups = range(gq * S5_GS, (gq + 1) * S5_GS)
        cols = slice(gq * wide, (gq + 1) * wide)
        s_scr[...] = jnp.dot(lb, m2_ref[:, cols], preferred_element_type=F32)
        are = [jnp.broadcast_to(coef_ref[g, 0:1, :], (8, ln)) for g in groups]
        aim = [jnp.broadcast_to(coef_ref[g, 1:2, :], (8, ln)) for g in groups]

        def scan(init, store):
            def body(c, carry):
                kf = pl.multiple_of(c * 8, 8)
                kr = pl.multiple_of((nc - 1 - c) * 8, 8)
                out = []
                for j in range(S5_GS):
                    p, q = carry[2 * j], carry[2 * j + 1]
                    o = j * S5_ST
                    if store:
                        hp_scr[pl.ds(kf, 8), o:o + S5_N] = p[:, 0:S5_N]
                        hp_scr[pl.ds(kr, 8), o + S5_N:o + ln] = p[:, S5_N:ln]
                        hp_scr[pl.ds(kf, 8), o + ln:o + ln + S5_N] = q[:, 0:S5_N]
                        hp_scr[pl.ds(kr, 8), o + ln + S5_N:o + 2 * ln] = q[:, S5_N:ln]
                    s_re = jnp.where(fwd, s_scr[pl.ds(kf, 8), o:o + ln], s_scr[pl.ds(kr, 8), o:o + ln])
                    s_im = jnp.where(fwd, s_scr[pl.ds(kf, 8), o + ln:o + 2 * ln],
                                     s_scr[pl.ds(kr, 8), o + ln:o + 2 * ln])
                    out += [are[j] * p - aim[j] * q + s_re, are[j] * q + aim[j] * p + s_im]
                return tuple(out)
            return lax.fori_loop(0, nc, body, tuple(init))

        if two_pass:
            h0 = [h0_ref[g, r] for g in groups for r in range(2)]
            fin = scan(h0, False)
            seed = [h + jnp.where(fwd & odd, pltpu.roll(f, 1, axis=0), 0.0)
                    + jnp.where(jnp.logical_not(fwd | odd), pltpu.roll(f, 7, axis=0), 0.0)
                    for h, f in zip(h0, fin)]
            scan(seed, True)
        else:
            fin = scan([jnp.zeros((8, ln), F32)] * (2 * S5_GS), True)
            for j, g in enumerate(groups):
                fin_ref[g, 0] = fin[2 * j]
                fin_ref[g, 1] = fin[2 * j + 1]
        yacc[...] += lax.dot_general(hp_scr[...].astype(BF16), m3_ref[:, cols], nt, preferred_element_type=F32)

    dvec = d_ref[...]

    def scatter(c, carry):
        blk = yacc[pl.ds(pl.multiple_of(c * 8, 8), 8), :].reshape(8, S5_T, 128)
        for k in range(8):
            rows = seq_rows(k, c)
            y_ref[rows, :] = blk[k] + u_ref[rows, :] * dvec
        return carry

    lax.fori_loop(0, nc, scatter, 0)


def _s5_call(u, row_block, mats, coef, dvec, h0, stride, nc, parts):
    m1, m2, m3 = mats
    rows = 8 * stride
    two_pass = h0 is not None
    single = dict(pipeline_mode=pl.Buffered(1)) if parts == 1 else {}
    mat = lambda n: pl.BlockSpec((None, S5_K, n), lambda j, i: (j, 0, 0), **single)
    tok = pl.BlockSpec((rows, 128), lambda j, i: (row_block * parts + i, j))
    in_specs = [
        tok, mat(S5_K), mat(S5_GB * S5_ST), mat(S5_GB * S5_ST),
        pl.BlockSpec((None, S5_GB, 2, 2 * S5_N), lambda j, i: (j, 0, 0, 0)),
        pl.BlockSpec((None, 1, 128), lambda j, i: (j, 0, 0)),
    ]
    args = [u, m1, m2, m3, coef, dvec]
    out_specs = [pl.BlockSpec((rows, 128), lambda j, i: (i, j))]
    out_shape = [jax.ShapeDtypeStruct((parts * rows, HALF), F32)]
    if two_pass:
        in_specs.append(pl.BlockSpec((None, S5_GB, 2, 8, 2 * S5_N), lambda j, i: (j, 0, 0, 0, 0)))
        args.append(h0)
    else:
        out_specs.append(pl.BlockSpec((None, S5_GB, 2, 8, 2 * S5_N), lambda j, i: (j, 0, 0, i, 0)))
        out_shape.append(jax.ShapeDtypeStruct((S5_NB, S5_GB, 2, parts * 8, 2 * S5_N), F32))
    return pl.pallas_call(
        functools.partial(_s5_kernel, stride=stride, nc=nc, two_pass=two_pass),
        grid=(S5_NB, parts),
        in_specs=in_specs,
        out_specs=out_specs,
        out_shape=out_shape,
        scratch_shapes=[
            pltpu.VMEM((nc * 8, S5_K), BF16),
            pltpu.VMEM((nc * 8, S5_K), F32),
            pltpu.VMEM((nc * 8, S5_GS * S5_ST), F32),
            pltpu.VMEM((nc * 8, S5_GS * S5_ST), F32),
        ],
        compiler_params=_params(("arbitrary", "arbitrary")),
        name="s5_%d" % stride,
    )(*args)


def _s5(u, mats, coef, h0, s5_d):
    dvec = s5_d.reshape(S5_NB, 1, 128)
    yp, fin = _s5_call(u, 0, mats, coef, dvec, None, SEQ, SEQ // S5_T, BATCH // 8)
    hh = h0.transpose(3, 2, 0, 1, 4)
    z = jnp.zeros_like(hh[:, :, :, 0])
    first = jnp.concatenate([hh[:, :, :, 0], z], axis=-1)
    second = jnp.concatenate([z, hh[:, :, :, 1]], axis=-1)
    h0g = jnp.stack([first, second], axis=3).reshape(S5_NB, S5_GB, 2, 2 * DEC_BATCH, 2 * S5_N)
    half = DEC_SEQ // 2
    (ys,) = _s5_call(u, 1, mats, coef, dvec, h0g, half, half // S5_T, 1)
    new_s5 = fin.reshape(S5_G, 2, BATCH, 2, S5_N).transpose(2, 3, 1, 0, 4)
    return yp, ys, new_s5


def _attn_kernel(*refs, lam_init, has_ctx, lk):
    if has_ctx:
        q_ref, k_ref, v_ref, kc_ref, vc_ref, dl_ref, g_ref, o_ref, s_a, s_b, vx, vcx = refs
    else:
        q_ref, k_ref, v_ref, dl_ref, g_ref, o_ref, s_a, s_b, vx = refs
    lt = lk + (PAST_LEN if has_ctx else 0)
    dl = dl_ref[...]
    lam = (jnp.exp(jnp.sum(dl[0:1] * dl[1:2], keepdims=True))
           - jnp.exp(jnp.sum(dl[2:3] * dl[3:4], keepdims=True)) + lam_init)
    lane = lax.broadcasted_iota(jnp.int32, (1, DA_DV), 1)
    first = lane < DA_DK
    g = g_ref[...]
    nt = (((1,), (1,)), ((), ()))

    @pl.when(pl.program_id(1) == 0)
    def _():
        onehot = (lane == 0).astype(BF16)
        for h in range(DA_HEADS):
            cols = slice(h * DA_DV, (h + 1) * DA_DV)
            vx[:, 2 * h * DA_DV:(2 * h + 1) * DA_DV] = v_ref[:, cols]
            vx[:, (2 * h + 1) * DA_DV:(2 * h + 2) * DA_DV] = jnp.broadcast_to(onehot, (lk, DA_DV))
            if has_ctx:
                vcx[:, 2 * h * DA_DV:(2 * h + 1) * DA_DV] = vc_ref[:, cols]
                vcx[:, (2 * h + 1) * DA_DV:(2 * h + 2) * DA_DV] = jnp.broadcast_to(onehot, (PAST_LEN, DA_DV))

    def scores(h, dst):
        cols = slice(h * DA_DV, (h + 1) * DA_DV)
        qh = q_ref[:, cols]
        zero = jnp.zeros_like(qh)
        for m in range(2):
            qm = jnp.where(first if m == 0 else jnp.logical_not(first), qh, zero)
            dst[m, :, 0:lk] = lax.dot_general(qm, k_ref[:, cols], nt, preferred_element_type=F32)
            if has_ctx:
                dst[m, :, lk:lt] = lax.dot_general(qm, kc_ref[:, cols], nt, preferred_element_type=F32)

    def head(h, src):
        xcols = slice(2 * h * DA_DV, (2 * h + 2) * DA_DV)
        outs = []
        for m in range(2):
            s = src[m]
            e = jnp.exp2(s - jnp.max(s, axis=-1, keepdims=True)).astype(BF16)
            ox = jnp.dot(e[:, 0:lk], vx[:, xcols], preferred_element_type=F32)
            if has_ctx:
                ox = ox + jnp.dot(e[:, lk:lt], vcx[:, xcols], preferred_element_type=F32)
            outs.append(ox[:, 0:DA_DV] * (1.0 / ox[:, DA_DV:DA_DV + 1]))
        o = outs[0] - lam * outs[1]
        o_ref[:, h * DA_DV:(h + 1) * DA_DV] = (_rms(o, g) * (1.0 - lam_init)).astype(BF16)

    scores(0, s_a)
    for h in range(DA_HEADS):
        cur, nxt = (s_a, s_b) if h % 2 == 0 else (s_b, s_a)
        if h + 1 < DA_HEADS:
            scores(h + 1, nxt)
        head(h, cur)


def _attention(qkv, q_tile0, nb, nq, lk, ctx, da_lam, da_g, lam_init):
    kb0 = q_tile0 * TM // lk
    in_specs = [
        pl.BlockSpec((TM, HALF), lambda b, j: (q_tile0 + b * nq + j, 0)),
        pl.BlockSpec((lk, HALF), lambda b, j: (kb0 + b, 1)),
        pl.BlockSpec((lk, HALF), lambda b, j: (kb0 + b, 2)),
    ]
    args = [qkv, qkv, qkv]
    if ctx is not None:
        in_specs += [pl.BlockSpec((None, PAST_LEN, HALF), lambda b, j: (b, 0, 0))] * 2
        args += list(ctx)
    in_specs += [
        pl.BlockSpec((4, DA_DK), lambda b, j: (0, 0)),
        pl.BlockSpec((1, DA_DV), lambda b, j: (0, 0)),
    ]
    lt = lk + (PAST_LEN if ctx is not None else 0)
    scratch = [pltpu.VMEM((2, TM, lt), F32), pltpu.VMEM((2, TM, lt), F32), pltpu.VMEM((lk, 2 * HALF), BF16)]
    if ctx is not None:
        scratch.append(pltpu.VMEM((PAST_LEN, 2 * HALF), BF16))
    return pl.pallas_call(
        functools.partial(_attn_kernel, lam_init=lam_init, has_ctx=ctx is not None, lk=lk),
        grid=(nb, nq),
        in_specs=in_specs,
        out_specs=pl.BlockSpec((TM, HALF), lambda b, j: (b * nq + j, 0)),
        out_shape=jax.ShapeDtypeStruct((nb * nq * TM, HALF), BF16),
        scratch_shapes=scratch,
        compiler_params=_params(("arbitrary", "arbitrary")),
        name="diff_attn_%d" % lk,
    )(*args, da_lam, da_g.reshape(1, DA_DV))


def _post(x, y, g, gate):
    return x + gate * _rms(y, g)


def _ab_out_kernel(ysp_ref, yss_ref, ybp_ref, ybs_ref, xp_ref, xs_ref, mod_ref, g_ref, wg_ref, bg_ref, wo_ref,
                   o_ref):
    ys = jax.nn.gelu(_pick(ysp_ref, yss_ref))
    glu = jnp.dot(ys.astype(BF16), wg_ref[...], preferred_element_type=F32) + bg_ref[...]
    ya = ys * jax.nn.sigmoid(glu)
    out = (jnp.dot(ya.astype(BF16), wo_ref[0:HALF, :], preferred_element_type=F32)
           + jnp.dot(_pick(ybp_ref, ybs_ref), wo_ref[HALF:2 * HALF, :], preferred_element_type=F32))
    o_ref[...] = _post(_pick(xp_ref, xs_ref), out, g_ref[...], mod_ref[2:3, :])


def _ab_out(ys5_p, ys5_s, yb_p, yb_s, xp, xs, mod, g, w_glu, b_glu, w_out):
    return pl.pallas_call(
        _ab_out_kernel,
        grid=(NT,),
        in_specs=_split_specs(HALF) + _split_specs(HALF) + _split_specs(D_MODEL) + [
            pl.BlockSpec((None, 6, D_MODEL), lambda i: (_mod_row(i), 0, 0)),
            pl.BlockSpec((1, D_MODEL), lambda i: (0, 0)),
            pl.BlockSpec((HALF, HALF), lambda i: (0, 0)),
            pl.BlockSpec((1, HALF), lambda i: (0, 0)),
            pl.BlockSpec((D_MODEL, D_MODEL), lambda i: (0, 0)),
        ],
        out_specs=pl.BlockSpec((TM, D_MODEL), lambda i: (i, 0)),
        out_shape=jax.ShapeDtypeStruct((N_TOK, D_MODEL), F32),
        compiler_params=_params(),
        name="ab_out",
    )(ys5_p, ys5_s, yb_p, yb_s, xp, xs, mod, g, w_glu, b_glu, w_out)


def _halo_specs(width):
    blocks = TM // HALO
    last = N_TOK // HALO - 1
    return [
        pl.BlockSpec((TM, width), lambda i: (i, 0)),
        pl.BlockSpec((HALO, width), lambda i: (jnp.maximum(i * blocks - 1, 0), 0)),
        pl.BlockSpec((HALO, width), lambda i: (jnp.minimum((i + 1) * blocks, last), 0)),
    ]


def _fill_hbuf(hbuf, x_ref, xp_ref, xn_ref, g, shift, scale, i):
    pos, n = _seq_pos(i)
    hp = jnp.where(pos > 0, _pre(xp_ref[...], g, shift, scale), 0.0)
    hn = jnp.where(pos < n - 1, _pre(xn_ref[...], g, shift, scale), 0.0)
    hbuf[0:TM, :] = _pre(x_ref[...], g, shift, scale).astype(BF16)
    hbuf[TM:ROWS, :] = jnp.concatenate([hn, hp], axis=0).astype(BF16)


def _shift_rows(x, s):
    if s == 0:
        return x[0:TM]
    return pltpu.roll(x, (-s) % ROWS, axis=0)[0:TM]


def _ffn_kernel(x_ref, xp_ref, xn_ref, mod_ref, g2_ref, g3_ref, wg_ref, wv_ref, wd_ref, cw_ref, cb_ref, *rest, split):
    hbuf, acc, u_a, u_b, wu_s, wd_s = rest[-6:]
    step = pl.program_id(0)

    @pl.when(step < FF_NCH)
    def _():
        wu_s[step] = wg_ref[...].astype(BF16)
        wu_s[FF_NCH + step] = wv_ref[...].astype(BF16)
        wd_s[step] = wd_ref[...].astype(BF16)

    @pl.when(step >= FF_NCH)
    def _():
        i = step - FF_NCH
        _fill_hbuf(hbuf, x_ref, xp_ref, xn_ref, g2_ref[...], mod_ref[3:4, :], mod_ref[4:5, :], i)

        def up(j, dst):
            for half in range(2):
                dst[half] = jnp.dot(hbuf[...], wu_s[half * FF_NCH + j], preferred_element_type=F32)

        def activation(j, src):
            parts = []
            for half in range(2):
                c0 = half * D_FF + j * FF_CHUNK
                sc = 1.0 if half == 0 else 0.5
                cw = cw_ref[:, c0:c0 + FF_CHUNK] * sc
                u = src[half]
                parts.append(_shift_rows(u, -1) * cw[0:1] + _shift_rows(u, 0) * cw[1:2]
                             + _shift_rows(u, 1) * cw[2:3] + cb_ref[:, c0:c0 + FF_CHUNK] * sc)
            gt = parts[0]
            z = gt * (gt * gt * (GELU_C * GELU_K) + GELU_K)
            return ((gt + gt * jnp.tanh(z)) * parts[1]).astype(BF16)

        up(0, u_a)
        for j in range(FF_NCH):
            cur, nxt = (u_a, u_b) if j % 2 == 0 else (u_b, u_a)
            if j + 1 < FF_NCH:
                up(j + 1, nxt)
            contrib = jnp.dot(activation(j, cur), wd_s[j], preferred_element_type=F32)
            if j == 0:
                acc[...] = contrib
            else:
                acc[...] += contrib
        res = _post(x_ref[...], acc[...], g3_ref[...], mod_ref[5:6, :])
        if split:
            @pl.when(i < NT_P)
            def _():
                rest[0][...] = res

            @pl.when(i >= NT_P)
            def _():
                rest[1][...] = res
        else:
            rest[0][...] = res


def _ffn(x, mod, g2, g3, w_up, cw, cb, w_down, layer, split):
    const = lambda s: (0, 0)
    tile = lambda s: jnp.maximum(s - FF_NCH, 0)
    chunk = lambda s: jnp.minimum(s, FF_NCH - 1)
    shifted = lambda spec: pl.BlockSpec(spec.block_shape, lambda s, f=spec.index_map: f(tile(s)))
    if split:
        out_specs = [shifted(sp) for sp in _split_specs(D_MODEL)]
        out_shape = [jax.ShapeDtypeStruct((NT_P * TM, D_MODEL), F32), jax.ShapeDtypeStruct((NT_S * TM, D_MODEL), F32)]
    else:
        out_specs = pl.BlockSpec((TM, D_MODEL), lambda s: (tile(s), 0))
        out_shape = jax.ShapeDtypeStruct((N_TOK, D_MODEL), F32)
    return pl.pallas_call(
        functools.partial(_ffn_kernel, split=split),
        grid=(FF_NCH + NT,),
        in_specs=[shifted(sp) for sp in _halo_specs(D_MODEL)] + [
            pl.BlockSpec((None, 6, D_MODEL), lambda s: (_mod_row(tile(s)), 0, 0)),
            pl.BlockSpec((1, D_MODEL), const),
            pl.BlockSpec((1, D_MODEL), const),
            pl.BlockSpec((None, D_MODEL, FF_CHUNK), lambda s: (layer, 0, chunk(s))),
            pl.BlockSpec((None, D_MODEL, FF_CHUNK), lambda s: (layer, 0, FF_NCH + chunk(s))),
            pl.BlockSpec((None, FF_CHUNK, D_MODEL), lambda s: (layer, chunk(s), 0)),
            pl.BlockSpec((3, 2 * D_FF), const),
            pl.BlockSpec((1, 2 * D_FF), const),
        ],
        out_specs=out_specs,
        out_shape=out_shape,
        scratch_shapes=[
            pltpu.VMEM((ROWS, D_MODEL), BF16),
            pltpu.VMEM((TM, D_MODEL), F32),
            pltpu.VMEM((2, ROWS, FF_CHUNK), F32),
            pltpu.VMEM((2, ROWS, FF_CHUNK), F32),
            pltpu.VMEM((2 * FF_NCH, D_MODEL, FF_CHUNK), BF16),
            pltpu.VMEM((FF_NCH, FF_CHUNK, D_MODEL), BF16),
        ],
        compiler_params=_params(),
        name="conv_ffn",
    )(x, x, x, mod, g2, g3, w_up, w_up, w_down, cw, cb)


def _softplus(z):
    return jnp.maximum(z, 0.0) + jnp.log(1.0 + jnp.exp(-jnp.abs(z)))


def _lru_scan(a_scr, b_scr, hs_scr, carry, reverse):
    nblk = TM // 8

    def body(blk, h):
        blk = (nblk - 1 - blk) if reverse else blk
        base = pl.multiple_of(blk * 8, 8)
        for j in (range(7, -1, -1) if reverse else range(8)):
            r = pl.ds(base + j, 1)
            h = a_scr[r, :] * h + b_scr[r, :]
            hs_scr[r, :] = h
        return h

    carry[...] = lax.fori_loop(0, nblk, body, carry[...])


def _cd_in_kernel(x_ref, xp_ref, xn_ref, mod_ref, g_ref, w_ref, scw_ref, cw_ref, cb_ref, wg_ref, bg_ref,
                  lam_ref, h0_ref, yc_ref, gate_ref, hsf_ref, ar_ref, br_ref, fin_ref,
                  hbuf, a_scr, b_scr, hs_scr, carry):
    i = pl.program_id(0)
    pos, _ = _seq_pos(i)
    _fill_hbuf(hbuf, x_ref, xp_ref, xn_ref, g_ref[...], mod_ref[0:1, :], mod_ref[1:2, :], i)

    def col(k, rows):
        return jnp.dot(hbuf[0:rows, :], w_ref[:, k * HALF:(k + 1) * HALF], preferred_element_type=F32)

    xr = col(3, ROWS)
    xin = col(0, ROWS)
    cg = col(2, ROWS)
    cw = cw_ref[...]
    xc = (_shift_rows(xr, -2) * cw[0:1] + _shift_rows(xr, -1) * cw[1:2] + _shift_rows(xr, 0) * cw[2:3]
          + _shift_rows(xr, 1) * cw[3:4] + cb_ref[...])
    pre = jnp.dot(xc.astype(BF16), wg_ref[...], preferred_element_type=F32) + bg_ref[...]
    bg = col(1, TM)
    gb = col(4, TM)
    prod = cg * xin
    scw = scw_ref[...]
    yc = bg * (_shift_rows(prod, -1) * scw[0:1] + _shift_rows(prod, 0) * scw[1:2]
               + _shift_rows(prod, 1) * scw[2:3])
    yc_ref[...] = yc.astype(BF16)
    gate_ref[...] = jax.nn.gelu(gb)
    gates = jax.nn.sigmoid(pre)
    for d in range(2):
        r = gates[:, 2 * d * HALF:(2 * d + 1) * HALF]
        ig = gates[:, (2 * d + 1) * HALF:(2 * d + 2) * HALF]
        log_a = (-LRU_C) * r * _softplus(-lam_ref[d:d + 1, :])
        a = jnp.exp(log_a)
        bval = jnp.sqrt(1.0 - a * a) * (ig * xc)
        if d == 0:
            a_scr[...] = a
            b_scr[...] = bval
        else:
            ar_ref[...] = a
            br_ref[...] = bval

    @pl.when(pos == 0)
    def _():
        carry[...] = h0_ref[0:1, :]

    _lru_scan(a_scr, b_scr, hs_scr, carry, False)
    hsf_ref[...] = hs_scr[...]
    fin_ref[...] = jnp.broadcast_to(carry[...], (8, HALF))


def _cd_in(x, mod, g, w_in, sc_w, conv_w, conv_b, w_gates, b_gates, lru_lam, h0t):
    const = lambda i: (0, 0)
    tok = lambda dt: jax.ShapeDtypeStruct((N_TOK, HALF), dt)
    row = pl.BlockSpec((TM, HALF), lambda i: (i, 0))
    return pl.pallas_call(
        _cd_in_kernel,
        grid=(NT,),
        in_specs=_halo_specs(D_MODEL) + [
            pl.BlockSpec((None, 6, D_MODEL), lambda i: (_mod_row(i), 0, 0)),
            pl.BlockSpec((1, D_MODEL), const),
            pl.BlockSpec((D_MODEL, 5 * HALF), const),
            pl.BlockSpec((3, HALF), const),
            pl.BlockSpec((4, HALF), const),
            pl.BlockSpec((1, HALF), const),
            pl.BlockSpec((HALF, 4 * HALF), const),
            pl.BlockSpec((1, 4 * HALF), const),
            pl.BlockSpec((2, HALF), const),
            pl.BlockSpec((None, 2, HALF), lambda i: (_mod_row(i), 0, 0)),
        ],
        out_specs=[row, row, row, row, row, pl.BlockSpec((8, HALF), lambda i: (i, 0))],
        out_shape=[tok(BF16), tok(F32), tok(F32), tok(F32), tok(F32), jax.ShapeDtypeStruct((NT * 8, HALF), F32)],
        scratch_shapes=[
            pltpu.VMEM((ROWS, D_MODEL), BF16),
            pltpu.VMEM((TM, HALF), F32),
            pltpu.VMEM((TM, HALF), F32),
            pltpu.VMEM((TM, HALF), F32),
            pltpu.VMEM((1, HALF), F32),
        ],
        compiler_params=_params(),
        name="cd_in",
    )(x, x, x, mod, g, w_in, sc_w, conv_w, conv_b, w_gates, b_gates, lru_lam, h0t)


def _cd_out_kernel(ar_ref, br_ref, hsf_ref, gate_ref, yc_ref, x_ref, mod_ref, g_ref, wo_ref, h0_ref,
                   o_ref, fin_ref, hs_scr, carry):
    ti = NT - 1 - pl.program_id(0)
    pos, n = _seq_pos(ti)

    @pl.when(pos == n - 1)
    def _():
        carry[...] = h0_ref[1:2, :]

    _lru_scan(ar_ref, br_ref, hs_scr, carry, True)
    fin_ref[...] = jnp.broadcast_to(carry[...], (8, HALF))
    yd = (hsf_ref[...] + hs_scr[...]) * gate_ref[...]
    out = (jnp.dot(yc_ref[...], wo_ref[0:HALF, :], preferred_element_type=F32)
           + jnp.dot(yd.astype(BF16), wo_ref[HALF:2 * HALF, :], preferred_element_type=F32))
    o_ref[...] = _post(x_ref[...], out, g_ref[...], mod_ref[2:3, :])


def _cd_out(a_r, b_r, hs_f, gate, yc, x, mod, g, w_out, h0t):
    const = lambda i: (0, 0)
    rev = lambda i: (NT - 1 - i, 0)
    row = pl.BlockSpec((TM, HALF), rev)
    return pl.pallas_call(
        _cd_out_kernel,
        grid=(NT,),
        in_specs=[
            row, row, row, row, row,
            pl.BlockSpec((TM, D_MODEL), rev),
            pl.BlockSpec((None, 6, D_MODEL), lambda i: (_mod_row(NT - 1 - i), 0, 0)),
            pl.BlockSpec((1, D_MODEL), const),
            pl.BlockSpec((D_MODEL, D_MODEL), const),
            pl.BlockSpec((None, 2, HALF), lambda i: (_mod_row(NT - 1 - i), 0, 0)),
        ],
        out_specs=[
            pl.BlockSpec((TM, D_MODEL), rev),
            pl.BlockSpec((8, HALF), rev),
        ],
        out_shape=[
            jax.ShapeDtypeStruct((N_TOK, D_MODEL), F32),
            jax.ShapeDtypeStruct((NT * 8, HALF), F32),
        ],
        scratch_shapes=[
            pltpu.VMEM((TM, HALF), F32),
            pltpu.VMEM((1, HALF), F32),
        ],
        compiler_params=_params(),
        name="cd_out",
    )(a_r, b_r, hs_f, gate, yc, x, mod, g, w_out, h0t)


def kernel(x_prompt, x_sample, cache_attn_k, cache_attn_v, state_s5, state_rglru, c, c_ctx, w_mod, b_mod, norm_g, w_in_ab, w_out_ab, s5_lam_re, s5_lam_im, s5_log_dt, s5_b, s5_c, s5_d, s5_w_glu, s5_b_glu, da_lam, da_g, w_in_cd, w_out_cd, sc_conv_w, lru_conv_w, lru_conv_b, lru_w_a, lru_b_a, lru_w_x, lru_b_x, lru_lam, ffn_w_up, ffn_conv_w, ffn_conv_b, ffn_w_down):
    assert DEPTH == 2
    xp = x_prompt.reshape(NT_P * TM, D_MODEL)
    xs = x_sample.reshape(NT_S * TM, D_MODEL)
    cv8 = jnp.zeros((8, D_MODEL), F32).at[0].set(c_ctx).at[1:1 + DEC_BATCH].set(c)
    mod = _modulation(cv8, w_mod, b_mod)
    cos_t, sin_t = _rope_tables()
    g = norm_g.reshape(DEPTH, 4, 1, D_MODEL)

    lam_init = 0.8 - 0.6 * math.exp(-0.3 * 0)
    u, qkv, k32, v32 = _ab_in(xp, xs, mod[0], g[0, 0], w_in_ab[0].astype(BF16), cos_t, sin_t)
    m1, m2, m3, coef = _s5_params(s5_lam_re[0], s5_lam_im[0], s5_log_dt[0], s5_b[0], s5_c[0])
    ys5_p, ys5_s, new_s5 = _s5(u, (m1, m2, m3), coef, state_s5[:, 0], s5_d[0])
    yb_p = _attention(qkv, 0, BATCH, 1, SEQ, None, da_lam[0], da_g[0], lam_init)
    ctx = (cache_attn_k[:, 0].reshape(DEC_BATCH, PAST_LEN, HALF).astype(BF16),
           cache_attn_v[:, 0].reshape(DEC_BATCH, PAST_LEN, HALF).astype(BF16))
    yb_s = _attention(qkv, NT_P, DEC_BATCH, TPS, DEC_SEQ, ctx, da_lam[0], da_g[0], lam_init)
    x = _ab_out(ys5_p, ys5_s, yb_p, yb_s, xp, xs, mod[0], g[0, 1], s5_w_glu[0].astype(BF16), s5_b_glu[0].reshape(1, HALF),
                w_out_ab[0].astype(BF16))
    assert NT_P == BATCH and TM == SEQ
    new_k = k32.reshape(BATCH, 1, DA_HEADS, 2, DA_DK, SEQ).transpose(0, 1, 5, 2, 3, 4)
    new_v = v32
    x = _ffn(x, mod[0], g[0, 2], g[0, 3], ffn_w_up, ffn_conv_w[0], ffn_conv_b[0].reshape(1, 2 * D_FF), ffn_w_down,
             0, False)

    eye = jnp.eye(LRU_BLOCKS, dtype=F32)
    dense = lambda w: jnp.einsum('kcd,kl->kcld', w, eye).reshape(LRU_WIDTH, LRU_WIDTH)
    w_gates = jnp.concatenate([dense(lru_w_a[0, 0]), dense(lru_w_x[0, 0]),
                               dense(lru_w_a[0, 1]), dense(lru_w_x[0, 1])], axis=1).astype(BF16)
    b_gates = jnp.concatenate([lru_b_a[0, 0], lru_b_x[0, 0], lru_b_a[0, 1], lru_b_x[0, 1]]).reshape(1, 4 * HALF)
    h0t = jnp.zeros((8, 2, HALF), F32).at[1:1 + DEC_BATCH].set(state_rglru[:, 0])
    yc, gate, hs_f, a_r, b_r, fin_f = _cd_in(x, mod[1], g[1, 0], w_in_cd[0].astype(BF16), sc_conv_w[0], lru_conv_w[0],
                                      lru_conv_b[0].reshape(1, HALF), w_gates, b_gates, lru_lam[0], h0t)
    x, fin_r = _cd_out(a_r, b_r, hs_f, gate, yc, x, mod[1], g[1, 1], w_out_cd[0].astype(BF16), h0t)
    tile_row0 = lambda f: f.reshape(NT, 8, HALF)[:NT_P, 0]
    new_lru = jnp.stack([tile_row0(fin_f), tile_row0(fin_r)], axis=1)[:, None]
    yp, ys = _ffn(x, mod[1], g[1, 2], g[1, 3], ffn_w_up, ffn_conv_w[1], ffn_conv_b[1].reshape(1, 2 * D_FF),
                  ffn_w_down, 1, True)
    return (yp.reshape(BATCH, SEQ, D_MODEL), ys.reshape(DEC_BATCH, DEC_SEQ, D_MODEL),
            new_k, new_v, new_s5[:, None], new_lru)
```

```python
import functools
import math

import jax
import jax.numpy as jnp
from jax import lax
from jax.experimental import pallas as pl
from jax.experimental.pallas import tpu as pltpu

D_MODEL = 1024
BATCH = 32
SEQ = 256
DEPTH = 2
DEC_BATCH = 4
DEC_SEQ = 2048
PAST_LEN = 512
GRID_W = 64
HALF = D_MODEL // 2
S5_P = 16
S5_G = HALF // S5_P
S5_N = 64
DA_DK = 64
DA_DV = 2 * DA_DK
DA_HEADS = HALF // DA_DV
ROPE_THETA = 10000.0
ROPE_F = DA_DK // 4
LRU_WIDTH = HALF
LRU_BLOCKS = 8
LRU_BS = LRU_WIDTH // LRU_BLOCKS
LRU_C = 8.0
D_FF = 2816
EPS = 1e-6

F32 = jnp.float32
BF16 = jnp.bfloat16
HIGHEST = lax.Precision.HIGHEST

TM = 256
NT_P = BATCH * SEQ // TM
TPS = DEC_SEQ // TM
NT_S = DEC_BATCH * TPS
NT = NT_P + NT_S
N_TOK = NT * TM
HALO = 8
ROWS = TM + 2 * HALO
S5_T = 8
S5_GB = 128 // S5_P
S5_NB = S5_G // S5_GB
S5_K = S5_T * 128
S5_ST = 4 * S5_N
S5_GS = 4
FF_CHUNK = 256
FF_NCH = D_FF // FF_CHUNK
Q_SCALE = math.log2(math.e) / math.sqrt(DA_DK)
GELU_K = math.sqrt(2.0 / math.pi)
GELU_C = 0.044715
VMEM_LIMIT = 56 * 1024 * 1024


def _mod_row(i):
    return jnp.where(i < NT_P, 0, 1 + (i - NT_P) // TPS)


def _seq_pos(i):
    return jnp.where(i < NT_P, 0, (i - NT_P) % TPS), jnp.where(i < NT_P, 1, TPS)


def _split_specs(width):
    return [
        pl.BlockSpec((TM, width), lambda i: (jnp.minimum(i, NT_P - 1), 0)),
        pl.BlockSpec((TM, width), lambda i: (jnp.maximum(i - NT_P, 0), 0)),
    ]


def _pick(p_ref, s_ref):
    return jnp.where(pl.program_id(0) < NT_P, p_ref[...], s_ref[...])


def _params(sem=("arbitrary",)):
    return pltpu.CompilerParams(dimension_semantics=sem, vmem_limit_bytes=VMEM_LIMIT)


def _rms(x, g):
    ms = jnp.mean(x * x, axis=-1, keepdims=True)
    return x * lax.rsqrt(ms + EPS) * g


def _pre(x, g, shift, scale):
    return _rms(x, g) * (1.0 + scale) + shift


def _mod_kernel(cv_ref, w_ref, b_ref, o_ref):
    cv = cv_ref[...]
    s = cv * jax.nn.sigmoid(cv)
    w = w_ref[...]
    s_hi = s.astype(BF16)
    s_lo = (s - s_hi.astype(F32)).astype(BF16)
    w_hi = w.astype(BF16)
    w_lo = (w - w_hi.astype(F32)).astype(BF16)
    dot = lambda a, b: jnp.dot(a, b, preferred_element_type=F32)
    o_ref[...] = dot(s_hi, w_hi) + (dot(s_hi, w_lo) + dot(s_lo, w_hi)) + b_ref[...]


def _modulation(cv8, w_mod, b_mod):
    nb = 1536
    out = pl.pallas_call(
        _mod_kernel,
        grid=(DEPTH, 6 * D_MODEL // nb),
        in_specs=[
            pl.BlockSpec((8, D_MODEL), lambda l, j: (0, 0)),
            pl.BlockSpec((None, D_MODEL, nb), lambda l, j: (l, 0, j)),
            pl.BlockSpec((None, 1, nb), lambda l, j: (l, 0, j)),
        ],
        out_specs=pl.BlockSpec((None, 8, nb), lambda l, j: (l, 0, j)),
        out_shape=jax.ShapeDtypeStruct((DEPTH, 8, 6 * D_MODEL), F32),
        compiler_params=_params(("arbitrary", "arbitrary")),
        name="modulation",
    )(cv8, w_mod, b_mod.reshape(DEPTH, 1, 6 * D_MODEL))
    return out.reshape(DEPTH, 8, 6, D_MODEL)


def _rope_partner(x):
    lane = lax.broadcasted_iota(jnp.int32, (1, HALF), 1)
    first = (lane % (2 * ROPE_F)) < ROPE_F
    return jnp.where(first, pltpu.roll(x, HALF - ROPE_F, axis=1), pltpu.roll(x, ROPE_F, axis=1))


def _ab_in_kernel(xp_ref, xs_ref, mod_ref, g_ref, w_ref, cos_ref, sin_ref, u_ref, qkv_ref, k_ref, v_ref):
    i = pl.program_id(0)
    h = _pre(_pick(xp_ref, xs_ref), g_ref[...], mod_ref[0:1, :], mod_ref[1:2, :])
    proj = jnp.dot(h.astype(BF16), w_ref[...], preferred_element_type=F32)
    u_ref[...] = proj[:, 0:HALF]
    q = proj[:, HALF:2 * HALF]
    k = proj[:, 2 * HALF:3 * HALF]
    v = proj[:, 3 * HALF:4 * HALF]
    cos = jnp.concatenate([cos_ref[...]] * DA_HEADS, axis=1)
    sin = jnp.concatenate([sin_ref[...]] * DA_HEADS, axis=1)
    qr = q * cos + _rope_partner(q) * sin
    kr = k * cos + _rope_partner(k) * sin
    qkv_ref[:, 0:HALF] = (qr * Q_SCALE).astype(BF16)
    qkv_ref[:, HALF:2 * HALF] = kr.astype(BF16)
    qkv_ref[:, 2 * HALF:3 * HALF] = v.astype(BF16)

    @pl.when(i < NT_P)
    def _():
        k_ref[...] = k.T
        v_ref[...] = v.reshape(TM, DA_HEADS, DA_DV)


def _ab_in(xp, xs, mod, g, w, cos_t, sin_t):
    def rope_tile(i):
        return jnp.where(i < NT_P, 0, 1 + (i - NT_P) % TPS)

    return pl.pallas_call(
        _ab_in_kernel,
        grid=(NT,),
        in_specs=_split_specs(D_MODEL) + [
            pl.BlockSpec((None, 6, D_MODEL), lambda i: (_mod_row(i), 0, 0)),
            pl.BlockSpec((1, D_MODEL), lambda i: (0, 0)),
            pl.BlockSpec((D_MODEL, 4 * HALF), lambda i: (0, 0)),
            pl.BlockSpec((TM, DA_DV), lambda i: (rope_tile(i), 0)),
            pl.BlockSpec((TM, DA_DV), lambda i: (rope_tile(i), 0)),
        ],
        out_specs=[
            pl.BlockSpec((TM, HALF), lambda i: (i, 0)),
            pl.BlockSpec((TM, 3 * HALF), lambda i: (i, 0)),
            pl.BlockSpec((None, HALF, TM), lambda i: (jnp.minimum(i, NT_P - 1), 0, 0)),
            pl.BlockSpec((None, None, TM, DA_HEADS, DA_DV), lambda i: (jnp.minimum(i, NT_P - 1), 0, 0, 0, 0)),
        ],
        out_shape=[
            jax.ShapeDtypeStruct((N_TOK, HALF), F32),
            jax.ShapeDtypeStruct((N_TOK, 3 * HALF), BF16),
            jax.ShapeDtypeStruct((NT_P, HALF, TM), F32),
            jax.ShapeDtypeStruct((NT_P, 1, TM, DA_HEADS, DA_DV), F32),
        ],
        compiler_params=_params(),
        name="ab_in",
    )(xp, xs, mod, g, w, cos_t, sin_t)


def _rope_tables():
    pos = jnp.arange(DEC_SEQ)
    t_row = (pos // GRID_W).astype(F32)
    t_col = (pos % GRID_W).astype(F32)
    inv = ROPE_THETA ** (-jnp.arange(ROPE_F, dtype=F32) / ROPE_F)
    ang = jnp.stack([t_row[:, None] * inv, t_col[:, None] * inv], axis=1)
    cos, sin = jnp.cos(ang), jnp.sin(ang)
    cos64 = jnp.stack([cos, cos], axis=2).reshape(DEC_SEQ, DA_DK)
    sin64 = jnp.stack([-sin, sin], axis=2).reshape(DEC_SEQ, DA_DK)
    cos_t = jnp.concatenate([jnp.ones((TM, DA_DV), F32), jnp.tile(cos64, (1, 2))], axis=0)
    sin_t = jnp.concatenate([jnp.zeros((TM, DA_DV), F32), jnp.tile(sin64, (1, 2))], axis=0)
    return cos_t, sin_t


def _s5_param_kernel(lre_ref, lim_ref, ldt_ref, bt_ref, c_ref, m1_ref, m2_ref, m3_ref, at_ref):
    nt = (((1,), (1,)), ((), ()))
    e_tab, f_tab, k_tab = [], [], []
    r_i = lax.broadcasted_iota(jnp.int32, (128, 128), 0) // S5_P
    c_i = lax.broadcasted_iota(jnp.int32, (128, 128), 1) // S5_P
    same_group = r_i == c_i
    for d in range(2):
        lre = lre_ref[d]
        lim = lim_ref[d]
        dt = jnp.exp(ldt_ref[d])
        mag = jnp.exp(lre * dt)
        ang = lim * dt
        ar = mag * jnp.cos(ang)
        ai = mag * jnp.sin(ang)
        den = lre * lre + lim * lim
        nr = ar - 1.0
        fr = (nr * lre + ai * lim) / den
        fi = (ai * lre - nr * lim) / den
        btr = bt_ref[d, 0]
        bti = bt_ref[d, 1]
        bfr = fr * btr - fi * bti
        bfi = fr * bti + fi * btr
        cr = c_ref[d, 0]
        ci = c_ref[d, 1]
        cst = jnp.concatenate([cr, -ci], axis=1)
        pr = jnp.ones_like(ar)
        pi = jnp.zeros_like(ar)
        e_d, f_d, k_d = [], [], []
        for tau in range(S5_T + 1):
            f_d.append((cr * pr - ci * pi, cr * pi + ci * pr))
            if tau < S5_T:
                er, ei = pr * bfr - pi * bfi, pr * bfi + pi * bfr
                e_d.append((er, ei))
                full = lax.dot_general(jnp.concatenate([er, ei], axis=1), cst, nt,
                                       precision=HIGHEST, preferred_element_type=F32)
                k_d.append(jnp.where(same_group, full, 0.0))
            else:
                at_ref[d, 0] = pr
                at_ref[d, 1] = pi
            pr, pi = pr * ar - pi * ai, pr * ai + pi * ar
        e_tab.append(e_d)
        f_tab.append(f_d)
        k_tab.append(k_d)

    for s in range(S5_T):
        for t in range(S5_T):
            blk = k_tab[0][t - s] if t >= s else k_tab[1][s - t]
            if t == s:
                blk = blk + k_tab[1][0]
            m1_ref[s * 128:(s + 1) * 128, t * 128:(t + 1) * 128] = blk.astype(BF16)
    row_group = lax.broadcasted_iota(jnp.int32, (128, S5_ST), 0) // S5_P
    for s in range(S5_T):
        ef, er_ = e_tab[0][S5_T - 1 - s], e_tab[1][s]
        v = jnp.concatenate([ef[0], er_[0], ef[1], er_[1]], axis=1)
        ff, fr_ = f_tab[0][s + 1], f_tab[1][S5_T - s]
        w = jnp.concatenate([ff[0], fr_[0], -ff[1], -fr_[1]], axis=1)
        for gi in range(S5_GB):
            sel = row_group == gi
            m2_ref[s * 128:(s + 1) * 128, gi * S5_ST:(gi + 1) * S5_ST] = jnp.where(sel, v, 0.0).astype(BF16)
            m3_ref[s * 128:(s + 1) * 128, gi * S5_ST:(gi + 1) * S5_ST] = jnp.where(sel, w, 0.0).astype(BF16)


def _s5_params(lam_re, lam_im, log_dt, b_w, c_w):
    rows = S5_G * S5_P
    per_row = lambda a: jnp.repeat(a, S5_P, axis=1)
    lre = per_row(lam_re)
    lim = per_row(lam_im)
    ldt = per_row(jnp.broadcast_to(log_dt[:, :, None], (2, S5_G, S5_N)))
    bt = b_w.transpose(0, 1, 2, 4, 3).reshape(2, 2, rows, S5_N)
    cc = c_w.reshape(2, 2, rows, S5_N)
    vec = pl.BlockSpec((2, 128, S5_N), lambda j: (0, j, 0))
    mat = pl.BlockSpec((2, 2, 128, S5_N), lambda j: (0, 0, j, 0))
    wide = lambda n: pl.BlockSpec((None, S5_K, n), lambda j: (j, 0, 0))
    m1, m2, m3, at = pl.pallas_call(
        _s5_param_kernel,
        grid=(S5_NB,),
        in_specs=[vec, vec, vec, mat, mat],
        out_specs=[wide(S5_K), wide(S5_GB * S5_ST), wide(S5_GB * S5_ST),
                   pl.BlockSpec((None, 2, 2, 128, S5_N), lambda j: (j, 0, 0, 0, 0))],
        out_shape=[
            jax.ShapeDtypeStruct((S5_NB, S5_K, S5_K), BF16),
            jax.ShapeDtypeStruct((S5_NB, S5_K, S5_GB * S5_ST), BF16),
            jax.ShapeDtypeStruct((S5_NB, S5_K, S5_GB * S5_ST), BF16),
            jax.ShapeDtypeStruct((S5_NB, 2, 2, 128, S5_N), F32),
        ],
        compiler_params=_params(),
        name="s5_params",
    )(lre, lim, ldt, bt, cc)
    coef = at[:, :, :, ::S5_P, :].transpose(0, 3, 2, 1, 4).reshape(S5_NB, S5_GB, 2, 2 * S5_N)
    return m1, m2, m3, coef


def _s5_kernel(u_ref, m1_ref, m2_ref, m3_ref, coef_ref, d_ref, *rest, stride, nc, two_pass):
    if two_pass:
        h0_ref, y_ref, lhs, yacc, s_scr, hp_scr = rest
    else:
        y_ref, fin_ref, lhs, yacc, s_scr, hp_scr = rest
    ln = 2 * S5_N
    nt = (((1,), (1,)), ((), ()))

    def seq_rows(k, c):
        return pl.ds(pl.multiple_of(k * stride + c * S5_T, S5_T), S5_T)

    def gather(c2, carry):
        halves = []
        for c in (2 * c2, 2 * c2 + 1):
            tiles = [u_ref[seq_rows(k, c), :] for k in range(8)]
            halves.append(jnp.stack(tiles, axis=0).reshape(8, S5_K))
        lhs[pl.ds(pl.multiple_of(c2 * 16, 16), 16), :] = jnp.concatenate(halves, axis=0).astype(BF16)
        return carry

    lax.fori_loop(0, nc // 2, gather, 0)
    lb = lhs[...]
    yacc[...] = jnp.dot(lb, m1_ref[...], preferred_element_type=F32)

    lane = lax.broadcasted_iota(jnp.int32, (8, ln), 1)
    row = lax.broadcasted_iota(jnp.int32, (8, ln), 0)
    fwd = lane < S5_N
    odd = (row % 2) == 1
    wide = S5_GS * S5_ST
    for gq in range(S5_GB // S5_GS):
        groups = range(gq * S5_GS, (gq + 1) * S5_GS)
        cols = slice(gq * wide, (gq + 1) * wide)
        s_scr[...] = jnp.dot(lb, m2_ref[:, cols], preferred_element_type=F32)
        are = [jnp.broadcast_to(coef_ref[g, 0:1, :], (8, ln)) for g in groups]
        aim = [jnp.broadcast_to(coef_ref[g, 1:2, :], (8, ln)) for g in groups]

        def scan(init, store):
            def body(c, carry):
                kf = pl.multiple_of(c * 8, 8)
                kr = pl.multiple_of((nc - 1 - c) * 8, 8)
                out = []
                for j in range(S5_GS):
                    p, q = carry[2 * j], carry[2 * j + 1]
                    o = j * S5_ST
                    if store:
                        hp_scr[pl.ds(kf, 8), o:o + S5_N] = p[:, 0:S5_N]
                        hp_scr[pl.ds(kr, 8), o + S5_N:o + ln] = p[:, S5_N:ln]
                        hp_scr[pl.ds(kf, 8), o + ln:o + ln + S5_N] = q[:, 0:S5_N]
                        hp_scr[pl.ds(kr, 8), o + ln + S5_N:o + 2 * ln] = q[:, S5_N:ln]
                    s_re = jnp.where(fwd, s_scr[pl.ds(kf, 8), o:o + ln], s_scr[pl.ds(kr, 8), o:o + ln])
                    s_im = jnp.where(fwd, s_scr[pl.ds(kf, 8), o + ln:o + 2 * ln],
                                     s_scr[pl.ds(kr, 8), o + ln:o + 2 * ln])
                    out += [are[j] * p - aim[j] * q + s_re, are[j] * q + aim[j] * p + s_im]
                return tuple(out)
            return lax.fori_loop(0, nc, body, tuple(init))

        if two_pass:
            h0 = [h0_ref[g, r] for g in groups for r in range(2)]
            fin = scan(h0, False)
            seed = [h + jnp.where(fwd & odd, pltpu.roll(f, 1, axis=0), 0.0)
                    + jnp.where(jnp.logical_not(fwd | odd), pltpu.roll(f, 7, axis=0), 0.0)
                    for h, f in zip(h0, fin)]
            scan(seed, True)
        else:
            fin = scan([jnp.zeros((8, ln), F32)] * (2 * S5_GS), True)
            for j, g in enumerate(groups):
                fin_ref[g, 0] = fin[2 * j]
                fin_ref[g, 1] = fin[2 * j + 1]
        yacc[...] += lax.dot_general(hp_scr[...].astype(BF16), m3_ref[:, cols], nt, preferred_element_type=F32)

    dvec = d_ref[...]

    def scatter(c, carry):
        blk = yacc[pl.ds(pl.multiple_of(c * 8, 8), 8), :].reshape(8, S5_T, 128)
        for k in range(8):
            rows = seq_rows(k, c)
            y_ref[rows, :] = blk[k] + u_ref[rows, :] * dvec
        return carry

    lax.fori_loop(0, nc, scatter, 0)


def _s5_call(u, row_block, mats, coef, dvec, h0, stride, nc, parts):
    m1, m2, m3 = mats
    rows = 8 * stride
    two_pass = h0 is not None
    single = dict(pipeline_mode=pl.Buffered(1)) if parts == 1 else {}
    mat = lambda n: pl.BlockSpec((None, S5_K, n), lambda j, i: (j, 0, 0), **single)
    tok = pl.BlockSpec((rows, 128), lambda j, i: (row_block * parts + i, j))
    in_specs = [
        tok, mat(S5_K), mat(S5_GB * S5_ST), mat(S5_GB * S5_ST),
        pl.BlockSpec((None, S5_GB, 2, 2 * S5_N), lambda j, i: (j, 0, 0, 0)),
        pl.BlockSpec((None, 1, 128), lambda j, i: (j, 0, 0)),
    ]
    args = [u, m1, m2, m3, coef, dvec]
    out_specs = [pl.BlockSpec((rows, 128), lambda j, i: (i, j))]
    out_shape = [jax.ShapeDtypeStruct((parts * rows, HALF), F32)]
    if two_pass:
        in_specs.append(pl.BlockSpec((None, S5_GB, 2, 8, 2 * S5_N), lambda j, i: (j, 0, 0, 0, 0)))
        args.append(h0)
    else:
        out_specs.append(pl.BlockSpec((None, S5_GB, 2, 8, 2 * S5_N), lambda j, i: (j, 0, 0, i, 0)))
        out_shape.append(jax.ShapeDtypeStruct((S5_NB, S5_GB, 2, parts * 8, 2 * S5_N), F32))
    return pl.pallas_call(
        functools.partial(_s5_kernel, stride=stride, nc=nc, two_pass=two_pass),
        grid=(S5_NB, parts),
        in_specs=in_specs,
        out_specs=out_specs,
        out_shape=out_shape,
        scratch_shapes=[
            pltpu.VMEM((nc * 8, S5_K), BF16),
            pltpu.VMEM((nc * 8, S5_K), F32),
            pltpu.VMEM((nc * 8, S5_GS * S5_ST), F32),
            pltpu.VMEM((nc * 8, S5_GS * S5_ST), F32),
        ],
        compiler_params=_params(("arbitrary", "arbitrary")),
        name="s5_%d" % stride,
    )(*args)


def _s5(u, mats, coef, h0, s5_d):
    dvec = s5_d.reshape(S5_NB, 1, 128)
    yp, fin = _s5_call(u, 0, mats, coef, dvec, None, SEQ, SEQ // S5_T, BATCH // 8)
    hh = h0.transpose(3, 2, 0, 1, 4)
    z = jnp.zeros_like(hh[:, :, :, 0])
    first = jnp.concatenate([hh[:, :, :, 0], z], axis=-1)
    second = jnp.concatenate([z, hh[:, :, :, 1]], axis=-1)
    h0g = jnp.stack([first, second], axis=3).reshape(S5_NB, S5_GB, 2, 2 * DEC_BATCH, 2 * S5_N)
    half = DEC_SEQ // 2
    (ys,) = _s5_call(u, 1, mats, coef, dvec, h0g, half, half // S5_T, 1)
    new_s5 = fin.reshape(S5_G, 2, BATCH, 2, S5_N).transpose(2, 3, 1, 0, 4)
    return yp, ys, new_s5


def _attn_kernel(*refs, lam_init, has_ctx, lk):
    if has_ctx:
        q_ref, k_ref, v_ref, kc_ref, vc_ref, dl_ref, g_ref, o_ref, s_a, s_b, vx, vcx = refs
    else:
        q_ref, k_ref, v_ref, dl_ref, g_ref, o_ref, s_a, s_b, vx = refs
    lt = lk + (PAST_LEN if has_ctx else 0)
    dl = dl_ref[...]
    lam = (jnp.exp(jnp.sum(dl[0:1] * dl[1:2], keepdims=True))
           - jnp.exp(jnp.sum(dl[2:3] * dl[3:4], keepdims=True)) + lam_init)
    lane = lax.broadcasted_iota(jnp.int32, (1, DA_DV), 1)
    first = lane < DA_DK
    g = g_ref[...]
    nt = (((1,), (1,)), ((), ()))

    @pl.when(pl.program_id(1) == 0)
    def _():
        onehot = (lane == 0).astype(BF16)
        for h in range(DA_HEADS):
            cols = slice(h * DA_DV, (h + 1) * DA_DV)
            vx[:, 2 * h * DA_DV:(2 * h + 1) * DA_DV] = v_ref[:, cols]
            vx[:, (2 * h + 1) * DA_DV:(2 * h + 2) * DA_DV] = jnp.broadcast_to(onehot, (lk, DA_DV))
            if has_ctx:
                vcx[:, 2 * h * DA_DV:(2 * h + 1) * DA_DV] = vc_ref[:, cols]
                vcx[:, (2 * h + 1) * DA_DV:(2 * h + 2) * DA_DV] = jnp.broadcast_to(onehot, (PAST_LEN, DA_DV))

    def scores(h, dst):
        cols = slice(h * DA_DV, (h + 1) * DA_DV)
        qh = q_ref[:, cols]
        zero = jnp.zeros_like(qh)
        for m in range(2):
            qm = jnp.where(first if m == 0 else jnp.logical_not(first), qh, zero)
            dst[m, :, 0:lk] = lax.dot_general(qm, k_ref[:, cols], nt, preferred_element_type=F32)
            if has_ctx:
                dst[m, :, lk:lt] = lax.dot_general(qm, kc_ref[:, cols], nt, preferred_element_type=F32)

    def head(h, src):
        xcols = slice(2 * h * DA_DV, (2 * h + 2) * DA_DV)
        outs = []
        for m in range(2):
            s = src[m]
            e = jnp.exp2(s - jnp.max(s, axis=-1, keepdims=True)).astype(BF16)
            ox = jnp.dot(e[:, 0:lk], vx[:, xcols], preferred_element_type=F32)
            if has_ctx:
                ox = ox + jnp.dot(e[:, lk:lt], vcx[:, xcols], preferred_element_type=F32)
            outs.append(ox[:, 0:DA_DV] * (1.0 / ox[:, DA_DV:DA_DV + 1]))
        o = outs[0] - lam * outs[1]
        o_ref[:, h * DA_DV:(h + 1) * DA_DV] = (_rms(o, g) * (1.0 - lam_init)).astype(BF16)

    scores(0, s_a)
    for h in range(DA_HEADS):
        cur, nxt = (s_a, s_b) if h % 2 == 0 else (s_b, s_a)
        if h + 1 < DA_HEADS:
            scores(h + 1, nxt)
        head(h, cur)


def _attention(qkv, q_tile0, nb, nq, lk, ctx, da_lam, da_g, lam_init):
    kb0 = q_tile0 * TM // lk
    in_specs = [
        pl.BlockSpec((TM, HALF), lambda b, j: (q_tile0 + b * nq + j, 0)),
        pl.BlockSpec((lk, HALF), lambda b, j: (kb0 + b, 1)),
        pl.BlockSpec((lk, HALF), lambda b, j: (kb0 + b, 2)),
    ]
    args = [qkv, qkv, qkv]
    if ctx is not None:
        in_specs += [pl.BlockSpec((None, PAST_LEN, HALF), lambda b, j: (b, 0, 0))] * 2
        args += list(ctx)
    in_specs += [
        pl.BlockSpec((4, DA_DK), lambda b, j: (0, 0)),
        pl.BlockSpec((1, DA_DV), lambda b, j: (0, 0)),
    ]
    lt = lk + (PAST_LEN if ctx is not None else 0)
    scratch = [pltpu.VMEM((2, TM, lt), F32), pltpu.VMEM((2, TM, lt), F32), pltpu.VMEM((lk, 2 * HALF), BF16)]
    if ctx is not None:
        scratch.append(pltpu.VMEM((PAST_LEN, 2 * HALF), BF16))
    return pl.pallas_call(
        functools.partial(_attn_kernel, lam_init=lam_init, has_ctx=ctx is not None, lk=lk),
        grid=(nb, nq),
        in_specs=in_specs,
        out_specs=pl.BlockSpec((TM, HALF), lambda b, j: (b * nq + j, 0)),
        out_shape=jax.ShapeDtypeStruct((nb * nq * TM, HALF), BF16),
        scratch_shapes=scratch,
        compiler_params=_params(("arbitrary", "arbitrary")),
        name="diff_attn_%d" % lk,
    )(*args, da_lam, da_g.reshape(1, DA_DV))


def _post(x, y, g, gate):
    return x + gate * _rms(y, g)


def _ab_out_kernel(ysp_ref, yss_ref, ybp_ref, ybs_ref, xp_ref, xs_ref, mod_ref, g_ref, wg_ref, bg_ref, wo_ref,
                   o_ref):
    ys = jax.nn.gelu(_pick(ysp_ref, yss_ref))
    glu = jnp.dot(ys.astype(BF16), wg_ref[...], preferred_element_type=F32) + bg_ref[...]
    ya = ys * jax.nn.sigmoid(glu)
    out = (jnp.dot(ya.astype(BF16), wo_ref[0:HALF, :], preferred_element_type=F32)
           + jnp.dot(_pick(ybp_ref, ybs_ref), wo_ref[HALF:2 * HALF, :], preferred_element_type=F32))
    o_ref[...] = _post(_pick(xp_ref, xs_ref), out, g_ref[...], mod_ref[2:3, :])


def _ab_out(ys5_p, ys5_s, yb_p, yb_s, xp, xs, mod, g, w_glu, b_glu, w_out):
    return pl.pallas_call(
        _ab_out_kernel,
        grid=(NT,),
        in_specs=_split_specs(HALF) + _split_specs(HALF) + _split_specs(D_MODEL) + [
            pl.BlockSpec((None, 6, D_MODEL), lambda i: (_mod_row(i), 0, 0)),
            pl.BlockSpec((1, D_MODEL), lambda i: (0, 0)),
            pl.BlockSpec((HALF, HALF), lambda i: (0, 0)),
            pl.BlockSpec((1, HALF), lambda i: (0, 0)),
            pl.BlockSpec((D_MODEL, D_MODEL), lambda i: (0, 0)),
        ],
        out_specs=pl.BlockSpec((TM, D_MODEL), lambda i: (i, 0)),
        out_shape=jax.ShapeDtypeStruct((N_TOK, D_MODEL), F32),
        compiler_params=_params(),
        name="ab_out",
    )(ys5_p, ys5_s, yb_p, yb_s, xp, xs, mod, g, w_glu, b_glu, w_out)


def _halo_specs(width):
    blocks = TM // HALO
    last = N_TOK // HALO - 1
    return [
        pl.BlockSpec((TM, width), lambda i: (i, 0)),
        pl.BlockSpec((HALO, width), lambda i: (jnp.maximum(i * blocks - 1, 0), 0)),
        pl.BlockSpec((HALO, width), lambda i: (jnp.minimum((i + 1) * blocks, last), 0)),
    ]


def _fill_hbuf(hbuf, x_ref, xp_ref, xn_ref, g, shift, scale, i):
    pos, n = _seq_pos(i)
    hp = jnp.where(pos > 0, _pre(xp_ref[...], g, shift, scale), 0.0)
    hn = jnp.where(pos < n - 1, _pre(xn_ref[...], g, shift, scale), 0.0)
    hbuf[0:TM, :] = _pre(x_ref[...], g, shift, scale).astype(BF16)
    hbuf[TM:ROWS, :] = jnp.concatenate([hn, hp], axis=0).astype(BF16)


def _shift_rows(x, s):
    if s == 0:
        return x[0:TM]
    return pltpu.roll(x, (-s) % ROWS, axis=0)[0:TM]


def _ffn_kernel(x_ref, xp_ref, xn_ref, mod_ref, g2_ref, g3_ref, wg_ref, wv_ref, wd_ref, cw_ref, cb_ref, *rest, split):
    hbuf, acc, u_a, u_b, wu_s, wd_s = rest[-6:]
    step = pl.program_id(0)

    @pl.when(step < FF_NCH)
    def _():
        wu_s[step] = wg_ref[...].astype(BF16)
        wu_s[FF_NCH + step] = wv_ref[...].astype(BF16)
        wd_s[step] = wd_ref[...].astype(BF16)

    @pl.when(step >= FF_NCH)
    def _():
        i = step - FF_NCH
        _fill_hbuf(hbuf, x_ref, xp_ref, xn_ref, g2_ref[...], mod_ref[3:4, :], mod_ref[4:5, :], i)

        def up(j, dst):
            for half in range(2):
                dst[half] = jnp.dot(hbuf[...], wu_s[half * FF_NCH + j], preferred_element_type=F32)

        def activation(j, src):
            parts = []
            for half in range(2):
                c0 = half * D_FF + j * FF_CHUNK
                sc = 1.0 if half == 0 else 0.5
                cw = cw_ref[:, c0:c0 + FF_CHUNK] * sc
                u = src[half]
                parts.append(_shift_rows(u, -1) * cw[0:1] + _shift_rows(u, 0) * cw[1:2]
                             + _shift_rows(u, 1) * cw[2:3] + cb_ref[:, c0:c0 + FF_CHUNK] * sc)
            gt = parts[0]
            z = gt * (gt * gt * (GELU_C * GELU_K) + GELU_K)
            return ((gt + gt * jnp.tanh(z)) * parts[1]).astype(BF16)

        up(0, u_a)
        for j in range(FF_NCH):
            cur, nxt = (u_a, u_b) if j % 2 == 0 else (u_b, u_a)
            if j + 1 < FF_NCH:
                up(j + 1, nxt)
            contrib = jnp.dot(activation(j, cur), wd_s[j], preferred_element_type=F32)
            if j == 0:
                acc[...] = contrib
            else:
                acc[...] += contrib
        res = _post(x_ref[...], acc[...], g3_ref[...], mod_ref[5:6, :])
        if split:
            @pl.when(i < NT_P)
            def _():
                rest[0][...] = res

            @pl.when(i >= NT_P)
            def _():
                rest[1][...] = res
        else:
            rest[0][...] = res


def _ffn(x, mod, g2, g3, w_up, cw, cb, w_down, layer, split):
    const = lambda s: (0, 0)
    tile = lambda s: jnp.maximum(s - FF_NCH, 0)
    chunk = lambda s: jnp.minimum(s, FF_NCH - 1)
    shifted = lambda spec: pl.BlockSpec(spec.block_shape, lambda s, f=spec.index_map: f(tile(s)))
    if split:
        out_specs = [shifted(sp) for sp in _split_specs(D_MODEL)]
        out_shape = [jax.ShapeDtypeStruct((NT_P * TM, D_MODEL), F32), jax.ShapeDtypeStruct((NT_S * TM, D_MODEL), F32)]
    else:
        out_specs = pl.BlockSpec((TM, D_MODEL), lambda s: (tile(s), 0))
        out_shape = jax.ShapeDtypeStruct((N_TOK, D_MODEL), F32)
    return pl.pallas_call(
        functools.partial(_ffn_kernel, split=split),
        grid=(FF_NCH + NT,),
        in_specs=[shifted(sp) for sp in _halo_specs(D_MODEL)] + [
            pl.BlockSpec((None, 6, D_MODEL), lambda s: (_mod_row(tile(s)), 0, 0)),
            pl.BlockSpec((1, D_MODEL), const),
            pl.BlockSpec((1, D_MODEL), const),
            pl.BlockSpec((None, D_MODEL, FF_CHUNK), lambda s: (layer, 0, chunk(s))),
            pl.BlockSpec((None, D_MODEL, FF_CHUNK), lambda s: (layer, 0, FF_NCH + chunk(s))),
            pl.BlockSpec((None, FF_CHUNK, D_MODEL), lambda s: (layer, chunk(s), 0)),
            pl.BlockSpec((3, 2 * D_FF), const),
            pl.BlockSpec((1, 2 * D_FF), const),
        ],
        out_specs=out_specs,
        out_shape=out_shape,
        scratch_shapes=[
            pltpu.VMEM((ROWS, D_MODEL), BF16),
            pltpu.VMEM((TM, D_MODEL), F32),
            pltpu.VMEM((2, ROWS, FF_CHUNK), F32),
            pltpu.VMEM((2, ROWS, FF_CHUNK), F32),
            pltpu.VMEM((2 * FF_NCH, D_MODEL, FF_CHUNK), BF16),
            pltpu.VMEM((FF_NCH, FF_CHUNK, D_MODEL), BF16),
        ],
        compiler_params=_params(),
        name="conv_ffn",
    )(x, x, x, mod, g2, g3, w_up, w_up, w_down, cw, cb)


def _softplus(z):
    return jnp.maximum(z, 0.0) + jnp.log(1.0 + jnp.exp(-jnp.abs(z)))


def _lru_scan(a_ref, b_ref, hs_ref, carry, reverse):
    h = carry[...]
    for r in (range(TM - 1, -1, -1) if reverse else range(TM)):
        h = a_ref[r:r + 1, :] * h + b_ref[r:r + 1, :]
        hs_ref[r:r + 1, :] = h
    carry[...] = h
    return h


def _cd_in_kernel(x_ref, xp_ref, xn_ref, mod_ref, g_ref, w_ref, scw_ref, cw_ref, cb_ref, wg_ref, bg_ref,
                  lam_ref, h0_ref, yc_ref, gate_ref, hsf_ref, ar_ref, br_ref, fin_ref,
                  hbuf, a_scr, b_scr, carry):
    i = pl.program_id(0)
    pos, _ = _seq_pos(i)
    _fill_hbuf(hbuf, x_ref, xp_ref, xn_ref, g_ref[...], mod_ref[0:1, :], mod_ref[1:2, :], i)

    def col(k, rows):
        return jnp.dot(hbuf[0:rows, :], w_ref[:, k * HALF:(k + 1) * HALF], preferred_element_type=F32)

    xr = col(3, ROWS)
    xin = col(0, ROWS)
    cg = col(2, ROWS)
    cw = cw_ref[...]
    xc = (_shift_rows(xr, -2) * cw[0:1] + _shift_rows(xr, -1) * cw[1:2] + _shift_rows(xr, 0) * cw[2:3]
          + _shift_rows(xr, 1) * cw[3:4] + cb_ref[...])
    xcb = xc.astype(BF16)

    def direction(d):
        cols = slice(2 * d * HALF, (2 * d + 2) * HALF)
        gates = jax.nn.sigmoid(jnp.dot(xcb, wg_ref[:, cols], preferred_element_type=F32) + bg_ref[:, cols])
        log_a = (-LRU_C) * gates[:, 0:HALF] * _softplus(-lam_ref[d:d + 1, :])
        a = jnp.exp(log_a)
        return a, jnp.sqrt(1.0 - a * a) * (gates[:, HALF:2 * HALF] * xc)

    a, bval = direction(0)
    a_scr[...] = a
    b_scr[...] = bval

    @pl.when(pos == 0)
    def _():
        carry[...] = h0_ref[0:1, :]

    h = _lru_scan(a_scr, b_scr, hsf_ref, carry, False)
    fin_ref[...] = jnp.broadcast_to(h, (8, HALF))

    a, bval = direction(1)
    ar_ref[...] = a
    br_ref[...] = bval
    bg = col(1, TM)
    gb = col(4, TM)
    prod = cg * xin
    scw = scw_ref[...]
    yc = bg * (_shift_rows(prod, -1) * scw[0:1] + _shift_rows(prod, 0) * scw[1:2]
               + _shift_rows(prod, 1) * scw[2:3])
    yc_ref[...] = yc.astype(BF16)
    gate_ref[...] = jax.nn.gelu(gb).astype(BF16)


def _cd_in(x, mod, g, w_in, sc_w, conv_w, conv_b, w_gates, b_gates, lru_lam, h0t):
    const = lambda i: (0, 0)
    tok = lambda dt: jax.ShapeDtypeStruct((N_TOK, HALF), dt)
    row = pl.BlockSpec((TM, HALF), lambda i: (i, 0))
    return pl.pallas_call(
        _cd_in_kernel,
        grid=(NT,),
        in_specs=_halo_specs(D_MODEL) + [
            pl.BlockSpec((None, 6, D_MODEL), lambda i: (_mod_row(i), 0, 0)),
            pl.BlockSpec((1, D_MODEL), const),
            pl.BlockSpec((D_MODEL, 5 * HALF), const),
            pl.BlockSpec((3, HALF), const),
            pl.BlockSpec((4, HALF), const),
            pl.BlockSpec((1, HALF), const),
            pl.BlockSpec((HALF, 4 * HALF), const),
            pl.BlockSpec((1, 4 * HALF), const),
            pl.BlockSpec((2, HALF), const),
            pl.BlockSpec((None, 2, HALF), lambda i: (_mod_row(i), 0, 0)),
        ],
        out_specs=[row, row, row, row, row, pl.BlockSpec((8, HALF), lambda i: (i, 0))],
        out_shape=[tok(BF16), tok(BF16), tok(F32), tok(F32), tok(F32), jax.ShapeDtypeStruct((NT * 8, HALF), F32)],
        scratch_shapes=[
            pltpu.VMEM((ROWS, D_MODEL), BF16),
            pltpu.VMEM((TM, HALF), F32),
            pltpu.VMEM((TM, HALF), F32),
            pltpu.VMEM((1, HALF), F32),
        ],
        compiler_params=_params(),
        name="cd_in",
    )(x, x, x, mod, g, w_in, sc_w, conv_w, conv_b, w_gates, b_gates, lru_lam, h0t)


def _cd_out_kernel(ar_ref, br_ref, hsf_ref, gate_ref, yc_ref, x_ref, mod_ref, g_ref, wo_ref, h0_ref,
                   o_ref, fin_ref, hs_scr, carry):
    ti = NT - 1 - pl.program_id(0)
    pos, n = _seq_pos(ti)

    @pl.when(pos == n - 1)
    def _():
        carry[...] = h0_ref[1:2, :]

    h = _lru_scan(ar_ref, br_ref, hs_scr, carry, True)
    fin_ref[...] = jnp.broadcast_to(h, (8, HALF))
    out_c = jnp.dot(yc_ref[...], wo_ref[0:HALF, :], preferred_element_type=F32)
    yd = (hsf_ref[...] + hs_scr[...]) * gate_ref[...].astype(F32)
    out = out_c + jnp.dot(yd.astype(BF16), wo_ref[HALF:2 * HALF, :], preferred_element_type=F32)
    o_ref[...] = _post(x_ref[...], out, g_ref[...], mod_ref[2:3, :])


def _cd_out(a_r, b_r, hs_f, gate, yc, x, mod, g, w_out, h0t):
    const = lambda i: (0, 0)
    rev = lambda i: (NT - 1 - i, 0)
    row = pl.BlockSpec((TM, HALF), rev)
    return pl.pallas_call(
        _cd_out_kernel,
        grid=(NT,),
        in_specs=[
            row, row, row, row, row,
            pl.BlockSpec((TM, D_MODEL), rev),
            pl.BlockSpec((None, 6, D_MODEL), lambda i: (_mod_row(NT - 1 - i), 0, 0)),
            pl.BlockSpec((1, D_MODEL), const),
            pl.BlockSpec((D_MODEL, D_MODEL), const),
            pl.BlockSpec((None, 2, HALF), lambda i: (_mod_row(NT - 1 - i), 0, 0)),
        ],
        out_specs=[
            pl.BlockSpec((TM, D_MODEL), rev),
            pl.BlockSpec((8, HALF), rev),
        ],
        out_shape=[
            jax.ShapeDtypeStruct((N_TOK, D_MODEL), F32),
            jax.ShapeDtypeStruct((NT * 8, HALF), F32),
        ],
        scratch_shapes=[
            pltpu.VMEM((TM, HALF), F32),
            pltpu.VMEM((1, HALF), F32),
        ],
        compiler_params=_params(),
        name="cd_out",
    )(a_r, b_r, hs_f, gate, yc, x, mod, g, w_out, h0t)


def kernel(x_prompt, x_sample, cache_attn_k, cache_attn_v, state_s5, state_rglru, c, c_ctx, w_mod, b_mod, norm_g, w_in_ab, w_out_ab, s5_lam_re, s5_lam_im, s5_log_dt, s5_b, s5_c, s5_d, s5_w_glu, s5_b_glu, da_lam, da_g, w_in_cd, w_out_cd, sc_conv_w, lru_conv_w, lru_conv_b, lru_w_a, lru_b_a, lru_w_x, lru_b_x, lru_lam, ffn_w_up, ffn_conv_w, ffn_conv_b, ffn_w_down):
    assert DEPTH == 2
    xp = x_prompt.reshape(NT_P * TM, D_MODEL)
    xs = x_sample.reshape(NT_S * TM, D_MODEL)
    cv8 = jnp.zeros((8, D_MODEL), F32).at[0].set(c_ctx).at[1:1 + DEC_BATCH].set(c)
    mod = _modulation(cv8, w_mod, b_mod)
    cos_t, sin_t = _rope_tables()
    g = norm_g.reshape(DEPTH, 4, 1, D_MODEL)

    lam_init = 0.8 - 0.6 * math.exp(-0.3 * 0)
    u, qkv, k32, v32 = _ab_in(xp, xs, mod[0], g[0, 0], w_in_ab[0].astype(BF16), cos_t, sin_t)
    m1, m2, m3, coef = _s5_params(s5_lam_re[0], s5_lam_im[0], s5_log_dt[0], s5_b[0], s5_c[0])
    ys5_p, ys5_s, new_s5 = _s5(u, (m1, m2, m3), coef, state_s5[:, 0], s5_d[0])
    yb_p = _attention(qkv, 0, BATCH, 1, SEQ, None, da_lam[0], da_g[0], lam_init)
    ctx = (cache_attn_k[:, 0].reshape(DEC_BATCH, PAST_LEN, HALF).astype(BF16),
           cache_attn_v[:, 0].reshape(DEC_BATCH, PAST_LEN, HALF).astype(BF16))
    yb_s = _attention(qkv, NT_P, DEC_BATCH, TPS, DEC_SEQ, ctx, da_lam[0], da_g[0], lam_init)
    x = _ab_out(ys5_p, ys5_s, yb_p, yb_s, xp, xs, mod[0], g[0, 1], s5_w_glu[0].astype(BF16), s5_b_glu[0].reshape(1, HALF),
                w_out_ab[0].astype(BF16))
    assert NT_P == BATCH and TM == SEQ
    new_k = k32.reshape(BATCH, 1, DA_HEADS, 2, DA_DK, SEQ).transpose(0, 1, 5, 2, 3, 4)
    new_v = v32
    x = _ffn(x, mod[0], g[0, 2], g[0, 3], ffn_w_up, ffn_conv_w[0], ffn_conv_b[0].reshape(1, 2 * D_FF), ffn_w_down,
             0, False)

    eye = jnp.eye(LRU_BLOCKS, dtype=F32)
    dense = lambda w: jnp.einsum('kcd,kl->kcld', w, eye).reshape(LRU_WIDTH, LRU_WIDTH)
    w_gates = jnp.concatenate([dense(lru_w_a[0, 0]), dense(lru_w_x[0, 0]),
                               dense(lru_w_a[0, 1]), dense(lru_w_x[0, 1])], axis=1).astype(BF16)
    b_gates = jnp.concatenate([lru_b_a[0, 0], lru_b_x[0, 0], lru_b_a[0, 1], lru_b_x[0, 1]]).reshape(1, 4 * HALF)
    h0t = jnp.zeros((8, 2, HALF), F32).at[1:1 + DEC_BATCH].set(state_rglru[:, 0])
    yc, gate, hs_f, a_r, b_r, fin_f = _cd_in(x, mod[1], g[1, 0], w_in_cd[0].astype(BF16), sc_conv_w[0], lru_conv_w[0],
                                      lru_conv_b[0].reshape(1, HALF), w_gates, b_gates, lru_lam[0], h0t)
    x, fin_r = _cd_out(a_r, b_r, hs_f, gate, yc, x, mod[1], g[1, 1], w_out_cd[0].astype(BF16), h0t)
    tile_row0 = lambda f: f.reshape(NT, 8, HALF)[:NT_P, 0]
    new_lru = jnp.stack([tile_row0(fin_f), tile_row0(fin_r)], axis=1)[:, None]
    yp, ys = _ffn(x, mod[1], g[1, 2], g[1, 3], ffn_w_up, ffn_conv_w[1], ffn_conv_b[1].reshape(1, 2 * D_FF),
                  ffn_w_down, 1, True)
    return (yp.reshape(BATCH, SEQ, D_MODEL), ys.reshape(DEC_BATCH, DEC_SEQ, D_MODEL),
            new_k, new_v, new_s5[:, None], new_lru)
```

```python
import functools
import math

import jax
import jax.numpy as jnp
from jax import lax
from jax.experimental import pallas as pl
from jax.experimental.pallas import tpu as pltpu

D_MODEL = 1024
BATCH = 32
SEQ = 256
DEPTH = 2
DEC_BATCH = 4
DEC_SEQ = 2048
PAST_LEN = 512
GRID_W = 64
HALF = D_MODEL // 2
S5_P = 16
S5_G = HALF // S5_P
S5_N = 64
DA_DK = 64
DA_DV = 2 * DA_DK
DA_HEADS = HALF // DA_DV
ROPE_THETA = 10000.0
ROPE_F = DA_DK // 4
LRU_WIDTH = HALF
LRU_BLOCKS = 8
LRU_BS = LRU_WIDTH // LRU_BLOCKS
LRU_C = 8.0
D_FF = 2816
EPS = 1e-6

F32 = jnp.float32
BF16 = jnp.bfloat16
HIGHEST = lax.Precision.HIGHEST

TM = 256
NT_P = BATCH * SEQ // TM
TPS = DEC_SEQ // TM
NT_S = DEC_BATCH * TPS
NT = NT_P + NT_S
N_TOK = NT * TM
TW = 2 * TM
NW_P = NT_P * TM // TW
WPS = DEC_SEQ // TW
NW = N_TOK // TW
HALO = 8
ROWS = TM + 2 * HALO
S5_T = 8
S5_GB = 128 // S5_P
S5_NB = S5_G // S5_GB
S5_K = S5_T * 128
S5_ST = 4 * S5_N
S5_GS = 4
FF_CHUNK = 256
FF_NCH = D_FF // FF_CHUNK
Q_SCALE = math.log2(math.e) / math.sqrt(DA_DK)
GELU_K = math.sqrt(2.0 / math.pi)
GELU_C = 0.044715
VMEM_LIMIT = 56 * 1024 * 1024


def _mod_row(i, ntp=NT_P, tps=TPS):
    return jnp.where(i < ntp, 0, 1 + (i - ntp) // tps)


def _seq_pos(i):
    return jnp.where(i < NT_P, 0, (i - NT_P) % TPS), jnp.where(i < NT_P, 1, TPS)


def _split_specs(width, tm=TM, ntp=NT_P):
    return [
        pl.BlockSpec((tm, width), lambda i: (jnp.minimum(i, ntp - 1), 0)),
        pl.BlockSpec((tm, width), lambda i: (jnp.maximum(i - ntp, 0), 0)),
    ]


def _pick(p_ref, s_ref, ntp=NT_P):
    return jnp.where(pl.program_id(0) < ntp, p_ref[...], s_ref[...])


def _params(sem=("arbitrary",)):
    return pltpu.CompilerParams(dimension_semantics=sem, vmem_limit_bytes=VMEM_LIMIT)


def _rms(x, g):
    ms = jnp.mean(x * x, axis=-1, keepdims=True)
    return x * lax.rsqrt(ms + EPS) * g


def _pre(x, g, shift, scale):
    return _rms(x, g) * (1.0 + scale) + shift


def _mod_kernel(cv_ref, w_ref, b_ref, o_ref):
    cv = cv_ref[...]
    s = cv * jax.nn.sigmoid(cv)
    w = w_ref[...]
    s_hi = s.astype(BF16)
    s_lo = (s - s_hi.astype(F32)).astype(BF16)
    w_hi = w.astype(BF16)
    w_lo = (w - w_hi.astype(F32)).astype(BF16)
    dot = lambda a, b: jnp.dot(a, b, preferred_element_type=F32)
    o_ref[...] = dot(s_hi, w_hi) + (dot(s_hi, w_lo) + dot(s_lo, w_hi)) + b_ref[...]


def _modulation(cv8, w_mod, b_mod):
    nb = 1536
    out = pl.pallas_call(
        _mod_kernel,
        grid=(DEPTH, 6 * D_MODEL // nb),
        in_specs=[
            pl.BlockSpec((8, D_MODEL), lambda l, j: (0, 0)),
            pl.BlockSpec((None, D_MODEL, nb), lambda l, j: (l, 0, j)),
            pl.BlockSpec((None, 1, nb), lambda l, j: (l, 0, j)),
        ],
        out_specs=pl.BlockSpec((None, 8, nb), lambda l, j: (l, 0, j)),
        out_shape=jax.ShapeDtypeStruct((DEPTH, 8, 6 * D_MODEL), F32),
        compiler_params=_params(("arbitrary", "arbitrary")),
        name="modulation",
    )(cv8, w_mod, b_mod.reshape(DEPTH, 1, 6 * D_MODEL))
    return out.reshape(DEPTH, 8, 6, D_MODEL)


def _rope_partner(x):
    lane = lax.broadcasted_iota(jnp.int32, (1, HALF), 1)
    first = (lane % (2 * ROPE_F)) < ROPE_F
    return jnp.where(first, pltpu.roll(x, HALF - ROPE_F, axis=1), pltpu.roll(x, ROPE_F, axis=1))


def _ab_in_kernel(xp_ref, xs_ref, mod_ref, g_ref, w_ref, cos_ref, sin_ref, u_ref, qkv_ref, k_ref, v_ref):
    i = pl.program_id(0)
    h = _pre(_pick(xp_ref, xs_ref, NW_P), g_ref[...], mod_ref[0:1, :], mod_ref[1:2, :])
    proj = jnp.dot(h.astype(BF16), w_ref[...], preferred_element_type=F32)
    u_ref[...] = proj[:, 0:HALF]
    q = proj[:, HALF:2 * HALF]
    k = proj[:, 2 * HALF:3 * HALF]
    v = proj[:, 3 * HALF:4 * HALF]
    cos = jnp.concatenate([cos_ref[...]] * DA_HEADS, axis=1)
    sin = jnp.concatenate([sin_ref[...]] * DA_HEADS, axis=1)
    qr = q * cos + _rope_partner(q) * sin
    kr = k * cos + _rope_partner(k) * sin
    qkv_ref[:, 0:HALF] = (qr * Q_SCALE).astype(BF16)
    qkv_ref[:, HALF:2 * HALF] = kr.astype(BF16)
    qkv_ref[:, 2 * HALF:3 * HALF] = v.astype(BF16)

    @pl.when(i < NW_P)
    def _():
        kt = k.T
        for s in range(TW // SEQ):
            k_ref[s] = kt[:, s * SEQ:(s + 1) * SEQ]
            v_ref[s] = v[s * SEQ:(s + 1) * SEQ].reshape(SEQ, DA_HEADS, DA_DV)


def _ab_in(xp, xs, mod, g, w, cos_t, sin_t):
    def rope_tile(i):
        return jnp.where(i < NW_P, 0, 1 + (i - NW_P) % WPS)

    per_tile = TW // SEQ
    return pl.pallas_call(
        _ab_in_kernel,
        grid=(NW,),
        in_specs=_split_specs(D_MODEL, TW, NW_P) + [
            pl.BlockSpec((None, 6, D_MODEL), lambda i: (_mod_row(i, NW_P, WPS), 0, 0)),
            pl.BlockSpec((1, D_MODEL), lambda i: (0, 0)),
            pl.BlockSpec((D_MODEL, 4 * HALF), lambda i: (0, 0)),
            pl.BlockSpec((TW, DA_DV), lambda i: (rope_tile(i), 0)),
            pl.BlockSpec((TW, DA_DV), lambda i: (rope_tile(i), 0)),
        ],
        out_specs=[
            pl.BlockSpec((TW, HALF), lambda i: (i, 0)),
            pl.BlockSpec((TW, 3 * HALF), lambda i: (i, 0)),
            pl.BlockSpec((per_tile, HALF, SEQ), lambda i: (jnp.minimum(i, NW_P - 1), 0, 0)),
            pl.BlockSpec((per_tile, None, SEQ, DA_HEADS, DA_DV), lambda i: (jnp.minimum(i, NW_P - 1), 0, 0, 0, 0)),
        ],
        out_shape=[
            jax.ShapeDtypeStruct((N_TOK, HALF), F32),
            jax.ShapeDtypeStruct((N_TOK, 3 * HALF), BF16),
            jax.ShapeDtypeStruct((BATCH, HALF, SEQ), F32),
            jax.ShapeDtypeStruct((BATCH, 1, SEQ, DA_HEADS, DA_DV), F32),
        ],
        compiler_params=_params(),
        name="ab_in",
    )(xp, xs, mod, g, w, cos_t, sin_t)


def _rope_tables():
    pos = jnp.arange(DEC_SEQ)
    t_row = (pos // GRID_W).astype(F32)
    t_col = (pos % GRID_W).astype(F32)
    inv = ROPE_THETA ** (-jnp.arange(ROPE_F, dtype=F32) / ROPE_F)
    ang = jnp.stack([t_row[:, None] * inv, t_col[:, None] * inv], axis=1)
    cos, sin = jnp.cos(ang), jnp.sin(ang)
    cos64 = jnp.stack([cos, cos], axis=2).reshape(DEC_SEQ, DA_DK)
    sin64 = jnp.stack([-sin, sin], axis=2).reshape(DEC_SEQ, DA_DK)
    cos_t = jnp.concatenate([jnp.ones((TW, DA_DV), F32), jnp.tile(cos64, (1, 2))], axis=0)
    sin_t = jnp.concatenate([jnp.zeros((TW, DA_DV), F32), jnp.tile(sin64, (1, 2))], axis=0)
    return cos_t, sin_t


def _s5_param_kernel(lre_ref, lim_ref, ldt_ref, bt_ref, c_ref, m1_ref, m2_ref, m3_ref, at_ref):
    nt = (((1,), (1,)), ((), ()))
    e_tab, f_tab, k_tab = [], [], []
    r_i = lax.broadcasted_iota(jnp.int32, (128, 128), 0) // S5_P
    c_i = lax.broadcasted_iota(jnp.int32, (128, 128), 1) // S5_P
    same_group = r_i == c_i
    for d in range(2):
        lre = lre_ref[d]
        lim = lim_ref[d]
        dt = jnp.exp(ldt_ref[d])
        mag = jnp.exp(lre * dt)
        ang = lim * dt
        ar = mag * jnp.cos(ang)
        ai = mag * jnp.sin(ang)
        den = lre * lre + lim * lim
        nr = ar - 1.0
        fr = (nr * lre + ai * lim) / den
        fi = (ai * lre - nr * lim) / den
        btr = bt_ref[d, 0]
        bti = bt_ref[d, 1]
        bfr = fr * btr - fi * bti
        bfi = fr * bti + fi * btr
        cr = c_ref[d, 0]
        ci = c_ref[d, 1]
        cst = jnp.concatenate([cr, -ci], axis=1)
        pr = jnp.ones_like(ar)
        pi = jnp.zeros_like(ar)
        e_d, f_d, k_d = [], [], []
        for tau in range(S5_T + 1):
            f_d.append((cr * pr - ci * pi, cr * pi + ci * pr))
            if tau < S5_T:
                er, ei = pr * bfr - pi * bfi, pr * bfi + pi * bfr
                e_d.append((er, ei))
                full = lax.dot_general(jnp.concatenate([er, ei], axis=1), cst, nt,
                                       precision=HIGHEST, preferred_element_type=F32)
                k_d.append(jnp.where(same_group, full, 0.0))
            else:
                at_ref[d, 0] = pr
                at_ref[d, 1] = pi
            pr, pi = pr * ar - pi * ai, pr * ai + pi * ar
        e_tab.append(e_d)
        f_tab.append(f_d)
        k_tab.append(k_d)

    for s in range(S5_T):
        for t in range(S5_T):
            blk = k_tab[0][t - s] if t >= s else k_tab[1][s - t]
            if t == s:
                blk = blk + k_tab[1][0]
            m1_ref[s * 128:(s + 1) * 128, t * 128:(t + 1) * 128] = blk.astype(BF16)
    row_group = lax.broadcasted_iota(jnp.int32, (128, S5_ST), 0) // S5_P
    for s in range(S5_T):
        ef, er_ = e_tab[0][S5_T - 1 - s], e_tab[1][s]
        v = jnp.concatenate([ef[0], er_[0], ef[1], er_[1]], axis=1)
        ff, fr_ = f_tab[0][s + 1], f_tab[1][S5_T - s]
        w = jnp.concatenate([ff[0], fr_[0], -ff[1], -fr_[1]], axis=1)
        for gi in range(S5_GB):
            sel = row_group == gi
            m2_ref[s * 128:(s + 1) * 128, gi * S5_ST:(gi + 1) * S5_ST] = jnp.where(sel, v, 0.0).astype(BF16)
            m3_ref[s * 128:(s + 1) * 128, gi * S5_ST:(gi + 1) * S5_ST] = jnp.where(sel, w, 0.0).astype(BF16)


def _s5_params(lam_re, lam_im, log_dt, b_w, c_w):
    rows = S5_G * S5_P
    per_row = lambda a: jnp.repeat(a, S5_P, axis=1)
    lre = per_row(lam_re)
    lim = per_row(lam_im)
    ldt = per_row(jnp.broadcast_to(log_dt[:, :, None], (2, S5_G, S5_N)))
    bt = b_w.transpose(0, 1, 2, 4, 3).reshape(2, 2, rows, S5_N)
    cc = c_w.reshape(2, 2, rows, S5_N)
    vec = pl.BlockSpec((2, 128, S5_N), lambda j: (0, j, 0))
    mat = pl.BlockSpec((2, 2, 128, S5_N), lambda j: (0, 0, j, 0))
    wide = lambda n: pl.BlockSpec((None, S5_K, n), lambda j: (j, 0, 0))
    m1, m2, m3, at = pl.pallas_call(
        _s5_param_kernel,
        grid=(S5_NB,),
        in_specs=[vec, vec, vec, mat, mat],
        out_specs=[wide(S5_K), wide(S5_GB * S5_ST), wide(S5_GB * S5_ST),
                   pl.BlockSpec((None, 2, 2, 128, S5_N), lambda j: (j, 0, 0, 0, 0))],
        out_shape=[
            jax.ShapeDtypeStruct((S5_NB, S5_K, S5_K), BF16),
            jax.ShapeDtypeStruct((S5_NB, S5_K, S5_GB * S5_ST), BF16),
            jax.ShapeDtypeStruct((S5_NB, S5_K, S5_GB * S5_ST), BF16),
            jax.ShapeDtypeStruct((S5_NB, 2, 2, 128, S5_N), F32),
        ],
        compiler_params=_params(),
        name="s5_params",
    )(lre, lim, ldt, bt, cc)
    coef = at[:, :, :, ::S5_P, :].transpose(0, 3, 2, 1, 4).reshape(S5_NB, S5_GB, 2, 2 * S5_N)
    return m1, m2, m3, coef


def _s5_kernel(u_ref, m1_ref, m2_ref, m3_ref, coef_ref, d_ref, *rest, stride, nc, two_pass):
    if two_pass:
        h0_ref, y_ref, lhs, yacc, s_scr, hp_scr = rest
    else:
        y_ref, fin_ref, lhs, yacc, s_scr, hp_scr = rest
    ln = 2 * S5_N
    nt = (((1,), (1,)), ((), ()))

    def seq_rows(k, c):
        return pl.ds(pl.multiple_of(k * stride + c * S5_T, S5_T), S5_T)

    def gather(c2, carry):
        halves = []
        for c in (2 * c2, 2 * c2 + 1):
            tiles = [u_ref[seq_rows(k, c), :] for k in range(8)]
            halves.append(jnp.stack(tiles, axis=0).reshape(8, S5_K))
        lhs[pl.ds(pl.multiple_of(c2 * 16, 16), 16), :] = jnp.concatenate(halves, axis=0).astype(BF16)
        return carry

    lax.fori_loop(0, nc // 2, gather, 0)
    lb = lhs[...]
    yacc[...] = jnp.dot(lb, m1_ref[...], preferred_element_type=F32)

    lane = lax.broadcasted_iota(jnp.int32, (8, ln), 1)
    row = lax.broadcasted_iota(jnp.int32, (8, ln), 0)
    fwd = lane < S5_N
    odd = (row % 2) == 1
    wide = S5_GS * S5_ST
    for gq in range(S5_GB // S5_GS):
        groups = range(gq * S5_GS, (gq + 1) * S5_GS)
        cols = slice(gq * wide, (gq + 1) * wide)
        s_scr[...] = jnp.dot(lb, m2_ref[:, cols], preferred_element_type=F32)
        are = [jnp.broadcast_to(coef_ref[g, 0:1, :], (8, ln)) for g in groups]
        aim = [jnp.broadcast_to(coef_ref[g, 1:2, :], (8, ln)) for g in groups]

        def scan(init, store):
            def body(c, carry):
                kf = pl.multiple_of(c * 8, 8)
                kr = pl.multiple_of((nc - 1 - c) * 8, 8)
                out = []
                for j in range(S5_GS):
                    p, q = carry[2 * j], carry[2 * j + 1]
                    o = j * S5_ST
                    if store:
                        hp_scr[pl.ds(kf, 8), o:o + S5_N] = p[:, 0:S5_N]
                        hp_scr[pl.ds(kr, 8), o + S5_N:o + ln] = p[:, S5_N:ln]
                        hp_scr[pl.ds(kf, 8), o + ln:o + ln + S5_N] = q[:, 0:S5_N]
                        hp_scr[pl.ds(kr, 8), o + ln + S5_N:o + 2 * ln] = q[:, S5_N:ln]
                    s_re = jnp.where(fwd, s_scr[pl.ds(kf, 8), o:o + ln], s_scr[pl.ds(kr, 8), o:o + ln])
                    s_im = jnp.where(fwd, s_scr[pl.ds(kf, 8), o + ln:o + 2 * ln],
                                     s_scr[pl.ds(kr, 8), o + ln:o + 2 * ln])
                    out += [are[j] * p - aim[j] * q + s_re, are[j] * q + aim[j] * p + s_im]
                return tuple(out)
            return lax.fori_loop(0, nc, body, tuple(init))

        if two_pass:
            h0 = [h0_ref[g, r] for g in groups for r in range(2)]
            fin = scan(h0, False)
            seed = [h + jnp.where(fwd & odd, pltpu.roll(f, 1, axis=0), 0.0)
                    + jnp.where(jnp.logical_not(fwd | odd), pltpu.roll(f, 7, axis=0), 0.0)
                    for h, f in zip(h0, fin)]
            scan(seed, True)
        else:
            fin = scan([jnp.zeros((8, ln), F32)] * (2 * S5_GS), True)
            for j, g in enumerate(groups):
                fin_ref[g, 0] = fin[2 * j]
                fin_ref[g, 1] = fin[2 * j + 1]
        yacc[...] += lax.dot_general(hp_scr[...].astype(BF16), m3_ref[:, cols], nt, preferred_element_type=F32)

    dvec = d_ref[...]

    def scatter(c, carry):
        blk = yacc[pl.ds(pl.multiple_of(c * 8, 8), 8), :].reshape(8, S5_T, 128)
        for k in range(8):
            rows = seq_rows(k, c)
            y_ref[rows, :] = blk[k] + u_ref[rows, :] * dvec
        return carry

    lax.fori_loop(0, nc, scatter, 0)


def _s5_call(u, row_block, mats, coef, dvec, h0, stride, nc, parts):
    m1, m2, m3 = mats
    rows = 8 * stride
    two_pass = h0 is not None
    single = dict(pipeline_mode=pl.Buffered(1)) if parts == 1 else {}
    mat = lambda n: pl.BlockSpec((None, S5_K, n), lambda j, i: (j, 0, 0), **single)
    tok = pl.BlockSpec((rows, 128), lambda j, i: (row_block * parts + i, j))
    in_specs = [
        tok, mat(S5_K), mat(S5_GB * S5_ST), mat(S5_GB * S5_ST),
        pl.BlockSpec((None, S5_GB, 2, 2 * S5_N), lambda j, i: (j, 0, 0, 0)),
        pl.BlockSpec((None, 1, 128), lambda j, i: (j, 0, 0)),
    ]
    args = [u, m1, m2, m3, coef, dvec]
    out_specs = [pl.BlockSpec((rows, 128), lambda j, i: (i, j))]
    out_shape = [jax.ShapeDtypeStruct((parts * rows, HALF), F32)]
    if two_pass:
        in_specs.append(pl.BlockSpec((None, S5_GB, 2, 8, 2 * S5_N), lambda j, i: (j, 0, 0, 0, 0)))
        args.append(h0)
    else:
        out_specs.append(pl.BlockSpec((None, S5_GB, 2, 8, 2 * S5_N), lambda j, i: (j, 0, 0, i, 0)))
        out_shape.append(jax.ShapeDtypeStruct((S5_NB, S5_GB, 2, parts * 8, 2 * S5_N), F32))
    return pl.pallas_call(
        functools.partial(_s5_kernel, stride=stride, nc=nc, two_pass=two_pass),
        grid=(S5_NB, parts),
        in_specs=in_specs,
        out_specs=out_specs,
        out_shape=out_shape,
        scratch_shapes=[
            pltpu.VMEM((nc * 8, S5_K), BF16),
            pltpu.VMEM((nc * 8, S5_K), F32),
            pltpu.VMEM((nc * 8, S5_GS * S5_ST), F32),
            pltpu.VMEM((nc * 8, S5_GS * S5_ST), F32),
        ],
        compiler_params=_params(("arbitrary", "arbitrary")),
        name="s5_%d" % stride,
    )(*args)


def _s5(u, mats, coef, h0, s5_d):
    dvec = s5_d.reshape(S5_NB, 1, 128)
    yp, fin = _s5_call(u, 0, mats, coef, dvec, None, SEQ, SEQ // S5_T, BATCH // 8)
    hh = h0.transpose(3, 2, 0, 1, 4)
    z = jnp.zeros_like(hh[:, :, :, 0])
    first = jnp.concatenate([hh[:, :, :, 0], z], axis=-1)
    second = jnp.concatenate([z, hh[:, :, :, 1]], axis=-1)
    h0g = jnp.stack([first, second], axis=3).reshape(S5_NB, S5_GB, 2, 2 * DEC_BATCH, 2 * S5_N)
    half = DEC_SEQ // 2
    (ys,) = _s5_call(u, 1, mats, coef, dvec, h0g, half, half // S5_T, 1)
    new_s5 = fin.reshape(S5_G, 2, BATCH, 2, S5_N).transpose(2, 3, 1, 0, 4)
    return yp, ys, new_s5


def _attn_kernel(*refs, lam_init, has_ctx, lk):
    if has_ctx:
        q_ref, k_ref, v_ref, kc_ref, vc_ref, dl_ref, g_ref, o_ref, s_a, s_b, vx, vcx = refs
    else:
        q_ref, k_ref, v_ref, dl_ref, g_ref, o_ref, s_a, s_b, vx = refs
    lt = lk + (PAST_LEN if has_ctx else 0)
    dl = dl_ref[...]
    lam = (jnp.exp(jnp.sum(dl[0:1] * dl[1:2], keepdims=True))
           - jnp.exp(jnp.sum(dl[2:3] * dl[3:4], keepdims=True)) + lam_init)
    lane = lax.broadcasted_iota(jnp.int32, (1, DA_DV), 1)
    first = lane < DA_DK
    g = g_ref[...]
    nt = (((1,), (1,)), ((), ()))

    @pl.when(pl.program_id(1) == 0)
    def _():
        onehot = (lane == 0).astype(BF16)
        for h in range(DA_HEADS):
            cols = slice(h * DA_DV, (h + 1) * DA_DV)
            vx[:, 2 * h * DA_DV:(2 * h + 1) * DA_DV] = v_ref[:, cols]
            vx[:, (2 * h + 1) * DA_DV:(2 * h + 2) * DA_DV] = jnp.broadcast_to(onehot, (lk, DA_DV))
            if has_ctx:
                vcx[:, 2 * h * DA_DV:(2 * h + 1) * DA_DV] = vc_ref[:, cols]
                vcx[:, (2 * h + 1) * DA_DV:(2 * h + 2) * DA_DV] = jnp.broadcast_to(onehot, (PAST_LEN, DA_DV))

    def scores(h, dst):
        cols = slice(h * DA_DV, (h + 1) * DA_DV)
        qh = q_ref[:, cols]
        zero = jnp.zeros_like(qh)
        for m in range(2):
            qm = jnp.where(first if m == 0 else jnp.logical_not(first), qh, zero)
            dst[m, :, 0:lk] = lax.dot_general(qm, k_ref[:, cols], nt, preferred_element_type=F32)
            if has_ctx:
                dst[m, :, lk:lt] = lax.dot_general(qm, kc_ref[:, cols], nt, preferred_element_type=F32)

    def head(h, src):
        xcols = slice(2 * h * DA_DV, (2 * h + 2) * DA_DV)
        outs = []
        for m in range(2):
            s = src[m]
            e = jnp.exp2(s - jnp.max(s, axis=-1, keepdims=True)).astype(BF16)
            ox = jnp.dot(e[:, 0:lk], vx[:, xcols], preferred_element_type=F32)
            if has_ctx:
                ox = ox + jnp.dot(e[:, lk:lt], vcx[:, xcols], preferred_element_type=F32)
            outs.append(ox[:, 0:DA_DV] * (1.0 / ox[:, DA_DV:DA_DV + 1]))
        o = outs[0] - lam * outs[1]
        o_ref[:, h * DA_DV:(h + 1) * DA_DV] = (_rms(o, g) * (1.0 - lam_init)).astype(BF16)

    scores(0, s_a)
    for h in range(DA_HEADS):
        cur, nxt = (s_a, s_b) if h % 2 == 0 else (s_b, s_a)
        if h + 1 < DA_HEADS:
            scores(h + 1, nxt)
        head(h, cur)


def _attention(qkv, q_tile0, nb, nq, lk, ctx, da_lam, da_g, lam_init):
    kb0 = q_tile0 * TM // lk
    in_specs = [
        pl.BlockSpec((TM, HALF), lambda b, j: (q_tile0 + b * nq + j, 0)),
        pl.BlockSpec((lk, HALF), lambda b, j: (kb0 + b, 1)),
        pl.BlockSpec((lk, HALF), lambda b, j: (kb0 + b, 2)),
    ]
    args = [qkv, qkv, qkv]
    if ctx is not None:
        in_specs += [pl.BlockSpec((None, PAST_LEN, HALF), lambda b, j: (b, 0, 0))] * 2
        args += list(ctx)
    in_specs += [
        pl.BlockSpec((4, DA_DK), lambda b, j: (0, 0)),
        pl.BlockSpec((1, DA_DV), lambda b, j: (0, 0)),
    ]
    lt = lk + (PAST_LEN if ctx is not None else 0)
    scratch = [pltpu.VMEM((2, TM, lt), F32), pltpu.VMEM((2, TM, lt), F32), pltpu.VMEM((lk, 2 * HALF), BF16)]
    if ctx is not None:
        scratch.append(pltpu.VMEM((PAST_LEN, 2 * HALF), BF16))
    return pl.pallas_call(
        functools.partial(_attn_kernel, lam_init=lam_init, has_ctx=ctx is not None, lk=lk),
        grid=(nb, nq),
        in_specs=in_specs,
        out_specs=pl.BlockSpec((TM, HALF), lambda b, j: (b * nq + j, 0)),
        out_shape=jax.ShapeDtypeStruct((nb * nq * TM, HALF), BF16),
        scratch_shapes=scratch,
        compiler_params=_params(("arbitrary", "arbitrary")),
        name="diff_attn_%d" % lk,
    )(*args, da_lam, da_g.reshape(1, DA_DV))


def _post(x, y, g, gate):
    return x + gate * _rms(y, g)


def _ab_out_kernel(ysp_ref, yss_ref, ybp_ref, ybs_ref, xp_ref, xs_ref, mod_ref, g_ref, wg_ref, bg_ref, wo_ref,
                   o_ref):
    ys = jax.nn.gelu(_pick(ysp_ref, yss_ref, NW_P))
    glu = jnp.dot(ys.astype(BF16), wg_ref[...], preferred_element_type=F32) + bg_ref[...]
    ya = ys * jax.nn.sigmoid(glu)
    out = (jnp.dot(ya.astype(BF16), wo_ref[0:HALF, :], preferred_element_type=F32)
           + jnp.dot(_pick(ybp_ref, ybs_ref, NW_P), wo_ref[HALF:2 * HALF, :], preferred_element_type=F32))
    o_ref[...] = _post(_pick(xp_ref, xs_ref, NW_P), out, g_ref[...], mod_ref[2:3, :])


def _ab_out(ys5_p, ys5_s, yb_p, yb_s, xp, xs, mod, g, w_glu, b_glu, w_out):
    return pl.pallas_call(
        _ab_out_kernel,
        grid=(NW,),
        in_specs=_split_specs(HALF, TW, NW_P) + _split_specs(HALF, TW, NW_P) + _split_specs(D_MODEL, TW, NW_P) + [
            pl.BlockSpec((None, 6, D_MODEL), lambda i: (_mod_row(i, NW_P, WPS), 0, 0)),
            pl.BlockSpec((1, D_MODEL), lambda i: (0, 0)),
            pl.BlockSpec((HALF, HALF), lambda i: (0, 0)),
            pl.BlockSpec((1, HALF), lambda i: (0, 0)),
            pl.BlockSpec((D_MODEL, D_MODEL), lambda i: (0, 0)),
        ],
        out_specs=pl.BlockSpec((TW, D_MODEL), lambda i: (i, 0)),
        out_shape=jax.ShapeDtypeStruct((N_TOK, D_MODEL), F32),
        compiler_params=_params(),
        name="ab_out",
    )(ys5_p, ys5_s, yb_p, yb_s, xp, xs, mod, g, w_glu, b_glu, w_out)


def _halo_specs(width):
    blocks = TM // HALO
    last = N_TOK // HALO - 1
    return [
        pl.BlockSpec((TM, width), lambda i: (i, 0)),
        pl.BlockSpec((HALO, width), lambda i: (jnp.maximum(i * blocks - 1, 0), 0)),
        pl.BlockSpec((HALO, width), lambda i: (jnp.minimum((i + 1) * blocks, last), 0)),
    ]


def _fill_hbuf(hbuf, x_ref, xp_ref, xn_ref, g, shift, scale, i):
    pos, n = _seq_pos(i)
    hp = jnp.where(pos > 0, _pre(xp_ref[...], g, shift, scale), 0.0)
    hn = jnp.where(pos < n - 1, _pre(xn_ref[...], g, shift, scale), 0.0)
    hbuf[0:TM, :] = _pre(x_ref[...], g, shift, scale).astype(BF16)
    hbuf[TM:ROWS, :] = jnp.concatenate([hn, hp], axis=0).astype(BF16)


def _shift_rows(x, s):
    if s == 0:
        return x[0:TM]
    return pltpu.roll(x, (-s) % ROWS, axis=0)[0:TM]


def _ffn_kernel(x_ref, xp_ref, xn_ref, mod_ref, g2_ref, g3_ref, wg_ref, wv_ref, wd_ref, cw_ref, cb_ref, *rest, split):
    hbuf, acc, u_a, u_b, wu_s, wd_s = rest[-6:]
    step = pl.program_id(0)

    @pl.when(step < FF_NCH)
    def _():
        wu_s[step] = wg_ref[...].astype(BF16)
        wu_s[FF_NCH + step] = wv_ref[...].astype(BF16)
        wd_s[step] = wd_ref[...].astype(BF16)

    @pl.when(step >= FF_NCH)
    def _():
        i = step - FF_NCH
        _fill_hbuf(hbuf, x_ref, xp_ref, xn_ref, g2_ref[...], mod_ref[3:4, :], mod_ref[4:5, :], i)

        def up(j, dst):
            for half in range(2):
                dst[half] = jnp.dot(hbuf[...], wu_s[half * FF_NCH + j], preferred_element_type=F32)

        def activation(j, src):
            parts = []
            for half in range(2):
                c0 = half * D_FF + j * FF_CHUNK
                sc = 1.0 if half == 0 else 0.5
                cw = cw_ref[:, c0:c0 + FF_CHUNK] * sc
                u = src[half]
                parts.append(_shift_rows(u, -1) * cw[0:1] + _shift_rows(u, 0) * cw[1:2]
                             + _shift_rows(u, 1) * cw[2:3] + cb_ref[:, c0:c0 + FF_CHUNK] * sc)
            gt = parts[0]
            z = gt * (gt * gt * (GELU_C * GELU_K) + GELU_K)
            return ((gt + gt * jnp.tanh(z)) * parts[1]).astype(BF16)

        up(0, u_a)
        for j in range(FF_NCH):
            cur, nxt = (u_a, u_b) if j % 2 == 0 else (u_b, u_a)
            if j + 1 < FF_NCH:
                up(j + 1, nxt)
            contrib = jnp.dot(activation(j, cur), wd_s[j], preferred_element_type=F32)
            if j == 0:
                acc[...] = contrib
            else:
                acc[...] += contrib
        res = _post(x_ref[...], acc[...], g3_ref[...], mod_ref[5:6, :])
        if split:
            @pl.when(i < NT_P)
            def _():
                rest[0][...] = res

            @pl.when(i >= NT_P)
            def _():
                rest[1][...] = res
        else:
            rest[0][...] = res


def _ffn(x, mod, g2, g3, w_up, cw, cb, w_down, layer, split):
    const = lambda s: (0, 0)
    tile = lambda s: jnp.maximum(s - FF_NCH, 0)
    chunk = lambda s: jnp.minimum(s, FF_NCH - 1)
    shifted = lambda spec: pl.BlockSpec(spec.block_shape, lambda s, f=spec.index_map: f(tile(s)))
    if split:
        out_specs = [shifted(sp) for sp in _split_specs(D_MODEL)]
        out_shape = [jax.ShapeDtypeStruct((NT_P * TM, D_MODEL), F32), jax.ShapeDtypeStruct((NT_S * TM, D_MODEL), F32)]
    else:
        out_specs = pl.BlockSpec((TM, D_MODEL), lambda s: (tile(s), 0))
        out_shape = jax.ShapeDtypeStruct((N_TOK, D_MODEL), F32)
    return pl.pallas_call(
        functools.partial(_ffn_kernel, split=split),
        grid=(FF_NCH + NT,),
        in_specs=[shifted(sp) for sp in _halo_specs(D_MODEL)] + [
            pl.BlockSpec((None, 6, D_MODEL), lambda s: (_mod_row(tile(s)), 0, 0)),
            pl.BlockSpec((1, D_MODEL), const),
            pl.BlockSpec((1, D_MODEL), const),
            pl.BlockSpec((None, D_MODEL, FF_CHUNK), lambda s: (layer, 0, chunk(s))),
            pl.BlockSpec((None, D_MODEL, FF_CHUNK), lambda s: (layer, 0, FF_NCH + chunk(s))),
            pl.BlockSpec((None, FF_CHUNK, D_MODEL), lambda s: (layer, chunk(s), 0)),
            pl.BlockSpec((3, 2 * D_FF), const),
            pl.BlockSpec((1, 2 * D_FF), const),
        ],
        out_specs=out_specs,
        out_shape=out_shape,
        scratch_shapes=[
            pltpu.VMEM((ROWS, D_MODEL), BF16),
            pltpu.VMEM((TM, D_MODEL), F32),
            pltpu.VMEM((2, ROWS, FF_CHUNK), F32),
            pltpu.VMEM((2, ROWS, FF_CHUNK), F32),
            pltpu.VMEM((2 * FF_NCH, D_MODEL, FF_CHUNK), BF16),
            pltpu.VMEM((FF_NCH, FF_CHUNK, D_MODEL), BF16),
        ],
        compiler_params=_params(),
        name="conv_ffn",
    )(x, x, x, mod, g2, g3, w_up, w_up, w_down, cw, cb)


def _softplus(z):
    return jnp.maximum(z, 0.0) + jnp.log(1.0 + jnp.exp(-jnp.abs(z)))


def _lru_scan(a_ref, b_ref, hs_ref, carry, reverse):
    h = carry[...]
    for r in (range(TM - 1, -1, -1) if reverse else range(TM)):
        h = a_ref[r:r + 1, :] * h + b_ref[r:r + 1, :]
        hs_ref[r:r + 1, :] = h
    carry[...] = h
    return h


def _cd_in_kernel(x_ref, xp_ref, xn_ref, mod_ref, g_ref, w_ref, scw_ref, cw_ref, cb_ref, wg_ref, bg_ref,
                  lam_ref, h0_ref, yc_ref, gate_ref, hsf_ref, ar_ref, br_ref, fin_ref,
                  hbuf, a_scr, b_scr, carry):
    i = pl.program_id(0)
    pos, _ = _seq_pos(i)
    _fill_hbuf(hbuf, x_ref, xp_ref, xn_ref, g_ref[...], mod_ref[0:1, :], mod_ref[1:2, :], i)

    def col(k, rows):
        return jnp.dot(hbuf[0:rows, :], w_ref[:, k * HALF:(k + 1) * HALF], preferred_element_type=F32)

    xr = col(3, ROWS)
    xin = col(0, ROWS)
    cg = col(2, ROWS)
    cw = cw_ref[...]
    xc = (_shift_rows(xr, -2) * cw[0:1] + _shift_rows(xr, -1) * cw[1:2] + _shift_rows(xr, 0) * cw[2:3]
          + _shift_rows(xr, 1) * cw[3:4] + cb_ref[...])
    xcb = xc.astype(BF16)

    def direction(d):
        cols = slice(2 * d * HALF, (2 * d + 2) * HALF)
        gates = jax.nn.sigmoid(jnp.dot(xcb, wg_ref[:, cols], preferred_element_type=F32) + bg_ref[:, cols])
        log_a = (-LRU_C) * gates[:, 0:HALF] * _softplus(-lam_ref[d:d + 1, :])
        a = jnp.exp(log_a)
        return a, jnp.sqrt(1.0 - a * a) * (gates[:, HALF:2 * HALF] * xc)

    a, bval = direction(0)
    a_scr[...] = a
    b_scr[...] = bval

    @pl.when(pos == 0)
    def _():
        carry[...] = h0_ref[0:1, :]

    h = _lru_scan(a_scr, b_scr, hsf_ref, carry, False)
    fin_ref[...] = jnp.broadcast_to(h, (8, HALF))

    a, bval = direction(1)
    ar_ref[...] = a
    br_ref[...] = bval
    bg = col(1, TM)
    gb = col(4, TM)
    prod = cg * xin
    scw = scw_ref[...]
    yc = bg * (_shift_rows(prod, -1) * scw[0:1] + _shift_rows(prod, 0) * scw[1:2]
               + _shift_rows(prod, 1) * scw[2:3])
    yc_ref[...] = yc.astype(BF16)
    gate_ref[...] = jax.nn.gelu(gb).astype(BF16)


def _cd_in(x, mod, g, w_in, sc_w, conv_w, conv_b, w_gates, b_gates, lru_lam, h0t):
    const = lambda i: (0, 0)
    tok = lambda dt: jax.ShapeDtypeStruct((N_TOK, HALF), dt)
    row = pl.BlockSpec((TM, HALF), lambda i: (i, 0))
    return pl.pallas_call(
        _cd_in_kernel,
        grid=(NT,),
        in_specs=_halo_specs(D_MODEL) + [
            pl.BlockSpec((None, 6, D_MODEL), lambda i: (_mod_row(i), 0, 0)),
            pl.BlockSpec((1, D_MODEL), const),
            pl.BlockSpec((D_MODEL, 5 * HALF), const),
            pl.BlockSpec((3, HALF), const),
            pl.BlockSpec((4, HALF), const),
            pl.BlockSpec((1, HALF), const),
            pl.BlockSpec((HALF, 4 * HALF), const),
            pl.BlockSpec((1, 4 * HALF), const),
            pl.BlockSpec((2, HALF), const),
            pl.BlockSpec((None, 2, HALF), lambda i: (_mod_row(i), 0, 0)),
        ],
        out_specs=[row, row, row, row, row, pl.BlockSpec((8, HALF), lambda i: (i, 0))],
        out_shape=[tok(BF16), tok(BF16), tok(F32), tok(F32), tok(F32), jax.ShapeDtypeStruct((NT * 8, HALF), F32)],
        scratch_shapes=[
            pltpu.VMEM((ROWS, D_MODEL), BF16),
            pltpu.VMEM((TM, HALF), F32),
            pltpu.VMEM((TM, HALF), F32),
            pltpu.VMEM((1, HALF), F32),
        ],
        compiler_params=_params(),
        name="cd_in",
    )(x, x, x, mod, g, w_in, sc_w, conv_w, conv_b, w_gates, b_gates, lru_lam, h0t)


def _cd_out_kernel(ar_ref, br_ref, hsf_ref, gate_ref, yc_ref, x_ref, mod_ref, g_ref, wo_ref, h0_ref,
                   o_ref, fin_ref, hs_scr, carry):
    ti = NT - 1 - pl.program_id(0)
    pos, n = _seq_pos(ti)

    @pl.when(pos == n - 1)
    def _():
        carry[...] = h0_ref[1:2, :]

    h = _lru_scan(ar_ref, br_ref, hs_scr, carry, True)
    fin_ref[...] = jnp.broadcast_to(h, (8, HALF))
    out_c = jnp.dot(yc_ref[...], wo_ref[0:HALF, :], preferred_element_type=F32)
    yd = (hsf_ref[...] + hs_scr[...]) * gate_ref[...].astype(F32)
    out = out_c + jnp.dot(yd.astype(BF16), wo_ref[HALF:2 * HALF, :], preferred_element_type=F32)
    o_ref[...] = _post(x_ref[...], out, g_ref[...], mod_ref[2:3, :])


def _cd_out(a_r, b_r, hs_f, gate, yc, x, mod, g, w_out, h0t):
    const = lambda i: (0, 0)
    rev = lambda i: (NT - 1 - i, 0)
    row = pl.BlockSpec((TM, HALF), rev)
    return pl.pallas_call(
        _cd_out_kernel,
        grid=(NT,),
        in_specs=[
            row, row, row, row, row,
            pl.BlockSpec((TM, D_MODEL), rev),
            pl.BlockSpec((None, 6, D_MODEL), lambda i: (_mod_row(NT - 1 - i), 0, 0)),
            pl.BlockSpec((1, D_MODEL), const),
            pl.BlockSpec((D_MODEL, D_MODEL), const),
            pl.BlockSpec((None, 2, HALF), lambda i: (_mod_row(NT - 1 - i), 0, 0)),
        ],
        out_specs=[
            pl.BlockSpec((TM, D_MODEL), rev),
            pl.BlockSpec((8, HALF), rev),
        ],
        out_shape=[
            jax.ShapeDtypeStruct((N_TOK, D_MODEL), F32),
            jax.ShapeDtypeStruct((NT * 8, HALF), F32),
        ],
        scratch_shapes=[
            pltpu.VMEM((TM, HALF), F32),
            pltpu.VMEM((1, HALF), F32),
        ],
        compiler_params=_params(),
        name="cd_out",
    )(a_r, b_r, hs_f, gate, yc, x, mod, g, w_out, h0t)


def kernel(x_prompt, x_sample, cache_attn_k, cache_attn_v, state_s5, state_rglru, c, c_ctx, w_mod, b_mod, norm_g, w_in_ab, w_out_ab, s5_lam_re, s5_lam_im, s5_log_dt, s5_b, s5_c, s5_d, s5_w_glu, s5_b_glu, da_lam, da_g, w_in_cd, w_out_cd, sc_conv_w, lru_conv_w, lru_conv_b, lru_w_a, lru_b_a, lru_w_x, lru_b_x, lru_lam, ffn_w_up, ffn_conv_w, ffn_conv_b, ffn_w_down):
    assert DEPTH == 2
    xp = x_prompt.reshape(NT_P * TM, D_MODEL)
    xs = x_sample.reshape(NT_S * TM, D_MODEL)
    cv8 = jnp.zeros((8, D_MODEL), F32).at[0].set(c_ctx).at[1:1 + DEC_BATCH].set(c)
    mod = _modulation(cv8, w_mod, b_mod)
    cos_t, sin_t = _rope_tables()
    g = norm_g.reshape(DEPTH, 4, 1, D_MODEL)

    lam_init = 0.8 - 0.6 * math.exp(-0.3 * 0)
    u, qkv, k32, v32 = _ab_in(xp, xs, mod[0], g[0, 0], w_in_ab[0].astype(BF16), cos_t, sin_t)
    m1, m2, m3, coef = _s5_params(s5_lam_re[0], s5_lam_im[0], s5_log_dt[0], s5_b[0], s5_c[0])
    ys5_p, ys5_s, new_s5 = _s5(u, (m1, m2, m3), coef, state_s5[:, 0], s5_d[0])
    yb_p = _attention(qkv, 0, BATCH, 1, SEQ, None, da_lam[0], da_g[0], lam_init)
    ctx = (cache_attn_k[:, 0].reshape(DEC_BATCH, PAST_LEN, HALF).astype(BF16),
           cache_attn_v[:, 0].reshape(DEC_BATCH, PAST_LEN, HALF).astype(BF16))
    yb_s = _attention(qkv, NT_P, DEC_BATCH, TPS, DEC_SEQ, ctx, da_lam[0], da_g[0], lam_init)
    x = _ab_out(ys5_p, ys5_s, yb_p, yb_s, xp, xs, mod[0], g[0, 1], s5_w_glu[0].astype(BF16), s5_b_glu[0].reshape(1, HALF),
                w_out_ab[0].astype(BF16))
    assert NT_P == BATCH and TM == SEQ
    new_k = k32.reshape(BATCH, 1, DA_HEADS, 2, DA_DK, SEQ).transpose(0, 1, 5, 2, 3, 4)
    new_v = v32
    x = _ffn(x, mod[0], g[0, 2], g[0, 3], ffn_w_up, ffn_conv_w[0], ffn_conv_b[0].reshape(1, 2 * D_FF), ffn_w_down,
             0, False)

    eye = jnp.eye(LRU_BLOCKS, dtype=F32)
    dense = lambda w: jnp.einsum('kcd,kl->kcld', w, eye).reshape(LRU_WIDTH, LRU_WIDTH)
    w_gates = jnp.concatenate([dense(lru_w_a[0, 0]), dense(lru_w_x[0, 0]),
                               dense(lru_w_a[0, 1]), dense(lru_w_x[0, 1])], axis=1).astype(BF16)
    b_gates = jnp.concatenate([lru_b_a[0, 0], lru_b_x[0, 0], lru_b_a[0, 1], lru_b_x[0, 1]]).reshape(1, 4 * HALF)
    h0t = jnp.zeros((8, 2, HALF), F32).at[1:1 + DEC_BATCH].set(state_rglru[:, 0])
    yc, gate, hs_f, a_r, b_r, fin_f = _cd_in(x, mod[1], g[1, 0], w_in_cd[0].astype(BF16), sc_conv_w[0], lru_conv_w[0],
                                      lru_conv_b[0].reshape(1, HALF), w_gates, b_gates, lru_lam[0], h0t)
    x, fin_r = _cd_out(a_r, b_r, hs_f, gate, yc, x, mod[1], g[1, 1], w_out_cd[0].astype(BF16), h0t)
    tile_row0 = lambda f: f.reshape(NT, 8, HALF)[:NT_P, 0]
    new_lru = jnp.stack([tile_row0(fin_f), tile_row0(fin_r)], axis=1)[:, None]
    yp, ys = _ffn(x, mod[1], g[1, 2], g[1, 3], ffn_w_up, ffn_conv_w[1], ffn_conv_b[1].reshape(1, 2 * D_FF),
                  ffn_w_down, 1, True)
    return (yp.reshape(BATCH, SEQ, D_MODEL), ys.reshape(DEC_BATCH, DEC_SEQ, D_MODEL),
            new_k, new_v, new_s5[:, None], new_lru)
```

```python
import functools
import math

import jax
import jax.numpy as jnp
from jax import lax
from jax.experimental import pallas as pl
from jax.experimental.pallas import tpu as pltpu

D_MODEL = 1024
BATCH = 32
SEQ = 256
DEPTH = 2
DEC_BATCH = 4
DEC_SEQ = 2048
PAST_LEN = 512
GRID_W = 64
HALF = D_MODEL // 2
S5_P = 16
S5_G = HALF // S5_P
S5_N = 64
DA_DK = 64
DA_DV = 2 * DA_DK
DA_HEADS = HALF // DA_DV
ROPE_THETA = 10000.0
ROPE_F = DA_DK // 4
LRU_WIDTH = HALF
LRU_BLOCKS = 8
LRU_BS = LRU_WIDTH // LRU_BLOCKS
LRU_C = 8.0
D_FF = 2816
EPS = 1e-6

F32 = jnp.float32
BF16 = jnp.bfloat16
HIGHEST = lax.Precision.HIGHEST

LANES = 128
SUBLANES = 8
VMEM_BYTES = 64 * 1024 * 1024

TM = 256
NT_P = BATCH * SEQ // TM
TPS = DEC_SEQ // TM
NT_S = DEC_BATCH * TPS
NT = NT_P + NT_S
N_TOK = NT * TM
TW = 2 * TM
NW_P = NT_P * TM // TW
WPS = DEC_SEQ // TW
NW = N_TOK // TW
HALO = 8
ROWS = TM + 2 * HALO
S5_T = 8
S5_GB = LANES // S5_P
S5_NB = S5_G // S5_GB
S5_K = S5_T * LANES
S5_ST = 4 * S5_N
S5_GS = 4
FF_CHUNK = 256
FF_NCH = D_FF // FF_CHUNK
Q_SCALE = math.log2(math.e) / math.sqrt(DA_DK)
GELU_K = math.sqrt(2.0 / math.pi)
GELU_C = 0.044715
VMEM_LIMIT = VMEM_BYTES - 8 * 1024 * 1024


def _mod_row(i, ntp=NT_P, tps=TPS):
    return jnp.where(i < ntp, 0, 1 + (i - ntp) // tps)


def _seq_pos(i):
    return jnp.where(i < NT_P, 0, (i - NT_P) % TPS), jnp.where(i < NT_P, 1, TPS)


def _split_specs(width, tm=TM, ntp=NT_P):
    return [
        pl.BlockSpec((tm, width), lambda i: (jnp.minimum(i, ntp - 1), 0)),
        pl.BlockSpec((tm, width), lambda i: (jnp.maximum(i - ntp, 0), 0)),
    ]


def _pick(p_ref, s_ref, ntp=NT_P):
    return jnp.where(pl.program_id(0) < ntp, p_ref[...], s_ref[...])


def _params(sem=("arbitrary",)):
    return pltpu.CompilerParams(dimension_semantics=sem, vmem_limit_bytes=VMEM_LIMIT)


def _rms(x, g):
    ms = jnp.mean(x * x, axis=-1, keepdims=True)
    return x * lax.rsqrt(ms + EPS) * g


def _pre(x, g, shift, scale):
    return _rms(x, g) * (1.0 + scale) + shift


def _mod_kernel(cv_ref, w_ref, b_ref, o_ref):
    cv = cv_ref[...]
    s = cv * jax.nn.sigmoid(cv)
    w = w_ref[...]
    s_hi = s.astype(BF16)
    s_lo = (s - s_hi.astype(F32)).astype(BF16)
    w_hi = w.astype(BF16)
    w_lo = (w - w_hi.astype(F32)).astype(BF16)
    dot = lambda a, b: jnp.dot(a, b, preferred_element_type=F32)
    o_ref[...] = dot(s_hi, w_hi) + (dot(s_hi, w_lo) + dot(s_lo, w_hi)) + b_ref[...]


def _modulation(cv8, w_mod, b_mod):
    nb = 1536
    out = pl.pallas_call(
        _mod_kernel,
        grid=(DEPTH, 6 * D_MODEL // nb),
        in_specs=[
            pl.BlockSpec((8, D_MODEL), lambda l, j: (0, 0)),
            pl.BlockSpec((None, D_MODEL, nb), lambda l, j: (l, 0, j)),
            pl.BlockSpec((None, 1, nb), lambda l, j: (l, 0, j)),
        ],
        out_specs=pl.BlockSpec((None, 8, nb), lambda l, j: (l, 0, j)),
        out_shape=jax.ShapeDtypeStruct((DEPTH, 8, 6 * D_MODEL), F32),
        compiler_params=_params(("arbitrary", "arbitrary")),
        name="modulation",
    )(cv8, w_mod, b_mod.reshape(DEPTH, 1, 6 * D_MODEL))
    return out.reshape(DEPTH, 8, 6, D_MODEL)


def _rope_partner(x):
    lane = lax.broadcasted_iota(jnp.int32, (1, HALF), 1)
    first = (lane % (2 * ROPE_F)) < ROPE_F
    return jnp.where(first, pltpu.roll(x, HALF - ROPE_F, axis=1), pltpu.roll(x, ROPE_F, axis=1))


def _ab_in_kernel(xp_ref, xs_ref, mod_ref, g_ref, w_ref, cos_ref, sin_ref, u_ref, qkv_ref, k_ref, v_ref):
    i = pl.program_id(0)
    h = _pre(_pick(xp_ref, xs_ref, NW_P), g_ref[...], mod_ref[0:1, :], mod_ref[1:2, :])
    proj = jnp.dot(h.astype(BF16), w_ref[...], preferred_element_type=F32)
    u_ref[...] = proj[:, 0:HALF]
    q = proj[:, HALF:2 * HALF]
    k = proj[:, 2 * HALF:3 * HALF]
    v = proj[:, 3 * HALF:4 * HALF]
    cos = jnp.concatenate([cos_ref[...]] * DA_HEADS, axis=1)
    sin = jnp.concatenate([sin_ref[...]] * DA_HEADS, axis=1)
    qr = q * cos + _rope_partner(q) * sin
    kr = k * cos + _rope_partner(k) * sin
    qkv_ref[:, 0:HALF] = (qr * Q_SCALE).astype(BF16)
    qkv_ref[:, HALF:2 * HALF] = kr.astype(BF16)
    qkv_ref[:, 2 * HALF:3 * HALF] = v.astype(BF16)

    @pl.when(i < NW_P)
    def _():
        kt = k.T
        for s in range(TW // SEQ):
            k_ref[s] = kt[:, s * SEQ:(s + 1) * SEQ]
            v_ref[s] = v[s * SEQ:(s + 1) * SEQ].reshape(SEQ, DA_HEADS, DA_DV)


def _ab_in(xp, xs, mod, g, w, cos_t, sin_t):
    def rope_tile(i):
        return jnp.where(i < NW_P, 0, 1 + (i - NW_P) % WPS)

    per_tile = TW // SEQ
    return pl.pallas_call(
        _ab_in_kernel,
        grid=(NW,),
        in_specs=_split_specs(D_MODEL, TW, NW_P) + [
            pl.BlockSpec((None, 6, D_MODEL), lambda i: (_mod_row(i, NW_P, WPS), 0, 0)),
            pl.BlockSpec((1, D_MODEL), lambda i: (0, 0)),
            pl.BlockSpec((D_MODEL, 4 * HALF), lambda i: (0, 0)),
            pl.BlockSpec((TW, DA_DV), lambda i: (rope_tile(i), 0)),
            pl.BlockSpec((TW, DA_DV), lambda i: (rope_tile(i), 0)),
        ],
        out_specs=[
            pl.BlockSpec((TW, HALF), lambda i: (i, 0)),
            pl.BlockSpec((TW, 3 * HALF), lambda i: (i, 0)),
            pl.BlockSpec((per_tile, HALF, SEQ), lambda i: (jnp.minimum(i, NW_P - 1), 0, 0)),
            pl.BlockSpec((per_tile, None, SEQ, DA_HEADS, DA_DV), lambda i: (jnp.minimum(i, NW_P - 1), 0, 0, 0, 0)),
        ],
        out_shape=[
            jax.ShapeDtypeStruct((N_TOK, HALF), F32),
            jax.ShapeDtypeStruct((N_TOK, 3 * HALF), BF16),
            jax.ShapeDtypeStruct((BATCH, HALF, SEQ), F32),
            jax.ShapeDtypeStruct((BATCH, 1, SEQ, DA_HEADS, DA_DV), F32),
        ],
        compiler_params=_params(),
        name="ab_in",
    )(xp, xs, mod, g, w, cos_t, sin_t)


def _rope_tables():
    pos = jnp.arange(DEC_SEQ)
    t_row = (pos // GRID_W).astype(F32)
    t_col = (pos % GRID_W).astype(F32)
    inv = ROPE_THETA ** (-jnp.arange(ROPE_F, dtype=F32) / ROPE_F)
    ang = jnp.stack([t_row[:, None] * inv, t_col[:, None] * inv], axis=1)
    cos, sin = jnp.cos(ang), jnp.sin(ang)
    cos64 = jnp.stack([cos, cos], axis=2).reshape(DEC_SEQ, DA_DK)
    sin64 = jnp.stack([-sin, sin], axis=2).reshape(DEC_SEQ, DA_DK)
    cos_t = jnp.concatenate([jnp.ones((TW, DA_DV), F32), jnp.tile(cos64, (1, 2))], axis=0)
    sin_t = jnp.concatenate([jnp.zeros((TW, DA_DV), F32), jnp.tile(sin64, (1, 2))], axis=0)
    return cos_t, sin_t


def _s5_param_kernel(lre_ref, lim_ref, ldt_ref, bt_ref, c_ref, m1_ref, m2_ref, m3_ref, at_ref):
    nt = (((1,), (1,)), ((), ()))
    e_tab, f_tab, k_tab = [], [], []
    r_i = lax.broadcasted_iota(jnp.int32, (LANES, LANES), 0) // S5_P
    c_i = lax.broadcasted_iota(jnp.int32, (LANES, LANES), 1) // S5_P
    same_group = r_i == c_i
    for d in range(2):
        lre = lre_ref[d]
        lim = lim_ref[d]
        dt = jnp.exp(ldt_ref[d])
        mag = jnp.exp(lre * dt)
        ang = lim * dt
        ar = mag * jnp.cos(ang)
        ai = mag * jnp.sin(ang)
        den = lre * lre + lim * lim
        nr = ar - 1.0
        fr = (nr * lre + ai * lim) / den
        fi = (ai * lre - nr * lim) / den
        btr = bt_ref[d, 0]
        bti = bt_ref[d, 1]
        bfr = fr * btr - fi * bti
        bfi = fr * bti + fi * btr
        cr = c_ref[d, 0]
        ci = c_ref[d, 1]
        cst = jnp.concatenate([cr, -ci], axis=1)
        pr = jnp.ones_like(ar)
        pi = jnp.zeros_like(ar)
        e_d, f_d, k_d = [], [], []
        for tau in range(S5_T + 1):
            f_d.append((cr * pr - ci * pi, cr * pi + ci * pr))
            if tau < S5_T:
                er, ei = pr * bfr - pi * bfi, pr * bfi + pi * bfr
                e_d.append((er, ei))
                full = lax.dot_general(jnp.concatenate([er, ei], axis=1), cst, nt,
                                       precision=HIGHEST, preferred_element_type=F32)
                k_d.append(jnp.where(same_group, full, 0.0))
            else:
                at_ref[d, 0] = pr
                at_ref[d, 1] = pi
            pr, pi = pr * ar - pi * ai, pr * ai + pi * ar
        e_tab.append(e_d)
        f_tab.append(f_d)
        k_tab.append(k_d)

    for s in range(S5_T):
        for t in range(S5_T):
            blk = k_tab[0][t - s] if t >= s else k_tab[1][s - t]
            if t == s:
                blk = blk + k_tab[1][0]
            m1_ref[s * LANES:(s + 1) * LANES, t * LANES:(t + 1) * LANES] = blk.astype(BF16)
    row_group = lax.broadcasted_iota(jnp.int32, (LANES, S5_ST), 0) // S5_P
    for s in range(S5_T):
        ef, er_ = e_tab[0][S5_T - 1 - s], e_tab[1][s]
        v = jnp.concatenate([ef[0], er_[0], ef[1], er_[1]], axis=1)
        ff, fr_ = f_tab[0][s + 1], f_tab[1][S5_T - s]
        w = jnp.concatenate([ff[0], fr_[0], -ff[1], -fr_[1]], axis=1)
        for gi in range(S5_GB):
            sel = row_group == gi
            m2_ref[s * LANES:(s + 1) * LANES, gi * S5_ST:(gi + 1) * S5_ST] = jnp.where(sel, v, 0.0).astype(BF16)
            m3_ref[s * LANES:(s + 1) * LANES, gi * S5_ST:(gi + 1) * S5_ST] = jnp.where(sel, w, 0.0).astype(BF16)


def _s5_params(lam_re, lam_im, log_dt, b_w, c_w):
    rows = S5_G * S5_P
    per_row = lambda a: jnp.repeat(a, S5_P, axis=1)
    lre = per_row(lam_re)
    lim = per_row(lam_im)
    ldt = per_row(jnp.broadcast_to(log_dt[:, :, None], (2, S5_G, S5_N)))
    bt = b_w.transpose(0, 1, 2, 4, 3).reshape(2, 2, rows, S5_N)
    cc = c_w.reshape(2, 2, rows, S5_N)
    vec = pl.BlockSpec((2, LANES, S5_N), lambda j: (0, j, 0))
    mat = pl.BlockSpec((2, 2, LANES, S5_N), lambda j: (0, 0, j, 0))
    wide = lambda n: pl.BlockSpec((None, S5_K, n), lambda j: (j, 0, 0))
    m1, m2, m3, at = pl.pallas_call(
        _s5_param_kernel,
        grid=(S5_NB,),
        in_specs=[vec, vec, vec, mat, mat],
        out_specs=[wide(S5_K), wide(S5_GB * S5_ST), wide(S5_GB * S5_ST),
                   pl.BlockSpec((None, 2, 2, LANES, S5_N), lambda j: (j, 0, 0, 0, 0))],
        out_shape=[
            jax.ShapeDtypeStruct((S5_NB, S5_K, S5_K), BF16),
            jax.ShapeDtypeStruct((S5_NB, S5_K, S5_GB * S5_ST), BF16),
            jax.ShapeDtypeStruct((S5_NB, S5_K, S5_GB * S5_ST), BF16),
            jax.ShapeDtypeStruct((S5_NB, 2, 2, LANES, S5_N), F32),
        ],
        compiler_params=_params(),
        name="s5_params",
    )(lre, lim, ldt, bt, cc)
    coef = at[:, :, :, ::S5_P, :].transpose(0, 3, 2, 1, 4).reshape(S5_NB, S5_GB, 2, 2 * S5_N)
    return m1, m2, m3, coef


def _s5_kernel(u_ref, m1_ref, m2_ref, m3_ref, coef_ref, d_ref, *rest, stride, nc, two_pass):
    if two_pass:
        h0_ref, y_ref, lhs, yacc, s_scr, hp_scr = rest
    else:
        y_ref, fin_ref, lhs, yacc, s_scr, hp_scr = rest
    ln = 2 * S5_N
    nt = (((1,), (1,)), ((), ()))

    def seq_rows(k, c):
        return pl.ds(pl.multiple_of(k * stride + c * S5_T, S5_T), S5_T)

    def gather(c2, carry):
        halves = []
        for c in (2 * c2, 2 * c2 + 1):
            tiles = [u_ref[seq_rows(k, c), :] for k in range(8)]
            halves.append(jnp.stack(tiles, axis=0).reshape(8, S5_K))
        lhs[pl.ds(pl.multiple_of(c2 * 16, 16), 16), :] = jnp.concatenate(halves, axis=0).astype(BF16)
        return carry

    lax.fori_loop(0, nc // 2, gather, 0)
    lb = lhs[...]
    yacc[...] = jnp.dot(lb, m1_ref[...], preferred_element_type=F32)

    lane = lax.broadcasted_iota(jnp.int32, (8, ln), 1)
    row = lax.broadcasted_iota(jnp.int32, (8, ln), 0)
    fwd = lane < S5_N
    odd = (row % 2) == 1
    wide = S5_GS * S5_ST
    for gq in range(S5_GB // S5_GS):
        groups = range(gq * S5_GS, (gq + 1) * S5_GS)
        cols = slice(gq * wide, (gq + 1) * wide)
        s_scr[...] = jnp.dot(lb, m2_ref[:, cols], preferred_element_type=F32)
        are = [jnp.broadcast_to(coef_ref[g, 0:1, :], (8, ln)) for g in groups]
        aim = [jnp.broadcast_to(coef_ref[g, 1:2, :], (8, ln)) for g in groups]

        def scan(init, store):
            def body(c, carry):
                kf = pl.multiple_of(c * 8, 8)
                kr = pl.multiple_of((nc - 1 - c) * 8, 8)
                out = []
                for j in range(S5_GS):
                    p, q = carry[2 * j], carry[2 * j + 1]
                    o = j * S5_ST
                    if store:
                        hp_scr[pl.ds(kf, 8), o:o + S5_N] = p[:, 0:S5_N]
                        hp_scr[pl.ds(kr, 8), o + S5_N:o + ln] = p[:, S5_N:ln]
                        hp_scr[pl.ds(kf, 8), o + ln:o + ln + S5_N] = q[:, 0:S5_N]
                        hp_scr[pl.ds(kr, 8), o + ln + S5_N:o + 2 * ln] = q[:, S5_N:ln]
                    s_re = jnp.where(fwd, s_scr[pl.ds(kf, 8), o:o + ln], s_scr[pl.ds(kr, 8), o:o + ln])
                    s_im = jnp.where(fwd, s_scr[pl.ds(kf, 8), o + ln:o + 2 * ln],
                                     s_scr[pl.ds(kr, 8), o + ln:o + 2 * ln])
                    out += [are[j] * p - aim[j] * q + s_re, are[j] * q + aim[j] * p + s_im]
                return tuple(out)
            return lax.fori_loop(0, nc, body, tuple(init))

        if two_pass:
            h0 = [h0_ref[g, r] for g in groups for r in range(2)]
            fin = scan(h0, False)
            seed = [h + jnp.where(fwd & odd, pltpu.roll(f, 1, axis=0), 0.0)
                    + jnp.where(jnp.logical_not(fwd | odd), pltpu.roll(f, 7, axis=0), 0.0)
                    for h, f in zip(h0, fin)]
            scan(seed, True)
        else:
            fin = scan([jnp.zeros((8, ln), F32)] * (2 * S5_GS), True)
            for j, g in enumerate(groups):
                fin_ref[g, 0] = fin[2 * j]
                fin_ref[g, 1] = fin[2 * j + 1]
        yacc[...] += lax.dot_general(hp_scr[...].astype(BF16), m3_ref[:, cols], nt, preferred_element_type=F32)

    dvec = d_ref[...]

    def scatter(c, carry):
        blk = yacc[pl.ds(pl.multiple_of(c * 8, 8), 8), :].reshape(8, S5_T, LANES)
        for k in range(8):
            rows = seq_rows(k, c)
            y_ref[rows, :] = blk[k] + u_ref[rows, :] * dvec
        return carry

    lax.fori_loop(0, nc, scatter, 0)


def _s5_call(u, row_block, mats, coef, dvec, h0, stride, nc, parts):
    m1, m2, m3 = mats
    rows = 8 * stride
    two_pass = h0 is not None
    mat = lambda n: pl.BlockSpec((None, S5_K, n), lambda j, i: (j, 0, 0))
    tok = pl.BlockSpec((rows, LANES), lambda j, i: (row_block * parts + i, j))
    in_specs = [
        tok, mat(S5_K), mat(S5_GB * S5_ST), mat(S5_GB * S5_ST),
        pl.BlockSpec((None, S5_GB, 2, 2 * S5_N), lambda j, i: (j, 0, 0, 0)),
        pl.BlockSpec((None, 1, LANES), lambda j, i: (j, 0, 0)),
    ]
    args = [u, m1, m2, m3, coef, dvec]
    out_specs = [pl.BlockSpec((rows, LANES), lambda j, i: (i, j))]
    out_shape = [jax.ShapeDtypeStruct((parts * rows, HALF), F32)]
    if two_pass:
        in_specs.append(pl.BlockSpec((None, S5_GB, 2, 8, 2 * S5_N), lambda j, i: (j, 0, 0, 0, 0)))
        args.append(h0)
    else:
        out_specs.append(pl.BlockSpec((None, S5_GB, 2, 8, 2 * S5_N), lambda j, i: (j, 0, 0, i, 0)))
        out_shape.append(jax.ShapeDtypeStruct((S5_NB, S5_GB, 2, parts * 8, 2 * S5_N), F32))
    return pl.pallas_call(
        functools.partial(_s5_kernel, stride=stride, nc=nc, two_pass=two_pass),
        grid=(S5_NB, parts),
        in_specs=in_specs,
        out_specs=out_specs,
        out_shape=out_shape,
        scratch_shapes=[
            pltpu.VMEM((nc * 8, S5_K), BF16),
            pltpu.VMEM((nc * 8, S5_K), F32),
            pltpu.VMEM((nc * 8, S5_GS * S5_ST), F32),
            pltpu.VMEM((nc * 8, S5_GS * S5_ST), F32),
        ],
        compiler_params=_params(("arbitrary", "arbitrary")),
        name="s5_%d" % stride,
    )(*args)


def _s5(u, mats, coef, h0, s5_d):
    dvec = s5_d.reshape(S5_NB, 1, LANES)
    yp, fin = _s5_call(u, 0, mats, coef, dvec, None, SEQ, SEQ // S5_T, BATCH // 8)
    hh = h0.transpose(3, 2, 0, 1, 4)
    z = jnp.zeros_like(hh[:, :, :, 0])
    first = jnp.concatenate([hh[:, :, :, 0], z], axis=-1)
    second = jnp.concatenate([z, hh[:, :, :, 1]], axis=-1)
    h0g = jnp.stack([first, second], axis=3).reshape(S5_NB, S5_GB, 2, 2 * DEC_BATCH, 2 * S5_N)
    half = DEC_SEQ // 2
    (ys,) = _s5_call(u, 1, mats, coef, dvec, h0g, half, half // S5_T, 1)
    new_s5 = fin.reshape(S5_G, 2, BATCH, 2, S5_N).transpose(2, 3, 1, 0, 4)
    return yp, ys, new_s5


def _attn_kernel(*refs, lam_init, has_ctx, lk):
    if has_ctx:
        q_ref, k_ref, v_ref, kc_ref, vc_ref, dl_ref, g_ref, o_ref, s_a, s_b, vx, vcx = refs
    else:
        q_ref, k_ref, v_ref, dl_ref, g_ref, o_ref, s_a, s_b, vx = refs
    lt = lk + (PAST_LEN if has_ctx else 0)
    dl = dl_ref[...]
    lam = (jnp.exp(jnp.sum(dl[0:1] * dl[1:2], keepdims=True))
           - jnp.exp(jnp.sum(dl[2:3] * dl[3:4], keepdims=True)) + lam_init)
    lane = lax.broadcasted_iota(jnp.int32, (1, DA_DV), 1)
    first = lane < DA_DK
    g = g_ref[...]
    nt = (((1,), (1,)), ((), ()))

    @pl.when(pl.program_id(1) == 0)
    def _():
        onehot = (lane == 0).astype(BF16)
        for h in range(DA_HEADS):
            cols = slice(h * DA_DV, (h + 1) * DA_DV)
            vx[:, 2 * h * DA_DV:(2 * h + 1) * DA_DV] = v_ref[:, cols]
            vx[:, (2 * h + 1) * DA_DV:(2 * h + 2) * DA_DV] = jnp.broadcast_to(onehot, (lk, DA_DV))
            if has_ctx:
                vcx[:, 2 * h * DA_DV:(2 * h + 1) * DA_DV] = vc_ref[:, cols]
                vcx[:, (2 * h + 1) * DA_DV:(2 * h + 2) * DA_DV] = jnp.broadcast_to(onehot, (PAST_LEN, DA_DV))

    def scores(h, dst):
        cols = slice(h * DA_DV, (h + 1) * DA_DV)
        qh = q_ref[:, cols]
        zero = jnp.zeros_like(qh)
        for m in range(2):
            qm = jnp.where(first if m == 0 else jnp.logical_not(first), qh, zero)
            dst[m, :, 0:lk] = lax.dot_general(qm, k_ref[:, cols], nt, preferred_element_type=F32)
            if has_ctx:
                dst[m, :, lk:lt] = lax.dot_general(qm, kc_ref[:, cols], nt, preferred_element_type=F32)

    def head(h, src):
        xcols = slice(2 * h * DA_DV, (2 * h + 2) * DA_DV)
        outs = []
        for m in range(2):
            s = src[m]
            e = jnp.exp2(s - jnp.max(s, axis=-1, keepdims=True)).astype(BF16)
            ox = jnp.dot(e[:, 0:lk], vx[:, xcols], preferred_element_type=F32)
            if has_ctx:
                ox = ox + jnp.dot(e[:, lk:lt], vcx[:, xcols], preferred_element_type=F32)
            outs.append(ox[:, 0:DA_DV] * (1.0 / ox[:, DA_DV:DA_DV + 1]))
        o = outs[0] - lam * outs[1]
        o_ref[:, h * DA_DV:(h + 1) * DA_DV] = (_rms(o, g) * (1.0 - lam_init)).astype(BF16)

    scores(0, s_a)
    for h in range(DA_HEADS):
        cur, nxt = (s_a, s_b) if h % 2 == 0 else (s_b, s_a)
        if h + 1 < DA_HEADS:
            scores(h + 1, nxt)
        head(h, cur)


def _attention(qkv, q_tile0, nb, nq, lk, ctx, da_lam, da_g, lam_init):
    kb0 = q_tile0 * TM // lk
    in_specs = [
        pl.BlockSpec((TM, HALF), lambda b, j: (q_tile0 + b * nq + j, 0)),
        pl.BlockSpec((lk, HALF), lambda b, j: (kb0 + b, 1)),
        pl.BlockSpec((lk, HALF), lambda b, j: (kb0 + b, 2)),
    ]
    args = [qkv, qkv, qkv]
    if ctx is not None:
        in_specs += [pl.BlockSpec((None, PAST_LEN, HALF), lambda b, j: (b, 0, 0))] * 2
        args += list(ctx)
    in_specs += [
        pl.BlockSpec((4, DA_DK), lambda b, j: (0, 0)),
        pl.BlockSpec((1, DA_DV), lambda b, j: (0, 0)),
    ]
    lt = lk + (PAST_LEN if ctx is not None else 0)
    scratch = [pltpu.VMEM((2, TM, lt), F32), pltpu.VMEM((2, TM, lt), F32), pltpu.VMEM((lk, 2 * HALF), BF16)]
    if ctx is not None:
        scratch.append(pltpu.VMEM((PAST_LEN, 2 * HALF), BF16))
    return pl.pallas_call(
        functools.partial(_attn_kernel, lam_init=lam_init, has_ctx=ctx is not None, lk=lk),
        grid=(nb, nq),
        in_specs=in_specs,
        out_specs=pl.BlockSpec((TM, HALF), lambda b, j: (b * nq + j, 0)),
        out_shape=jax.ShapeDtypeStruct((nb * nq * TM, HALF), BF16),
        scratch_shapes=scratch,
        compiler_params=_params(("arbitrary", "arbitrary")),
        name="diff_attn_%d" % lk,
    )(*args, da_lam, da_g.reshape(1, DA_DV))


def _post(x, y, g, gate):
    return x + gate * _rms(y, g)


def _ab_out_kernel(ysp_ref, yss_ref, ybp_ref, ybs_ref, xp_ref, xs_ref, mod_ref, g_ref, wg_ref, bg_ref, wo_ref,
                   o_ref):
    ys = jax.nn.gelu(_pick(ysp_ref, yss_ref, NW_P))
    glu = jnp.dot(ys.astype(BF16), wg_ref[...], preferred_element_type=F32) + bg_ref[...]
    ya = ys * jax.nn.sigmoid(glu)
    out = (jnp.dot(ya.astype(BF16), wo_ref[0:HALF, :], preferred_element_type=F32)
           + jnp.dot(_pick(ybp_ref, ybs_ref, NW_P), wo_ref[HALF:2 * HALF, :], preferred_element_type=F32))
    o_ref[...] = _post(_pick(xp_ref, xs_ref, NW_P), out, g_ref[...], mod_ref[2:3, :])


def _ab_out(ys5_p, ys5_s, yb_p, yb_s, xp, xs, mod, g, w_glu, b_glu, w_out):
    return pl.pallas_call(
        _ab_out_kernel,
        grid=(NW,),
        in_specs=_split_specs(HALF, TW, NW_P) + _split_specs(HALF, TW, NW_P) + _split_specs(D_MODEL, TW, NW_P) + [
            pl.BlockSpec((None, 6, D_MODEL), lambda i: (_mod_row(i, NW_P, WPS), 0, 0)),
            pl.BlockSpec((1, D_MODEL), lambda i: (0, 0)),
            pl.BlockSpec((HALF, HALF), lambda i: (0, 0)),
            pl.BlockSpec((1, HALF), lambda i: (0, 0)),
            pl.BlockSpec((D_MODEL, D_MODEL), lambda i: (0, 0)),
        ],
        out_specs=pl.BlockSpec((TW, D_MODEL), lambda i: (i, 0)),
        out_shape=jax.ShapeDtypeStruct((N_TOK, D_MODEL), F32),
        compiler_params=_params(),
        name="ab_out",
    )(ys5_p, ys5_s, yb_p, yb_s, xp, xs, mod, g, w_glu, b_glu, w_out)


def _halo_specs(width):
    blocks = TM // HALO
    last = N_TOK // HALO - 1
    return [
        pl.BlockSpec((TM, width), lambda i: (i, 0)),
        pl.BlockSpec((HALO, width), lambda i: (jnp.maximum(i * blocks - 1, 0), 0)),
        pl.BlockSpec((HALO, width), lambda i: (jnp.minimum((i + 1) * blocks, last), 0)),
    ]


def _fill_hbuf(hbuf, x_ref, xp_ref, xn_ref, g, shift, scale, i):
    pos, n = _seq_pos(i)
    hp = jnp.where(pos > 0, _pre(xp_ref[...], g, shift, scale), 0.0)
    hn = jnp.where(pos < n - 1, _pre(xn_ref[...], g, shift, scale), 0.0)
    hbuf[0:TM, :] = _pre(x_ref[...], g, shift, scale).astype(BF16)
    hbuf[TM:ROWS, :] = jnp.concatenate([hn, hp], axis=0).astype(BF16)


def _shift_rows(x, s):
    if s == 0:
        return x[0:TM]
    return pltpu.roll(x, (-s) % ROWS, axis=0)[0:TM]


def _ffn_kernel(x_ref, xp_ref, xn_ref, mod_ref, g2_ref, g3_ref, wg_ref, wv_ref, wd_ref, cw_ref, cb_ref, *rest, split):
    hbuf, acc, u_a, u_b, wu_s, wd_s = rest[-6:]
    step = pl.program_id(0)

    @pl.when(step < FF_NCH)
    def _():
        wu_s[step] = wg_ref[...].astype(BF16)
        wu_s[FF_NCH + step] = wv_ref[...].astype(BF16)
        wd_s[step] = wd_ref[...].astype(BF16)

    @pl.when(step >= FF_NCH)
    def _():
        i = step - FF_NCH
        _fill_hbuf(hbuf, x_ref, xp_ref, xn_ref, g2_ref[...], mod_ref[3:4, :], mod_ref[4:5, :], i)

        def up(j, dst):
            for half in range(2):
                dst[half] = jnp.dot(hbuf[...], wu_s[half * FF_NCH + j], preferred_element_type=F32)

        def activation(j, src):
            parts = []
            for half in range(2):
                c0 = half * D_FF + j * FF_CHUNK
                sc = 1.0 if half == 0 else 0.5
                cw = cw_ref[:, c0:c0 + FF_CHUNK] * sc
                u = src[half]
                parts.append(_shift_rows(u, -1) * cw[0:1] + _shift_rows(u, 0) * cw[1:2]
                             + _shift_rows(u, 1) * cw[2:3] + cb_ref[:, c0:c0 + FF_CHUNK] * sc)
            gt = parts[0]
            z = gt * (gt * gt * (GELU_C * GELU_K) + GELU_K)
            return ((gt + gt * jnp.tanh(z)) * parts[1]).astype(BF16)

        up(0, u_a)
        for j in range(FF_NCH):
            cur, nxt = (u_a, u_b) if j % 2 == 0 else (u_b, u_a)
            if j + 1 < FF_NCH:
                up(j + 1, nxt)
            contrib = jnp.dot(activation(j, cur), wd_s[j], preferred_element_type=F32)
            if j == 0:
                acc[...] = contrib
            else:
                acc[...] += contrib
        res = _post(x_ref[...], acc[...], g3_ref[...], mod_ref[5:6, :])
        if split:
            @pl.when(i < NT_P)
            def _():
                rest[0][...] = res

            @pl.when(i >= NT_P)
            def _():
                rest[1][...] = res
        else:
            rest[0][...] = res


def _ffn(x, mod, g2, g3, w_up, cw, cb, w_down, layer, split):
    const = lambda s: (0, 0)
    tile = lambda s: jnp.maximum(s - FF_NCH, 0)
    chunk = lambda s: jnp.minimum(s, FF_NCH - 1)
    shifted = lambda spec: pl.BlockSpec(spec.block_shape, lambda s, f=spec.index_map: f(tile(s)))
    if split:
        out_specs = [shifted(sp) for sp in _split_specs(D_MODEL)]
        out_shape = [jax.ShapeDtypeStruct((NT_P * TM, D_MODEL), F32), jax.ShapeDtypeStruct((NT_S * TM, D_MODEL), F32)]
    else:
        out_specs = pl.BlockSpec((TM, D_MODEL), lambda s: (tile(s), 0))
        out_shape = jax.ShapeDtypeStruct((N_TOK, D_MODEL), F32)
    return pl.pallas_call(
        functools.partial(_ffn_kernel, split=split),
        grid=(FF_NCH + NT,),
        in_specs=[shifted(sp) for sp in _halo_specs(D_MODEL)] + [
            pl.BlockSpec((None, 6, D_MODEL), lambda s: (_mod_row(tile(s)), 0, 0)),
            pl.BlockSpec((1, D_MODEL), const),
            pl.BlockSpec((1, D_MODEL), const),
            pl.BlockSpec((None, D_MODEL, FF_CHUNK), lambda s: (layer, 0, chunk(s))),
            pl.BlockSpec((None, D_MODEL, FF_CHUNK), lambda s: (layer, 0, FF_NCH + chunk(s))),
            pl.BlockSpec((None, FF_CHUNK, D_MODEL), lambda s: (layer, chunk(s), 0)),
            pl.BlockSpec((3, 2 * D_FF), const),
            pl.BlockSpec((1, 2 * D_FF), const),
        ],
        out_specs=out_specs,
        out_shape=out_shape,
        scratch_shapes=[
            pltpu.VMEM((ROWS, D_MODEL), BF16),
            pltpu.VMEM((TM, D_MODEL), F32),
            pltpu.VMEM((2, ROWS, FF_CHUNK), F32),
            pltpu.VMEM((2, ROWS, FF_CHUNK), F32),
            pltpu.VMEM((2 * FF_NCH, D_MODEL, FF_CHUNK), BF16),
            pltpu.VMEM((FF_NCH, FF_CHUNK, D_MODEL), BF16),
        ],
        compiler_params=_params(),
        name="conv_ffn",
    )(x, x, x, mod, g2, g3, w_up, w_up, w_down, cw, cb)


def _softplus(z):
    return jnp.maximum(z, 0.0) + jnp.log(1.0 + jnp.exp(-jnp.abs(z)))


def _lru_scan(a_ref, b_ref, hs_ref, carry, reverse):
    h = carry[...]
    for r in (range(TM - 1, -1, -1) if reverse else range(TM)):
        h = a_ref[r:r + 1, :] * h + b_ref[r:r + 1, :]
        hs_ref[r:r + 1, :] = h
    carry[...] = h
    return h


def _cd_in_kernel(x_ref, xp_ref, xn_ref, mod_ref, g_ref, w_ref, scw_ref, cw_ref, cb_ref, wg_ref, bg_ref,
                  lam_ref, h0_ref, yc_ref, gate_ref, hsf_ref, ar_ref, br_ref, fin_ref,
                  hbuf, a_scr, b_scr, carry):
    i = pl.program_id(0)
    pos, _ = _seq_pos(i)
    _fill_hbuf(hbuf, x_ref, xp_ref, xn_ref, g_ref[...], mod_ref[0:1, :], mod_ref[1:2, :], i)

    def col(k, rows):
        return jnp.dot(hbuf[0:rows, :], w_ref[:, k * HALF:(k + 1) * HALF], preferred_element_type=F32)

    xr = col(3, ROWS)
    xin = col(0, ROWS)
    cg = col(2, ROWS)
    cw = cw_ref[...]
    xc = (_shift_rows(xr, -2) * cw[0:1] + _shift_rows(xr, -1) * cw[1:2] + _shift_rows(xr, 0) * cw[2:3]
          + _shift_rows(xr, 1) * cw[3:4] + cb_ref[...])
    xcb = xc.astype(BF16)

    def direction(d):
        cols = slice(2 * d * HALF, (2 * d + 2) * HALF)
        gates = jax.nn.sigmoid(jnp.dot(xcb, wg_ref[:, cols], preferred_element_type=F32) + bg_ref[:, cols])
        log_a = (-LRU_C) * gates[:, 0:HALF] * _softplus(-lam_ref[d:d + 1, :])
        a = jnp.exp(log_a)
        return a, jnp.sqrt(1.0 - a * a) * (gates[:, HALF:2 * HALF] * xc)

    a, bval = direction(0)
    a_scr[...] = a
    b_scr[...] = bval

    @pl.when(pos == 0)
    def _():
        carry[...] = h0_ref[0:1, :]

    h = _lru_scan(a_scr, b_scr, hsf_ref, carry, False)
    fin_ref[...] = jnp.broadcast_to(h, (8, HALF))

    a, bval = direction(1)
    ar_ref[...] = a
    br_ref[...] = bval
    bg = col(1, TM)
    gb = col(4, TM)
    prod = cg * xin
    scw = scw_ref[...]
    yc = bg * (_shift_rows(prod, -1) * scw[0:1] + _shift_rows(prod, 0) * scw[1:2]
               + _shift_rows(prod, 1) * scw[2:3])
    yc_ref[...] = yc.astype(BF16)
    gate_ref[...] = jax.nn.gelu(gb).astype(BF16)


def _cd_in(x, mod, g, w_in, sc_w, conv_w, conv_b, w_gates, b_gates, lru_lam, h0t):
    const = lambda i: (0, 0)
    tok = lambda dt: jax.ShapeDtypeStruct((N_TOK, HALF), dt)
    row = pl.BlockSpec((TM, HALF), lambda i: (i, 0))
    return pl.pallas_call(
        _cd_in_kernel,
        grid=(NT,),
        in_specs=_halo_specs(D_MODEL) + [
            pl.BlockSpec((None, 6, D_MODEL), lambda i: (_mod_row(i), 0, 0)),
            pl.BlockSpec((1, D_MODEL), const),
            pl.BlockSpec((D_MODEL, 5 * HALF), const),
            pl.BlockSpec((3, HALF), const),
            pl.BlockSpec((4, HALF), const),
            pl.BlockSpec((1, HALF), const),
            pl.BlockSpec((HALF, 4 * HALF), const),
            pl.BlockSpec((1, 4 * HALF), const),
            pl.BlockSpec((2, HALF), const),
            pl.BlockSpec((None, 2, HALF), lambda i: (_mod_row(i), 0, 0)),
        ],
        out_specs=[row, row, row, row, row, pl.BlockSpec((8, HALF), lambda i: (i, 0))],
        out_shape=[tok(BF16), tok(BF16), tok(F32), tok(F32), tok(F32), jax.ShapeDtypeStruct((NT * 8, HALF), F32)],
        scratch_shapes=[
            pltpu.VMEM((ROWS, D_MODEL), BF16),
            pltpu.VMEM((TM, HALF), F32),
            pltpu.VMEM((TM, HALF), F32),
            pltpu.VMEM((1, HALF), F32),
        ],
        compiler_params=_params(),
        name="cd_in",
    )(x, x, x, mod, g, w_in, sc_w, conv_w, conv_b, w_gates, b_gates, lru_lam, h0t)


def _cd_out_kernel(ar_ref, br_ref, hsf_ref, gate_ref, yc_ref, x_ref, mod_ref, g_ref, wo_ref, h0_ref,
                   o_ref, fin_ref, hs_scr, carry):
    ti = NT - 1 - pl.program_id(0)
    pos, n = _seq_pos(ti)

    @pl.when(pos == n - 1)
    def _():
        carry[...] = h0_ref[1:2, :]

    h = _lru_scan(ar_ref, br_ref, hs_scr, carry, True)
    fin_ref[...] = jnp.broadcast_to(h, (8, HALF))
    out_c = jnp.dot(yc_ref[...], wo_ref[0:HALF, :], preferred_element_type=F32)
    yd = (hsf_ref[...] + hs_scr[...]) * gate_ref[...].astype(F32)
    out = out_c + jnp.dot(yd.astype(BF16), wo_ref[HALF:2 * HALF, :], preferred_element_type=F32)
    o_ref[...] = _post(x_ref[...], out, g_ref[...], mod_ref[2:3, :])


def _cd_out(a_r, b_r, hs_f, gate, yc, x, mod, g, w_out, h0t):
    const = lambda i: (0, 0)
    rev = lambda i: (NT - 1 - i, 0)
    row = pl.BlockSpec((TM, HALF), rev)
    return pl.pallas_call(
        _cd_out_kernel,
        grid=(NT,),
        in_specs=[
            row, row, row, row, row,
            pl.BlockSpec((TM, D_MODEL), rev),
            pl.BlockSpec((None, 6, D_MODEL), lambda i: (_mod_row(NT - 1 - i), 0, 0)),
            pl.BlockSpec((1, D_MODEL), const),
            pl.BlockSpec((D_MODEL, D_MODEL), const),
            pl.BlockSpec((None, 2, HALF), lambda i: (_mod_row(NT - 1 - i), 0, 0)),
        ],
        out_specs=[
            pl.BlockSpec((TM, D_MODEL), rev),
            pl.BlockSpec((8, HALF), rev),
        ],
        out_shape=[
            jax.ShapeDtypeStruct((N_TOK, D_MODEL), F32),
            jax.ShapeDtypeStruct((NT * 8, HALF), F32),
        ],
        scratch_shapes=[
            pltpu.VMEM((TM, HALF), F32),
            pltpu.VMEM((1, HALF), F32),
        ],
        compiler_params=_params(),
        name="cd_out",
    )(a_r, b_r, hs_f, gate, yc, x, mod, g, w_out, h0t)


def kernel(x_prompt, x_sample, cache_attn_k, cache_attn_v, state_s5, state_rglru, c, c_ctx, w_mod, b_mod, norm_g, w_in_ab, w_out_ab, s5_lam_re, s5_lam_im, s5_log_dt, s5_b, s5_c, s5_d, s5_w_glu, s5_b_glu, da_lam, da_g, w_in_cd, w_out_cd, sc_conv_w, lru_conv_w, lru_conv_b, lru_w_a, lru_b_a, lru_w_x, lru_b_x, lru_lam, ffn_w_up, ffn_conv_w, ffn_conv_b, ffn_w_down):
    assert DEPTH == 2
    xp = x_prompt.reshape(NT_P * TM, D_MODEL)
    xs = x_sample.reshape(NT_S * TM, D_MODEL)
    cv8 = jnp.zeros((8, D_MODEL), F32).at[0].set(c_ctx).at[1:1 + DEC_BATCH].set(c)
    mod = _modulation(cv8, w_mod, b_mod)
    cos_t, sin_t = _rope_tables()
    g = norm_g.reshape(DEPTH, 4, 1, D_MODEL)

    lam_init = 0.8 - 0.6 * math.exp(-0.3 * 0)
    u, qkv, k32, v32 = _ab_in(xp, xs, mod[0], g[0, 0], w_in_ab[0].astype(BF16), cos_t, sin_t)
    m1, m2, m3, coef = _s5_params(s5_lam_re[0], s5_lam_im[0], s5_log_dt[0], s5_b[0], s5_c[0])
    ys5_p, ys5_s, new_s5 = _s5(u, (m1, m2, m3), coef, state_s5[:, 0], s5_d[0])
    yb_p = _attention(qkv, 0, BATCH, 1, SEQ, None, da_lam[0], da_g[0], lam_init)
    ctx = (cache_attn_k[:, 0].reshape(DEC_BATCH, PAST_LEN, HALF).astype(BF16),
           cache_attn_v[:, 0].reshape(DEC_BATCH, PAST_LEN, HALF).astype(BF16))
    yb_s = _attention(qkv, NT_P, DEC_BATCH, TPS, DEC_SEQ, ctx, da_lam[0], da_g[0], lam_init)
    x = _ab_out(ys5_p, ys5_s, yb_p, yb_s, xp, xs, mod[0], g[0, 1], s5_w_glu[0].astype(BF16), s5_b_glu[0].reshape(1, HALF),
                w_out_ab[0].astype(BF16))
    assert NT_P == BATCH and TM == SEQ
    new_k = k32.reshape(BATCH, 1, DA_HEADS, 2, DA_DK, SEQ).transpose(0, 1, 5, 2, 3, 4)
    new_v = v32
    x = _ffn(x, mod[0], g[0, 2], g[0, 3], ffn_w_up, ffn_conv_w[0], ffn_conv_b[0].reshape(1, 2 * D_FF), ffn_w_down,
             0, False)

    eye = jnp.eye(LRU_BLOCKS, dtype=F32)
    dense = lambda w: jnp.einsum('kcd,kl->kcld', w, eye).reshape(LRU_WIDTH, LRU_WIDTH)
    w_gates = jnp.concatenate([dense(lru_w_a[0, 0]), dense(lru_w_x[0, 0]),
                               dense(lru_w_a[0, 1]), dense(lru_w_x[0, 1])], axis=1).astype(BF16)
    b_gates = jnp.concatenate([lru_b_a[0, 0], lru_b_x[0, 0], lru_b_a[0, 1], lru_b_x[0, 1]]).reshape(1, 4 * HALF)
    h0t = jnp.zeros((8, 2, HALF), F32).at[1:1 + DEC_BATCH].set(state_rglru[:, 0])
    yc, gate, hs_f, a_r, b_r, fin_f = _cd_in(x, mod[1], g[1, 0], w_in_cd[0].astype(BF16), sc_conv_w[0], lru_conv_w[0],
                                      lru_conv_b[0].reshape(1, HALF), w_gates, b_gates, lru_lam[0], h0t)
    x, fin_r = _cd_out(a_r, b_r, hs_f, gate, yc, x, mod[1], g[1, 1], w_out_cd[0].astype(BF16), h0t)
    tile_row0 = lambda f: f.reshape(NT, 8, HALF)[:NT_P, 0]
    new_lru = jnp.stack([tile_row0(fin_f), tile_row0(fin_r)], axis=1)[:, None]
    yp, ys = _ffn(x, mod[1], g[1, 2], g[1, 3], ffn_w_up, ffn_conv_w[1], ffn_conv_b[1].reshape(1, 2 * D_FF),
                  ffn_w_down, 1, True)
    return (yp.reshape(BATCH, SEQ, D_MODEL), ys.reshape(DEC_BATCH, DEC_SEQ, D_MODEL),
            new_k, new_v, new_s5[:, None], new_lru)
```

```python
import functools
import math

import jax
import jax.numpy as jnp
from jax import lax
from jax.experimental import pallas as pl
from jax.experimental.pallas import tpu as pltpu

D_MODEL = 1024
BATCH = 32
SEQ = 256
DEPTH = 2
DEC_BATCH = 4
DEC_SEQ = 2048
PAST_LEN = 512
GRID_W = 64
HALF = D_MODEL // 2
S5_P = 16
S5_G = HALF // S5_P
S5_N = 64
DA_DK = 64
DA_DV = 2 * DA_DK
DA_HEADS = HALF // DA_DV
ROPE_THETA = 10000.0
ROPE_F = DA_DK // 4
LRU_WIDTH = HALF
LRU_BLOCKS = 8
LRU_BS = LRU_WIDTH // LRU_BLOCKS
LRU_C = 8.0
D_FF = 2816
EPS = 1e-6

F32 = jnp.float32
BF16 = jnp.bfloat16
HIGHEST = lax.Precision.HIGHEST

LANES = 128
SUBLANES = 8
VMEM_BYTES = 64 * 1024 * 1024

TM = 256
NT_P = BATCH * SEQ // TM
TPS = DEC_SEQ // TM
NT_S = DEC_BATCH * TPS
NT = NT_P + NT_S
N_TOK = NT * TM
TW = 2 * TM
NW_P = NT_P * TM // TW
WPS = DEC_SEQ // TW
NW = N_TOK // TW
HALO = 8
ROWS = TM + 2 * HALO
S5_T = 8
S5_GB = LANES // S5_P
S5_NB = S5_G // S5_GB
S5_K = S5_T * LANES
S5_ST = 4 * S5_N
S5_GS = 4
FF_CHUNK = 256
FF_NCH = D_FF // FF_CHUNK
Q_SCALE = math.log2(math.e) / math.sqrt(DA_DK)
GELU_K = math.sqrt(2.0 / math.pi)
GELU_C = 0.044715
VMEM_LIMIT = VMEM_BYTES - 8 * 1024 * 1024


def _mod_row(i, ntp=NT_P, tps=TPS):
    return jnp.where(i < ntp, 0, 1 + (i - ntp) // tps)


def _seq_pos(i):
    return jnp.where(i < NT_P, 0, (i - NT_P) % TPS), jnp.where(i < NT_P, 1, TPS)


def _split_specs(width, tm=TM, ntp=NT_P):
    return [
        pl.BlockSpec((tm, width), lambda i: (jnp.minimum(i, ntp - 1), 0)),
        pl.BlockSpec((tm, width), lambda i: (jnp.maximum(i - ntp, 0), 0)),
    ]


def _pick(p_ref, s_ref, ntp=NT_P):
    return jnp.where(pl.program_id(0) < ntp, p_ref[...], s_ref[...])


def _params(sem=("arbitrary",)):
    return pltpu.CompilerParams(dimension_semantics=sem, vmem_limit_bytes=VMEM_LIMIT)


def _rms(x, g):
    ms = jnp.mean(x * x, axis=-1, keepdims=True)
    return x * lax.rsqrt(ms + EPS) * g


def _pre(x, g, shift, scale):
    return _rms(x, g) * (1.0 + scale) + shift


def _mod_kernel(cv_ref, w_ref, b_ref, o_ref):
    cv = cv_ref[...]
    s = cv * jax.nn.sigmoid(cv)
    w = w_ref[...]
    s_hi = s.astype(BF16)
    s_lo = (s - s_hi.astype(F32)).astype(BF16)
    w_hi = w.astype(BF16)
    w_lo = (w - w_hi.astype(F32)).astype(BF16)
    dot = lambda a, b: jnp.dot(a, b, preferred_element_type=F32)
    o_ref[...] = dot(s_hi, w_hi) + (dot(s_hi, w_lo) + dot(s_lo, w_hi)) + b_ref[...]


def _modulation(cv8, w_mod, b_mod):
    nb = 1536
    out = pl.pallas_call(
        _mod_kernel,
        grid=(DEPTH, 6 * D_MODEL // nb),
        in_specs=[
            pl.BlockSpec((8, D_MODEL), lambda l, j: (0, 0)),
            pl.BlockSpec((None, D_MODEL, nb), lambda l, j: (l, 0, j)),
            pl.BlockSpec((None, 1, nb), lambda l, j: (l, 0, j)),
        ],
        out_specs=pl.BlockSpec((None, 8, nb), lambda l, j: (l, 0, j)),
        out_shape=jax.ShapeDtypeStruct((DEPTH, 8, 6 * D_MODEL), F32),
        compiler_params=_params(("arbitrary", "arbitrary")),
        name="modulation",
    )(cv8, w_mod, b_mod.reshape(DEPTH, 1, 6 * D_MODEL))
    return out.reshape(DEPTH, 8, 6, D_MODEL)


def _rope_partner(x):
    lane = lax.broadcasted_iota(jnp.int32, (1, HALF), 1)
    first = (lane % (2 * ROPE_F)) < ROPE_F
    return jnp.where(first, pltpu.roll(x, HALF - ROPE_F, axis=1), pltpu.roll(x, ROPE_F, axis=1))


def _ab_in_kernel(xp_ref, xs_ref, mod_ref, g_ref, w_ref, cos_ref, sin_ref, u_ref, qkv_ref, k_ref, v_ref):
    i = pl.program_id(0)
    h = _pre(_pick(xp_ref, xs_ref, NW_P), g_ref[...], mod_ref[0:1, :], mod_ref[1:2, :])
    proj = jnp.dot(h.astype(BF16), w_ref[...], preferred_element_type=F32)
    u_ref[...] = proj[:, 0:HALF]
    q = proj[:, HALF:2 * HALF]
    k = proj[:, 2 * HALF:3 * HALF]
    v = proj[:, 3 * HALF:4 * HALF]
    cos = jnp.concatenate([cos_ref[...]] * DA_HEADS, axis=1)
    sin = jnp.concatenate([sin_ref[...]] * DA_HEADS, axis=1)
    qr = q * cos + _rope_partner(q) * sin
    kr = k * cos + _rope_partner(k) * sin
    qkv_ref[:, 0:HALF] = (qr * Q_SCALE).astype(BF16)
    qkv_ref[:, HALF:2 * HALF] = kr.astype(BF16)
    qkv_ref[:, 2 * HALF:3 * HALF] = v.astype(BF16)

    @pl.when(i < NW_P)
    def _():
        kt = k.T
        for s in range(TW // SEQ):
            k_ref[s] = kt[:, s * SEQ:(s + 1) * SEQ]
            v_ref[s] = v[s * SEQ:(s + 1) * SEQ].reshape(SEQ, DA_HEADS, DA_DV)


def _ab_in(xp, xs, mod, g, w, cos_t, sin_t):
    def rope_tile(i):
        return jnp.where(i < NW_P, 0, 1 + (i - NW_P) % WPS)

    per_tile = TW // SEQ
    return pl.pallas_call(
        _ab_in_kernel,
        grid=(NW,),
        in_specs=_split_specs(D_MODEL, TW, NW_P) + [
            pl.BlockSpec((None, 6, D_MODEL), lambda i: (_mod_row(i, NW_P, WPS), 0, 0)),
            pl.BlockSpec((1, D_MODEL), lambda i: (0, 0)),
            pl.BlockSpec((D_MODEL, 4 * HALF), lambda i: (0, 0)),
            pl.BlockSpec((TW, DA_DV), lambda i: (rope_tile(i), 0)),
            pl.BlockSpec((TW, DA_DV), lambda i: (rope_tile(i), 0)),
        ],
        out_specs=[
            pl.BlockSpec((TW, HALF), lambda i: (i, 0)),
            pl.BlockSpec((TW, 3 * HALF), lambda i: (i, 0)),
            pl.BlockSpec((per_tile, HALF, SEQ), lambda i: (jnp.minimum(i, NW_P - 1), 0, 0)),
            pl.BlockSpec((per_tile, None, SEQ, DA_HEADS, DA_DV), lambda i: (jnp.minimum(i, NW_P - 1), 0, 0, 0, 0)),
        ],
        out_shape=[
            jax.ShapeDtypeStruct((N_TOK, HALF), F32),
            jax.ShapeDtypeStruct((N_TOK, 3 * HALF), BF16),
            jax.ShapeDtypeStruct((BATCH, HALF, SEQ), F32),
            jax.ShapeDtypeStruct((BATCH, 1, SEQ, DA_HEADS, DA_DV), F32),
        ],
        compiler_params=_params(),
        name="ab_in",
    )(xp, xs, mod, g, w, cos_t, sin_t)


def _rope_tables():
    pos = jnp.arange(DEC_SEQ)
    t_row = (pos // GRID_W).astype(F32)
    t_col = (pos % GRID_W).astype(F32)
    inv = ROPE_THETA ** (-jnp.arange(ROPE_F, dtype=F32) / ROPE_F)
    ang = jnp.stack([t_row[:, None] * inv, t_col[:, None] * inv], axis=1)
    cos, sin = jnp.cos(ang), jnp.sin(ang)
    cos64 = jnp.stack([cos, cos], axis=2).reshape(DEC_SEQ, DA_DK)
    sin64 = jnp.stack([-sin, sin], axis=2).reshape(DEC_SEQ, DA_DK)
    cos_t = jnp.concatenate([jnp.ones((TW, DA_DV), F32), jnp.tile(cos64, (1, 2))], axis=0)
    sin_t = jnp.concatenate([jnp.zeros((TW, DA_DV), F32), jnp.tile(sin64, (1, 2))], axis=0)
    return cos_t, sin_t


def _s5_param_kernel(lre_ref, lim_ref, ldt_ref, bt_ref, c_ref, m1_ref, m2_ref, m3_ref, at_ref):
    nt = (((1,), (1,)), ((), ()))
    e_tab, f_tab, k_tab = [], [], []
    r_i = lax.broadcasted_iota(jnp.int32, (LANES, LANES), 0) // S5_P
    c_i = lax.broadcasted_iota(jnp.int32, (LANES, LANES), 1) // S5_P
    same_group = r_i == c_i
    for d in range(2):
        lre = lre_ref[d]
        lim = lim_ref[d]
        dt = jnp.exp(ldt_ref[d])
        mag = jnp.exp(lre * dt)
        ang = lim * dt
        ar = mag * jnp.cos(ang)
        ai = mag * jnp.sin(ang)
        den = lre * lre + lim * lim
        nr = ar - 1.0
        fr = (nr * lre + ai * lim) / den
        fi = (ai * lre - nr * lim) / den
        btr = bt_ref[d, 0]
        bti = bt_ref[d, 1]
        bfr = fr * btr - fi * bti
        bfi = fr * bti + fi * btr
        cr = c_ref[d, 0]
        ci = c_ref[d, 1]
        cst = jnp.concatenate([cr, -ci], axis=1)
        pr = jnp.ones_like(ar)
        pi = jnp.zeros_like(ar)
        e_d, f_d, k_d = [], [], []
        for tau in range(S5_T + 1):
            f_d.append((cr * pr - ci * pi, cr * pi + ci * pr))
            if tau < S5_T:
                er, ei = pr * bfr - pi * bfi, pr * bfi + pi * bfr
                e_d.append((er, ei))
                full = lax.dot_general(jnp.concatenate([er, ei], axis=1), cst, nt,
                                       precision=HIGHEST, preferred_element_type=F32)
                k_d.append(jnp.where(same_group, full, 0.0))
            else:
                at_ref[d, 0] = pr
                at_ref[d, 1] = pi
            pr, pi = pr * ar - pi * ai, pr * ai + pi * ar
        e_tab.append(e_d)
        f_tab.append(f_d)
        k_tab.append(k_d)

    for s in range(S5_T):
        for t in range(S5_T):
            blk = k_tab[0][t - s] if t >= s else k_tab[1][s - t]
            if t == s:
                blk = blk + k_tab[1][0]
            m1_ref[s * LANES:(s + 1) * LANES, t * LANES:(t + 1) * LANES] = blk.astype(BF16)
    m2_ref[...] = jnp.zeros(m2_ref.shape, BF16)
    m3_ref[...] = jnp.zeros(m3_ref.shape, BF16)
    for s in range(S5_T):
        ef, er_ = e_tab[0][S5_T - 1 - s], e_tab[1][s]
        v = jnp.concatenate([ef[0], er_[0], ef[1], er_[1]], axis=1).astype(BF16)
        ff, fr_ = f_tab[0][s + 1], f_tab[1][S5_T - s]
        w = jnp.concatenate([ff[0], fr_[0], -ff[1], -fr_[1]], axis=1).astype(BF16)
        for gi in range(S5_GB):
            rows = slice(s * LANES + gi * S5_P, s * LANES + (gi + 1) * S5_P)
            cols = slice(gi * S5_ST, (gi + 1) * S5_ST)
            m2_ref[rows, cols] = v[gi * S5_P:(gi + 1) * S5_P]
            m3_ref[rows, cols] = w[gi * S5_P:(gi + 1) * S5_P]


def _s5_params(lam_re, lam_im, log_dt, b_w, c_w):
    rows = S5_G * S5_P
    per_row = lambda a: jnp.repeat(a, S5_P, axis=1)
    lre = per_row(lam_re)
    lim = per_row(lam_im)
    ldt = per_row(jnp.broadcast_to(log_dt[:, :, None], (2, S5_G, S5_N)))
    bt = b_w.transpose(0, 1, 2, 4, 3).reshape(2, 2, rows, S5_N)
    cc = c_w.reshape(2, 2, rows, S5_N)
    vec = pl.BlockSpec((2, LANES, S5_N), lambda j: (0, j, 0))
    mat = pl.BlockSpec((2, 2, LANES, S5_N), lambda j: (0, 0, j, 0))
    wide = lambda n: pl.BlockSpec((None, S5_K, n), lambda j: (j, 0, 0))
    m1, m2, m3, at = pl.pallas_call(
        _s5_param_kernel,
        grid=(S5_NB,),
        in_specs=[vec, vec, vec, mat, mat],
        out_specs=[wide(S5_K), wide(S5_GB * S5_ST), wide(S5_GB * S5_ST),
                   pl.BlockSpec((None, 2, 2, LANES, S5_N), lambda j: (j, 0, 0, 0, 0))],
        out_shape=[
            jax.ShapeDtypeStruct((S5_NB, S5_K, S5_K), BF16),
            jax.ShapeDtypeStruct((S5_NB, S5_K, S5_GB * S5_ST), BF16),
            jax.ShapeDtypeStruct((S5_NB, S5_K, S5_GB * S5_ST), BF16),
            jax.ShapeDtypeStruct((S5_NB, 2, 2, LANES, S5_N), F32),
        ],
        compiler_params=_params(),
        name="s5_params",
    )(lre, lim, ldt, bt, cc)
    coef = at[:, :, :, ::S5_P, :].transpose(0, 3, 2, 1, 4).reshape(S5_NB, S5_GB, 2, 2 * S5_N)
    return m1, m2, m3, coef


def _s5_kernel(u_ref, m1_ref, m2_ref, m3_ref, coef_ref, d_ref, *rest, stride, nc, two_pass):
    if two_pass:
        h0_ref, y_ref, lhs, yacc, s_scr, hp_scr = rest
    else:
        y_ref, fin_ref, lhs, yacc, s_scr, hp_scr = rest
    ln = 2 * S5_N
    nt = (((1,), (1,)), ((), ()))

    def seq_rows(k, c):
        return pl.ds(pl.multiple_of(k * stride + c * S5_T, S5_T), S5_T)

    def gather(c2, carry):
        halves = []
        for c in (2 * c2, 2 * c2 + 1):
            tiles = [u_ref[seq_rows(k, c), :] for k in range(8)]
            halves.append(jnp.stack(tiles, axis=0).reshape(8, S5_K))
        lhs[pl.ds(pl.multiple_of(c2 * 16, 16), 16), :] = jnp.concatenate(halves, axis=0).astype(BF16)
        return carry

    lax.fori_loop(0, nc // 2, gather, 0)
    lb = lhs[...]
    yacc[...] = jnp.dot(lb, m1_ref[...], preferred_element_type=F32)

    lane = lax.broadcasted_iota(jnp.int32, (8, ln), 1)
    row = lax.broadcasted_iota(jnp.int32, (8, ln), 0)
    fwd = lane < S5_N
    odd = (row % 2) == 1
    wide = S5_GS * S5_ST
    for gq in range(S5_GB // S5_GS):
        groups = range(gq * S5_GS, (gq + 1) * S5_GS)
        cols = slice(gq * wide, (gq + 1) * wide)
        s_scr[...] = jnp.dot(lb, m2_ref[:, cols], preferred_element_type=F32)
        are = [jnp.broadcast_to(coef_ref[g, 0:1, :], (8, ln)) for g in groups]
        aim = [jnp.broadcast_to(coef_ref[g, 1:2, :], (8, ln)) for g in groups]

        def scan(init, store):
            def body(c, carry):
                kf = pl.multiple_of(c * 8, 8)
                kr = pl.multiple_of((nc - 1 - c) * 8, 8)
                out = []
                for j in range(S5_GS):
                    p, q = carry[2 * j], carry[2 * j + 1]
                    o = j * S5_ST
                    if store:
                        hp_scr[pl.ds(kf, 8), o:o + S5_N] = p[:, 0:S5_N]
                        hp_scr[pl.ds(kr, 8), o + S5_N:o + ln] = p[:, S5_N:ln]
                        hp_scr[pl.ds(kf, 8), o + ln:o + ln + S5_N] = q[:, 0:S5_N]
                        hp_scr[pl.ds(kr, 8), o + ln + S5_N:o + 2 * ln] = q[:, S5_N:ln]
                    s_re = jnp.where(fwd, s_scr[pl.ds(kf, 8), o:o + ln], s_scr[pl.ds(kr, 8), o:o + ln])
                    s_im = jnp.where(fwd, s_scr[pl.ds(kf, 8), o + ln:o + 2 * ln],
                                     s_scr[pl.ds(kr, 8), o + ln:o + 2 * ln])
                    out += [are[j] * p - aim[j] * q + s_re, are[j] * q + aim[j] * p + s_im]
                return tuple(out)
            return lax.fori_loop(0, nc, body, tuple(init))

        if two_pass:
            h0 = [h0_ref[g, r] for g in groups for r in range(2)]
            fin = scan(h0, False)
            seed = [h + jnp.where(fwd & odd, pltpu.roll(f, 1, axis=0), 0.0)
                    + jnp.where(jnp.logical_not(fwd | odd), pltpu.roll(f, 7, axis=0), 0.0)
                    for h, f in zip(h0, fin)]
            scan(seed, True)
        else:
            fin = scan([jnp.zeros((8, ln), F32)] * (2 * S5_GS), True)
            for j, g in enumerate(groups):
                fin_ref[g, 0] = fin[2 * j]
                fin_ref[g, 1] = fin[2 * j + 1]
        yacc[...] += lax.dot_general(hp_scr[...].astype(BF16), m3_ref[:, cols], nt, preferred_element_type=F32)

    dvec = d_ref[...]

    def scatter(c, carry):
        blk = yacc[pl.ds(pl.multiple_of(c * 8, 8), 8), :].reshape(8, S5_T, LANES)
        for k in range(8):
            rows = seq_rows(k, c)
            y_ref[rows, :] = blk[k] + u_ref[rows, :] * dvec
        return carry

    lax.fori_loop(0, nc, scatter, 0)


def _s5_call(u, row_block, mats, coef, dvec, h0, stride, nc, parts):
    m1, m2, m3 = mats
    rows = 8 * stride
    two_pass = h0 is not None
    mat = lambda n: pl.BlockSpec((None, S5_K, n), lambda j, i: (j, 0, 0))
    tok = pl.BlockSpec((rows, LANES), lambda j, i: (row_block * parts + i, j))
    in_specs = [
        tok, mat(S5_K), mat(S5_GB * S5_ST), mat(S5_GB * S5_ST),
        pl.BlockSpec((None, S5_GB, 2, 2 * S5_N), lambda j, i: (j, 0, 0, 0)),
        pl.BlockSpec((None, 1, LANES), lambda j, i: (j, 0, 0)),
    ]
    args = [u, m1, m2, m3, coef, dvec]
    out_specs = [pl.BlockSpec((rows, LANES), lambda j, i: (i, j))]
    out_shape = [jax.ShapeDtypeStruct((parts * rows, HALF), F32)]
    if two_pass:
        in_specs.append(pl.BlockSpec((None, S5_GB, 2, 8, 2 * S5_N), lambda j, i: (j, 0, 0, 0, 0)))
        args.append(h0)
    else:
        out_specs.append(pl.BlockSpec((None, S5_GB, 2, 8, 2 * S5_N), lambda j, i: (j, 0, 0, i, 0)))
        out_shape.append(jax.ShapeDtypeStruct((S5_NB, S5_GB, 2, parts * 8, 2 * S5_N), F32))
    return pl.pallas_call(
        functools.partial(_s5_kernel, stride=stride, nc=nc, two_pass=two_pass),
        grid=(S5_NB, parts),
        in_specs=in_specs,
        out_specs=out_specs,
        out_shape=out_shape,
        scratch_shapes=[
            pltpu.VMEM((nc * 8, S5_K), BF16),
            pltpu.VMEM((nc * 8, S5_K), F32),
            pltpu.VMEM((nc * 8, S5_GS * S5_ST), F32),
            pltpu.VMEM((nc * 8, S5_GS * S5_ST), F32),
        ],
        compiler_params=_params(("arbitrary", "arbitrary")),
        name="s5_%d" % stride,
    )(*args)


def _s5(u, mats, coef, h0, s5_d):
    dvec = s5_d.reshape(S5_NB, 1, LANES)
    yp, fin = _s5_call(u, 0, mats, coef, dvec, None, SEQ, SEQ // S5_T, BATCH // 8)
    hh = h0.transpose(3, 2, 0, 1, 4)
    z = jnp.zeros_like(hh[:, :, :, 0])
    first = jnp.concatenate([hh[:, :, :, 0], z], axis=-1)
    second = jnp.concatenate([z, hh[:, :, :, 1]], axis=-1)
    h0g = jnp.stack([first, second], axis=3).reshape(S5_NB, S5_GB, 2, 2 * DEC_BATCH, 2 * S5_N)
    half = DEC_SEQ // 2
    (ys,) = _s5_call(u, 1, mats, coef, dvec, h0g, half, half // S5_T, 1)
    new_s5 = fin.reshape(S5_G, 2, BATCH, 2, S5_N).transpose(2, 3, 1, 0, 4)
    return yp, ys, new_s5


def _attn_kernel(*refs, lam_init, has_ctx, lk):
    if has_ctx:
        q_ref, k_ref, v_ref, kc_ref, vc_ref, dl_ref, g_ref, o_ref, s_a, s_b, vx, vcx = refs
    else:
        q_ref, k_ref, v_ref, dl_ref, g_ref, o_ref, s_a, s_b, vx = refs
    lt = lk + (PAST_LEN if has_ctx else 0)
    dl = dl_ref[...]
    lam = (jnp.exp(jnp.sum(dl[0:1] * dl[1:2], keepdims=True))
           - jnp.exp(jnp.sum(dl[2:3] * dl[3:4], keepdims=True)) + lam_init)
    lane = lax.broadcasted_iota(jnp.int32, (1, DA_DV), 1)
    first = lane < DA_DK
    g = g_ref[...]
    nt = (((1,), (1,)), ((), ()))

    @pl.when(pl.program_id(1) == 0)
    def _():
        onehot = (lane == 0).astype(BF16)
        for h in range(DA_HEADS):
            cols = slice(h * DA_DV, (h + 1) * DA_DV)
            vx[:, 2 * h * DA_DV:(2 * h + 1) * DA_DV] = v_ref[:, cols]
            vx[:, (2 * h + 1) * DA_DV:(2 * h + 2) * DA_DV] = jnp.broadcast_to(onehot, (lk, DA_DV))
            if has_ctx:
                vcx[:, 2 * h * DA_DV:(2 * h + 1) * DA_DV] = vc_ref[:, cols]
                vcx[:, (2 * h + 1) * DA_DV:(2 * h + 2) * DA_DV] = jnp.broadcast_to(onehot, (PAST_LEN, DA_DV))

    def scores(h, dst):
        cols = slice(h * DA_DV, (h + 1) * DA_DV)
        qh = q_ref[:, cols]
        zero = jnp.zeros_like(qh)
        for m in range(2):
            qm = jnp.where(first if m == 0 else jnp.logical_not(first), qh, zero)
            dst[m, :, 0:lk] = lax.dot_general(qm, k_ref[:, cols], nt, preferred_element_type=F32)
            if has_ctx:
                dst[m, :, lk:lt] = lax.dot_general(qm, kc_ref[:, cols], nt, preferred_element_type=F32)

    def head(h, src):
        xcols = slice(2 * h * DA_DV, (2 * h + 2) * DA_DV)
        outs = []
        for m in range(2):
            s = src[m]
            e = jnp.exp2(s - jnp.max(s, axis=-1, keepdims=True)).astype(BF16)
            ox = jnp.dot(e[:, 0:lk], vx[:, xcols], preferred_element_type=F32)
            if has_ctx:
                ox = ox + jnp.dot(e[:, lk:lt], vcx[:, xcols], preferred_element_type=F32)
            outs.append(ox[:, 0:DA_DV] * (1.0 / ox[:, DA_DV:DA_DV + 1]))
        o = outs[0] - lam * outs[1]
        o_ref[:, h * DA_DV:(h + 1) * DA_DV] = (_rms(o, g) * (1.0 - lam_init)).astype(BF16)

    scores(0, s_a)
    for h in range(DA_HEADS):
        cur, nxt = (s_a, s_b) if h % 2 == 0 else (s_b, s_a)
        if h + 1 < DA_HEADS:
            scores(h + 1, nxt)
        head(h, cur)


def _attention(qkv, q_tile0, nb, nq, lk, ctx, da_lam, da_g, lam_init):
    kb0 = q_tile0 * TM // lk
    in_specs = [
        pl.BlockSpec((TM, HALF), lambda b, j: (q_tile0 + b * nq + j, 0)),
        pl.BlockSpec((lk, HALF), lambda b, j: (kb0 + b, 1)),
        pl.BlockSpec((lk, HALF), lambda b, j: (kb0 + b, 2)),
    ]
    args = [qkv, qkv, qkv]
    if ctx is not None:
        in_specs += [pl.BlockSpec((None, PAST_LEN, HALF), lambda b, j: (b, 0, 0))] * 2
        args += list(ctx)
    in_specs += [
        pl.BlockSpec((4, DA_DK), lambda b, j: (0, 0)),
        pl.BlockSpec((1, DA_DV), lambda b, j: (0, 0)),
    ]
    lt = lk + (PAST_LEN if ctx is not None else 0)
    scratch = [pltpu.VMEM((2, TM, lt), F32), pltpu.VMEM((2, TM, lt), F32), pltpu.VMEM((lk, 2 * HALF), BF16)]
    if ctx is not None:
        scratch.append(pltpu.VMEM((PAST_LEN, 2 * HALF), BF16))
    return pl.pallas_call(
        functools.partial(_attn_kernel, lam_init=lam_init, has_ctx=ctx is not None, lk=lk),
        grid=(nb, nq),
        in_specs=in_specs,
        out_specs=pl.BlockSpec((TM, HALF), lambda b, j: (b * nq + j, 0)),
        out_shape=jax.ShapeDtypeStruct((nb * nq * TM, HALF), BF16),
        scratch_shapes=scratch,
        compiler_params=_params(("arbitrary", "arbitrary")),
        name="diff_attn_%d" % lk,
    )(*args, da_lam, da_g.reshape(1, DA_DV))


def _post(x, y, g, gate):
    return x + gate * _rms(y, g)


def _ab_out_kernel(ysp_ref, yss_ref, ybp_ref, ybs_ref, xp_ref, xs_ref, mod_ref, g_ref, wg_ref, bg_ref, wo_ref,
                   o_ref):
    ys = jax.nn.gelu(_pick(ysp_ref, yss_ref, NW_P))
    glu = jnp.dot(ys.astype(BF16), wg_ref[...], preferred_element_type=F32) + bg_ref[...]
    ya = ys * jax.nn.sigmoid(glu)
    out = (jnp.dot(ya.astype(BF16), wo_ref[0:HALF, :], preferred_element_type=F32)
           + jnp.dot(_pick(ybp_ref, ybs_ref, NW_P), wo_ref[HALF:2 * HALF, :], preferred_element_type=F32))
    o_ref[...] = _post(_pick(xp_ref, xs_ref, NW_P), out, g_ref[...], mod_ref[2:3, :])


def _ab_out(ys5_p, ys5_s, yb_p, yb_s, xp, xs, mod, g, w_glu, b_glu, w_out):
    return pl.pallas_call(
        _ab_out_kernel,
        grid=(NW,),
        in_specs=_split_specs(HALF, TW, NW_P) + _split_specs(HALF, TW, NW_P) + _split_specs(D_MODEL, TW, NW_P) + [
            pl.BlockSpec((None, 6, D_MODEL), lambda i: (_mod_row(i, NW_P, WPS), 0, 0)),
            pl.BlockSpec((1, D_MODEL), lambda i: (0, 0)),
            pl.BlockSpec((HALF, HALF), lambda i: (0, 0)),
            pl.BlockSpec((1, HALF), lambda i: (0, 0)),
            pl.BlockSpec((D_MODEL, D_MODEL), lambda i: (0, 0)),
        ],
        out_specs=pl.BlockSpec((TW, D_MODEL), lambda i: (i, 0)),
        out_shape=jax.ShapeDtypeStruct((N_TOK, D_MODEL), F32),
        compiler_params=_params(),
        name="ab_out",
    )(ys5_p, ys5_s, yb_p, yb_s, xp, xs, mod, g, w_glu, b_glu, w_out)


def _halo_specs(width):
    blocks = TM // HALO
    last = N_TOK // HALO - 1
    return [
        pl.BlockSpec((TM, width), lambda i: (i, 0)),
        pl.BlockSpec((HALO, width), lambda i: (jnp.maximum(i * blocks - 1, 0), 0)),
        pl.BlockSpec((HALO, width), lambda i: (jnp.minimum((i + 1) * blocks, last), 0)),
    ]


def _fill_hbuf(hbuf, x_ref, xp_ref, xn_ref, g, shift, scale, i):
    pos, n = _seq_pos(i)
    hp = jnp.where(pos > 0, _pre(xp_ref[...], g, shift, scale), 0.0)
    hn = jnp.where(pos < n - 1, _pre(xn_ref[...], g, shift, scale), 0.0)
    hbuf[0:TM, :] = _pre(x_ref[...], g, shift, scale).astype(BF16)
    hbuf[TM:ROWS, :] = jnp.concatenate([hn, hp], axis=0).astype(BF16)


def _shift_rows(x, s):
    if s == 0:
        return x[0:TM]
    return pltpu.roll(x, (-s) % ROWS, axis=0)[0:TM]


def _ffn_kernel(x_ref, xp_ref, xn_ref, mod_ref, g2_ref, g3_ref, wg_ref, wv_ref, wd_ref, cw_ref, cb_ref, *rest, split):
    hbuf, acc, u_a, u_b, wu_s, wd_s = rest[-6:]
    step = pl.program_id(0)

    @pl.when(step < FF_NCH)
    def _():
        wu_s[step] = wg_ref[...].astype(BF16)
        wu_s[FF_NCH + step] = wv_ref[...].astype(BF16)
        wd_s[step] = wd_ref[...].astype(BF16)

    @pl.when(step >= FF_NCH)
    def _():
        i = step - FF_NCH
        _fill_hbuf(hbuf, x_ref, xp_ref, xn_ref, g2_ref[...], mod_ref[3:4, :], mod_ref[4:5, :], i)

        def up(j, dst):
            for half in range(2):
                dst[half] = jnp.dot(hbuf[...], wu_s[half * FF_NCH + j], preferred_element_type=F32)

        def activation(j, src):
            parts = []
            for half in range(2):
                c0 = half * D_FF + j * FF_CHUNK
                sc = 1.0 if half == 0 else 0.5
                cw = cw_ref[:, c0:c0 + FF_CHUNK] * sc
                u = src[half]
                parts.append(_shift_rows(u, -1) * cw[0:1] + _shift_rows(u, 0) * cw[1:2]
                             + _shift_rows(u, 1) * cw[2:3] + cb_ref[:, c0:c0 + FF_CHUNK] * sc)
            gt = parts[0]
            z = gt * (gt * gt * (GELU_C * GELU_K) + GELU_K)
            return ((gt + gt * jnp.tanh(z)) * parts[1]).astype(BF16)

        up(0, u_a)
        for j in range(FF_NCH):
            cur, nxt = (u_a, u_b) if j % 2 == 0 else (u_b, u_a)
            if j + 1 < FF_NCH:
                up(j + 1, nxt)
            contrib = jnp.dot(activation(j, cur), wd_s[j], preferred_element_type=F32)
            if j == 0:
                acc[...] = contrib
            else:
                acc[...] += contrib
        res = _post(x_ref[...], acc[...], g3_ref[...], mod_ref[5:6, :])
        if split:
            @pl.when(i < NT_P)
            def _():
                rest[0][...] = res

            @pl.when(i >= NT_P)
            def _():
                rest[1][...] = res
        else:
            rest[0][...] = res


def _ffn(x, mod, g2, g3, w_up, cw, cb, w_down, layer, split):
    const = lambda s: (0, 0)
    tile = lambda s: jnp.maximum(s - FF_NCH, 0)
    chunk = lambda s: jnp.minimum(s, FF_NCH - 1)
    shifted = lambda spec: pl.BlockSpec(spec.block_shape, lambda s, f=spec.index_map: f(tile(s)))
    if split:
        out_specs = [shifted(sp) for sp in _split_specs(D_MODEL)]
        out_shape = [jax.ShapeDtypeStruct((NT_P * TM, D_MODEL), F32), jax.ShapeDtypeStruct((NT_S * TM, D_MODEL), F32)]
    else:
        out_specs = pl.BlockSpec((TM, D_MODEL), lambda s: (tile(s), 0))
        out_shape = jax.ShapeDtypeStruct((N_TOK, D_MODEL), F32)
    return pl.pallas_call(
        functools.partial(_ffn_kernel, split=split),
        grid=(FF_NCH + NT,),
        in_specs=[shifted(sp) for sp in _halo_specs(D_MODEL)] + [
            pl.BlockSpec((None, 6, D_MODEL), lambda s: (_mod_row(tile(s)), 0, 0)),
            pl.BlockSpec((1, D_MODEL), const),
            pl.BlockSpec((1, D_MODEL), const),
            pl.BlockSpec((None, D_MODEL, FF_CHUNK), lambda s: (layer, 0, chunk(s))),
            pl.BlockSpec((None, D_MODEL, FF_CHUNK), lambda s: (layer, 0, FF_NCH + chunk(s))),
            pl.BlockSpec((None, FF_CHUNK, D_MODEL), lambda s: (layer, chunk(s), 0)),
            pl.BlockSpec((3, 2 * D_FF), const),
            pl.BlockSpec((1, 2 * D_FF), const),
        ],
        out_specs=out_specs,
        out_shape=out_shape,
        scratch_shapes=[
            pltpu.VMEM((ROWS, D_MODEL), BF16),
            pltpu.VMEM((TM, D_MODEL), F32),
            pltpu.VMEM((2, ROWS, FF_CHUNK), F32),
            pltpu.VMEM((2, ROWS, FF_CHUNK), F32),
            pltpu.VMEM((2 * FF_NCH, D_MODEL, FF_CHUNK), BF16),
            pltpu.VMEM((FF_NCH, FF_CHUNK, D_MODEL), BF16),
        ],
        compiler_params=_params(),
        name="conv_ffn",
    )(x, x, x, mod, g2, g3, w_up, w_up, w_down, cw, cb)


def _softplus(z):
    return jnp.maximum(z, 0.0) + jnp.log(1.0 + jnp.exp(-jnp.abs(z)))


def _lru_scan(a_ref, b_ref, hs_ref, p_scr, carry, reverse):
    half = TM // 2
    h_a = carry[...]
    h_b = jnp.zeros_like(h_a)
    p_b = jnp.ones_like(h_a)
    for j in range(half):
        r_a = TM - 1 - j if reverse else j
        r_b = half - 1 - j if reverse else half + j
        h_a = a_ref[r_a:r_a + 1, :] * h_a + b_ref[r_a:r_a + 1, :]
        hs_ref[r_a:r_a + 1, :] = h_a
        a_b = a_ref[r_b:r_b + 1, :]
        h_b = a_b * h_b + b_ref[r_b:r_b + 1, :]
        p_b = p_b * a_b
        hs_ref[r_b:r_b + 1, :] = h_b
        p_scr[r_b % half:r_b % half + 1, :] = p_b
    second = slice(0, half) if reverse else slice(half, TM)
    hs_ref[second, :] = hs_ref[second, :] + p_scr[...] * h_a
    h = h_b + p_b * h_a
    carry[...] = h
    return h


def _cd_in_kernel(x_ref, xp_ref, xn_ref, mod_ref, g_ref, w_ref, scw_ref, cw_ref, cb_ref, wg_ref, bg_ref,
                  lam_ref, h0_ref, yc_ref, gate_ref, hsf_ref, ar_ref, br_ref, fin_ref,
                  hbuf, a_scr, b_scr, p_scr, carry):
    i = pl.program_id(0)
    pos, _ = _seq_pos(i)
    _fill_hbuf(hbuf, x_ref, xp_ref, xn_ref, g_ref[...], mod_ref[0:1, :], mod_ref[1:2, :], i)

    def col(k, rows):
        return jnp.dot(hbuf[0:rows, :], w_ref[:, k * HALF:(k + 1) * HALF], preferred_element_type=F32)

    xr = col(3, ROWS)
    xin = col(0, ROWS)
    cg = col(2, ROWS)
    cw = cw_ref[...]
    xc = (_shift_rows(xr, -2) * cw[0:1] + _shift_rows(xr, -1) * cw[1:2] + _shift_rows(xr, 0) * cw[2:3]
          + _shift_rows(xr, 1) * cw[3:4] + cb_ref[...])
    xcb = xc.astype(BF16)

    def direction(d):
        cols = slice(2 * d * HALF, (2 * d + 2) * HALF)
        gates = jax.nn.sigmoid(jnp.dot(xcb, wg_ref[:, cols], preferred_element_type=F32) + bg_ref[:, cols])
        log_a = (-LRU_C) * gates[:, 0:HALF] * _softplus(-lam_ref[d:d + 1, :])
        a = jnp.exp(log_a)
        return a, jnp.sqrt(1.0 - a * a) * (gates[:, HALF:2 * HALF] * xc)

    a, bval = direction(0)
    a_scr[...] = a
    b_scr[...] = bval

    @pl.when(pos == 0)
    def _():
        carry[...] = h0_ref[0:1, :]

    h = _lru_scan(a_scr, b_scr, hsf_ref, p_scr, carry, False)
    fin_ref[...] = jnp.broadcast_to(h, (8, HALF))

    a, bval = direction(1)
    ar_ref[...] = a
    br_ref[...] = bval
    bg = col(1, TM)
    gb = col(4, TM)
    prod = cg * xin
    scw = scw_ref[...]
    yc = bg * (_shift_rows(prod, -1) * scw[0:1] + _shift_rows(prod, 0) * scw[1:2]
               + _shift_rows(prod, 1) * scw[2:3])
    yc_ref[...] = yc.astype(BF16)
    gate_ref[...] = jax.nn.gelu(gb).astype(BF16)


def _cd_in(x, mod, g, w_in, sc_w, conv_w, conv_b, w_gates, b_gates, lru_lam, h0t):
    const = lambda i: (0, 0)
    tok = lambda dt: jax.ShapeDtypeStruct((N_TOK, HALF), dt)
    row = pl.BlockSpec((TM, HALF), lambda i: (i, 0))
    return pl.pallas_call(
        _cd_in_kernel,
        grid=(NT,),
        in_specs=_halo_specs(D_MODEL) + [
            pl.BlockSpec((None, 6, D_MODEL), lambda i: (_mod_row(i), 0, 0)),
            pl.BlockSpec((1, D_MODEL), const),
            pl.BlockSpec((D_MODEL, 5 * HALF), const),
            pl.BlockSpec((3, HALF), const),
            pl.BlockSpec((4, HALF), const),
            pl.BlockSpec((1, HALF), const),
            pl.BlockSpec((HALF, 4 * HALF), const),
            pl.BlockSpec((1, 4 * HALF), const),
            pl.BlockSpec((2, HALF), const),
            pl.BlockSpec((None, 2, HALF), lambda i: (_mod_row(i), 0, 0)),
        ],
        out_specs=[row, row, row, row, row, pl.BlockSpec((8, HALF), lambda i: (i, 0))],
        out_shape=[tok(BF16), tok(BF16), tok(F32), tok(F32), tok(F32), jax.ShapeDtypeStruct((NT * 8, HALF), F32)],
        scratch_shapes=[
            pltpu.VMEM((ROWS, D_MODEL), BF16),
            pltpu.VMEM((TM, HALF), F32),
            pltpu.VMEM((TM, HALF), F32),
            pltpu.VMEM((TM // 2, HALF), F32),
            pltpu.VMEM((1, HALF), F32),
        ],
        compiler_params=_params(),
        name="cd_in",
    )(x, x, x, mod, g, w_in, sc_w, conv_w, conv_b, w_gates, b_gates, lru_lam, h0t)


def _cd_out_kernel(ar_ref, br_ref, hsf_ref, gate_ref, yc_ref, x_ref, mod_ref, g_ref, wo_ref, h0_ref,
                   o_ref, fin_ref, hs_scr, p_scr, carry):
    ti = NT - 1 - pl.program_id(0)
    pos, n = _seq_pos(ti)

    @pl.when(pos == n - 1)
    def _():
        carry[...] = h0_ref[1:2, :]

    h = _lru_scan(ar_ref, br_ref, hs_scr, p_scr, carry, True)
    fin_ref[...] = jnp.broadcast_to(h, (8, HALF))
    out_c = jnp.dot(yc_ref[...], wo_ref[0:HALF, :], preferred_element_type=F32)
    yd = (hsf_ref[...] + hs_scr[...]) * gate_ref[...].astype(F32)
    out = out_c + jnp.dot(yd.astype(BF16), wo_ref[HALF:2 * HALF, :], preferred_element_type=F32)
    o_ref[...] = _post(x_ref[...], out, g_ref[...], mod_ref[2:3, :])


def _cd_out(a_r, b_r, hs_f, gate, yc, x, mod, g, w_out, h0t):
    const = lambda i: (0, 0)
    rev = lambda i: (NT - 1 - i, 0)
    row = pl.BlockSpec((TM, HALF), rev)
    return pl.pallas_call(
        _cd_out_kernel,
        grid=(NT,),
        in_specs=[
            row, row, row, row, row,
            pl.BlockSpec((TM, D_MODEL), rev),
            pl.BlockSpec((None, 6, D_MODEL), lambda i: (_mod_row(NT - 1 - i), 0, 0)),
            pl.BlockSpec((1, D_MODEL), const),
            pl.BlockSpec((D_MODEL, D_MODEL), const),
            pl.BlockSpec((None, 2, HALF), lambda i: (_mod_row(NT - 1 - i), 0, 0)),
        ],
        out_specs=[
            pl.BlockSpec((TM, D_MODEL), rev),
            pl.BlockSpec((8, HALF), rev),
        ],
        out_shape=[
            jax.ShapeDtypeStruct((N_TOK, D_MODEL), F32),
            jax.ShapeDtypeStruct((NT * 8, HALF), F32),
        ],
        scratch_shapes=[
            pltpu.VMEM((TM, HALF), F32),
            pltpu.VMEM((TM // 2, HALF), F32),
            pltpu.VMEM((1, HALF), F32),
        ],
        compiler_params=_params(),
        name="cd_out",
    )(a_r, b_r, hs_f, gate, yc, x, mod, g, w_out, h0t)


def kernel(x_prompt, x_sample, cache_attn_k, cache_attn_v, state_s5, state_rglru, c, c_ctx, w_mod, b_mod, norm_g, w_in_ab, w_out_ab, s5_lam_re, s5_lam_im, s5_log_dt, s5_b, s5_c, s5_d, s5_w_glu, s5_b_glu, da_lam, da_g, w_in_cd, w_out_cd, sc_conv_w, lru_conv_w, lru_conv_b, lru_w_a, lru_b_a, lru_w_x, lru_b_x, lru_lam, ffn_w_up, ffn_conv_w, ffn_conv_b, ffn_w_down):
    assert DEPTH == 2
    xp = x_prompt.reshape(NT_P * TM, D_MODEL)
    xs = x_sample.reshape(NT_S * TM, D_MODEL)
    cv8 = jnp.zeros((8, D_MODEL), F32).at[0].set(c_ctx).at[1:1 + DEC_BATCH].set(c)
    mod = _modulation(cv8, w_mod, b_mod)
    cos_t, sin_t = _rope_tables()
    g = norm_g.reshape(DEPTH, 4, 1, D_MODEL)

    lam_init = 0.8 - 0.6 * math.exp(-0.3 * 0)
    u, qkv, k32, v32 = _ab_in(xp, xs, mod[0], g[0, 0], w_in_ab[0].astype(BF16), cos_t, sin_t)
    m1, m2, m3, coef = _s5_params(s5_lam_re[0], s5_lam_im[0], s5_log_dt[0], s5_b[0], s5_c[0])
    ys5_p, ys5_s, new_s5 = _s5(u, (m1, m2, m3), coef, state_s5[:, 0], s5_d[0])
    yb_p = _attention(qkv, 0, BATCH, 1, SEQ, None, da_lam[0], da_g[0], lam_init)
    ctx = (cache_attn_k[:, 0].reshape(DEC_BATCH, PAST_LEN, HALF).astype(BF16),
           cache_attn_v[:, 0].reshape(DEC_BATCH, PAST_LEN, HALF).astype(BF16))
    yb_s = _attention(qkv, NT_P, DEC_BATCH, TPS, DEC_SEQ, ctx, da_lam[0], da_g[0], lam_init)
    x = _ab_out(ys5_p, ys5_s, yb_p, yb_s, xp, xs, mod[0], g[0, 1], s5_w_glu[0].astype(BF16), s5_b_glu[0].reshape(1, HALF),
                w_out_ab[0].astype(BF16))
    assert NT_P == BATCH and TM == SEQ
    new_k = k32.reshape(BATCH, 1, DA_HEADS, 2, DA_DK, SEQ).transpose(0, 1, 5, 2, 3, 4)
    new_v = v32
    x = _ffn(x, mod[0], g[0, 2], g[0, 3], ffn_w_up, ffn_conv_w[0], ffn_conv_b[0].reshape(1, 2 * D_FF), ffn_w_down,
             0, False)

    eye = jnp.eye(LRU_BLOCKS, dtype=F32)
    dense = lambda w: jnp.einsum('kcd,kl->kcld', w, eye).reshape(LRU_WIDTH, LRU_WIDTH)
    w_gates = jnp.concatenate([dense(lru_w_a[0, 0]), dense(lru_w_x[0, 0]),
                               dense(lru_w_a[0, 1]), dense(lru_w_x[0, 1])], axis=1).astype(BF16)
    b_gates = jnp.concatenate([lru_b_a[0, 0], lru_b_x[0, 0], lru_b_a[0, 1], lru_b_x[0, 1]]).reshape(1, 4 * HALF)
    h0t = jnp.zeros((8, 2, HALF), F32).at[1:1 + DEC_BATCH].set(state_rglru[:, 0])
    yc, gate, hs_f, a_r, b_r, fin_f = _cd_in(x, mod[1], g[1, 0], w_in_cd[0].astype(BF16), sc_conv_w[0], lru_conv_w[0],
                                      lru_conv_b[0].reshape(1, HALF), w_gates, b_gates, lru_lam[0], h0t)
    x, fin_r = _cd_out(a_r, b_r, hs_f, gate, yc, x, mod[1], g[1, 1], w_out_cd[0].astype(BF16), h0t)
    tile_row0 = lambda f: f.reshape(NT, 8, HALF)[:NT_P, 0]
    new_lru = jnp.stack([tile_row0(fin_f), tile_row0(fin_r)], axis=1)[:, None]
    yp, ys = _ffn(x, mod[1], g[1, 2], g[1, 3], ffn_w_up, ffn_conv_w[1], ffn_conv_b[1].reshape(1, 2 * D_FF),
                  ffn_w_down, 1, True)
    return (yp.reshape(BATCH, SEQ, D_MODEL), ys.reshape(DEC_BATCH, DEC_SEQ, D_MODEL),
            new_k, new_v, new_s5[:, None], new_lru)
```

```python
import functools
import math

import jax
import jax.numpy as jnp
from jax import lax
from jax.experimental import pallas as pl
from jax.experimental.pallas import tpu as pltpu

D_MODEL = 1024
BATCH = 32
SEQ = 256
DEPTH = 2
DEC_BATCH = 4
DEC_SEQ = 2048
PAST_LEN = 512
GRID_W = 64
HALF = D_MODEL // 2
S5_P = 16
S5_G = HALF // S5_P
S5_N = 64
DA_DK = 64
DA_DV = 2 * DA_DK
DA_HEADS = HALF // DA_DV
ROPE_THETA = 10000.0
ROPE_F = DA_DK // 4
LRU_WIDTH = HALF
LRU_BLOCKS = 8
LRU_BS = LRU_WIDTH // LRU_BLOCKS
LRU_C = 8.0
D_FF = 2816
EPS = 1e-6

F32 = jnp.float32
BF16 = jnp.bfloat16
HIGHEST = lax.Precision.HIGHEST

LANES = 128
SUBLANES = 8
VMEM_BYTES = 64 * 1024 * 1024

TM = 256
NT_P = BATCH * SEQ // TM
TPS = DEC_SEQ // TM
NT_S = DEC_BATCH * TPS
NT = NT_P + NT_S
N_TOK = NT * TM
TW = 2 * TM
NW_P = NT_P * TM // TW
WPS = DEC_SEQ // TW
NW = N_TOK // TW
HALO = 8
ROWS = TM + 2 * HALO
S5_T = 8
S5_GB = LANES // S5_P
S5_NB = S5_G // S5_GB
S5_K = S5_T * LANES
S5_ST = 4 * S5_N
S5_GS = 4
FF_CHUNK = 256
FF_NCH = D_FF // FF_CHUNK
Q_SCALE = math.log2(math.e) / math.sqrt(DA_DK)
GELU_K = math.sqrt(2.0 / math.pi)
GELU_C = 0.044715
VMEM_LIMIT = VMEM_BYTES - 8 * 1024 * 1024


def _mod_row(i, ntp=NT_P, tps=TPS):
    return jnp.where(i < ntp, 0, 1 + (i - ntp) // tps)


def _seq_pos(i):
    return jnp.where(i < NT_P, 0, (i - NT_P) % TPS), jnp.where(i < NT_P, 1, TPS)


def _split_specs(width, tm=TM, ntp=NT_P):
    return [
        pl.BlockSpec((tm, width), lambda i: (jnp.minimum(i, ntp - 1), 0)),
        pl.BlockSpec((tm, width), lambda i: (jnp.maximum(i - ntp, 0), 0)),
    ]


def _pick(p_ref, s_ref, ntp=NT_P):
    return jnp.where(pl.program_id(0) < ntp, p_ref[...], s_ref[...])


def _params(sem=("arbitrary",)):
    return pltpu.CompilerParams(dimension_semantics=sem, vmem_limit_bytes=VMEM_LIMIT)


def _rms(x, g):
    ms = jnp.mean(x * x, axis=-1, keepdims=True)
    return x * lax.rsqrt(ms + EPS) * g


def _pre(x, g, shift, scale):
    return _rms(x, g) * (1.0 + scale) + shift


def _mod_kernel(cv_ref, w_ref, b_ref, o_ref):
    cv = cv_ref[...]
    s = cv * jax.nn.sigmoid(cv)
    w = w_ref[...]
    s_hi = s.astype(BF16)
    s_lo = (s - s_hi.astype(F32)).astype(BF16)
    w_hi = w.astype(BF16)
    w_lo = (w - w_hi.astype(F32)).astype(BF16)
    dot = lambda a, b: jnp.dot(a, b, preferred_element_type=F32)
    o_ref[...] = dot(s_hi, w_hi) + (dot(s_hi, w_lo) + dot(s_lo, w_hi)) + b_ref[...]


def _modulation(cv8, w_mod, b_mod):
    nb = 1536
    out = pl.pallas_call(
        _mod_kernel,
        grid=(DEPTH, 6 * D_MODEL // nb),
        in_specs=[
            pl.BlockSpec((8, D_MODEL), lambda l, j: (0, 0)),
            pl.BlockSpec((None, D_MODEL, nb), lambda l, j: (l, 0, j)),
            pl.BlockSpec((None, 1, nb), lambda l, j: (l, 0, j)),
        ],
        out_specs=pl.BlockSpec((None, 8, nb), lambda l, j: (l, 0, j)),
        out_shape=jax.ShapeDtypeStruct((DEPTH, 8, 6 * D_MODEL), F32),
        compiler_params=_params(("arbitrary", "arbitrary")),
        name="modulation",
    )(cv8, w_mod, b_mod.reshape(DEPTH, 1, 6 * D_MODEL))
    return out.reshape(DEPTH, 8, 6, D_MODEL)


def _rope_partner(x):
    lane = lax.broadcasted_iota(jnp.int32, (1, HALF), 1)
    first = (lane % (2 * ROPE_F)) < ROPE_F
    return jnp.where(first, pltpu.roll(x, HALF - ROPE_F, axis=1), pltpu.roll(x, ROPE_F, axis=1))


def _ab_in_kernel(xp_ref, xs_ref, mod_ref, g_ref, w_ref, cos_ref, sin_ref, u_ref, qkv_ref, k_ref, v_ref):
    i = pl.program_id(0)
    h = _pre(_pick(xp_ref, xs_ref, NW_P), g_ref[...], mod_ref[0:1, :], mod_ref[1:2, :])
    proj = jnp.dot(h.astype(BF16), w_ref[...], preferred_element_type=F32)
    u_ref[...] = proj[:, 0:HALF]
    q = proj[:, HALF:2 * HALF]
    k = proj[:, 2 * HALF:3 * HALF]
    v = proj[:, 3 * HALF:4 * HALF]
    cos = jnp.concatenate([cos_ref[...]] * DA_HEADS, axis=1)
    sin = jnp.concatenate([sin_ref[...]] * DA_HEADS, axis=1)
    qr = q * cos + _rope_partner(q) * sin
    kr = k * cos + _rope_partner(k) * sin
    qkv_ref[:, 0:HALF] = (qr * Q_SCALE).astype(BF16)
    qkv_ref[:, HALF:2 * HALF] = kr.astype(BF16)
    qkv_ref[:, 2 * HALF:3 * HALF] = v.astype(BF16)

    @pl.when(i < NW_P)
    def _():
        kt = k.T
        for s in range(TW // SEQ):
            k_ref[s] = kt[:, s * SEQ:(s + 1) * SEQ]
            v_ref[s] = v[s * SEQ:(s + 1) * SEQ].reshape(SEQ, DA_HEADS, DA_DV)


def _ab_in(xp, xs, mod, g, w, cos_t, sin_t):
    def rope_tile(i):
        return jnp.where(i < NW_P, 0, 1 + (i - NW_P) % WPS)

    per_tile = TW // SEQ
    return pl.pallas_call(
        _ab_in_kernel,
        grid=(NW,),
        in_specs=_split_specs(D_MODEL, TW, NW_P) + [
            pl.BlockSpec((None, 6, D_MODEL), lambda i: (_mod_row(i, NW_P, WPS), 0, 0)),
            pl.BlockSpec((1, D_MODEL), lambda i: (0, 0)),
            pl.BlockSpec((D_MODEL, 4 * HALF), lambda i: (0, 0)),
            pl.BlockSpec((TW, DA_DV), lambda i: (rope_tile(i), 0)),
            pl.BlockSpec((TW, DA_DV), lambda i: (rope_tile(i), 0)),
        ],
        out_specs=[
            pl.BlockSpec((TW, HALF), lambda i: (i, 0)),
            pl.BlockSpec((TW, 3 * HALF), lambda i: (i, 0)),
            pl.BlockSpec((per_tile, HALF, SEQ), lambda i: (jnp.minimum(i, NW_P - 1), 0, 0)),
            pl.BlockSpec((per_tile, None, SEQ, DA_HEADS, DA_DV), lambda i: (jnp.minimum(i, NW_P - 1), 0, 0, 0, 0)),
        ],
        out_shape=[
            jax.ShapeDtypeStruct((N_TOK, HALF), F32),
            jax.ShapeDtypeStruct((N_TOK, 3 * HALF), BF16),
            jax.ShapeDtypeStruct((BATCH, HALF, SEQ), F32),
            jax.ShapeDtypeStruct((BATCH, 1, SEQ, DA_HEADS, DA_DV), F32),
        ],
        compiler_params=_params(),
        name="ab_in",
    )(xp, xs, mod, g, w, cos_t, sin_t)


def _rope_tables():
    pos = jnp.arange(DEC_SEQ)
    t_row = (pos // GRID_W).astype(F32)
    t_col = (pos % GRID_W).astype(F32)
    inv = ROPE_THETA ** (-jnp.arange(ROPE_F, dtype=F32) / ROPE_F)
    ang = jnp.stack([t_row[:, None] * inv, t_col[:, None] * inv], axis=1)
    cos, sin = jnp.cos(ang), jnp.sin(ang)
    cos64 = jnp.stack([cos, cos], axis=2).reshape(DEC_SEQ, DA_DK)
    sin64 = jnp.stack([-sin, sin], axis=2).reshape(DEC_SEQ, DA_DK)
    cos_t = jnp.concatenate([jnp.ones((TW, DA_DV), F32), jnp.tile(cos64, (1, 2))], axis=0)
    sin_t = jnp.concatenate([jnp.zeros((TW, DA_DV), F32), jnp.tile(sin64, (1, 2))], axis=0)
    return cos_t, sin_t


def _s5_param_kernel(lre_ref, lim_ref, ldt_ref, bt_ref, c_ref, m1_ref, m2_ref, m3_ref, at_ref):
    nt = (((1,), (1,)), ((), ()))
    e_tab, f_tab, k_tab = [], [], []
    r_i = lax.broadcasted_iota(jnp.int32, (LANES, LANES), 0) // S5_P
    c_i = lax.broadcasted_iota(jnp.int32, (LANES, LANES), 1) // S5_P
    same_group = r_i == c_i
    for d in range(2):
        lre = lre_ref[d]
        lim = lim_ref[d]
        dt = jnp.exp(ldt_ref[d])
        mag = jnp.exp(lre * dt)
        ang = lim * dt
        ar = mag * jnp.cos(ang)
        ai = mag * jnp.sin(ang)
        den = lre * lre + lim * lim
        nr = ar - 1.0
        fr = (nr * lre + ai * lim) / den
        fi = (ai * lre - nr * lim) / den
        btr = bt_ref[d, 0]
        bti = bt_ref[d, 1]
        bfr = fr * btr - fi * bti
        bfi = fr * bti + fi * btr
        cr = c_ref[d, 0]
        ci = c_ref[d, 1]
        cst = jnp.concatenate([cr, -ci], axis=1)
        pr = jnp.ones_like(ar)
        pi = jnp.zeros_like(ar)
        e_d, f_d, k_d = [], [], []
        for tau in range(S5_T + 1):
            f_d.append((cr * pr - ci * pi, cr * pi + ci * pr))
            if tau < S5_T:
                er, ei = pr * bfr - pi * bfi, pr * bfi + pi * bfr
                e_d.append((er, ei))
                full = lax.dot_general(jnp.concatenate([er, ei], axis=1), cst, nt,
                                       precision=HIGHEST, preferred_element_type=F32)
                k_d.append(jnp.where(same_group, full, 0.0))
            else:
                at_ref[d, 0] = pr
                at_ref[d, 1] = pi
            pr, pi = pr * ar - pi * ai, pr * ai + pi * ar
        e_tab.append(e_d)
        f_tab.append(f_d)
        k_tab.append(k_d)

    for s in range(S5_T):
        for t in range(S5_T):
            blk = k_tab[0][t - s] if t >= s else k_tab[1][s - t]
            if t == s:
                blk = blk + k_tab[1][0]
            m1_ref[s * LANES:(s + 1) * LANES, t * LANES:(t + 1) * LANES] = blk.astype(BF16)
    m2_ref[...] = jnp.zeros(m2_ref.shape, BF16)
    m3_ref[...] = jnp.zeros(m3_ref.shape, BF16)
    for s in range(S5_T):
        ef, er_ = e_tab[0][S5_T - 1 - s], e_tab[1][s]
        v = jnp.concatenate([ef[0], er_[0], ef[1], er_[1]], axis=1).astype(BF16)
        ff, fr_ = f_tab[0][s + 1], f_tab[1][S5_T - s]
        w = jnp.concatenate([ff[0], fr_[0], -ff[1], -fr_[1]], axis=1).astype(BF16)
        for gi in range(S5_GB):
            rows = slice(s * LANES + gi * S5_P, s * LANES + (gi + 1) * S5_P)
            cols = slice(gi * S5_ST, (gi + 1) * S5_ST)
            m2_ref[rows, cols] = v[gi * S5_P:(gi + 1) * S5_P]
            m3_ref[rows, cols] = w[gi * S5_P:(gi + 1) * S5_P]


def _s5_params(lam_re, lam_im, log_dt, b_w, c_w):
    rows = S5_G * S5_P
    per_row = lambda a: jnp.repeat(a, S5_P, axis=1)
    lre = per_row(lam_re)
    lim = per_row(lam_im)
    ldt = per_row(jnp.broadcast_to(log_dt[:, :, None], (2, S5_G, S5_N)))
    bt = b_w.transpose(0, 1, 2, 4, 3).reshape(2, 2, rows, S5_N)
    cc = c_w.reshape(2, 2, rows, S5_N)
    vec = pl.BlockSpec((2, LANES, S5_N), lambda j: (0, j, 0))
    mat = pl.BlockSpec((2, 2, LANES, S5_N), lambda j: (0, 0, j, 0))
    wide = lambda n: pl.BlockSpec((None, S5_K, n), lambda j: (j, 0, 0))
    m1, m2, m3, at = pl.pallas_call(
        _s5_param_kernel,
        grid=(S5_NB,),
        in_specs=[vec, vec, vec, mat, mat],
        out_specs=[wide(S5_K), wide(S5_GB * S5_ST), wide(S5_GB * S5_ST),
                   pl.BlockSpec((None, 2, 2, LANES, S5_N), lambda j: (j, 0, 0, 0, 0))],
        out_shape=[
            jax.ShapeDtypeStruct((S5_NB, S5_K, S5_K), BF16),
            jax.ShapeDtypeStruct((S5_NB, S5_K, S5_GB * S5_ST), BF16),
            jax.ShapeDtypeStruct((S5_NB, S5_K, S5_GB * S5_ST), BF16),
            jax.ShapeDtypeStruct((S5_NB, 2, 2, LANES, S5_N), F32),
        ],
        compiler_params=_params(),
        name="s5_params",
    )(lre, lim, ldt, bt, cc)
    coef = at[:, :, :, ::S5_P, :].transpose(0, 3, 2, 1, 4).reshape(S5_NB, S5_GB, 2, 2 * S5_N)
    return m1, m2, m3, coef


def _s5_kernel(u_ref, m1_ref, m2_ref, m3_ref, coef_ref, d_ref, *rest, stride, nc, two_pass):
    if two_pass:
        h0_ref, y_ref, lhs, yacc, s_scr, hp_scr = rest
    else:
        y_ref, fin_ref, lhs, yacc, s_scr, hp_scr = rest
    ln = 2 * S5_N
    nt = (((1,), (1,)), ((), ()))

    def seq_rows(k, c):
        return pl.ds(pl.multiple_of(k * stride + c * S5_T, S5_T), S5_T)

    def gather(c2, carry):
        halves = []
        for c in (2 * c2, 2 * c2 + 1):
            tiles = [u_ref[seq_rows(k, c), :] for k in range(8)]
            halves.append(jnp.stack(tiles, axis=0).reshape(8, S5_K))
        lhs[pl.ds(pl.multiple_of(c2 * 16, 16), 16), :] = jnp.concatenate(halves, axis=0).astype(BF16)
        return carry

    lax.fori_loop(0, nc // 2, gather, 0)
    lb = lhs[...]
    yacc[...] = jnp.dot(lb, m1_ref[...], preferred_element_type=F32)

    lane = lax.broadcasted_iota(jnp.int32, (8, ln), 1)
    row = lax.broadcasted_iota(jnp.int32, (8, ln), 0)
    fwd = lane < S5_N
    odd = (row % 2) == 1
    wide = S5_GS * S5_ST
    for gq in range(S5_GB // S5_GS):
        groups = range(gq * S5_GS, (gq + 1) * S5_GS)
        cols = slice(gq * wide, (gq + 1) * wide)
        s_scr[...] = jnp.dot(lb, m2_ref[:, cols], preferred_element_type=F32)
        are = [jnp.broadcast_to(coef_ref[g, 0:1, :], (8, ln)) for g in groups]
        aim = [jnp.broadcast_to(coef_ref[g, 1:2, :], (8, ln)) for g in groups]

        def scan(init, store):
            def body(c, carry):
                kf = pl.multiple_of(c * 8, 8)
                kr = pl.multiple_of((nc - 1 - c) * 8, 8)
                out = []
                for j in range(S5_GS):
                    p, q = carry[2 * j], carry[2 * j + 1]
                    o = j * S5_ST
                    if store:
                        hp_scr[pl.ds(kf, 8), o:o + S5_N] = p[:, 0:S5_N]
                        hp_scr[pl.ds(kr, 8), o + S5_N:o + ln] = p[:, S5_N:ln]
                        hp_scr[pl.ds(kf, 8), o + ln:o + ln + S5_N] = q[:, 0:S5_N]
                        hp_scr[pl.ds(kr, 8), o + ln + S5_N:o + 2 * ln] = q[:, S5_N:ln]
                    s_re = jnp.where(fwd, s_scr[pl.ds(kf, 8), o:o + ln], s_scr[pl.ds(kr, 8), o:o + ln])
                    s_im = jnp.where(fwd, s_scr[pl.ds(kf, 8), o + ln:o + 2 * ln],
                                     s_scr[pl.ds(kr, 8), o + ln:o + 2 * ln])
                    out += [are[j] * p - aim[j] * q + s_re, are[j] * q + aim[j] * p + s_im]
                return tuple(out)
            return lax.fori_loop(0, nc, body, tuple(init))

        if two_pass:
            h0 = [h0_ref[g, r] for g in groups for r in range(2)]
            fin = scan(h0, False)
            seed = [h + jnp.where(fwd & odd, pltpu.roll(f, 1, axis=0), 0.0)
                    + jnp.where(jnp.logical_not(fwd | odd), pltpu.roll(f, 7, axis=0), 0.0)
                    for h, f in zip(h0, fin)]
            scan(seed, True)
        else:
            fin = scan([jnp.zeros((8, ln), F32)] * (2 * S5_GS), True)
            for j, g in enumerate(groups):
                fin_ref[g, 0] = fin[2 * j]
                fin_ref[g, 1] = fin[2 * j + 1]
        yacc[...] += lax.dot_general(hp_scr[...].astype(BF16), m3_ref[:, cols], nt, preferred_element_type=F32)

    dvec = d_ref[...]

    def scatter(c, carry):
        blk = yacc[pl.ds(pl.multiple_of(c * 8, 8), 8), :].reshape(8, S5_T, LANES)
        for k in range(8):
            rows = seq_rows(k, c)
            y_ref[rows, :] = blk[k] + u_ref[rows, :] * dvec
        return carry

    lax.fori_loop(0, nc, scatter, 0)


def _s5_call(u, row_block, mats, coef, dvec, h0, stride, nc, parts):
    m1, m2, m3 = mats
    rows = 8 * stride
    two_pass = h0 is not None
    mat = lambda n: pl.BlockSpec((None, S5_K, n), lambda j, i: (j, 0, 0))
    tok = pl.BlockSpec((rows, LANES), lambda j, i: (row_block * parts + i, j))
    in_specs = [
        tok, mat(S5_K), mat(S5_GB * S5_ST), mat(S5_GB * S5_ST),
        pl.BlockSpec((None, S5_GB, 2, 2 * S5_N), lambda j, i: (j, 0, 0, 0)),
        pl.BlockSpec((None, 1, LANES), lambda j, i: (j, 0, 0)),
    ]
    args = [u, m1, m2, m3, coef, dvec]
    out_specs = [pl.BlockSpec((rows, LANES), lambda j, i: (i, j))]
    out_shape = [jax.ShapeDtypeStruct((parts * rows, HALF), F32)]
    if two_pass:
        in_specs.append(pl.BlockSpec((None, S5_GB, 2, 8, 2 * S5_N), lambda j, i: (j, 0, 0, 0, 0)))
        args.append(h0)
    else:
        out_specs.append(pl.BlockSpec((None, S5_GB, 2, 8, 2 * S5_N), lambda j, i: (j, 0, 0, i, 0)))
        out_shape.append(jax.ShapeDtypeStruct((S5_NB, S5_GB, 2, parts * 8, 2 * S5_N), F32))
    return pl.pallas_call(
        functools.partial(_s5_kernel, stride=stride, nc=nc, two_pass=two_pass),
        grid=(S5_NB, parts),
        in_specs=in_specs,
        out_specs=out_specs,
        out_shape=out_shape,
        scratch_shapes=[
            pltpu.VMEM((nc * 8, S5_K), BF16),
            pltpu.VMEM((nc * 8, S5_K), F32),
            pltpu.VMEM((nc * 8, S5_GS * S5_ST), F32),
            pltpu.VMEM((nc * 8, S5_GS * S5_ST), F32),
        ],
        compiler_params=_params(("arbitrary", "arbitrary")),
        name="s5_%d" % stride,
    )(*args)


def _s5(u, mats, coef, h0, s5_d):
    dvec = s5_d.reshape(S5_NB, 1, LANES)
    yp, fin = _s5_call(u, 0, mats, coef, dvec, None, SEQ, SEQ // S5_T, BATCH // 8)
    hh = h0.transpose(3, 2, 0, 1, 4)
    z = jnp.zeros_like(hh[:, :, :, 0])
    first = jnp.concatenate([hh[:, :, :, 0], z], axis=-1)
    second = jnp.concatenate([z, hh[:, :, :, 1]], axis=-1)
    h0g = jnp.stack([first, second], axis=3).reshape(S5_NB, S5_GB, 2, 2 * DEC_BATCH, 2 * S5_N)
    half = DEC_SEQ // 2
    (ys,) = _s5_call(u, 1, mats, coef, dvec, h0g, half, half // S5_T, 1)
    new_s5 = fin.reshape(S5_G, 2, BATCH, 2, S5_N).transpose(2, 3, 1, 0, 4)
    return yp, ys, new_s5


def _attn_kernel(*refs, lam_init, has_ctx, lk):
    if has_ctx:
        q_ref, k_ref, v_ref, kc_ref, vc_ref, dl_ref, g_ref, o_ref, s_a, s_b, vx, vcx = refs
    else:
        q_ref, k_ref, v_ref, dl_ref, g_ref, o_ref, s_a, s_b, vx = refs
    lt = lk + (PAST_LEN if has_ctx else 0)
    dl = dl_ref[...]
    lam = (jnp.exp(jnp.sum(dl[0:1] * dl[1:2], keepdims=True))
           - jnp.exp(jnp.sum(dl[2:3] * dl[3:4], keepdims=True)) + lam_init)
    lane = lax.broadcasted_iota(jnp.int32, (1, DA_DV), 1)
    first = lane < DA_DK
    g = g_ref[...]
    nt = (((1,), (1,)), ((), ()))

    @pl.when(pl.program_id(1) == 0)
    def _():
        onehot = (lane == 0).astype(BF16)
        for h in range(DA_HEADS):
            cols = slice(h * DA_DV, (h + 1) * DA_DV)
            vx[:, 2 * h * DA_DV:(2 * h + 1) * DA_DV] = v_ref[:, cols]
            vx[:, (2 * h + 1) * DA_DV:(2 * h + 2) * DA_DV] = jnp.broadcast_to(onehot, (lk, DA_DV))
            if has_ctx:
                vcx[:, 2 * h * DA_DV:(2 * h + 1) * DA_DV] = vc_ref[:, cols]
                vcx[:, (2 * h + 1) * DA_DV:(2 * h + 2) * DA_DV] = jnp.broadcast_to(onehot, (PAST_LEN, DA_DV))

    def scores(h, dst):
        cols = slice(h * DA_DV, (h + 1) * DA_DV)
        qh = q_ref[:, cols]
        zero = jnp.zeros_like(qh)
        for m in range(2):
            qm = jnp.where(first if m == 0 else jnp.logical_not(first), qh, zero)
            dst[m, :, 0:lk] = lax.dot_general(qm, k_ref[:, cols], nt, preferred_element_type=F32)
            if has_ctx:
                dst[m, :, lk:lt] = lax.dot_general(qm, kc_ref[:, cols], nt, preferred_element_type=F32)

    def head(h, src):
        xcols = slice(2 * h * DA_DV, (2 * h + 2) * DA_DV)
        outs = []
        for m in range(2):
            s = src[m]
            e = jnp.exp2(s - jnp.max(s, axis=-1, keepdims=True)).astype(BF16)
            ox = jnp.dot(e[:, 0:lk], vx[:, xcols], preferred_element_type=F32)
            if has_ctx:
                ox = ox + jnp.dot(e[:, lk:lt], vcx[:, xcols], preferred_element_type=F32)
            outs.append(ox[:, 0:DA_DV] * (1.0 / ox[:, DA_DV:DA_DV + 1]))
        o = outs[0] - lam * outs[1]
        o_ref[:, h * DA_DV:(h + 1) * DA_DV] = (_rms(o, g) * (1.0 - lam_init)).astype(BF16)

    scores(0, s_a)
    for h in range(DA_HEADS):
        cur, nxt = (s_a, s_b) if h % 2 == 0 else (s_b, s_a)
        if h + 1 < DA_HEADS:
            scores(h + 1, nxt)
        head(h, cur)


def _attention(qkv, q_tile0, nb, nq, lk, ctx, da_lam, da_g, lam_init):
    kb0 = q_tile0 * TM // lk
    in_specs = [
        pl.BlockSpec((TM, HALF), lambda b, j: (q_tile0 + b * nq + j, 0)),
        pl.BlockSpec((lk, HALF), lambda b, j: (kb0 + b, 1)),
        pl.BlockSpec((lk, HALF), lambda b, j: (kb0 + b, 2)),
    ]
    args = [qkv, qkv, qkv]
    if ctx is not None:
        in_specs += [pl.BlockSpec((None, PAST_LEN, HALF), lambda b, j: (b, 0, 0))] * 2
        args += list(ctx)
    in_specs += [
        pl.BlockSpec((4, DA_DK), lambda b, j: (0, 0)),
        pl.BlockSpec((1, DA_DV), lambda b, j: (0, 0)),
    ]
    lt = lk + (PAST_LEN if ctx is not None else 0)
    scratch = [pltpu.VMEM((2, TM, lt), F32), pltpu.VMEM((2, TM, lt), F32), pltpu.VMEM((lk, 2 * HALF), BF16)]
    if ctx is not None:
        scratch.append(pltpu.VMEM((PAST_LEN, 2 * HALF), BF16))
    return pl.pallas_call(
        functools.partial(_attn_kernel, lam_init=lam_init, has_ctx=ctx is not None, lk=lk),
        grid=(nb, nq),
        in_specs=in_specs,
        out_specs=pl.BlockSpec((TM, HALF), lambda b, j: (b * nq + j, 0)),
        out_shape=jax.ShapeDtypeStruct((nb * nq * TM, HALF), BF16),
        scratch_shapes=scratch,
        compiler_params=_params(("arbitrary", "arbitrary")),
        name="diff_attn_%d" % lk,
    )(*args, da_lam, da_g.reshape(1, DA_DV))


def _post(x, y, g, gate):
    return x + gate * _rms(y, g)


def _ab_out_kernel(ysp_ref, yss_ref, ybp_ref, ybs_ref, xp_ref, xs_ref, mod_ref, g_ref, wg_ref, bg_ref, wo_ref,
                   o_ref):
    ys = jax.nn.gelu(_pick(ysp_ref, yss_ref, NW_P))
    glu = jnp.dot(ys.astype(BF16), wg_ref[...], preferred_element_type=F32) + bg_ref[...]
    ya = ys * jax.nn.sigmoid(glu)
    out = (jnp.dot(ya.astype(BF16), wo_ref[0:HALF, :], preferred_element_type=F32)
           + jnp.dot(_pick(ybp_ref, ybs_ref, NW_P), wo_ref[HALF:2 * HALF, :], preferred_element_type=F32))
    o_ref[...] = _post(_pick(xp_ref, xs_ref, NW_P), out, g_ref[...], mod_ref[2:3, :])


def _ab_out(ys5_p, ys5_s, yb_p, yb_s, xp, xs, mod, g, w_glu, b_glu, w_out):
    return pl.pallas_call(
        _ab_out_kernel,
        grid=(NW,),
        in_specs=_split_specs(HALF, TW, NW_P) + _split_specs(HALF, TW, NW_P) + _split_specs(D_MODEL, TW, NW_P) + [
            pl.BlockSpec((None, 6, D_MODEL), lambda i: (_mod_row(i, NW_P, WPS), 0, 0)),
            pl.BlockSpec((1, D_MODEL), lambda i: (0, 0)),
            pl.BlockSpec((HALF, HALF), lambda i: (0, 0)),
            pl.BlockSpec((1, HALF), lambda i: (0, 0)),
            pl.BlockSpec((D_MODEL, D_MODEL), lambda i: (0, 0)),
        ],
        out_specs=pl.BlockSpec((TW, D_MODEL), lambda i: (i, 0)),
        out_shape=jax.ShapeDtypeStruct((N_TOK, D_MODEL), F32),
        compiler_params=_params(),
        name="ab_out",
    )(ys5_p, ys5_s, yb_p, yb_s, xp, xs, mod, g, w_glu, b_glu, w_out)


def _halo_specs(width):
    blocks = TM // HALO
    last = N_TOK // HALO - 1
    return [
        pl.BlockSpec((TM, width), lambda i: (i, 0)),
        pl.BlockSpec((HALO, width), lambda i: (jnp.maximum(i * blocks - 1, 0), 0)),
        pl.BlockSpec((HALO, width), lambda i: (jnp.minimum((i + 1) * blocks, last), 0)),
    ]


def _fill_hbuf(hbuf, x_ref, xp_ref, xn_ref, g, shift, scale, i):
    pos, n = _seq_pos(i)
    hp = jnp.where(pos > 0, _pre(xp_ref[...], g, shift, scale), 0.0)
    hn = jnp.where(pos < n - 1, _pre(xn_ref[...], g, shift, scale), 0.0)
    hbuf[0:TM, :] = _pre(x_ref[...], g, shift, scale).astype(BF16)
    hbuf[TM:ROWS, :] = jnp.concatenate([hn, hp], axis=0).astype(BF16)


def _shift_rows(x, s):
    if s == 0:
        return x[0:TM]
    return pltpu.roll(x, (-s) % ROWS, axis=0)[0:TM]


def _ffn_kernel(x_ref, xp_ref, xn_ref, mod_ref, g2_ref, g3_ref, wg_ref, wv_ref, wd_ref, cw_ref, cb_ref, *rest, split):
    hbuf, acc, u_a, u_b, wu_s, wd_s = rest[-6:]
    step = pl.program_id(0)

    @pl.when(step < FF_NCH)
    def _():
        wu_s[step] = wg_ref[...].astype(BF16)
        wu_s[FF_NCH + step] = wv_ref[...].astype(BF16)
        wd_s[step] = wd_ref[...].astype(BF16)

    @pl.when(step >= FF_NCH)
    def _():
        i = step - FF_NCH
        _fill_hbuf(hbuf, x_ref, xp_ref, xn_ref, g2_ref[...], mod_ref[3:4, :], mod_ref[4:5, :], i)

        def up(j, dst):
            for half in range(2):
                dst[half] = jnp.dot(hbuf[...], wu_s[half * FF_NCH + j], preferred_element_type=F32)

        def activation(j, src):
            parts = []
            for half in range(2):
                c0 = half * D_FF + j * FF_CHUNK
                sc = 1.0 if half == 0 else 0.5
                cw = cw_ref[:, c0:c0 + FF_CHUNK] * sc
                u = src[half]
                parts.append(_shift_rows(u, -1) * cw[0:1] + _shift_rows(u, 0) * cw[1:2]
                             + _shift_rows(u, 1) * cw[2:3] + cb_ref[:, c0:c0 + FF_CHUNK] * sc)
            gt = parts[0]
            z = gt * (gt * gt * (GELU_C * GELU_K) + GELU_K)
            return ((gt + gt * jnp.tanh(z)) * parts[1]).astype(BF16)

        up(0, u_a)
        for j in range(FF_NCH):
            cur, nxt = (u_a, u_b) if j % 2 == 0 else (u_b, u_a)
            if j + 1 < FF_NCH:
                up(j + 1, nxt)
            contrib = jnp.dot(activation(j, cur), wd_s[j], preferred_element_type=F32)
            if j == 0:
                acc[...] = contrib
            else:
                acc[...] += contrib
        res = _post(x_ref[...], acc[...], g3_ref[...], mod_ref[5:6, :])
        if split:
            @pl.when(i < NT_P)
            def _():
                rest[0][...] = res

            @pl.when(i >= NT_P)
            def _():
                rest[1][...] = res
        else:
            rest[0][...] = res


def _ffn(x, mod, g2, g3, w_up, cw, cb, w_down, layer, split):
    const = lambda s: (0, 0)
    tile = lambda s: jnp.maximum(s - FF_NCH, 0)
    chunk = lambda s: jnp.minimum(s, FF_NCH - 1)
    shifted = lambda spec: pl.BlockSpec(spec.block_shape, lambda s, f=spec.index_map: f(tile(s)))
    if split:
        out_specs = [shifted(sp) for sp in _split_specs(D_MODEL)]
        out_shape = [jax.ShapeDtypeStruct((NT_P * TM, D_MODEL), F32), jax.ShapeDtypeStruct((NT_S * TM, D_MODEL), F32)]
    else:
        out_specs = pl.BlockSpec((TM, D_MODEL), lambda s: (tile(s), 0))
        out_shape = jax.ShapeDtypeStruct((N_TOK, D_MODEL), F32)
    return pl.pallas_call(
        functools.partial(_ffn_kernel, split=split),
        grid=(FF_NCH + NT,),
        in_specs=[shifted(sp) for sp in _halo_specs(D_MODEL)] + [
            pl.BlockSpec((None, 6, D_MODEL), lambda s: (_mod_row(tile(s)), 0, 0)),
            pl.BlockSpec((1, D_MODEL), const),
            pl.BlockSpec((1, D_MODEL), const),
            pl.BlockSpec((None, D_MODEL, FF_CHUNK), lambda s: (layer, 0, chunk(s))),
            pl.BlockSpec((None, D_MODEL, FF_CHUNK), lambda s: (layer, 0, FF_NCH + chunk(s))),
            pl.BlockSpec((None, FF_CHUNK, D_MODEL), lambda s: (layer, chunk(s), 0)),
            pl.BlockSpec((3, 2 * D_FF), const),
            pl.BlockSpec((1, 2 * D_FF), const),
        ],
        out_specs=out_specs,
        out_shape=out_shape,
        scratch_shapes=[
            pltpu.VMEM((ROWS, D_MODEL), BF16),
            pltpu.VMEM((TM, D_MODEL), F32),
            pltpu.VMEM((2, ROWS, FF_CHUNK), F32),
            pltpu.VMEM((2, ROWS, FF_CHUNK), F32),
            pltpu.VMEM((2 * FF_NCH, D_MODEL, FF_CHUNK), BF16),
            pltpu.VMEM((FF_NCH, FF_CHUNK, D_MODEL), BF16),
        ],
        compiler_params=_params(),
        name="conv_ffn",
    )(x, x, x, mod, g2, g3, w_up, w_up, w_down, cw, cb)


def _softplus(z):
    e = jnp.exp(-jnp.abs(z))
    u = 1.0 + e
    tiny = u == 1.0
    log1p = jnp.where(tiny, e, jnp.log(u) * (e / jnp.where(tiny, 1.0, u - 1.0)))
    return jnp.maximum(z, 0.0) + log1p


def _lru_scan(a_ref, b_ref, hs_ref, p_scr, carry, reverse):
    half = TM // 2
    h_a = carry[...]
    h_b = jnp.zeros_like(h_a)
    p_b = jnp.ones_like(h_a)
    for j in range(half):
        r_a = TM - 1 - j if reverse else j
        r_b = half - 1 - j if reverse else half + j
        h_a = a_ref[r_a:r_a + 1, :] * h_a + b_ref[r_a:r_a + 1, :]
        hs_ref[r_a:r_a + 1, :] = h_a
        a_b = a_ref[r_b:r_b + 1, :]
        h_b = a_b * h_b + b_ref[r_b:r_b + 1, :]
        p_b = p_b * a_b
        hs_ref[r_b:r_b + 1, :] = h_b
        p_scr[r_b % half:r_b % half + 1, :] = p_b
    second = slice(0, half) if reverse else slice(half, TM)
    hs_ref[second, :] = hs_ref[second, :] + p_scr[...] * h_a
    h = h_b + p_b * h_a
    carry[...] = h
    return h


def _cd_in_kernel(x_ref, xp_ref, xn_ref, mod_ref, g_ref, w_ref, scw_ref, cw_ref, cb_ref, wg_ref, bg_ref,
                  lam_ref, h0_ref, yc_ref, gate_ref, hsf_ref, ar_ref, br_ref, fin_ref,
                  hbuf, a_scr, b_scr, p_scr, carry):
    i = pl.program_id(0)
    pos, _ = _seq_pos(i)
    _fill_hbuf(hbuf, x_ref, xp_ref, xn_ref, g_ref[...], mod_ref[0:1, :], mod_ref[1:2, :], i)

    def col(k, rows):
        return jnp.dot(hbuf[0:rows, :], w_ref[:, k * HALF:(k + 1) * HALF], preferred_element_type=F32)

    xr = col(3, ROWS)
    xin = col(0, ROWS)
    cg = col(2, ROWS)
    cw = cw_ref[...]
    xc = (_shift_rows(xr, -2) * cw[0:1] + _shift_rows(xr, -1) * cw[1:2] + _shift_rows(xr, 0) * cw[2:3]
          + _shift_rows(xr, 1) * cw[3:4] + cb_ref[...])
    xcb = xc.astype(BF16)

    def direction(d):
        cols = slice(2 * d * HALF, (2 * d + 2) * HALF)
        gates = jax.nn.sigmoid(jnp.dot(xcb, wg_ref[:, cols], preferred_element_type=F32) + bg_ref[:, cols])
        log_a = (-LRU_C) * gates[:, 0:HALF] * _softplus(-lam_ref[d:d + 1, :])
        a = jnp.exp(log_a)
        drive = jnp.sqrt(-jnp.tanh(log_a) * (a * a + 1.0))
        return a, drive * (gates[:, HALF:2 * HALF] * xc)

    a, bval = direction(0)
    a_scr[...] = a
    b_scr[...] = bval

    @pl.when(pos == 0)
    def _():
        carry[...] = h0_ref[0:1, :]

    h = _lru_scan(a_scr, b_scr, hsf_ref, p_scr, carry, False)
    fin_ref[...] = jnp.broadcast_to(h, (8, HALF))

    a, bval = direction(1)
    ar_ref[...] = a
    br_ref[...] = bval
    bg = col(1, TM)
    gb = col(4, TM)
    prod = cg * xin
    scw = scw_ref[...]
    yc = bg * (_shift_rows(prod, -1) * scw[0:1] + _shift_rows(prod, 0) * scw[1:2]
               + _shift_rows(prod, 1) * scw[2:3])
    yc_ref[...] = yc.astype(BF16)
    gate_ref[...] = jax.nn.gelu(gb).astype(BF16)


def _cd_in(x, mod, g, w_in, sc_w, conv_w, conv_b, w_gates, b_gates, lru_lam, h0t):
    const = lambda i: (0, 0)
    tok = lambda dt: jax.ShapeDtypeStruct((N_TOK, HALF), dt)
    row = pl.BlockSpec((TM, HALF), lambda i: (i, 0))
    return pl.pallas_call(
        _cd_in_kernel,
        grid=(NT,),
        in_specs=_halo_specs(D_MODEL) + [
            pl.BlockSpec((None, 6, D_MODEL), lambda i: (_mod_row(i), 0, 0)),
            pl.BlockSpec((1, D_MODEL), const),
            pl.BlockSpec((D_MODEL, 5 * HALF), const),
            pl.BlockSpec((3, HALF), const),
            pl.BlockSpec((4, HALF), const),
            pl.BlockSpec((1, HALF), const),
            pl.BlockSpec((HALF, 4 * HALF), const),
            pl.BlockSpec((1, 4 * HALF), const),
            pl.BlockSpec((2, HALF), const),
            pl.BlockSpec((None, 2, HALF), lambda i: (_mod_row(i), 0, 0)),
        ],
        out_specs=[row, row, row, row, row, pl.BlockSpec((8, HALF), lambda i: (i, 0))],
        out_shape=[tok(BF16), tok(BF16), tok(F32), tok(F32), tok(F32), jax.ShapeDtypeStruct((NT * 8, HALF), F32)],
        scratch_shapes=[
            pltpu.VMEM((ROWS, D_MODEL), BF16),
            pltpu.VMEM((TM, HALF), F32),
            pltpu.VMEM((TM, HALF), F32),
            pltpu.VMEM((TM // 2, HALF), F32),
            pltpu.VMEM((1, HALF), F32),
        ],
        compiler_params=_params(),
        name="cd_in",
    )(x, x, x, mod, g, w_in, sc_w, conv_w, conv_b, w_gates, b_gates, lru_lam, h0t)


def _cd_out_kernel(ar_ref, br_ref, hsf_ref, gate_ref, yc_ref, x_ref, mod_ref, g_ref, wo_ref, h0_ref,
                   o_ref, fin_ref, hs_scr, p_scr, carry):
    ti = NT - 1 - pl.program_id(0)
    pos, n = _seq_pos(ti)

    @pl.when(pos == n - 1)
    def _():
        carry[...] = h0_ref[1:2, :]

    h = _lru_scan(ar_ref, br_ref, hs_scr, p_scr, carry, True)
    fin_ref[...] = jnp.broadcast_to(h, (8, HALF))
    out_c = jnp.dot(yc_ref[...], wo_ref[0:HALF, :], preferred_element_type=F32)
    yd = (hsf_ref[...] + hs_scr[...]) * gate_ref[...].astype(F32)
    out = out_c + jnp.dot(yd.astype(BF16), wo_ref[HALF:2 * HALF, :], preferred_element_type=F32)
    o_ref[...] = _post(x_ref[...], out, g_ref[...], mod_ref[2:3, :])


def _cd_out(a_r, b_r, hs_f, gate, yc, x, mod, g, w_out, h0t):
    const = lambda i: (0, 0)
    rev = lambda i: (NT - 1 - i, 0)
    row = pl.BlockSpec((TM, HALF), rev)
    return pl.pallas_call(
        _cd_out_kernel,
        grid=(NT,),
        in_specs=[
            row, row, row, row, row,
            pl.BlockSpec((TM, D_MODEL), rev),
            pl.BlockSpec((None, 6, D_MODEL), lambda i: (_mod_row(NT - 1 - i), 0, 0)),
            pl.BlockSpec((1, D_MODEL), const),
            pl.BlockSpec((D_MODEL, D_MODEL), const),
            pl.BlockSpec((None, 2, HALF), lambda i: (_mod_row(NT - 1 - i), 0, 0)),
        ],
        out_specs=[
            pl.BlockSpec((TM, D_MODEL), rev),
            pl.BlockSpec((8, HALF), rev),
        ],
        out_shape=[
            jax.ShapeDtypeStruct((N_TOK, D_MODEL), F32),
            jax.ShapeDtypeStruct((NT * 8, HALF), F32),
        ],
        scratch_shapes=[
            pltpu.VMEM((TM, HALF), F32),
            pltpu.VMEM((TM // 2, HALF), F32),
            pltpu.VMEM((1, HALF), F32),
        ],
        compiler_params=_params(),
        name="cd_out",
    )(a_r, b_r, hs_f, gate, yc, x, mod, g, w_out, h0t)


def kernel(x_prompt, x_sample, cache_attn_k, cache_attn_v, state_s5, state_rglru, c, c_ctx, w_mod, b_mod, norm_g, w_in_ab, w_out_ab, s5_lam_re, s5_lam_im, s5_log_dt, s5_b, s5_c, s5_d, s5_w_glu, s5_b_glu, da_lam, da_g, w_in_cd, w_out_cd, sc_conv_w, lru_conv_w, lru_conv_b, lru_w_a, lru_b_a, lru_w_x, lru_b_x, lru_lam, ffn_w_up, ffn_conv_w, ffn_conv_b, ffn_w_down):
    assert DEPTH == 2
    xp = x_prompt.reshape(NT_P * TM, D_MODEL)
    xs = x_sample.reshape(NT_S * TM, D_MODEL)
    cv8 = jnp.zeros((8, D_MODEL), F32).at[0].set(c_ctx).at[1:1 + DEC_BATCH].set(c)
    mod = _modulation(cv8, w_mod, b_mod)
    cos_t, sin_t = _rope_tables()
    g = norm_g.reshape(DEPTH, 4, 1, D_MODEL)

    lam_init = 0.8 - 0.6 * math.exp(-0.3 * 0)
    u, qkv, k32, v32 = _ab_in(xp, xs, mod[0], g[0, 0], w_in_ab[0].astype(BF16), cos_t, sin_t)
    m1, m2, m3, coef = _s5_params(s5_lam_re[0], s5_lam_im[0], s5_log_dt[0], s5_b[0], s5_c[0])
    ys5_p, ys5_s, new_s5 = _s5(u, (m1, m2, m3), coef, state_s5[:, 0], s5_d[0])
    yb_p = _attention(qkv, 0, BATCH, 1, SEQ, None, da_lam[0], da_g[0], lam_init)
    ctx = (cache_attn_k[:, 0].reshape(DEC_BATCH, PAST_LEN, HALF).astype(BF16),
           cache_attn_v[:, 0].reshape(DEC_BATCH, PAST_LEN, HALF).astype(BF16))
    yb_s = _attention(qkv, NT_P, DEC_BATCH, TPS, DEC_SEQ, ctx, da_lam[0], da_g[0], lam_init)
    x = _ab_out(ys5_p, ys5_s, yb_p, yb_s, xp, xs, mod[0], g[0, 1], s5_w_glu[0].astype(BF16), s5_b_glu[0].reshape(1, HALF),
                w_out_ab[0].astype(BF16))
    assert NT_P == BATCH and TM == SEQ
    new_k = k32.reshape(BATCH, 1, DA_HEADS, 2, DA_DK, SEQ).transpose(0, 1, 5, 2, 3, 4)
    new_v = v32
    x = _ffn(x, mod[0], g[0, 2], g[0, 3], ffn_w_up, ffn_conv_w[0], ffn_conv_b[0].reshape(1, 2 * D_FF), ffn_w_down,
             0, False)

    eye = jnp.eye(LRU_BLOCKS, dtype=F32)
    dense = lambda w: jnp.einsum('kcd,kl->kcld', w, eye).reshape(LRU_WIDTH, LRU_WIDTH)
    w_gates = jnp.concatenate([dense(lru_w_a[0, 0]), dense(lru_w_x[0, 0]),
                               dense(lru_w_a[0, 1]), dense(lru_w_x[0, 1])], axis=1).astype(BF16)
    b_gates = jnp.concatenate([lru_b_a[0, 0], lru_b_x[0, 0], lru_b_a[0, 1], lru_b_x[0, 1]]).reshape(1, 4 * HALF)
    h0t = jnp.zeros((8, 2, HALF), F32).at[1:1 + DEC_BATCH].set(state_rglru[:, 0])
    yc, gate, hs_f, a_r, b_r, fin_f = _cd_in(x, mod[1], g[1, 0], w_in_cd[0].astype(BF16), sc_conv_w[0], lru_conv_w[0],
                                      lru_conv_b[0].reshape(1, HALF), w_gates, b_gates, lru_lam[0], h0t)
    x, fin_r = _cd_out(a_r, b_r, hs_f, gate, yc, x, mod[1], g[1, 1], w_out_cd[0].astype(BF16), h0t)
    tile_row0 = lambda f: f.reshape(NT, 8, HALF)[:NT_P, 0]
    new_lru = jnp.stack([tile_row0(fin_f), tile_row0(fin_r)], axis=1)[:, None]
    yp, ys = _ffn(x, mod[1], g[1, 2], g[1, 3], ffn_w_up, ffn_conv_w[1], ffn_conv_b[1].reshape(1, 2 * D_FF),
                  ffn_w_down, 1, True)
    return (yp.reshape(BATCH, SEQ, D_MODEL), ys.reshape(DEC_BATCH, DEC_SEQ, D_MODEL),
            new_k, new_v, new_s5[:, None], new_lru)
```

```python
import functools
import math

import jax
import jax.numpy as jnp
from jax import lax
from jax.experimental import pallas as pl
from jax.experimental.pallas import tpu as pltpu

D_MODEL = 1024
BATCH = 32
SEQ = 256
DEPTH = 2
DEC_BATCH = 4
DEC_SEQ = 2048
PAST_LEN = 512
GRID_W = 64
HALF = D_MODEL // 2
S5_P = 16
S5_G = HALF // S5_P
S5_N = 64
DA_DK = 64
DA_DV = 2 * DA_DK
DA_HEADS = HALF // DA_DV
ROPE_THETA = 10000.0
ROPE_F = DA_DK // 4
LRU_WIDTH = HALF
LRU_BLOCKS = 8
LRU_BS = LRU_WIDTH // LRU_BLOCKS
LRU_C = 8.0
D_FF = 2816
EPS = 1e-6

F32 = jnp.float32
BF16 = jnp.bfloat16
HIGHEST = lax.Precision.HIGHEST

LANES = 128
SUBLANES = 8
VMEM_BYTES = 64 * 1024 * 1024

TM = 256
NT_P = BATCH * SEQ // TM
TPS = DEC_SEQ // TM
NT_S = DEC_BATCH * TPS
NT = NT_P + NT_S
N_TOK = NT * TM
TW = 2 * TM
NW_P = NT_P * TM // TW
WPS = DEC_SEQ // TW
NW = N_TOK // TW
HALO = 8
ROWS = TM + 2 * HALO
S5_T = 8
S5_GB = LANES // S5_P
S5_NB = S5_G // S5_GB
S5_K = S5_T * LANES
S5_ST = 4 * S5_N
S5_GS = 4
FF_CHUNK = 256
FF_NCH = D_FF // FF_CHUNK
Q_SCALE = math.log2(math.e) / math.sqrt(DA_DK)
GELU_K = math.sqrt(2.0 / math.pi)
GELU_C = 0.044715
VMEM_LIMIT = VMEM_BYTES - 8 * 1024 * 1024


def _mod_row(i, ntp=NT_P, tps=TPS):
    return jnp.where(i < ntp, 0, 1 + (i - ntp) // tps)


def _seq_pos(i):
    return jnp.where(i < NT_P, 0, (i - NT_P) % TPS), jnp.where(i < NT_P, 1, TPS)


def _split_specs(width, tm=TM, ntp=NT_P):
    return [
        pl.BlockSpec((tm, width), lambda i: (jnp.minimum(i, ntp - 1), 0)),
        pl.BlockSpec((tm, width), lambda i: (jnp.maximum(i - ntp, 0), 0)),
    ]


def _pick(p_ref, s_ref, ntp=NT_P):
    return jnp.where(pl.program_id(0) < ntp, p_ref[...], s_ref[...])


def _params(sem=("arbitrary",)):
    return pltpu.CompilerParams(dimension_semantics=sem, vmem_limit_bytes=VMEM_LIMIT)


def _rms(x, g):
    ms = jnp.mean(x * x, axis=-1, keepdims=True)
    return x * lax.rsqrt(ms + EPS) * g


def _sigmoid(x):
    return 0.5 * jnp.tanh(0.5 * x) + 0.5


def _gelu(x):
    return x * (0.5 * jnp.tanh(x * (x * x * (GELU_C * GELU_K) + GELU_K)) + 0.5)


def _pre(x, g, shift, scale):
    return _rms(x, g) * (1.0 + scale) + shift


def _mod_kernel(cv_ref, w_ref, b_ref, o_ref):
    cv = cv_ref[...]
    s = cv * jax.nn.sigmoid(cv)
    w = w_ref[...]
    s_hi = s.astype(BF16)
    s_lo = (s - s_hi.astype(F32)).astype(BF16)
    w_hi = w.astype(BF16)
    w_lo = (w - w_hi.astype(F32)).astype(BF16)
    dot = lambda a, b: jnp.dot(a, b, preferred_element_type=F32)
    o_ref[...] = dot(s_hi, w_hi) + (dot(s_hi, w_lo) + dot(s_lo, w_hi)) + b_ref[...]


def _modulation(cv8, w_mod, b_mod):
    nb = 1536
    out = pl.pallas_call(
        _mod_kernel,
        grid=(DEPTH, 6 * D_MODEL // nb),
        in_specs=[
            pl.BlockSpec((8, D_MODEL), lambda l, j: (0, 0)),
            pl.BlockSpec((None, D_MODEL, nb), lambda l, j: (l, 0, j)),
            pl.BlockSpec((None, 1, nb), lambda l, j: (l, 0, j)),
        ],
        out_specs=pl.BlockSpec((None, 8, nb), lambda l, j: (l, 0, j)),
        out_shape=jax.ShapeDtypeStruct((DEPTH, 8, 6 * D_MODEL), F32),
        compiler_params=_params(("arbitrary", "arbitrary")),
        name="modulation",
    )(cv8, w_mod, b_mod.reshape(DEPTH, 1, 6 * D_MODEL))
    return out.reshape(DEPTH, 8, 6, D_MODEL)


def _rope_partner(x):
    lane = lax.broadcasted_iota(jnp.int32, (1, HALF), 1)
    first = (lane % (2 * ROPE_F)) < ROPE_F
    return jnp.where(first, pltpu.roll(x, HALF - ROPE_F, axis=1), pltpu.roll(x, ROPE_F, axis=1))


def _ab_in_kernel(xp_ref, xs_ref, mod_ref, g_ref, w_ref, cos_ref, sin_ref, u_ref, qkv_ref, k_ref, v_ref):
    i = pl.program_id(0)
    h = _pre(_pick(xp_ref, xs_ref, NW_P), g_ref[...], mod_ref[0:1, :], mod_ref[1:2, :])
    proj = jnp.dot(h.astype(BF16), w_ref[...], preferred_element_type=F32)
    u_ref[...] = proj[:, 0:HALF]
    q = proj[:, HALF:2 * HALF]
    k = proj[:, 2 * HALF:3 * HALF]
    v = proj[:, 3 * HALF:4 * HALF]
    cos = jnp.concatenate([cos_ref[...]] * DA_HEADS, axis=1)
    sin = jnp.concatenate([sin_ref[...]] * DA_HEADS, axis=1)
    qr = q * cos + _rope_partner(q) * sin
    kr = k * cos + _rope_partner(k) * sin
    qkv_ref[:, 0:HALF] = (qr * Q_SCALE).astype(BF16)
    qkv_ref[:, HALF:2 * HALF] = kr.astype(BF16)
    qkv_ref[:, 2 * HALF:3 * HALF] = v.astype(BF16)

    @pl.when(i < NW_P)
    def _():
        kt = k.T
        for s in range(TW // SEQ):
            k_ref[s] = kt[:, s * SEQ:(s + 1) * SEQ]
            v_ref[s] = v[s * SEQ:(s + 1) * SEQ].reshape(SEQ, DA_HEADS, DA_DV)


def _ab_in(xp, xs, mod, g, w, cos_t, sin_t):
    def rope_tile(i):
        return jnp.where(i < NW_P, 0, 1 + (i - NW_P) % WPS)

    per_tile = TW // SEQ
    return pl.pallas_call(
        _ab_in_kernel,
        grid=(NW,),
        in_specs=_split_specs(D_MODEL, TW, NW_P) + [
            pl.BlockSpec((None, 6, D_MODEL), lambda i: (_mod_row(i, NW_P, WPS), 0, 0)),
            pl.BlockSpec((1, D_MODEL), lambda i: (0, 0)),
            pl.BlockSpec((D_MODEL, 4 * HALF), lambda i: (0, 0)),
            pl.BlockSpec((TW, DA_DV), lambda i: (rope_tile(i), 0)),
            pl.BlockSpec((TW, DA_DV), lambda i: (rope_tile(i), 0)),
        ],
        out_specs=[
            pl.BlockSpec((TW, HALF), lambda i: (i, 0)),
            pl.BlockSpec((TW, 3 * HALF), lambda i: (i, 0)),
            pl.BlockSpec((per_tile, HALF, SEQ), lambda i: (jnp.minimum(i, NW_P - 1), 0, 0)),
            pl.BlockSpec((per_tile, None, SEQ, DA_HEADS, DA_DV), lambda i: (jnp.minimum(i, NW_P - 1), 0, 0, 0, 0)),
        ],
        out_shape=[
            jax.ShapeDtypeStruct((N_TOK, HALF), F32),
            jax.ShapeDtypeStruct((N_TOK, 3 * HALF), BF16),
            jax.ShapeDtypeStruct((BATCH, HALF, SEQ), F32),
            jax.ShapeDtypeStruct((BATCH, 1, SEQ, DA_HEADS, DA_DV), F32),
        ],
        compiler_params=_params(),
        name="ab_in",
    )(xp, xs, mod, g, w, cos_t, sin_t)


def _rope_tables():
    pos = jnp.arange(DEC_SEQ)
    t_row = (pos // GRID_W).astype(F32)
    t_col = (pos % GRID_W).astype(F32)
    inv = ROPE_THETA ** (-jnp.arange(ROPE_F, dtype=F32) / ROPE_F)
    ang = jnp.stack([t_row[:, None] * inv, t_col[:, None] * inv], axis=1)
    cos, sin = jnp.cos(ang), jnp.sin(ang)
    cos64 = jnp.stack([cos, cos], axis=2).reshape(DEC_SEQ, DA_DK)
    sin64 = jnp.stack([-sin, sin], axis=2).reshape(DEC_SEQ, DA_DK)
    cos_t = jnp.concatenate([jnp.ones((TW, DA_DV), F32), jnp.tile(cos64, (1, 2))], axis=0)
    sin_t = jnp.concatenate([jnp.zeros((TW, DA_DV), F32), jnp.tile(sin64, (1, 2))], axis=0)
    return cos_t, sin_t


def _s5_param_kernel(lre_ref, lim_ref, ldt_ref, bt_ref, c_ref, m1_ref, m2_ref, m3_ref, at_ref):
    nt = (((1,), (1,)), ((), ()))
    e_tab, f_tab, k_tab = [], [], []
    r_i = lax.broadcasted_iota(jnp.int32, (LANES, LANES), 0) // S5_P
    c_i = lax.broadcasted_iota(jnp.int32, (LANES, LANES), 1) // S5_P
    same_group = r_i == c_i
    for d in range(2):
        lre = lre_ref[d]
        lim = lim_ref[d]
        dt = jnp.exp(ldt_ref[d])
        mag = jnp.exp(lre * dt)
        ang = lim * dt
        ar = mag * jnp.cos(ang)
        ai = mag * jnp.sin(ang)
        den = lre * lre + lim * lim
        nr = ar - 1.0
        fr = (nr * lre + ai * lim) / den
        fi = (ai * lre - nr * lim) / den
        btr = bt_ref[d, 0]
        bti = bt_ref[d, 1]
        bfr = fr * btr - fi * bti
        bfi = fr * bti + fi * btr
        cr = c_ref[d, 0]
        ci = c_ref[d, 1]
        cst = jnp.concatenate([cr, -ci], axis=1)
        pr = jnp.ones_like(ar)
        pi = jnp.zeros_like(ar)
        e_d, f_d, k_d = [], [], []
        for tau in range(S5_T + 1):
            f_d.append((cr * pr - ci * pi, cr * pi + ci * pr))
            if tau < S5_T:
                er, ei = pr * bfr - pi * bfi, pr * bfi + pi * bfr
                e_d.append((er, ei))
                full = lax.dot_general(jnp.concatenate([er, ei], axis=1), cst, nt,
                                       precision=HIGHEST, preferred_element_type=F32)
                k_d.append(jnp.where(same_group, full, 0.0))
            else:
                at_ref[d, 0] = pr
                at_ref[d, 1] = pi
            pr, pi = pr * ar - pi * ai, pr * ai + pi * ar
        e_tab.append(e_d)
        f_tab.append(f_d)
        k_tab.append(k_d)

    for s in range(S5_T):
        for t in range(S5_T):
            blk = k_tab[0][t - s] if t >= s else k_tab[1][s - t]
            if t == s:
                blk = blk + k_tab[1][0]
            m1_ref[s * LANES:(s + 1) * LANES, t * LANES:(t + 1) * LANES] = blk.astype(BF16)
    m2_ref[...] = jnp.zeros(m2_ref.shape, BF16)
    m3_ref[...] = jnp.zeros(m3_ref.shape, BF16)
    for s in range(S5_T):
        ef, er_ = e_tab[0][S5_T - 1 - s], e_tab[1][s]
        v = jnp.concatenate([ef[0], er_[0], ef[1], er_[1]], axis=1).astype(BF16)
        ff, fr_ = f_tab[0][s + 1], f_tab[1][S5_T - s]
        w = jnp.concatenate([ff[0], fr_[0], -ff[1], -fr_[1]], axis=1).astype(BF16)
        for gi in range(S5_GB):
            rows = slice(s * LANES + gi * S5_P, s * LANES + (gi + 1) * S5_P)
            cols = slice(gi * S5_ST, (gi + 1) * S5_ST)
            m2_ref[rows, cols] = v[gi * S5_P:(gi + 1) * S5_P]
            m3_ref[rows, cols] = w[gi * S5_P:(gi + 1) * S5_P]


def _s5_params(lam_re, lam_im, log_dt, b_w, c_w):
    rows = S5_G * S5_P
    per_row = lambda a: jnp.repeat(a, S5_P, axis=1)
    lre = per_row(lam_re)
    lim = per_row(lam_im)
    ldt = per_row(jnp.broadcast_to(log_dt[:, :, None], (2, S5_G, S5_N)))
    bt = b_w.transpose(0, 1, 2, 4, 3).reshape(2, 2, rows, S5_N)
    cc = c_w.reshape(2, 2, rows, S5_N)
    vec = pl.BlockSpec((2, LANES, S5_N), lambda j: (0, j, 0))
    mat = pl.BlockSpec((2, 2, LANES, S5_N), lambda j: (0, 0, j, 0))
    wide = lambda n: pl.BlockSpec((None, S5_K, n), lambda j: (j, 0, 0))
    m1, m2, m3, at = pl.pallas_call(
        _s5_param_kernel,
        grid=(S5_NB,),
        in_specs=[vec, vec, vec, mat, mat],
        out_specs=[wide(S5_K), wide(S5_GB * S5_ST), wide(S5_GB * S5_ST),
                   pl.BlockSpec((None, 2, 2, LANES, S5_N), lambda j: (j, 0, 0, 0, 0))],
        out_shape=[
            jax.ShapeDtypeStruct((S5_NB, S5_K, S5_K), BF16),
            jax.ShapeDtypeStruct((S5_NB, S5_K, S5_GB * S5_ST), BF16),
            jax.ShapeDtypeStruct((S5_NB, S5_K, S5_GB * S5_ST), BF16),
            jax.ShapeDtypeStruct((S5_NB, 2, 2, LANES, S5_N), F32),
        ],
        compiler_params=_params(),
        name="s5_params",
    )(lre, lim, ldt, bt, cc)
    coef = at[:, :, :, ::S5_P, :].transpose(0, 3, 2, 1, 4).reshape(S5_NB, S5_GB, 2, 2 * S5_N)
    return m1, m2, m3, coef


def _s5_kernel(u_ref, m1_ref, m2_ref, m3_ref, coef_ref, d_ref, *rest, stride, nc, two_pass):
    if two_pass:
        h0_ref, y_ref, lhs, yacc, s_scr, hp_scr = rest
    else:
        y_ref, fin_ref, lhs, yacc, s_scr, hp_scr = rest
    ln = 2 * S5_N
    nt = (((1,), (1,)), ((), ()))

    def seq_rows(k, c):
        return pl.ds(pl.multiple_of(k * stride + c * S5_T, S5_T), S5_T)

    def gather(c2, carry):
        halves = []
        for c in (2 * c2, 2 * c2 + 1):
            tiles = [u_ref[seq_rows(k, c), :] for k in range(8)]
            halves.append(jnp.stack(tiles, axis=0).reshape(8, S5_K))
        lhs[pl.ds(pl.multiple_of(c2 * 16, 16), 16), :] = jnp.concatenate(halves, axis=0).astype(BF16)
        return carry

    lax.fori_loop(0, nc // 2, gather, 0)
    lb = lhs[...]
    yacc[...] = jnp.dot(lb, m1_ref[...], preferred_element_type=F32)

    lane = lax.broadcasted_iota(jnp.int32, (8, ln), 1)
    row = lax.broadcasted_iota(jnp.int32, (8, ln), 0)
    fwd = lane < S5_N
    odd = (row % 2) == 1
    wide = S5_GS * S5_ST
    for gq in range(S5_GB // S5_GS):
        groups = range(gq * S5_GS, (gq + 1) * S5_GS)
        cols = slice(gq * wide, (gq + 1) * wide)
        s_scr[...] = jnp.dot(lb, m2_ref[:, cols], preferred_element_type=F32)
        are = [jnp.broadcast_to(coef_ref[g, 0:1, :], (8, ln)) for g in groups]
        aim = [jnp.broadcast_to(coef_ref[g, 1:2, :], (8, ln)) for g in groups]

        def scan(init, store):
            def body(c, carry):
                kf = pl.multiple_of(c * 8, 8)
                kr = pl.multiple_of((nc - 1 - c) * 8, 8)
                out = []
                for j in range(S5_GS):
                    p, q = carry[2 * j], carry[2 * j + 1]
                    o = j * S5_ST
                    if store:
                        hp_scr[pl.ds(kf, 8), o:o + S5_N] = p[:, 0:S5_N]
                        hp_scr[pl.ds(kr, 8), o + S5_N:o + ln] = p[:, S5_N:ln]
                        hp_scr[pl.ds(kf, 8), o + ln:o + ln + S5_N] = q[:, 0:S5_N]
                        hp_scr[pl.ds(kr, 8), o + ln + S5_N:o + 2 * ln] = q[:, S5_N:ln]
                    s_re = jnp.where(fwd, s_scr[pl.ds(kf, 8), o:o + ln], s_scr[pl.ds(kr, 8), o:o + ln])
                    s_im = jnp.where(fwd, s_scr[pl.ds(kf, 8), o + ln:o + 2 * ln],
                                     s_scr[pl.ds(kr, 8), o + ln:o + 2 * ln])
                    out += [are[j] * p - aim[j] * q + s_re, are[j] * q + aim[j] * p + s_im]
                return tuple(out)
            return lax.fori_loop(0, nc, body, tuple(init))

        if two_pass:
            h0 = [h0_ref[g, r] for g in groups for r in range(2)]
            fin = scan(h0, False)
            seed = [h + jnp.where(fwd & odd, pltpu.roll(f, 1, axis=0), 0.0)
                    + jnp.where(jnp.logical_not(fwd | odd), pltpu.roll(f, 7, axis=0), 0.0)
                    for h, f in zip(h0, fin)]
            scan(seed, True)
        else:
            fin = scan([jnp.zeros((8, ln), F32)] * (2 * S5_GS), True)
            for j, g in enumerate(groups):
                fin_ref[g, 0] = fin[2 * j]
                fin_ref[g, 1] = fin[2 * j + 1]
        yacc[...] += lax.dot_general(hp_scr[...].astype(BF16), m3_ref[:, cols], nt, preferred_element_type=F32)

    dvec = d_ref[...]

    def scatter(c, carry):
        blk = yacc[pl.ds(pl.multiple_of(c * 8, 8), 8), :].reshape(8, S5_T, LANES)
        for k in range(8):
            rows = seq_rows(k, c)
            y_ref[rows, :] = blk[k] + u_ref[rows, :] * dvec
        return carry

    lax.fori_loop(0, nc, scatter, 0)


def _s5_call(u, row_block, mats, coef, dvec, h0, stride, nc, parts):
    m1, m2, m3 = mats
    rows = 8 * stride
    two_pass = h0 is not None
    mat = lambda n: pl.BlockSpec((None, S5_K, n), lambda j, i: (j, 0, 0))
    tok = pl.BlockSpec((rows, LANES), lambda j, i: (row_block * parts + i, j))
    in_specs = [
        tok, mat(S5_K), mat(S5_GB * S5_ST), mat(S5_GB * S5_ST),
        pl.BlockSpec((None, S5_GB, 2, 2 * S5_N), lambda j, i: (j, 0, 0, 0)),
        pl.BlockSpec((None, 1, LANES), lambda j, i: (j, 0, 0)),
    ]
    args = [u, m1, m2, m3, coef, dvec]
    out_specs = [pl.BlockSpec((rows, LANES), lambda j, i: (i, j))]
    out_shape = [jax.ShapeDtypeStruct((parts * rows, HALF), F32)]
    if two_pass:
        in_specs.append(pl.BlockSpec((None, S5_GB, 2, 8, 2 * S5_N), lambda j, i: (j, 0, 0, 0, 0)))
        args.append(h0)
    else:
        out_specs.append(pl.BlockSpec((None, S5_GB, 2, 8, 2 * S5_N), lambda j, i: (j, 0, 0, i, 0)))
        out_shape.append(jax.ShapeDtypeStruct((S5_NB, S5_GB, 2, parts * 8, 2 * S5_N), F32))
    return pl.pallas_call(
        functools.partial(_s5_kernel, stride=stride, nc=nc, two_pass=two_pass),
        grid=(S5_NB, parts),
        in_specs=in_specs,
        out_specs=out_specs,
        out_shape=out_shape,
        scratch_shapes=[
            pltpu.VMEM((nc * 8, S5_K), BF16),
            pltpu.VMEM((nc * 8, S5_K), F32),
            pltpu.VMEM((nc * 8, S5_GS * S5_ST), F32),
            pltpu.VMEM((nc * 8, S5_GS * S5_ST), F32),
        ],
        compiler_params=_params(("arbitrary", "arbitrary")),
        name="s5_%d" % stride,
    )(*args)


def _s5(u, mats, coef, h0, s5_d):
    dvec = s5_d.reshape(S5_NB, 1, LANES)
    yp, fin = _s5_call(u, 0, mats, coef, dvec, None, SEQ, SEQ // S5_T, BATCH // 8)
    hh = h0.transpose(3, 2, 0, 1, 4)
    z = jnp.zeros_like(hh[:, :, :, 0])
    first = jnp.concatenate([hh[:, :, :, 0], z], axis=-1)
    second = jnp.concatenate([z, hh[:, :, :, 1]], axis=-1)
    h0g = jnp.stack([first, second], axis=3).reshape(S5_NB, S5_GB, 2, 2 * DEC_BATCH, 2 * S5_N)
    half = DEC_SEQ // 2
    (ys,) = _s5_call(u, 1, mats, coef, dvec, h0g, half, half // S5_T, 1)
    new_s5 = fin.reshape(S5_G, 2, BATCH, 2, S5_N).transpose(2, 3, 1, 0, 4)
    return yp, ys, new_s5


def _attn_kernel(*refs, lam_init, has_ctx, lk):
    if has_ctx:
        q_ref, k_ref, v_ref, kc_ref, vc_ref, dl_ref, g_ref, o_ref, s_a, s_b, vx, vcx = refs
    else:
        q_ref, k_ref, v_ref, dl_ref, g_ref, o_ref, s_a, s_b, vx = refs
    lt = lk + (PAST_LEN if has_ctx else 0)
    dl = dl_ref[...]
    lam = (jnp.exp(jnp.sum(dl[0:1] * dl[1:2], keepdims=True))
           - jnp.exp(jnp.sum(dl[2:3] * dl[3:4], keepdims=True)) + lam_init)
    lane = lax.broadcasted_iota(jnp.int32, (1, DA_DV), 1)
    first = lane < DA_DK
    g = g_ref[...]
    nt = (((1,), (1,)), ((), ()))

    @pl.when(pl.program_id(1) == 0)
    def _():
        onehot = (lane == 0).astype(BF16)
        for h in range(DA_HEADS):
            cols = slice(h * DA_DV, (h + 1) * DA_DV)
            vx[:, 2 * h * DA_DV:(2 * h + 1) * DA_DV] = v_ref[:, cols]
            vx[:, (2 * h + 1) * DA_DV:(2 * h + 2) * DA_DV] = jnp.broadcast_to(onehot, (lk, DA_DV))
            if has_ctx:
                vcx[:, 2 * h * DA_DV:(2 * h + 1) * DA_DV] = vc_ref[:, cols]
                vcx[:, (2 * h + 1) * DA_DV:(2 * h + 2) * DA_DV] = jnp.broadcast_to(onehot, (PAST_LEN, DA_DV))

    def scores(h, dst):
        cols = slice(h * DA_DV, (h + 1) * DA_DV)
        qh = q_ref[:, cols]
        zero = jnp.zeros_like(qh)
        for m in range(2):
            qm = jnp.where(first if m == 0 else jnp.logical_not(first), qh, zero)
            dst[m, :, 0:lk] = lax.dot_general(qm, k_ref[:, cols], nt, preferred_element_type=F32)
            if has_ctx:
                dst[m, :, lk:lt] = lax.dot_general(qm, kc_ref[:, cols], nt, preferred_element_type=F32)

    def head(h, src):
        xcols = slice(2 * h * DA_DV, (2 * h + 2) * DA_DV)
        outs = []
        for m in range(2):
            s = src[m]
            e = jnp.exp2(s - jnp.max(s, axis=-1, keepdims=True)).astype(BF16)
            ox = jnp.dot(e[:, 0:lk], vx[:, xcols], preferred_element_type=F32)
            if has_ctx:
                ox = ox + jnp.dot(e[:, lk:lt], vcx[:, xcols], preferred_element_type=F32)
            outs.append(ox[:, 0:DA_DV] * (1.0 / ox[:, DA_DV:DA_DV + 1]))
        o = outs[0] - lam * outs[1]
        o_ref[:, h * DA_DV:(h + 1) * DA_DV] = (_rms(o, g) * (1.0 - lam_init)).astype(BF16)

    scores(0, s_a)
    for h in range(DA_HEADS):
        cur, nxt = (s_a, s_b) if h % 2 == 0 else (s_b, s_a)
        if h + 1 < DA_HEADS:
            scores(h + 1, nxt)
        head(h, cur)


def _attention(qkv, q_tile0, nb, nq, lk, ctx, da_lam, da_g, lam_init):
    kb0 = q_tile0 * TM // lk
    in_specs = [
        pl.BlockSpec((TM, HALF), lambda b, j: (q_tile0 + b * nq + j, 0)),
        pl.BlockSpec((lk, HALF), lambda b, j: (kb0 + b, 1)),
        pl.BlockSpec((lk, HALF), lambda b, j: (kb0 + b, 2)),
    ]
    args = [qkv, qkv, qkv]
    if ctx is not None:
        in_specs += [pl.BlockSpec((None, PAST_LEN, HALF), lambda b, j: (b, 0, 0))] * 2
        args += list(ctx)
    in_specs += [
        pl.BlockSpec((4, DA_DK), lambda b, j: (0, 0)),
        pl.BlockSpec((1, DA_DV), lambda b, j: (0, 0)),
    ]
    lt = lk + (PAST_LEN if ctx is not None else 0)
    scratch = [pltpu.VMEM((2, TM, lt), F32), pltpu.VMEM((2, TM, lt), F32), pltpu.VMEM((lk, 2 * HALF), BF16)]
    if ctx is not None:
        scratch.append(pltpu.VMEM((PAST_LEN, 2 * HALF), BF16))
    return pl.pallas_call(
        functools.partial(_attn_kernel, lam_init=lam_init, has_ctx=ctx is not None, lk=lk),
        grid=(nb, nq),
        in_specs=in_specs,
        out_specs=pl.BlockSpec((TM, HALF), lambda b, j: (b * nq + j, 0)),
        out_shape=jax.ShapeDtypeStruct((nb * nq * TM, HALF), BF16),
        scratch_shapes=scratch,
        compiler_params=_params(("arbitrary", "arbitrary")),
        name="diff_attn_%d" % lk,
    )(*args, da_lam, da_g.reshape(1, DA_DV))


def _post(x, y, g, gate):
    return x + gate * _rms(y, g)


def _ab_out_kernel(ysp_ref, yss_ref, ybp_ref, ybs_ref, xp_ref, xs_ref, mod_ref, g_ref, wg_ref, bg_ref, wo_ref,
                   o_ref):
    ys = _gelu(_pick(ysp_ref, yss_ref, NW_P))
    glu = jnp.dot(ys.astype(BF16), wg_ref[...], preferred_element_type=F32) + bg_ref[...]
    ya = ys * _sigmoid(glu)
    out = (jnp.dot(ya.astype(BF16), wo_ref[0:HALF, :], preferred_element_type=F32)
           + jnp.dot(_pick(ybp_ref, ybs_ref, NW_P), wo_ref[HALF:2 * HALF, :], preferred_element_type=F32))
    o_ref[...] = _post(_pick(xp_ref, xs_ref, NW_P), out, g_ref[...], mod_ref[2:3, :])


def _ab_out(ys5_p, ys5_s, yb_p, yb_s, xp, xs, mod, g, w_glu, b_glu, w_out):
    return pl.pallas_call(
        _ab_out_kernel,
        grid=(NW,),
        in_specs=_split_specs(HALF, TW, NW_P) + _split_specs(HALF, TW, NW_P) + _split_specs(D_MODEL, TW, NW_P) + [
            pl.BlockSpec((None, 6, D_MODEL), lambda i: (_mod_row(i, NW_P, WPS), 0, 0)),
            pl.BlockSpec((1, D_MODEL), lambda i: (0, 0)),
            pl.BlockSpec((HALF, HALF), lambda i: (0, 0)),
            pl.BlockSpec((1, HALF), lambda i: (0, 0)),
            pl.BlockSpec((D_MODEL, D_MODEL), lambda i: (0, 0)),
        ],
        out_specs=pl.BlockSpec((TW, D_MODEL), lambda i: (i, 0)),
        out_shape=jax.ShapeDtypeStruct((N_TOK, D_MODEL), F32),
        compiler_params=_params(),
        name="ab_out",
    )(ys5_p, ys5_s, yb_p, yb_s, xp, xs, mod, g, w_glu, b_glu, w_out)


def _halo_specs(width):
    blocks = TM // HALO
    last = N_TOK // HALO - 1
    return [
        pl.BlockSpec((TM, width), lambda i: (i, 0)),
        pl.BlockSpec((HALO, width), lambda i: (jnp.maximum(i * blocks - 1, 0), 0)),
        pl.BlockSpec((HALO, width), lambda i: (jnp.minimum((i + 1) * blocks, last), 0)),
    ]


def _fill_hbuf(hbuf, x_ref, xp_ref, xn_ref, g, shift, scale, i):
    pos, n = _seq_pos(i)
    hp = jnp.where(pos > 0, _pre(xp_ref[...], g, shift, scale), 0.0)
    hn = jnp.where(pos < n - 1, _pre(xn_ref[...], g, shift, scale), 0.0)
    hbuf[0:TM, :] = _pre(x_ref[...], g, shift, scale).astype(BF16)
    hbuf[TM:ROWS, :] = jnp.concatenate([hn, hp], axis=0).astype(BF16)


def _shift_rows(x, s):
    if s == 0:
        return x[0:TM]
    return pltpu.roll(x, (-s) % ROWS, axis=0)[0:TM]


def _ffn_kernel(x_ref, xp_ref, xn_ref, mod_ref, g2_ref, g3_ref, wg_ref, wv_ref, wd_ref, cw_ref, cb_ref, *rest, split):
    hbuf, acc, u_a, u_b, wu_s, wd_s = rest[-6:]
    step = pl.program_id(0)

    @pl.when(step < FF_NCH)
    def _():
        wu_s[step] = wg_ref[...].astype(BF16)
        wu_s[FF_NCH + step] = wv_ref[...].astype(BF16)
        wd_s[step] = wd_ref[...].astype(BF16)

    @pl.when(step >= FF_NCH)
    def _():
        i = step - FF_NCH
        _fill_hbuf(hbuf, x_ref, xp_ref, xn_ref, g2_ref[...], mod_ref[3:4, :], mod_ref[4:5, :], i)

        def up(j, dst):
            for half in range(2):
                dst[half] = jnp.dot(hbuf[...], wu_s[half * FF_NCH + j], preferred_element_type=F32)

        def activation(j, src):
            parts = []
            for half in range(2):
                c0 = half * D_FF + j * FF_CHUNK
                sc = 1.0 if half == 0 else 0.5
                cw = cw_ref[:, c0:c0 + FF_CHUNK] * sc
                u = src[half]
                parts.append(_shift_rows(u, -1) * cw[0:1] + _shift_rows(u, 0) * cw[1:2]
                             + _shift_rows(u, 1) * cw[2:3] + cb_ref[:, c0:c0 + FF_CHUNK] * sc)
            gt = parts[0]
            z = gt * (gt * gt * (GELU_C * GELU_K) + GELU_K)
            return ((gt + gt * jnp.tanh(z)) * parts[1]).astype(BF16)

        up(0, u_a)
        for j in range(FF_NCH):
            cur, nxt = (u_a, u_b) if j % 2 == 0 else (u_b, u_a)
            if j + 1 < FF_NCH:
                up(j + 1, nxt)
            contrib = jnp.dot(activation(j, cur), wd_s[j], preferred_element_type=F32)
            if j == 0:
                acc[...] = contrib
            else:
                acc[...] += contrib
        res = _post(x_ref[...], acc[...], g3_ref[...], mod_ref[5:6, :])
        if split:
            @pl.when(i < NT_P)
            def _():
                rest[0][...] = res

            @pl.when(i >= NT_P)
            def _():
                rest[1][...] = res
        else:
            rest[0][...] = res


def _ffn(x, mod, g2, g3, w_up, cw, cb, w_down, layer, split):
    const = lambda s: (0, 0)
    tile = lambda s: jnp.maximum(s - FF_NCH, 0)
    chunk = lambda s: jnp.minimum(s, FF_NCH - 1)
    shifted = lambda spec: pl.BlockSpec(spec.block_shape, lambda s, f=spec.index_map: f(tile(s)))
    if split:
        out_specs = [shifted(sp) for sp in _split_specs(D_MODEL)]
        out_shape = [jax.ShapeDtypeStruct((NT_P * TM, D_MODEL), F32), jax.ShapeDtypeStruct((NT_S * TM, D_MODEL), F32)]
    else:
        out_specs = pl.BlockSpec((TM, D_MODEL), lambda s: (tile(s), 0))
        out_shape = jax.ShapeDtypeStruct((N_TOK, D_MODEL), F32)
    return pl.pallas_call(
        functools.partial(_ffn_kernel, split=split),
        grid=(FF_NCH + NT,),
        in_specs=[shifted(sp) for sp in _halo_specs(D_MODEL)] + [
            pl.BlockSpec((None, 6, D_MODEL), lambda s: (_mod_row(tile(s)), 0, 0)),
            pl.BlockSpec((1, D_MODEL), const),
            pl.BlockSpec((1, D_MODEL), const),
            pl.BlockSpec((None, D_MODEL, FF_CHUNK), lambda s: (layer, 0, chunk(s))),
            pl.BlockSpec((None, D_MODEL, FF_CHUNK), lambda s: (layer, 0, FF_NCH + chunk(s))),
            pl.BlockSpec((None, FF_CHUNK, D_MODEL), lambda s: (layer, chunk(s), 0)),
            pl.BlockSpec((3, 2 * D_FF), const),
            pl.BlockSpec((1, 2 * D_FF), const),
        ],
        out_specs=out_specs,
        out_shape=out_shape,
        scratch_shapes=[
            pltpu.VMEM((ROWS, D_MODEL), BF16),
            pltpu.VMEM((TM, D_MODEL), F32),
            pltpu.VMEM((2, ROWS, FF_CHUNK), F32),
            pltpu.VMEM((2, ROWS, FF_CHUNK), F32),
            pltpu.VMEM((2 * FF_NCH, D_MODEL, FF_CHUNK), BF16),
            pltpu.VMEM((FF_NCH, FF_CHUNK, D_MODEL), BF16),
        ],
        compiler_params=_params(),
        name="conv_ffn",
    )(x, x, x, mod, g2, g3, w_up, w_up, w_down, cw, cb)


def _softplus(z):
    e = jnp.exp(-jnp.abs(z))
    u = 1.0 + e
    tiny = u == 1.0
    log1p = jnp.where(tiny, e, jnp.log(u) * (e / jnp.where(tiny, 1.0, u - 1.0)))
    return jnp.maximum(z, 0.0) + log1p


def _lru_scan(a_ref, b_ref, hs_ref, p_scr, carry, reverse):
    half = TM // 2
    h_a = carry[...]
    h_b = jnp.zeros_like(h_a)
    p_b = jnp.ones_like(h_a)
    for j in range(half):
        r_a = TM - 1 - j if reverse else j
        r_b = half - 1 - j if reverse else half + j
        h_a = a_ref[r_a:r_a + 1, :] * h_a + b_ref[r_a:r_a + 1, :]
        hs_ref[r_a:r_a + 1, :] = h_a
        a_b = a_ref[r_b:r_b + 1, :]
        h_b = a_b * h_b + b_ref[r_b:r_b + 1, :]
        p_b = p_b * a_b
        hs_ref[r_b:r_b + 1, :] = h_b
        p_scr[r_b % half:r_b % half + 1, :] = p_b
    second = slice(0, half) if reverse else slice(half, TM)
    hs_ref[second, :] = hs_ref[second, :] + p_scr[...] * h_a
    h = h_b + p_b * h_a
    carry[...] = h
    return h


def _cd_in_kernel(x_ref, xp_ref, xn_ref, mod_ref, g_ref, w_ref, scw_ref, cw_ref, cb_ref, wg_ref, bg_ref,
                  lam_ref, h0_ref, yc_ref, gate_ref, hsf_ref, ar_ref, br_ref, fin_ref,
                  hbuf, a_scr, b_scr, p_scr, carry):
    i = pl.program_id(0)
    pos, _ = _seq_pos(i)
    _fill_hbuf(hbuf, x_ref, xp_ref, xn_ref, g_ref[...], mod_ref[0:1, :], mod_ref[1:2, :], i)

    def col(k, rows):
        return jnp.dot(hbuf[0:rows, :], w_ref[:, k * HALF:(k + 1) * HALF], preferred_element_type=F32)

    xr = col(3, ROWS)
    xin = col(0, ROWS)
    cg = col(2, ROWS)
    cw = cw_ref[...]
    xc = (_shift_rows(xr, -2) * cw[0:1] + _shift_rows(xr, -1) * cw[1:2] + _shift_rows(xr, 0) * cw[2:3]
          + _shift_rows(xr, 1) * cw[3:4] + cb_ref[...])
    xcb = xc.astype(BF16)

    def direction(d):
        cols = slice(2 * d * HALF, (2 * d + 2) * HALF)
        gates = _sigmoid(jnp.dot(xcb, wg_ref[:, cols], preferred_element_type=F32) + bg_ref[:, cols])
        log_a = (-LRU_C) * gates[:, 0:HALF] * _softplus(-lam_ref[d:d + 1, :])
        a = jnp.exp(log_a)
        drive = jnp.sqrt(-jnp.tanh(log_a) * (a * a + 1.0))
        return a, drive * (gates[:, HALF:2 * HALF] * xc)

    a, bval = direction(0)
    a_scr[...] = a
    b_scr[...] = bval

    @pl.when(pos == 0)
    def _():
        carry[...] = h0_ref[0:1, :]

    h = _lru_scan(a_scr, b_scr, hsf_ref, p_scr, carry, False)
    fin_ref[...] = jnp.broadcast_to(h, (8, HALF))

    a, bval = direction(1)
    ar_ref[...] = a
    br_ref[...] = bval
    bg = col(1, TM)
    gb = col(4, TM)
    prod = cg * xin
    scw = scw_ref[...]
    yc = bg * (_shift_rows(prod, -1) * scw[0:1] + _shift_rows(prod, 0) * scw[1:2]
               + _shift_rows(prod, 1) * scw[2:3])
    yc_ref[...] = yc.astype(BF16)
    gate_ref[...] = _gelu(gb).astype(BF16)


def _cd_in(x, mod, g, w_in, sc_w, conv_w, conv_b, w_gates, b_gates, lru_lam, h0t):
    const = lambda i: (0, 0)
    tok = lambda dt: jax.ShapeDtypeStruct((N_TOK, HALF), dt)
    row = pl.BlockSpec((TM, HALF), lambda i: (i, 0))
    return pl.pallas_call(
        _cd_in_kernel,
        grid=(NT,),
        in_specs=_halo_specs(D_MODEL) + [
            pl.BlockSpec((None, 6, D_MODEL), lambda i: (_mod_row(i), 0, 0)),
            pl.BlockSpec((1, D_MODEL), const),
            pl.BlockSpec((D_MODEL, 5 * HALF), const),
            pl.BlockSpec((3, HALF), const),
            pl.BlockSpec((4, HALF), const),
            pl.BlockSpec((1, HALF), const),
            pl.BlockSpec((HALF, 4 * HALF), const),
            pl.BlockSpec((1, 4 * HALF), const),
            pl.BlockSpec((2, HALF), const),
            pl.BlockSpec((None, 2, HALF), lambda i: (_mod_row(i), 0, 0)),
        ],
        out_specs=[row, row, row, row, row, pl.BlockSpec((8, HALF), lambda i: (i, 0))],
        out_shape=[tok(BF16), tok(BF16), tok(F32), tok(F32), tok(F32), jax.ShapeDtypeStruct((NT * 8, HALF), F32)],
        scratch_shapes=[
            pltpu.VMEM((ROWS, D_MODEL), BF16),
            pltpu.VMEM((TM, HALF), F32),
            pltpu.VMEM((TM, HALF), F32),
            pltpu.VMEM((TM // 2, HALF), F32),
            pltpu.VMEM((1, HALF), F32),
        ],
        compiler_params=_params(),
        name="cd_in",
    )(x, x, x, mod, g, w_in, sc_w, conv_w, conv_b, w_gates, b_gates, lru_lam, h0t)


def _cd_out_kernel(ar_ref, br_ref, hsf_ref, gate_ref, yc_ref, x_ref, mod_ref, g_ref, wo_ref, h0_ref,
                   o_ref, fin_ref, hs_scr, p_scr, carry):
    ti = NT - 1 - pl.program_id(0)
    pos, n = _seq_pos(ti)

    @pl.when(pos == n - 1)
    def _():
        carry[...] = h0_ref[1:2, :]

    h = _lru_scan(ar_ref, br_ref, hs_scr, p_scr, carry, True)
    fin_ref[...] = jnp.broadcast_to(h, (8, HALF))
    out_c = jnp.dot(yc_ref[...], wo_ref[0:HALF, :], preferred_element_type=F32)
    yd = (hsf_ref[...] + hs_scr[...]) * gate_ref[...].astype(F32)
    out = out_c + jnp.dot(yd.astype(BF16), wo_ref[HALF:2 * HALF, :], preferred_element_type=F32)
    o_ref[...] = _post(x_ref[...], out, g_ref[...], mod_ref[2:3, :])


def _cd_out(a_r, b_r, hs_f, gate, yc, x, mod, g, w_out, h0t):
    const = lambda i: (0, 0)
    rev = lambda i: (NT - 1 - i, 0)
    row = pl.BlockSpec((TM, HALF), rev)
    return pl.pallas_call(
        _cd_out_kernel,
        grid=(NT,),
        in_specs=[
            row, row, row, row, row,
            pl.BlockSpec((TM, D_MODEL), rev),
            pl.BlockSpec((None, 6, D_MODEL), lambda i: (_mod_row(NT - 1 - i), 0, 0)),
            pl.BlockSpec((1, D_MODEL), const),
            pl.BlockSpec((D_MODEL, D_MODEL), const),
            pl.BlockSpec((None, 2, HALF), lambda i: (_mod_row(NT - 1 - i), 0, 0)),
        ],
        out_specs=[
            pl.BlockSpec((TM, D_MODEL), rev),
            pl.BlockSpec((8, HALF), rev),
        ],
        out_shape=[
            jax.ShapeDtypeStruct((N_TOK, D_MODEL), F32),
            jax.ShapeDtypeStruct((NT * 8, HALF), F32),
        ],
        scratch_shapes=[
            pltpu.VMEM((TM, HALF), F32),
            pltpu.VMEM((TM // 2, HALF), F32),
            pltpu.VMEM((1, HALF), F32),
        ],
        compiler_params=_params(),
        name="cd_out",
    )(a_r, b_r, hs_f, gate, yc, x, mod, g, w_out, h0t)


def kernel(x_prompt, x_sample, cache_attn_k, cache_attn_v, state_s5, state_rglru, c, c_ctx, w_mod, b_mod, norm_g, w_in_ab, w_out_ab, s5_lam_re, s5_lam_im, s5_log_dt, s5_b, s5_c, s5_d, s5_w_glu, s5_b_glu, da_lam, da_g, w_in_cd, w_out_cd, sc_conv_w, lru_conv_w, lru_conv_b, lru_w_a, lru_b_a, lru_w_x, lru_b_x, lru_lam, ffn_w_up, ffn_conv_w, ffn_conv_b, ffn_w_down):
    assert DEPTH == 2
    xp = x_prompt.reshape(NT_P * TM, D_MODEL)
    xs = x_sample.reshape(NT_S * TM, D_MODEL)
    cv8 = jnp.zeros((8, D_MODEL), F32).at[0].set(c_ctx).at[1:1 + DEC_BATCH].set(c)
    mod = _modulation(cv8, w_mod, b_mod)
    cos_t, sin_t = _rope_tables()
    g = norm_g.reshape(DEPTH, 4, 1, D_MODEL)

    lam_init = 0.8 - 0.6 * math.exp(-0.3 * 0)
    u, qkv, k32, v32 = _ab_in(xp, xs, mod[0], g[0, 0], w_in_ab[0].astype(BF16), cos_t, sin_t)
    m1, m2, m3, coef = _s5_params(s5_lam_re[0], s5_lam_im[0], s5_log_dt[0], s5_b[0], s5_c[0])
    ys5_p, ys5_s, new_s5 = _s5(u, (m1, m2, m3), coef, state_s5[:, 0], s5_d[0])
    yb_p = _attention(qkv, 0, BATCH, 1, SEQ, None, da_lam[0], da_g[0], lam_init)
    ctx = (cache_attn_k[:, 0].reshape(DEC_BATCH, PAST_LEN, HALF).astype(BF16),
           cache_attn_v[:, 0].reshape(DEC_BATCH, PAST_LEN, HALF).astype(BF16))
    yb_s = _attention(qkv, NT_P, DEC_BATCH, TPS, DEC_SEQ, ctx, da_lam[0], da_g[0], lam_init)
    x = _ab_out(ys5_p, ys5_s, yb_p, yb_s, xp, xs, mod[0], g[0, 1], s5_w_glu[0].astype(BF16), s5_b_glu[0].reshape(1, HALF),
                w_out_ab[0].astype(BF16))
    assert NT_P == BATCH and TM == SEQ
    new_k = k32.reshape(BATCH, 1, DA_HEADS, 2, DA_DK, SEQ).transpose(0, 1, 5, 2, 3, 4)
    new_v = v32
    x = _ffn(x, mod[0], g[0, 2], g[0, 3], ffn_w_up, ffn_conv_w[0], ffn_conv_b[0].reshape(1, 2 * D_FF), ffn_w_down,
             0, False)

    eye = jnp.eye(LRU_BLOCKS, dtype=F32)
    dense = lambda w: jnp.einsum('kcd,kl->kcld', w, eye).reshape(LRU_WIDTH, LRU_WIDTH)
    w_gates = jnp.concatenate([dense(lru_w_a[0, 0]), dense(lru_w_x[0, 0]),
                               dense(lru_w_a[0, 1]), dense(lru_w_x[0, 1])], axis=1).astype(BF16)
    b_gates = jnp.concatenate([lru_b_a[0, 0], lru_b_x[0, 0], lru_b_a[0, 1], lru_b_x[0, 1]]).reshape(1, 4 * HALF)
    h0t = jnp.zeros((8, 2, HALF), F32).at[1:1 + DEC_BATCH].set(state_rglru[:, 0])
    yc, gate, hs_f, a_r, b_r, fin_f = _cd_in(x, mod[1], g[1, 0], w_in_cd[0].astype(BF16), sc_conv_w[0], lru_conv_w[0],
                                      lru_conv_b[0].reshape(1, HALF), w_gates, b_gates, lru_lam[0], h0t)
    x, fin_r = _cd_out(a_r, b_r, hs_f, gate, yc, x, mod[1], g[1, 1], w_out_cd[0].astype(BF16), h0t)
    tile_row0 = lambda f: f.reshape(NT, 8, HALF)[:NT_P, 0]
    new_lru = jnp.stack([tile_row0(fin_f), tile_row0(fin_r)], axis=1)[:, None]
    yp, ys = _ffn(x, mod[1], g[1, 2], g[1, 3], ffn_w_up, ffn_conv_w[1], ffn_conv_b[1].reshape(1, 2 * D_FF),
                  ffn_w_down, 1, True)
    return (yp.reshape(BATCH, SEQ, D_MODEL), ys.reshape(DEC_BATCH, DEC_SEQ, D_MODEL),
            new_k, new_v, new_s5[:, None], new_lru)
```

```python
import functools
import math

import jax
import jax.numpy as jnp
from jax import lax
from jax.experimental import pallas as pl
from jax.experimental.pallas import tpu as pltpu

D_MODEL = 1024
BATCH = 32
SEQ = 256
DEPTH = 2
DEC_BATCH = 4
DEC_SEQ = 2048
PAST_LEN = 512
GRID_W = 64
HALF = D_MODEL // 2
S5_P = 16
S5_G = HALF // S5_P
S5_N = 64
DA_DK = 64
DA_DV = 2 * DA_DK
DA_HEADS = HALF // DA_DV
ROPE_THETA = 10000.0
ROPE_F = DA_DK // 4
LRU_WIDTH = HALF
LRU_BLOCKS = 8
LRU_BS = LRU_WIDTH // LRU_BLOCKS
LRU_C = 8.0
D_FF = 2816
EPS = 1e-6

F32 = jnp.float32
BF16 = jnp.bfloat16
HIGHEST = lax.Precision.HIGHEST

LANES = 128
SUBLANES = 8
VMEM_BYTES = 64 * 1024 * 1024

TM = 256
NT_P = BATCH * SEQ // TM
TPS = DEC_SEQ // TM
NT_S = DEC_BATCH * TPS
NT = NT_P + NT_S
N_TOK = NT * TM
TW = 2 * TM
NW_P = NT_P * TM // TW
WPS = DEC_SEQ // TW
NW = N_TOK // TW
HALO = 8
ROWS = TM + 2 * HALO
S5_T = 8
S5_GB = LANES // S5_P
S5_NB = S5_G // S5_GB
S5_K = S5_T * LANES
S5_ST = 4 * S5_N
S5_GS = 4
FF_CHUNK = 256
FF_NCH = D_FF // FF_CHUNK
Q_SCALE = math.log2(math.e) / math.sqrt(DA_DK)
GELU_K = math.sqrt(2.0 / math.pi)
GELU_C = 0.044715
VMEM_LIMIT = VMEM_BYTES - 8 * 1024 * 1024


def _mod_row(i, ntp=NT_P, tps=TPS):
    return jnp.where(i < ntp, 0, 1 + (i - ntp) // tps)


def _seq_pos(i):
    return jnp.where(i < NT_P, 0, (i - NT_P) % TPS), jnp.where(i < NT_P, 1, TPS)


def _split_specs(width, tm=TM, ntp=NT_P):
    return [
        pl.BlockSpec((tm, width), lambda i: (jnp.minimum(i, ntp - 1), 0)),
        pl.BlockSpec((tm, width), lambda i: (jnp.maximum(i - ntp, 0), 0)),
    ]


def _pick(p_ref, s_ref, ntp=NT_P):
    return jnp.where(pl.program_id(0) < ntp, p_ref[...], s_ref[...])


def _params(sem=("arbitrary",)):
    return pltpu.CompilerParams(dimension_semantics=sem, vmem_limit_bytes=VMEM_LIMIT)


def _rms(x, g):
    ms = jnp.mean(x * x, axis=-1, keepdims=True)
    return x * lax.rsqrt(ms + EPS) * g


def _sigmoid(x):
    return 0.5 * jnp.tanh(0.5 * x) + 0.5


def _gelu(x):
    return x * (0.5 * jnp.tanh(x * (x * x * (GELU_C * GELU_K) + GELU_K)) + 0.5)


def _pre(x, g, shift, scale):
    return _rms(x, g) * (1.0 + scale) + shift


def _mod_kernel(cv_ref, w_ref, b_ref, o_ref):
    cv = cv_ref[...]
    s = cv * jax.nn.sigmoid(cv)
    w = w_ref[...]
    s_hi = s.astype(BF16)
    s_lo = (s - s_hi.astype(F32)).astype(BF16)
    w_hi = w.astype(BF16)
    w_lo = (w - w_hi.astype(F32)).astype(BF16)
    dot = lambda a, b: jnp.dot(a, b, preferred_element_type=F32)
    o_ref[...] = dot(s_hi, w_hi) + (dot(s_hi, w_lo) + dot(s_lo, w_hi)) + b_ref[...]


def _modulation(cv8, w_mod, b_mod):
    nb = 1536
    out = pl.pallas_call(
        _mod_kernel,
        grid=(DEPTH, 6 * D_MODEL // nb),
        in_specs=[
            pl.BlockSpec((8, D_MODEL), lambda l, j: (0, 0)),
            pl.BlockSpec((None, D_MODEL, nb), lambda l, j: (l, 0, j)),
            pl.BlockSpec((None, 1, nb), lambda l, j: (l, 0, j)),
        ],
        out_specs=pl.BlockSpec((None, 8, nb), lambda l, j: (l, 0, j)),
        out_shape=jax.ShapeDtypeStruct((DEPTH, 8, 6 * D_MODEL), F32),
        compiler_params=_params(("arbitrary", "arbitrary")),
        name="modulation",
    )(cv8, w_mod, b_mod.reshape(DEPTH, 1, 6 * D_MODEL))
    return out.reshape(DEPTH, 8, 6, D_MODEL)


def _rope_partner(x):
    lane = lax.broadcasted_iota(jnp.int32, (1, HALF), 1)
    first = (lane % (2 * ROPE_F)) < ROPE_F
    return jnp.where(first, pltpu.roll(x, HALF - ROPE_F, axis=1), pltpu.roll(x, ROPE_F, axis=1))


def _ab_in_kernel(xp_ref, xs_ref, mod_ref, g_ref, w_ref, cos_ref, sin_ref, u_ref, qkv_ref, k_ref, v_ref):
    i = pl.program_id(0)
    h = _pre(_pick(xp_ref, xs_ref, NW_P), g_ref[...], mod_ref[0:1, :], mod_ref[1:2, :])
    proj = jnp.dot(h.astype(BF16), w_ref[...], preferred_element_type=F32)
    u_ref[...] = proj[:, 0:HALF]
    q = proj[:, HALF:2 * HALF]
    k = proj[:, 2 * HALF:3 * HALF]
    v = proj[:, 3 * HALF:4 * HALF]
    cos = jnp.concatenate([cos_ref[...]] * DA_HEADS, axis=1)
    sin = jnp.concatenate([sin_ref[...]] * DA_HEADS, axis=1)
    qr = q * cos + _rope_partner(q) * sin
    kr = k * cos + _rope_partner(k) * sin
    qkv_ref[:, 0:HALF] = (qr * Q_SCALE).astype(BF16)
    qkv_ref[:, HALF:2 * HALF] = kr.astype(BF16)
    qkv_ref[:, 2 * HALF:3 * HALF] = v.astype(BF16)

    @pl.when(i < NW_P)
    def _():
        kt = k.T
        for s in range(TW // SEQ):
            k_ref[s] = kt[:, s * SEQ:(s + 1) * SEQ]
            v_ref[s] = v[s * SEQ:(s + 1) * SEQ].reshape(SEQ, DA_HEADS, DA_DV)


def _ab_in(xp, xs, mod, g, w, cos_t, sin_t):
    def rope_tile(i):
        return jnp.where(i < NW_P, 0, 1 + (i - NW_P) % WPS)

    per_tile = TW // SEQ
    return pl.pallas_call(
        _ab_in_kernel,
        grid=(NW,),
        in_specs=_split_specs(D_MODEL, TW, NW_P) + [
            pl.BlockSpec((None, 6, D_MODEL), lambda i: (_mod_row(i, NW_P, WPS), 0, 0)),
            pl.BlockSpec((1, D_MODEL), lambda i: (0, 0)),
            pl.BlockSpec((D_MODEL, 4 * HALF), lambda i: (0, 0)),
            pl.BlockSpec((TW, DA_DV), lambda i: (rope_tile(i), 0)),
            pl.BlockSpec((TW, DA_DV), lambda i: (rope_tile(i), 0)),
        ],
        out_specs=[
            pl.BlockSpec((TW, HALF), lambda i: (i, 0)),
            pl.BlockSpec((TW, 3 * HALF), lambda i: (i, 0)),
            pl.BlockSpec((per_tile, HALF, SEQ), lambda i: (jnp.minimum(i, NW_P - 1), 0, 0)),
            pl.BlockSpec((per_tile, None, SEQ, DA_HEADS, DA_DV), lambda i: (jnp.minimum(i, NW_P - 1), 0, 0, 0, 0)),
        ],
        out_shape=[
            jax.ShapeDtypeStruct((N_TOK, HALF), F32),
            jax.ShapeDtypeStruct((N_TOK, 3 * HALF), BF16),
            jax.ShapeDtypeStruct((BATCH, HALF, SEQ), F32),
            jax.ShapeDtypeStruct((BATCH, 1, SEQ, DA_HEADS, DA_DV), F32),
        ],
        compiler_params=_params(),
        name="ab_in",
    )(xp, xs, mod, g, w, cos_t, sin_t)


def _rope_tables():
    pos = jnp.arange(DEC_SEQ)
    t_row = (pos // GRID_W).astype(F32)
    t_col = (pos % GRID_W).astype(F32)
    inv = ROPE_THETA ** (-jnp.arange(ROPE_F, dtype=F32) / ROPE_F)
    ang = jnp.stack([t_row[:, None] * inv, t_col[:, None] * inv], axis=1)
    cos, sin = jnp.cos(ang), jnp.sin(ang)
    cos64 = jnp.stack([cos, cos], axis=2).reshape(DEC_SEQ, DA_DK)
    sin64 = jnp.stack([-sin, sin], axis=2).reshape(DEC_SEQ, DA_DK)
    cos_t = jnp.concatenate([jnp.ones((TW, DA_DV), F32), jnp.tile(cos64, (1, 2))], axis=0)
    sin_t = jnp.concatenate([jnp.zeros((TW, DA_DV), F32), jnp.tile(sin64, (1, 2))], axis=0)
    return cos_t, sin_t


def _s5_param_kernel(lre_ref, lim_ref, ldt_ref, bt_ref, c_ref, m1_ref, m2_ref, m3_ref, at_ref):
    nt = (((1,), (1,)), ((), ()))
    e_tab, f_tab, k_tab = [], [], []
    r_i = lax.broadcasted_iota(jnp.int32, (LANES, LANES), 0) // S5_P
    c_i = lax.broadcasted_iota(jnp.int32, (LANES, LANES), 1) // S5_P
    same_group = r_i == c_i
    for d in range(2):
        lre = lre_ref[d]
        lim = lim_ref[d]
        dt = jnp.exp(ldt_ref[d])
        mag = jnp.exp(lre * dt)
        ang = lim * dt
        ar = mag * jnp.cos(ang)
        ai = mag * jnp.sin(ang)
        den = lre * lre + lim * lim
        nr = ar - 1.0
        fr = (nr * lre + ai * lim) / den
        fi = (ai * lre - nr * lim) / den
        btr = bt_ref[d, 0]
        bti = bt_ref[d, 1]
        bfr = fr * btr - fi * bti
        bfi = fr * bti + fi * btr
        cr = c_ref[d, 0]
        ci = c_ref[d, 1]
        cst = jnp.concatenate([cr, -ci], axis=1)
        pr = jnp.ones_like(ar)
        pi = jnp.zeros_like(ar)
        e_d, f_d, k_d = [], [], []
        for tau in range(S5_T + 1):
            f_d.append((cr * pr - ci * pi, cr * pi + ci * pr))
            if tau < S5_T:
                er, ei = pr * bfr - pi * bfi, pr * bfi + pi * bfr
                e_d.append((er, ei))
                full = lax.dot_general(jnp.concatenate([er, ei], axis=1), cst, nt,
                                       precision=HIGHEST, preferred_element_type=F32)
                k_d.append(jnp.where(same_group, full, 0.0))
            else:
                at_ref[d, 0] = pr
                at_ref[d, 1] = pi
            pr, pi = pr * ar - pi * ai, pr * ai + pi * ar
        e_tab.append(e_d)
        f_tab.append(f_d)
        k_tab.append(k_d)

    for s in range(S5_T):
        for t in range(S5_T):
            blk = k_tab[0][t - s] if t >= s else k_tab[1][s - t]
            if t == s:
                blk = blk + k_tab[1][0]
            m1_ref[s * LANES:(s + 1) * LANES, t * LANES:(t + 1) * LANES] = blk.astype(BF16)
    m2_ref[...] = jnp.zeros(m2_ref.shape, BF16)
    m3_ref[...] = jnp.zeros(m3_ref.shape, BF16)
    for s in range(S5_T):
        ef, er_ = e_tab[0][S5_T - 1 - s], e_tab[1][s]
        v = jnp.concatenate([ef[0], er_[0], ef[1], er_[1]], axis=1).astype(BF16)
        ff, fr_ = f_tab[0][s + 1], f_tab[1][S5_T - s]
        w = jnp.concatenate([ff[0], fr_[0], -ff[1], -fr_[1]], axis=1).astype(BF16)
        for gi in range(S5_GB):
            rows = slice(s * LANES + gi * S5_P, s * LANES + (gi + 1) * S5_P)
            cols = slice(gi * S5_ST, (gi + 1) * S5_ST)
            m2_ref[rows, cols] = v[gi * S5_P:(gi + 1) * S5_P]
            m3_ref[rows, cols] = w[gi * S5_P:(gi + 1) * S5_P]


def _s5_params(lam_re, lam_im, log_dt, b_w, c_w):
    rows = S5_G * S5_P
    per_row = lambda a: jnp.repeat(a, S5_P, axis=1)
    lre = per_row(lam_re)
    lim = per_row(lam_im)
    ldt = per_row(jnp.broadcast_to(log_dt[:, :, None], (2, S5_G, S5_N)))
    bt = b_w.transpose(0, 1, 2, 4, 3).reshape(2, 2, rows, S5_N)
    cc = c_w.reshape(2, 2, rows, S5_N)
    vec = pl.BlockSpec((2, LANES, S5_N), lambda j: (0, j, 0))
    mat = pl.BlockSpec((2, 2, LANES, S5_N), lambda j: (0, 0, j, 0))
    wide = lambda n: pl.BlockSpec((None, S5_K, n), lambda j: (j, 0, 0))
    m1, m2, m3, at = pl.pallas_call(
        _s5_param_kernel,
        grid=(S5_NB,),
        in_specs=[vec, vec, vec, mat, mat],
        out_specs=[wide(S5_K), wide(S5_GB * S5_ST), wide(S5_GB * S5_ST),
                   pl.BlockSpec((None, 2, 2, LANES, S5_N), lambda j: (j, 0, 0, 0, 0))],
        out_shape=[
            jax.ShapeDtypeStruct((S5_NB, S5_K, S5_K), BF16),
            jax.ShapeDtypeStruct((S5_NB, S5_K, S5_GB * S5_ST), BF16),
            jax.ShapeDtypeStruct((S5_NB, S5_K, S5_GB * S5_ST), BF16),
            jax.ShapeDtypeStruct((S5_NB, 2, 2, LANES, S5_N), F32),
        ],
        compiler_params=_params(),
        name="s5_params",
    )(lre, lim, ldt, bt, cc)
    coef = at[:, :, :, ::S5_P, :].transpose(0, 3, 2, 1, 4).reshape(S5_NB, S5_GB, 2, 2 * S5_N)
    return m1, m2, m3, coef


def _s5_kernel(u_ref, m1_ref, m2_ref, m3_ref, coef_ref, d_ref, *rest, stride, nc, two_pass):
    if two_pass:
        h0_ref, y_ref, lhs, yacc, s_scr, hp_scr = rest
    else:
        y_ref, fin_ref, lhs, yacc, s_scr, hp_scr = rest
    ln = 2 * S5_N
    nt = (((1,), (1,)), ((), ()))

    def seq_rows(k, c):
        return pl.ds(pl.multiple_of(k * stride + c * S5_T, S5_T), S5_T)

    def gather(c2, carry):
        halves = []
        for c in (2 * c2, 2 * c2 + 1):
            tiles = [u_ref[seq_rows(k, c), :] for k in range(8)]
            halves.append(jnp.stack(tiles, axis=0).reshape(8, S5_K))
        lhs[pl.ds(pl.multiple_of(c2 * 16, 16), 16), :] = jnp.concatenate(halves, axis=0).astype(BF16)
        return carry

    lax.fori_loop(0, nc // 2, gather, 0)
    lb = lhs[...]
    yacc[...] = jnp.dot(lb, m1_ref[...], preferred_element_type=F32)

    lane = lax.broadcasted_iota(jnp.int32, (8, ln), 1)
    row = lax.broadcasted_iota(jnp.int32, (8, ln), 0)
    fwd = lane < S5_N
    odd = (row % 2) == 1
    wide = S5_GS * S5_ST
    for gq in range(S5_GB // S5_GS):
        groups = range(gq * S5_GS, (gq + 1) * S5_GS)
        cols = slice(gq * wide, (gq + 1) * wide)
        s_scr[...] = jnp.dot(lb, m2_ref[:, cols], preferred_element_type=F32)
        are = [jnp.broadcast_to(coef_ref[g, 0:1, :], (8, ln)) for g in groups]
        aim = [jnp.broadcast_to(coef_ref[g, 1:2, :], (8, ln)) for g in groups]

        def scan(init, store):
            def body(c, carry):
                kf = pl.multiple_of(c * 8, 8)
                kr = pl.multiple_of((nc - 1 - c) * 8, 8)
                out = []
                for j in range(S5_GS):
                    p, q = carry[2 * j], carry[2 * j + 1]
                    o = j * S5_ST
                    if store:
                        hp_scr[pl.ds(kf, 8), o:o + S5_N] = p[:, 0:S5_N]
                        hp_scr[pl.ds(kr, 8), o + S5_N:o + ln] = p[:, S5_N:ln]
                        hp_scr[pl.ds(kf, 8), o + ln:o + ln + S5_N] = q[:, 0:S5_N]
                        hp_scr[pl.ds(kr, 8), o + ln + S5_N:o + 2 * ln] = q[:, S5_N:ln]
                    s_re = jnp.where(fwd, s_scr[pl.ds(kf, 8), o:o + ln], s_scr[pl.ds(kr, 8), o:o + ln])
                    s_im = jnp.where(fwd, s_scr[pl.ds(kf, 8), o + ln:o + 2 * ln],
                                     s_scr[pl.ds(kr, 8), o + ln:o + 2 * ln])
                    out += [are[j] * p - aim[j] * q + s_re, are[j] * q + aim[j] * p + s_im]
                return tuple(out)
            return lax.fori_loop(0, nc, body, tuple(init))

        if two_pass:
            h0 = [h0_ref[g, r] for g in groups for r in range(2)]
            fin = scan(h0, False)
            seed = [h + jnp.where(fwd & odd, pltpu.roll(f, 1, axis=0), 0.0)
                    + jnp.where(jnp.logical_not(fwd | odd), pltpu.roll(f, 7, axis=0), 0.0)
                    for h, f in zip(h0, fin)]
            scan(seed, True)
        else:
            fin = scan([jnp.zeros((8, ln), F32)] * (2 * S5_GS), True)
            for j, g in enumerate(groups):
                fin_ref[g, 0] = fin[2 * j]
                fin_ref[g, 1] = fin[2 * j + 1]
        yacc[...] += lax.dot_general(hp_scr[...].astype(BF16), m3_ref[:, cols], nt, preferred_element_type=F32)

    dvec = d_ref[...]

    def scatter(c, carry):
        blk = yacc[pl.ds(pl.multiple_of(c * 8, 8), 8), :].reshape(8, S5_T, LANES)
        for k in range(8):
            rows = seq_rows(k, c)
            y_ref[rows, :] = blk[k] + u_ref[rows, :] * dvec
        return carry

    lax.fori_loop(0, nc, scatter, 0)


def _s5_call(u, row_block, mats, coef, dvec, h0, stride, nc, parts):
    m1, m2, m3 = mats
    rows = 8 * stride
    two_pass = h0 is not None
    mat = lambda n: pl.BlockSpec((None, S5_K, n), lambda j, i: (j, 0, 0))
    tok = pl.BlockSpec((rows, LANES), lambda j, i: (row_block * parts + i, j))
    in_specs = [
        tok, mat(S5_K), mat(S5_GB * S5_ST), mat(S5_GB * S5_ST),
        pl.BlockSpec((None, S5_GB, 2, 2 * S5_N), lambda j, i: (j, 0, 0, 0)),
        pl.BlockSpec((None, 1, LANES), lambda j, i: (j, 0, 0)),
    ]
    args = [u, m1, m2, m3, coef, dvec]
    out_specs = [pl.BlockSpec((rows, LANES), lambda j, i: (i, j))]
    out_shape = [jax.ShapeDtypeStruct((parts * rows, HALF), F32)]
    if two_pass:
        in_specs.append(pl.BlockSpec((None, S5_GB, 2, 8, 2 * S5_N), lambda j, i: (j, 0, 0, 0, 0)))
        args.append(h0)
    else:
        out_specs.append(pl.BlockSpec((None, S5_GB, 2, 8, 2 * S5_N), lambda j, i: (j, 0, 0, i, 0)))
        out_shape.append(jax.ShapeDtypeStruct((S5_NB, S5_GB, 2, parts * 8, 2 * S5_N), F32))
    return pl.pallas_call(
        functools.partial(_s5_kernel, stride=stride, nc=nc, two_pass=two_pass),
        grid=(S5_NB, parts),
        in_specs=in_specs,
        out_specs=out_specs,
        out_shape=out_shape,
        scratch_shapes=[
            pltpu.VMEM((nc * 8, S5_K), BF16),
            pltpu.VMEM((nc * 8, S5_K), F32),
            pltpu.VMEM((nc * 8, S5_GS * S5_ST), F32),
            pltpu.VMEM((nc * 8, S5_GS * S5_ST), F32),
        ],
        compiler_params=_params(("arbitrary", "arbitrary")),
        name="s5_%d" % stride,
    )(*args)


def _s5(u, mats, coef, h0, s5_d):
    dvec = s5_d.reshape(S5_NB, 1, LANES)
    yp, fin = _s5_call(u, 0, mats, coef, dvec, None, SEQ, SEQ // S5_T, BATCH // 8)
    hh = h0.transpose(3, 2, 0, 1, 4)
    z = jnp.zeros_like(hh[:, :, :, 0])
    first = jnp.concatenate([hh[:, :, :, 0], z], axis=-1)
    second = jnp.concatenate([z, hh[:, :, :, 1]], axis=-1)
    h0g = jnp.stack([first, second], axis=3).reshape(S5_NB, S5_GB, 2, 2 * DEC_BATCH, 2 * S5_N)
    half = DEC_SEQ // 2
    (ys,) = _s5_call(u, 1, mats, coef, dvec, h0g, half, half // S5_T, 1)
    new_s5 = fin.reshape(S5_G, 2, BATCH, 2, S5_N).transpose(2, 3, 1, 0, 4)
    return yp, ys, new_s5


def _attn_kernel(*refs, lam_init, has_ctx, lk):
    if has_ctx:
        q_ref, k_ref, v_ref, kc_ref, vc_ref, dl_ref, g_ref, o_ref, s_a, s_b, vx, vcx = refs
    else:
        q_ref, k_ref, v_ref, dl_ref, g_ref, o_ref, s_a, s_b, vx = refs
    lt = lk + (PAST_LEN if has_ctx else 0)
    dl = dl_ref[...]
    lam = (jnp.exp(jnp.sum(dl[0:1] * dl[1:2], keepdims=True))
           - jnp.exp(jnp.sum(dl[2:3] * dl[3:4], keepdims=True)) + lam_init)
    lane = lax.broadcasted_iota(jnp.int32, (1, DA_DV), 1)
    first = lane < DA_DK
    g = g_ref[...]
    nt = (((1,), (1,)), ((), ()))

    @pl.when(pl.program_id(1) == 0)
    def _():
        onehot = (lane == 0).astype(BF16)
        for h in range(DA_HEADS):
            cols = slice(h * DA_DV, (h + 1) * DA_DV)
            vx[:, 2 * h * DA_DV:(2 * h + 1) * DA_DV] = v_ref[:, cols]
            vx[:, (2 * h + 1) * DA_DV:(2 * h + 2) * DA_DV] = jnp.broadcast_to(onehot, (lk, DA_DV))
            if has_ctx:
                vcx[:, 2 * h * DA_DV:(2 * h + 1) * DA_DV] = vc_ref[:, cols]
                vcx[:, (2 * h + 1) * DA_DV:(2 * h + 2) * DA_DV] = jnp.broadcast_to(onehot, (PAST_LEN, DA_DV))

    def scores(h, dst):
        cols = slice(h * DA_DV, (h + 1) * DA_DV)
        qh = q_ref[:, cols]
        zero = jnp.zeros_like(qh)
        for m in range(2):
            qm = jnp.where(first if m == 0 else jnp.logical_not(first), qh, zero)
            dst[m, :, 0:lk] = lax.dot_general(qm, k_ref[:, cols], nt, preferred_element_type=F32)
            if has_ctx:
                dst[m, :, lk:lt] = lax.dot_general(qm, kc_ref[:, cols], nt, preferred_element_type=F32)

    def head(h, src):
        xcols = slice(2 * h * DA_DV, (2 * h + 2) * DA_DV)
        outs = []
        for m in range(2):
            s = src[m]
            e = jnp.exp2(s - jnp.max(s, axis=-1, keepdims=True)).astype(BF16)
            ox = jnp.dot(e[:, 0:lk], vx[:, xcols], preferred_element_type=F32)
            if has_ctx:
                ox = ox + jnp.dot(e[:, lk:lt], vcx[:, xcols], preferred_element_type=F32)
            outs.append(ox[:, 0:DA_DV] * (1.0 / ox[:, DA_DV:DA_DV + 1]))
        o = outs[0] - lam * outs[1]
        o_ref[:, h * DA_DV:(h + 1) * DA_DV] = (_rms(o, g) * (1.0 - lam_init)).astype(BF16)

    scores(0, s_a)
    for h in range(DA_HEADS):
        cur, nxt = (s_a, s_b) if h % 2 == 0 else (s_b, s_a)
        if h + 1 < DA_HEADS:
            scores(h + 1, nxt)
        head(h, cur)


def _attention(qkv, q_tile0, nb, nq, lk, ctx, da_lam, da_g, lam_init):
    kb0 = q_tile0 * TM // lk
    in_specs = [
        pl.BlockSpec((TM, HALF), lambda b, j: (q_tile0 + b * nq + j, 0)),
        pl.BlockSpec((lk, HALF), lambda b, j: (kb0 + b, 1)),
        pl.BlockSpec((lk, HALF), lambda b, j: (kb0 + b, 2)),
    ]
    args = [qkv, qkv, qkv]
    if ctx is not None:
        in_specs += [pl.BlockSpec((None, PAST_LEN, HALF), lambda b, j: (b, 0, 0))] * 2
        args += list(ctx)
    in_specs += [
        pl.BlockSpec((4, DA_DK), lambda b, j: (0, 0)),
        pl.BlockSpec((1, DA_DV), lambda b, j: (0, 0)),
    ]
    lt = lk + (PAST_LEN if ctx is not None else 0)
    scratch = [pltpu.VMEM((2, TM, lt), F32), pltpu.VMEM((2, TM, lt), F32), pltpu.VMEM((lk, 2 * HALF), BF16)]
    if ctx is not None:
        scratch.append(pltpu.VMEM((PAST_LEN, 2 * HALF), BF16))
    return pl.pallas_call(
        functools.partial(_attn_kernel, lam_init=lam_init, has_ctx=ctx is not None, lk=lk),
        grid=(nb, nq),
        in_specs=in_specs,
        out_specs=pl.BlockSpec((TM, HALF), lambda b, j: (b * nq + j, 0)),
        out_shape=jax.ShapeDtypeStruct((nb * nq * TM, HALF), BF16),
        scratch_shapes=scratch,
        compiler_params=_params(("arbitrary", "arbitrary")),
        name="diff_attn_%d" % lk,
    )(*args, da_lam, da_g.reshape(1, DA_DV))


def _post(x, y, g, gate):
    return x + gate * _rms(y, g)


def _ab_out_kernel(ysp_ref, yss_ref, ybp_ref, ybs_ref, xp_ref, xs_ref, mod_ref, g_ref, wg_ref, bg_ref, wo_ref,
                   o_ref):
    ys = _gelu(_pick(ysp_ref, yss_ref, NW_P))
    glu = jnp.dot(ys.astype(BF16), wg_ref[...], preferred_element_type=F32) + bg_ref[...]
    ya = ys * _sigmoid(glu)
    out = (jnp.dot(ya.astype(BF16), wo_ref[0:HALF, :], preferred_element_type=F32)
           + jnp.dot(_pick(ybp_ref, ybs_ref, NW_P), wo_ref[HALF:2 * HALF, :], preferred_element_type=F32))
    o_ref[...] = _post(_pick(xp_ref, xs_ref, NW_P), out, g_ref[...], mod_ref[2:3, :])


def _ab_out(ys5_p, ys5_s, yb_p, yb_s, xp, xs, mod, g, w_glu, b_glu, w_out):
    return pl.pallas_call(
        _ab_out_kernel,
        grid=(NW,),
        in_specs=_split_specs(HALF, TW, NW_P) + _split_specs(HALF, TW, NW_P) + _split_specs(D_MODEL, TW, NW_P) + [
            pl.BlockSpec((None, 6, D_MODEL), lambda i: (_mod_row(i, NW_P, WPS), 0, 0)),
            pl.BlockSpec((1, D_MODEL), lambda i: (0, 0)),
            pl.BlockSpec((HALF, HALF), lambda i: (0, 0)),
            pl.BlockSpec((1, HALF), lambda i: (0, 0)),
            pl.BlockSpec((D_MODEL, D_MODEL), lambda i: (0, 0)),
        ],
        out_specs=pl.BlockSpec((TW, D_MODEL), lambda i: (i, 0)),
        out_shape=jax.ShapeDtypeStruct((N_TOK, D_MODEL), F32),
        compiler_params=_params(),
        name="ab_out",
    )(ys5_p, ys5_s, yb_p, yb_s, xp, xs, mod, g, w_glu, b_glu, w_out)


def _halo_specs(width):
    blocks = TM // HALO
    last = N_TOK // HALO - 1
    return [
        pl.BlockSpec((TM, width), lambda i: (i, 0)),
        pl.BlockSpec((HALO, width), lambda i: (jnp.maximum(i * blocks - 1, 0), 0)),
        pl.BlockSpec((HALO, width), lambda i: (jnp.minimum((i + 1) * blocks, last), 0)),
    ]


def _fill_hbuf(hbuf, x_ref, xp_ref, xn_ref, g, shift, scale, i):
    pos, n = _seq_pos(i)
    hp = jnp.where(pos > 0, _pre(xp_ref[...], g, shift, scale), 0.0)
    hn = jnp.where(pos < n - 1, _pre(xn_ref[...], g, shift, scale), 0.0)
    hbuf[0:TM, :] = _pre(x_ref[...], g, shift, scale).astype(BF16)
    hbuf[TM:ROWS, :] = jnp.concatenate([hn, hp], axis=0).astype(BF16)


def _shift_rows(x, s):
    if s == 0:
        return x[0:TM]
    return pltpu.roll(x, (-s) % ROWS, axis=0)[0:TM]


def _ffn_kernel(x_ref, xp_ref, xn_ref, mod_ref, g2_ref, g3_ref, wg_ref, wv_ref, wd_ref, cw_ref, cb_ref, *rest, split):
    hbuf, acc, u_a, u_b, u_c, wu_s, wd_s = rest[-7:]
    step = pl.program_id(0)

    @pl.when(step < FF_NCH)
    def _():
        wu_s[step] = wg_ref[...].astype(BF16)
        wu_s[FF_NCH + step] = wv_ref[...].astype(BF16)
        wd_s[step] = wd_ref[...].astype(BF16)

    @pl.when(step >= FF_NCH)
    def _():
        i = step - FF_NCH
        _fill_hbuf(hbuf, x_ref, xp_ref, xn_ref, g2_ref[...], mod_ref[3:4, :], mod_ref[4:5, :], i)

        def up(j, dst):
            for half in range(2):
                dst[half] = jnp.dot(hbuf[...], wu_s[half * FF_NCH + j], preferred_element_type=F32)

        def activation(j, src):
            parts = []
            for half in range(2):
                c0 = half * D_FF + j * FF_CHUNK
                sc = 1.0 if half == 0 else 0.5
                cw = cw_ref[:, c0:c0 + FF_CHUNK] * sc
                u = src[half]
                parts.append(_shift_rows(u, -1) * cw[0:1] + _shift_rows(u, 0) * cw[1:2]
                             + _shift_rows(u, 1) * cw[2:3] + cb_ref[:, c0:c0 + FF_CHUNK] * sc)
            gt = parts[0]
            z = gt * (gt * gt * (GELU_C * GELU_K) + GELU_K)
            return ((gt + gt * jnp.tanh(z)) * parts[1]).astype(BF16)

        bufs = (u_a, u_b, u_c)
        up(0, bufs[0])
        up(1, bufs[1])
        for j in range(FF_NCH):
            if j + 2 < FF_NCH:
                up(j + 2, bufs[(j + 2) % 3])
            contrib = jnp.dot(activation(j, bufs[j % 3]), wd_s[j], preferred_element_type=F32)
            if j == 0:
                acc[...] = contrib
            else:
                acc[...] += contrib
        res = _post(x_ref[...], acc[...], g3_ref[...], mod_ref[5:6, :])
        if split:
            @pl.when(i < NT_P)
            def _():
                rest[0][...] = res

            @pl.when(i >= NT_P)
            def _():
                rest[1][...] = res
        else:
            rest[0][...] = res


def _ffn(x, mod, g2, g3, w_up, cw, cb, w_down, layer, split):
    const = lambda s: (0, 0)
    tile = lambda s: jnp.maximum(s - FF_NCH, 0)
    chunk = lambda s: jnp.minimum(s, FF_NCH - 1)
    shifted = lambda spec: pl.BlockSpec(spec.block_shape, lambda s, f=spec.index_map: f(tile(s)))
    if split:
        out_specs = [shifted(sp) for sp in _split_specs(D_MODEL)]
        out_shape = [jax.ShapeDtypeStruct((NT_P * TM, D_MODEL), F32), jax.ShapeDtypeStruct((NT_S * TM, D_MODEL), F32)]
    else:
        out_specs = pl.BlockSpec((TM, D_MODEL), lambda s: (tile(s), 0))
        out_shape = jax.ShapeDtypeStruct((N_TOK, D_MODEL), F32)
    return pl.pallas_call(
        functools.partial(_ffn_kernel, split=split),
        grid=(FF_NCH + NT,),
        in_specs=[shifted(sp) for sp in _halo_specs(D_MODEL)] + [
            pl.BlockSpec((None, 6, D_MODEL), lambda s: (_mod_row(tile(s)), 0, 0)),
            pl.BlockSpec((1, D_MODEL), const),
            pl.BlockSpec((1, D_MODEL), const),
            pl.BlockSpec((None, D_MODEL, FF_CHUNK), lambda s: (layer, 0, chunk(s))),
            pl.BlockSpec((None, D_MODEL, FF_CHUNK), lambda s: (layer, 0, FF_NCH + chunk(s))),
            pl.BlockSpec((None, FF_CHUNK, D_MODEL), lambda s: (layer, chunk(s), 0)),
            pl.BlockSpec((3, 2 * D_FF), const),
            pl.BlockSpec((1, 2 * D_FF), const),
        ],
        out_specs=out_specs,
        out_shape=out_shape,
        scratch_shapes=[
            pltpu.VMEM((ROWS, D_MODEL), BF16),
            pltpu.VMEM((TM, D_MODEL), F32),
            pltpu.VMEM((2, ROWS, FF_CHUNK), F32),
            pltpu.VMEM((2, ROWS, FF_CHUNK), F32),
            pltpu.VMEM((2, ROWS, FF_CHUNK), F32),
            pltpu.VMEM((2 * FF_NCH, D_MODEL, FF_CHUNK), BF16),
            pltpu.VMEM((FF_NCH, FF_CHUNK, D_MODEL), BF16),
        ],
        compiler_params=_params(),
        name="conv_ffn",
    )(x, x, x, mod, g2, g3, w_up, w_up, w_down, cw, cb)


def _softplus(z):
    e = jnp.exp(-jnp.abs(z))
    u = 1.0 + e
    tiny = u == 1.0
    log1p = jnp.where(tiny, e, jnp.log(u) * (e / jnp.where(tiny, 1.0, u - 1.0)))
    return jnp.maximum(z, 0.0) + log1p


def _lru_scan(a_ref, b_ref, hs_ref, p_scr, carry, reverse):
    half = TM // 2
    h_a = carry[...]
    h_b = jnp.zeros_like(h_a)
    p_b = jnp.ones_like(h_a)
    for j in range(half):
        r_a = TM - 1 - j if reverse else j
        r_b = half - 1 - j if reverse else half + j
        h_a = a_ref[r_a:r_a + 1, :] * h_a + b_ref[r_a:r_a + 1, :]
        hs_ref[r_a:r_a + 1, :] = h_a
        a_b = a_ref[r_b:r_b + 1, :]
        h_b = a_b * h_b + b_ref[r_b:r_b + 1, :]
        p_b = p_b * a_b
        hs_ref[r_b:r_b + 1, :] = h_b
        p_scr[r_b % half:r_b % half + 1, :] = p_b
    second = slice(0, half) if reverse else slice(half, TM)
    hs_ref[second, :] = hs_ref[second, :] + p_scr[...] * h_a
    h = h_b + p_b * h_a
    carry[...] = h
    return h


def _cd_in_kernel(x_ref, xp_ref, xn_ref, mod_ref, g_ref, w_ref, scw_ref, cw_ref, cb_ref, wg_ref, bg_ref,
                  lam_ref, h0_ref, yc_ref, gate_ref, hsf_ref, ar_ref, br_ref, fin_ref,
                  hbuf, a_scr, b_scr, p_scr, carry):
    i = pl.program_id(0)
    pos, _ = _seq_pos(i)
    _fill_hbuf(hbuf, x_ref, xp_ref, xn_ref, g_ref[...], mod_ref[0:1, :], mod_ref[1:2, :], i)

    def col(k, rows):
        return jnp.dot(hbuf[0:rows, :], w_ref[:, k * HALF:(k + 1) * HALF], preferred_element_type=F32)

    xr = col(3, ROWS)
    xin = col(0, ROWS)
    cg = col(2, ROWS)
    cw = cw_ref[...]
    xc = (_shift_rows(xr, -2) * cw[0:1] + _shift_rows(xr, -1) * cw[1:2] + _shift_rows(xr, 0) * cw[2:3]
          + _shift_rows(xr, 1) * cw[3:4] + cb_ref[...])
    xcb = xc.astype(BF16)

    def direction(d):
        cols = slice(2 * d * HALF, (2 * d + 2) * HALF)
        gates = _sigmoid(jnp.dot(xcb, wg_ref[:, cols], preferred_element_type=F32) + bg_ref[:, cols])
        log_a = (-LRU_C) * gates[:, 0:HALF] * _softplus(-lam_ref[d:d + 1, :])
        a = jnp.exp(log_a)
        drive = jnp.sqrt(-jnp.tanh(log_a) * (a * a + 1.0))
        return a, drive * (gates[:, HALF:2 * HALF] * xc)

    a, bval = direction(0)
    a_scr[...] = a
    b_scr[...] = bval

    @pl.when(pos == 0)
    def _():
        carry[...] = h0_ref[0:1, :]

    h = _lru_scan(a_scr, b_scr, hsf_ref, p_scr, carry, False)
    fin_ref[...] = jnp.broadcast_to(h, (8, HALF))

    a, bval = direction(1)
    ar_ref[...] = a
    br_ref[...] = bval
    bg = col(1, TM)
    gb = col(4, TM)
    prod = cg * xin
    scw = scw_ref[...]
    yc = bg * (_shift_rows(prod, -1) * scw[0:1] + _shift_rows(prod, 0) * scw[1:2]
               + _shift_rows(prod, 1) * scw[2:3])
    yc_ref[...] = yc.astype(BF16)
    gate_ref[...] = _gelu(gb).astype(BF16)


def _cd_in(x, mod, g, w_in, sc_w, conv_w, conv_b, w_gates, b_gates, lru_lam, h0t):
    const = lambda i: (0, 0)
    tok = lambda dt: jax.ShapeDtypeStruct((N_TOK, HALF), dt)
    row = pl.BlockSpec((TM, HALF), lambda i: (i, 0))
    return pl.pallas_call(
        _cd_in_kernel,
        grid=(NT,),
        in_specs=_halo_specs(D_MODEL) + [
            pl.BlockSpec((None, 6, D_MODEL), lambda i: (_mod_row(i), 0, 0)),
            pl.BlockSpec((1, D_MODEL), const),
            pl.BlockSpec((D_MODEL, 5 * HALF), const),
            pl.BlockSpec((3, HALF), const),
            pl.BlockSpec((4, HALF), const),
            pl.BlockSpec((1, HALF), const),
            pl.BlockSpec((HALF, 4 * HALF), const),
            pl.BlockSpec((1, 4 * HALF), const),
            pl.BlockSpec((2, HALF), const),
            pl.BlockSpec((None, 2, HALF), lambda i: (_mod_row(i), 0, 0)),
        ],
        out_specs=[row, row, row, row, row, pl.BlockSpec((8, HALF), lambda i: (i, 0))],
        out_shape=[tok(BF16), tok(BF16), tok(F32), tok(F32), tok(F32), jax.ShapeDtypeStruct((NT * 8, HALF), F32)],
        scratch_shapes=[
            pltpu.VMEM((ROWS, D_MODEL), BF16),
            pltpu.VMEM((TM, HALF), F32),
            pltpu.VMEM((TM, HALF), F32),
            pltpu.VMEM((TM // 2, HALF), F32),
            pltpu.VMEM((1, HALF), F32),
        ],
        compiler_params=_params(),
        name="cd_in",
    )(x, x, x, mod, g, w_in, sc_w, conv_w, conv_b, w_gates, b_gates, lru_lam, h0t)


def _cd_out_kernel(ar_ref, br_ref, hsf_ref, gate_ref, yc_ref, x_ref, mod_ref, g_ref, wo_ref, h0_ref,
                   o_ref, fin_ref, hs_scr, p_scr, carry):
    ti = NT - 1 - pl.program_id(0)
    pos, n = _seq_pos(ti)

    @pl.when(pos == n - 1)
    def _():
        carry[...] = h0_ref[1:2, :]

    h = _lru_scan(ar_ref, br_ref, hs_scr, p_scr, carry, True)
    fin_ref[...] = jnp.broadcast_to(h, (8, HALF))
    out_c = jnp.dot(yc_ref[...], wo_ref[0:HALF, :], preferred_element_type=F32)
    yd = (hsf_ref[...] + hs_scr[...]) * gate_ref[...].astype(F32)
    out = out_c + jnp.dot(yd.astype(BF16), wo_ref[HALF:2 * HALF, :], preferred_element_type=F32)
    o_ref[...] = _post(x_ref[...], out, g_ref[...], mod_ref[2:3, :])


def _cd_out(a_r, b_r, hs_f, gate, yc, x, mod, g, w_out, h0t):
    const = lambda i: (0, 0)
    rev = lambda i: (NT - 1 - i, 0)
    row = pl.BlockSpec((TM, HALF), rev)
    return pl.pallas_call(
        _cd_out_kernel,
        grid=(NT,),
        in_specs=[
            row, row, row, row, row,
            pl.BlockSpec((TM, D_MODEL), rev),
            pl.BlockSpec((None, 6, D_MODEL), lambda i: (_mod_row(NT - 1 - i), 0, 0)),
            pl.BlockSpec((1, D_MODEL), const),
            pl.BlockSpec((D_MODEL, D_MODEL), const),
            pl.BlockSpec((None, 2, HALF), lambda i: (_mod_row(NT - 1 - i), 0, 0)),
        ],
        out_specs=[
            pl.BlockSpec((TM, D_MODEL), rev),
            pl.BlockSpec((8, HALF), rev),
        ],
        out_shape=[
            jax.ShapeDtypeStruct((N_TOK, D_MODEL), F32),
            jax.ShapeDtypeStruct((NT * 8, HALF), F32),
        ],
        scratch_shapes=[
            pltpu.VMEM((TM, HALF), F32),
            pltpu.VMEM((TM // 2, HALF), F32),
            pltpu.VMEM((1, HALF), F32),
        ],
        compiler_params=_params(),
        name="cd_out",
    )(a_r, b_r, hs_f, gate, yc, x, mod, g, w_out, h0t)


def kernel(x_prompt, x_sample, cache_attn_k, cache_attn_v, state_s5, state_rglru, c, c_ctx, w_mod, b_mod, norm_g, w_in_ab, w_out_ab, s5_lam_re, s5_lam_im, s5_log_dt, s5_b, s5_c, s5_d, s5_w_glu, s5_b_glu, da_lam, da_g, w_in_cd, w_out_cd, sc_conv_w, lru_conv_w, lru_conv_b, lru_w_a, lru_b_a, lru_w_x, lru_b_x, lru_lam, ffn_w_up, ffn_conv_w, ffn_conv_b, ffn_w_down):
    assert DEPTH == 2
    xp = x_prompt.reshape(NT_P * TM, D_MODEL)
    xs = x_sample.reshape(NT_S * TM, D_MODEL)
    cv8 = jnp.zeros((8, D_MODEL), F32).at[0].set(c_ctx).at[1:1 + DEC_BATCH].set(c)
    mod = _modulation(cv8, w_mod, b_mod)
    cos_t, sin_t = _rope_tables()
    g = norm_g.reshape(DEPTH, 4, 1, D_MODEL)

    lam_init = 0.8 - 0.6 * math.exp(-0.3 * 0)
    u, qkv, k32, v32 = _ab_in(xp, xs, mod[0], g[0, 0], w_in_ab[0].astype(BF16), cos_t, sin_t)
    m1, m2, m3, coef = _s5_params(s5_lam_re[0], s5_lam_im[0], s5_log_dt[0], s5_b[0], s5_c[0])
    ys5_p, ys5_s, new_s5 = _s5(u, (m1, m2, m3), coef, state_s5[:, 0], s5_d[0])
    yb_p = _attention(qkv, 0, BATCH, 1, SEQ, None, da_lam[0], da_g[0], lam_init)
    ctx = (cache_attn_k[:, 0].reshape(DEC_BATCH, PAST_LEN, HALF).astype(BF16),
           cache_attn_v[:, 0].reshape(DEC_BATCH, PAST_LEN, HALF).astype(BF16))
    yb_s = _attention(qkv, NT_P, DEC_BATCH, TPS, DEC_SEQ, ctx, da_lam[0], da_g[0], lam_init)
    x = _ab_out(ys5_p, ys5_s, yb_p, yb_s, xp, xs, mod[0], g[0, 1], s5_w_glu[0].astype(BF16), s5_b_glu[0].reshape(1, HALF),
                w_out_ab[0].astype(BF16))
    assert NT_P == BATCH and TM == SEQ
    new_k = k32.reshape(BATCH, 1, DA_HEADS, 2, DA_DK, SEQ).transpose(0, 1, 5, 2, 3, 4)
    new_v = v32
    x = _ffn(x, mod[0], g[0, 2], g[0, 3], ffn_w_up, ffn_conv_w[0], ffn_conv_b[0].reshape(1, 2 * D_FF), ffn_w_down,
             0, False)

    eye = jnp.eye(LRU_BLOCKS, dtype=F32)
    dense = lambda w: jnp.einsum('kcd,kl->kcld', w, eye).reshape(LRU_WIDTH, LRU_WIDTH)
    w_gates = jnp.concatenate([dense(lru_w_a[0, 0]), dense(lru_w_x[0, 0]),
                               dense(lru_w_a[0, 1]), dense(lru_w_x[0, 1])], axis=1).astype(BF16)
    b_gates = jnp.concatenate([lru_b_a[0, 0], lru_b_x[0, 0], lru_b_a[0, 1], lru_b_x[0, 1]]).reshape(1, 4 * HALF)
    h0t = jnp.zeros((8, 2, HALF), F32).at[1:1 + DEC_BATCH].set(state_rglru[:, 0])
    yc, gate, hs_f, a_r, b_r, fin_f = _cd_in(x, mod[1], g[1, 0], w_in_cd[0].astype(BF16), sc_conv_w[0], lru_conv_w[0],
                                      lru_conv_b[0].reshape(1, HALF), w_gates, b_gates, lru_lam[0], h0t)
    x, fin_r = _cd_out(a_r, b_r, hs_f, gate, yc, x, mod[1], g[1, 1], w_out_cd[0].astype(BF16), h0t)
    tile_row0 = lambda f: f.reshape(NT, 8, HALF)[:NT_P, 0]
    new_lru = jnp.stack([tile_row0(fin_f), tile_row0(fin_r)], axis=1)[:, None]
    yp, ys = _ffn(x, mod[1], g[1, 2], g[1, 3], ffn_w_up, ffn_conv_w[1], ffn_conv_b[1].reshape(1, 2 * D_FF),
                  ffn_w_down, 1, True)
    return (yp.reshape(BATCH, SEQ, D_MODEL), ys.reshape(DEC_BATCH, DEC_SEQ, D_MODEL),
            new_k, new_v, new_s5[:, None], new_lru)
```
